```python
import functools
import jax
import jax.numpy as jnp
from jax import lax
import numpy as np

D_MODEL = 2048
BATCH = 8
SEQ = 4096
DEPTH = 4

CTX_LEN = 256
GRID_W = 64
ROPE_BASE = 10000.0
NORM_EPS = 1e-6
NEG_INF = -1e30
QBLOCK = 128

MLA_HEADS = 4
MLA_NOPE = 128
MLA_ROPE = 64
MLA_V = 128
MLA_Q_LORA = 512
MLA_KV_LORA = 256
MLA_SCALE = (MLA_NOPE + MLA_ROPE) ** -0.5
POOL_WINDOWS = (2, 4, 8, 16)
POOL_GROUP = 128
POOL_WIDTH = POOL_GROUP * len(POOL_WINDOWS)
SWA_HEADS = 8
SWA_KV_HEADS = 2
SWA_HEAD_DIM = 64
SWA_WINDOW = 128
SWA_BLOCK = 128
NA_HEADS = 8
NA_HEAD_DIM = 64
NA_KH = 8
NA_KW = 16
NA_QC = 16
NA_KC = NA_QC + NA_KW
FFN_DIM = 5632

A_COLS = MLA_Q_LORA + MLA_KV_LORA + MLA_ROPE
B_COLS = POOL_WIDTH
C_COLS = (SWA_HEADS + 2 * SWA_KV_HEADS) * SWA_HEAD_DIM
D_COLS = 3 * NA_HEADS * NA_HEAD_DIM
IN_COLS = A_COLS + B_COLS + C_COLS + D_COLS
IN_SPLITS = (A_COLS, A_COLS + B_COLS, A_COLS + B_COLS + C_COLS)
MIX_OUT = MLA_HEADS * MLA_V + POOL_WIDTH + SWA_HEADS * SWA_HEAD_DIM + NA_HEADS * NA_HEAD_DIM

kernel_name = 'hybrid_pargroup_dit_block'

F32 = jnp.float32


def rms_norm(x, g):
    xf = x.astype(F32)
    y = xf * lax.rsqrt(jnp.mean(xf * xf, axis=-1, keepdims=True) + NORM_EPS)
    return (y * g.astype(F32)).astype(x.dtype)


def adaln(cond, w_mod, b_mod):
    return jnp.split(jax.nn.silu(cond) @ w_mod + b_mod, 6, axis=-1)


def modulate(h, shift, scale):
    return h * (1.0 + scale) + shift


def axial_angles(n, d_rot):
    t = jnp.arange(n)
    row = (t // GRID_W).astype(F32)
    col = (t % GRID_W).astype(F32)
    d_axis = d_rot // 2
    inv_freq = ROPE_BASE ** (-jnp.arange(0, d_axis, 2, dtype=F32) / d_axis)
    return (row[:, None] * inv_freq, col[:, None] * inv_freq)


def rope_segment(x, ang):
    cos = jnp.cos(ang)[:, None, :].astype(x.dtype)
    sin = jnp.sin(ang)[:, None, :].astype(x.dtype)
    x1, x2 = jnp.split(x, 2, axis=-1)
    return jnp.concatenate([x1 * cos - x2 * sin, x2 * cos + x1 * sin], axis=-1)


def axial_rope(x, ang):
    half = x.shape[-1] // 2
    return jnp.concatenate([rope_segment(x[..., :half], ang[0]), rope_segment(x[..., half:], ang[1])], axis=-1)


def map_query_blocks(fn, qs):
    b, n = qs[0].shape[:2]
    nb = n // QBLOCK
    blocks = tuple(jnp.moveaxis(q.reshape(b, nb, QBLOCK, *q.shape[2:]), 1, 0) for q in qs)
    out = lax.map(lambda qb: fn(*qb), blocks)
    return jnp.moveaxis(out, 0, 1).reshape(b, n, *out.shape[3:])


def mla_project(p, q_a_norm, w_qb, kv_a_norm, w_kvb, q_nope_norm, q_rope_norm, k_nope_norm, k_rope_norm, ang):
    b, n, _ = p.shape
    cq, ckv, kr = jnp.split(p, [MLA_Q_LORA, MLA_Q_LORA + MLA_KV_LORA], axis=-1)
    q = (rms_norm(cq, q_a_norm) @ w_qb).reshape(b, n, MLA_HEADS, MLA_NOPE + MLA_ROPE)
    kv = (rms_norm(ckv, kv_a_norm) @ w_kvb).reshape(b, n, MLA_HEADS, MLA_NOPE + MLA_V)
    q_nope = rms_norm(q[..., :MLA_NOPE], q_nope_norm)
    q_rope = rms_norm(q[..., MLA_NOPE:], q_rope_norm)
    k_nope = rms_norm(kv[..., :MLA_NOPE], k_nope_norm)
    v = kv[..., MLA_NOPE:]
    k_rope = rms_norm(kr, k_rope_norm)[:, :, None, :]
    if ang is not None:
        q_rope = axial_rope(q_rope, ang)
        k_rope = axial_rope(k_rope, ang)
    return q_nope, q_rope, k_nope, k_rope[:, :, 0, :], v


def mla_attend(q_nope, q_rope, k_nope, k_rope, v):
    s = (jnp.einsum('bqhd,bkhd->bhqk', q_nope, k_nope, preferred_element_type=F32)
         + jnp.einsum('bqhr,bkr->bhqk', q_rope, k_rope, preferred_element_type=F32))
    p = jax.nn.softmax(s * MLA_SCALE, axis=-1).astype(v.dtype)
    return jnp.einsum('bhqk,bkhd->bqhd', p, v)


def pool_mixer(u, w_pool, scale):
    b, n, _ = u.shape
    uf = u.astype(F32)
    csum = jnp.pad(jnp.cumsum(uf, axis=1), ((0, 0), (1, 0), (0, 0)))
    t = jnp.arange(n)
    diffs = []
    for g, w in enumerate(POOL_WINDOWS):
        sl = slice(g * POOL_GROUP, (g + 1) * POOL_GROUP)
        lo = jnp.clip(t - w // 2, 0, n)
        hi = jnp.clip(t + w // 2, 0, n)
        cs = csum[..., sl]
        mean = (cs[:, hi] - cs[:, lo]) / (hi - lo).astype(F32)[None, :, None]
        diffs.append(mean - uf[..., sl])
    d = jnp.stack(diffs, axis=2).astype(u.dtype)
    y = jnp.einsum('bngc,gcd->bngd', d, w_pool).reshape(b, n, POOL_WIDTH)
    return y * scale


def swa_project(p, q_norm, k_norm, ang):
    b, n, _ = p.shape
    q, k, v = jnp.split(p, [SWA_HEADS * SWA_HEAD_DIM, (SWA_HEADS + SWA_KV_HEADS) * SWA_HEAD_DIM], axis=-1)
    q = rms_norm(q.reshape(b, n, SWA_HEADS, SWA_HEAD_DIM), q_norm)
    k = rms_norm(k.reshape(b, n, SWA_KV_HEADS, SWA_HEAD_DIM), k_norm)
    v = v.reshape(b, n, SWA_KV_HEADS, SWA_HEAD_DIM)
    if ang is not None:
        q = axial_rope(q, ang)
        k = axial_rope(k, ang)
    return q, k, v


def swa_latent(q, k, v, k_ctx, v_ctx, sink):
    b, n, hq, d = q.shape
    hkv = k.shape[2]
    grp = hq // hkv
    blk = SWA_BLOCK
    nb = n // blk
    scale = d ** -0.5
    qg = q.reshape(b, nb, blk, hkv, grp, d)

    def band(t):
        tp = jnp.pad(t, ((0, 0), (blk, blk), (0, 0), (0, 0))).reshape(b, nb + 2, blk, hkv, d)
        return jnp.concatenate([tp[:, :-2], tp[:, 1:-1], tp[:, 2:]], axis=2)

    k_band, v_band = band(k), band(v)
    s_loc = jnp.einsum('bnqhgd,bnkhd->bnhgqk', qg, k_band, preferred_element_type=F32) * scale
    qpos = jnp.arange(nb)[:, None] * blk + jnp.arange(blk)[None, :]
    kpos = (jnp.arange(nb)[:, None] - 1) * blk + jnp.arange(3 * blk)[None, :]
    valid = ((jnp.abs(kpos[:, None, :] - qpos[:, :, None]) <= SWA_WINDOW)
             & (kpos[:, None, :] >= 0) & (kpos[:, None, :] < n))
    s_loc = jnp.where(valid[None, :, None, None], s_loc, NEG_INF)
    s_ctx = jnp.einsum('bnqhgd,bkhd->bnhgqk', qg, k_ctx, preferred_element_type=F32) * scale
    s_sink = jnp.broadcast_to(sink.astype(F32).reshape(hkv, grp)[:, :, None, None], s_loc.shape[:-1] + (1,))
    p = jax.nn.softmax(jnp.concatenate([s_loc, s_ctx, s_sink], axis=-1), axis=-1).astype(v.dtype)
    nk = 3 * blk
    n_ctx = k_ctx.shape[1]
    o = (jnp.einsum('bnhgqk,bnkhd->bnqhgd', p[..., :nk], v_band)
         + jnp.einsum('bnhgqk,bkhd->bnqhgd', p[..., nk:nk + n_ctx], v_ctx))
    return o.reshape(b, n, hq * d)


def ctx_attention(q, k, v, sink):
    b, nq, hq, d = q.shape
    hkv = k.shape[2]
    grp = hq // hkv
    nk = k.shape[1]
    qg = q.reshape(b, nq, hkv, grp, d)
    s = jnp.einsum('bqhgd,bkhd->bhgqk', qg, k, preferred_element_type=F32) * (d ** -0.5)
    if sink is not None:
        s_sink = jnp.broadcast_to(sink.astype(F32).reshape(hkv, grp)[None, :, :, None, None], s.shape[:-1] + (1,))
        s = jnp.concatenate([s, s_sink], axis=-1)
    p = jax.nn.softmax(s, axis=-1).astype(v.dtype)[..., :nk]
    o = jnp.einsum('bhgqk,bkhd->bqhgd', p, v)
    return o.reshape(b, nq, hq * d)


def na_project(p, q_norm, k_norm):
    b, n, _ = p.shape
    q, k, v = jnp.split(p, 3, axis=-1)
    q = rms_norm(q.reshape(b, n, NA_HEADS, NA_HEAD_DIM), q_norm)
    k = rms_norm(k.reshape(b, n, NA_HEADS, NA_HEAD_DIM), k_norm)
    v = v.reshape(b, n, NA_HEADS, NA_HEAD_DIM)
    return q, k, v


def na_latent(q, k, v, k_ctx, v_ctx, rpb):
    b, n, h, d = q.shape
    rows = n // GRID_W
    kh = min(NA_KH, rows)
    ncb = GRID_W // NA_QC
    nk = kh * NA_KC
    r = jnp.arange(rows)
    row_idx = jnp.clip(r - kh // 2, 0, rows - kh)[:, None] + jnp.arange(kh)[None, :]
    cb = jnp.arange(ncb)
    col_idx = jnp.clip(cb * NA_QC - NA_KW // 2, 0, GRID_W - NA_KC)[:, None] + jnp.arange(NA_KC)[None, :]
    q_col = cb[:, None] * NA_QC + jnp.arange(NA_QC)[None, :]
    q_col0 = jnp.clip(q_col - NA_KW // 2, 0, GRID_W - NA_KW)
    col_ok = (col_idx[:, None, :] >= q_col0[:, :, None]) & (col_idx[:, None, :] < q_col0[:, :, None] + NA_KW)
    mask = jnp.broadcast_to(col_ok[:, :, None, :], (ncb, NA_QC, kh, NA_KC)).reshape(ncb, 1, NA_QC, nk)

    def gather(t):
        grid = t.reshape(b, rows, GRID_W, h, d)
        g = grid[:, row_idx[:, None, :, None], col_idx[None, :, None, :]]
        return g.reshape(b, rows, ncb, nk, h, d)

    k_nb, v_nb = gather(k), gather(v)
    q_blk = q.reshape(b, rows, ncb, NA_QC, h, d)
    scale = d ** -0.5
    dr = row_idx - r[:, None] + (NA_KH - 1)
    dc = jnp.clip(col_idx[:, None, :] - q_col[:, :, None], 1 - NA_KW, NA_KW - 1) + (NA_KW - 1)
    bias = rpb.astype(F32)[:, dr[:, None, None, :, None], dc[None, :, :, None, :]]
    bias = jnp.moveaxis(bias, 0, 2).reshape(rows, ncb, h, NA_QC, nk)
    s_loc = jnp.einsum('brcqhd,brckhd->brchqk', q_blk, k_nb, preferred_element_type=F32) * scale + bias
    s_loc = jnp.where(mask[None, None], s_loc, NEG_INF)
    s_ctx = jnp.einsum('brcqhd,bkhd->brchqk', q_blk, k_ctx, preferred_element_type=F32) * scale
    p = jax.nn.softmax(jnp.concatenate([s_loc, s_ctx], axis=-1), axis=-1).astype(v.dtype)
    o = (jnp.einsum('brchqk,brckhd->brcqhd', p[..., :nk], v_nb)
         + jnp.einsum('brchqk,bkhd->brcqhd', p[..., nk:], v_ctx))
    return o.reshape(b, n, h * d)


def conv_ffn(h, w_up, conv_w, conv_b, w_down):
    a = h @ w_up
    ap = jnp.pad(a, ((0, 0), (1, 1), (0, 0)))
    a = ap[:, :-2] * conv_w[0] + ap[:, 1:-1] * conv_w[1] + ap[:, 2:] * conv_w[2] + conv_b
    gate, val = jnp.split(a, 2, axis=-1)
    return (jax.nn.silu(gate) * val) @ w_down


def _fwd_setup_inputs(seed: int = 0) -> dict:
    key = jax.random.key(seed)
    ks = jax.random.split(key, 32)
    L = DEPTH

    def nrm(k, shape, scale):
        return jax.random.normal(k, shape, F32) * scale

    def gain(k, shape):
        return 1.0 + 0.1 * jax.random.normal(k, shape, F32)

    return {
        'x': nrm(ks[0], (BATCH, SEQ, D_MODEL), 1.0),
        'c': nrm(ks[1], (BATCH, D_MODEL), 1.0),
        'ctx': nrm(ks[2], (BATCH, CTX_LEN, D_MODEL), 1.0),
        'c_ctx': nrm(ks[3], (D_MODEL,), 1.0),
        'w_mod': nrm(ks[4], (L, D_MODEL, 6 * D_MODEL), 0.5 * D_MODEL ** -0.5),
        'b_mod': nrm(ks[5], (L, 6 * D_MODEL), 0.01),
        'g_mix': gain(ks[6], (L, D_MODEL)),
        'g_ffn': gain(ks[7], (L, D_MODEL)),
        'w_in': nrm(ks[8], (L, D_MODEL, IN_COLS), D_MODEL ** -0.5),
        'w_out': nrm(ks[9], (L, MIX_OUT, D_MODEL), MIX_OUT ** -0.5),
        'mla_q_a_norm': gain(ks[10], (L, MLA_Q_LORA)),
        'mla_w_qb': nrm(ks[11], (L, MLA_Q_LORA, MLA_HEADS * (MLA_NOPE + MLA_ROPE)), MLA_Q_LORA ** -0.5),
        'mla_kv_a_norm': gain(ks[12], (L, MLA_KV_LORA)),
        'mla_w_kvb': nrm(ks[13], (L, MLA_KV_LORA, MLA_HEADS * (MLA_NOPE + MLA_V)), MLA_KV_LORA ** -0.5),
        'mla_q_nope_norm': gain(ks[14], (L, MLA_NOPE)),
        'mla_q_rope_norm': gain(ks[15], (L, MLA_ROPE)),
        'mla_k_nope_norm': gain(ks[16], (L, MLA_NOPE)),
        'mla_k_rope_norm': gain(ks[17], (L, MLA_ROPE)),
        'pool_w': nrm(ks[18], (L, len(POOL_WINDOWS), POOL_GROUP, POOL_GROUP), POOL_GROUP ** -0.5),
        'pool_scale': gain(ks[19], (L, POOL_WIDTH)),
        'swa_q_norm': gain(ks[20], (L, SWA_HEAD_DIM)),
        'swa_k_norm': gain(ks[21], (L, SWA_HEAD_DIM)),
        'swa_sink': nrm(ks[22], (L, SWA_HEADS), 0.5),
        'na_q_norm': gain(ks[23], (L, NA_HEAD_DIM)),
        'na_k_norm': gain(ks[24], (L, NA_HEAD_DIM)),
        'na_rpb': nrm(ks[25], (L, NA_HEADS, 2 * NA_KH - 1, 2 * NA_KW - 1), 0.5),
        'ffn_w_up': nrm(ks[26], (L, D_MODEL, 2 * FFN_DIM), D_MODEL ** -0.5),
        'ffn_conv_w': nrm(ks[27], (L, 3, 2 * FFN_DIM), 3 ** -0.5),
        'ffn_conv_b': nrm(ks[28], (L, 2 * FFN_DIM), 0.02),
        'ffn_w_down': nrm(ks[29], (L, FFN_DIM, D_MODEL), FFN_DIM ** -0.5),
    }


def _fwd_reference(x, c, ctx, c_ctx, w_mod, b_mod, g_mix, g_ffn, w_in, w_out,
              mla_q_a_norm, mla_w_qb, mla_kv_a_norm, mla_w_kvb,
              mla_q_nope_norm, mla_q_rope_norm, mla_k_nope_norm, mla_k_rope_norm,
              pool_w, pool_scale, swa_q_norm, swa_k_norm, swa_sink,
              na_q_norm, na_k_norm, na_rpb,
              ffn_w_up, ffn_conv_w, ffn_conv_b, ffn_w_down):
    b, n, _ = x.shape
    n_ctx = ctx.shape[1]
    ang_mla = axial_angles(n, MLA_ROPE)
    ang_swa = axial_angles(n, SWA_HEAD_DIM)
    cond_x = c[:, None, :]
    for l in range(DEPTH):
        update_ctx = l < DEPTH - 1
        sh_m, sc_m, gt_m, sh_f, sc_f, gt_f = adaln(cond_x, w_mod[l], b_mod[l])
        csh_m, csc_m, cgt_m, csh_f, csc_f, cgt_f = adaln(c_ctx, w_mod[l], b_mod[l])

        px = modulate(rms_norm(x, g_mix[l]), sh_m, sc_m) @ w_in[l]
        pc = modulate(rms_norm(ctx, g_mix[l]), csh_m, csc_m) @ w_in[l]
        mla_x, pool_x, swa_x, na_x = jnp.split(px, IN_SPLITS, axis=-1)
        mla_c, pool_c, swa_c, na_c = jnp.split(pc, IN_SPLITS, axis=-1)

        mla_w = (mla_q_a_norm[l], mla_w_qb[l], mla_kv_a_norm[l], mla_w_kvb[l],
                 mla_q_nope_norm[l], mla_q_rope_norm[l], mla_k_nope_norm[l], mla_k_rope_norm[l])
        aq_n, aq_r, ak_n, ak_r, av = mla_project(mla_x, *mla_w, ang_mla)
        cq_n, cq_r, ck_n, ck_r, cv = mla_project(mla_c, *mla_w, None)
        attend_all = functools.partial(
            mla_attend,
            k_nope=jnp.concatenate([ak_n, ck_n], axis=1),
            k_rope=jnp.concatenate([ak_r, ck_r], axis=1),
            v=jnp.concatenate([av, cv], axis=1))
        out_a = map_query_blocks(attend_all, (aq_n, aq_r)).reshape(b, n, MLA_HEADS * MLA_V)

        out_b = pool_mixer(pool_x, pool_w[l], pool_scale[l])

        sq, sk, sv = swa_project(swa_x, swa_q_norm[l], swa_k_norm[l], ang_swa)
        csq, csk, csv = swa_project(swa_c, swa_q_norm[l], swa_k_norm[l], None)
        out_c = swa_latent(sq, sk, sv, csk, csv, swa_sink[l])

        nq, nkk, nv = na_project(na_x, na_q_norm[l], na_k_norm[l])
        cnq, cnk, cnv = na_project(na_c, na_q_norm[l], na_k_norm[l])
        out_d = na_latent(nq, nkk, nv, cnk, cnv, na_rpb[l])

        if update_ctx:
            mix_c = jnp.concatenate([
                mla_attend(cq_n, cq_r, ck_n, ck_r, cv).reshape(b, n_ctx, MLA_HEADS * MLA_V),
                pool_mixer(pool_c, pool_w[l], pool_scale[l]),
                ctx_attention(csq, csk, csv, swa_sink[l]),
                ctx_attention(cnq, cnk, cnv, None),
            ], axis=-1)
            ctx = ctx + cgt_m * (mix_c @ w_out[l])
            ctx = ctx + cgt_f * conv_ffn(modulate(rms_norm(ctx, g_ffn[l]), csh_f, csc_f),
                                         ffn_w_up[l], ffn_conv_w[l], ffn_conv_b[l], ffn_w_down[l])

        mix_x = jnp.concatenate([out_a, out_b, out_c, out_d], axis=-1)
        x = x + gt_m * (mix_x @ w_out[l])
        x = x + gt_f * conv_ffn(modulate(rms_norm(x, g_ffn[l]), sh_f, sc_f),
                                ffn_w_up[l], ffn_conv_w[l], ffn_conv_b[l], ffn_w_down[l])
    return x


import jax as _jax
import jax.numpy as _jnp

TWIN_FORMAT = 'train_step'
FWD_PARAMS = ['x', 'c', 'ctx', 'c_ctx', 'w_mod', 'b_mod', 'g_mix', 'g_ffn', 'w_in', 'w_out', 'mla_q_a_norm', 'mla_w_qb', 'mla_kv_a_norm', 'mla_w_kvb', 'mla_q_nope_norm', 'mla_q_rope_norm', 'mla_k_nope_norm', 'mla_k_rope_norm', 'pool_w', 'pool_scale', 'swa_q_norm', 'swa_k_norm', 'swa_sink', 'na_q_norm', 'na_k_norm', 'na_rpb', 'ffn_w_up', 'ffn_conv_w', 'ffn_conv_b', 'ffn_w_down']
TWIN_WEIGHTS = ['c_ctx', 'w_mod', 'b_mod', 'g_mix', 'g_ffn', 'w_in', 'w_out', 'mla_q_a_norm', 'mla_w_qb', 'mla_kv_a_norm', 'mla_w_kvb', 'mla_q_nope_norm', 'mla_q_rope_norm', 'mla_k_nope_norm', 'mla_k_rope_norm', 'pool_w', 'pool_scale', 'swa_q_norm', 'swa_k_norm', 'swa_sink', 'na_q_norm', 'na_k_norm', 'na_rpb', 'ffn_w_up', 'ffn_conv_w', 'ffn_conv_b', 'ffn_w_down']
TWIN_DIFF_INPUT = 'x'
TWIN_INPUTS = ['x', 'c', 'ctx', 'c_ctx', 'w_mod', 'b_mod', 'g_mix', 'g_ffn', 'w_in', 'w_out', 'mla_q_a_norm', 'mla_w_qb', 'mla_kv_a_norm', 'mla_w_kvb', 'mla_q_nope_norm', 'mla_q_rope_norm', 'mla_k_nope_norm', 'mla_k_rope_norm', 'pool_w', 'pool_scale', 'swa_q_norm', 'swa_k_norm', 'swa_sink', 'na_q_norm', 'na_k_norm', 'na_rpb', 'ffn_w_up', 'ffn_conv_w', 'ffn_conv_b', 'ffn_w_down', 'loss_target', 'm_c_ctx', 'm_w_mod', 'm_b_mod', 'm_g_mix', 'm_g_ffn', 'm_w_in', 'm_w_out', 'm_mla_q_a_norm', 'm_mla_w_qb', 'm_mla_kv_a_norm', 'm_mla_w_kvb', 'm_mla_q_nope_norm', 'm_mla_q_rope_norm', 'm_mla_k_nope_norm', 'm_mla_k_rope_norm', 'm_pool_w', 'm_pool_scale', 'm_swa_q_norm', 'm_swa_k_norm', 'm_swa_sink', 'm_na_q_norm', 'm_na_k_norm', 'm_na_rpb', 'm_ffn_w_up', 'm_ffn_conv_w', 'm_ffn_conv_b', 'm_ffn_w_down', 'v_c_ctx', 'v_w_mod', 'v_b_mod', 'v_g_mix', 'v_g_ffn', 'v_w_in', 'v_w_out', 'v_mla_q_a_norm', 'v_mla_w_qb', 'v_mla_kv_a_norm', 'v_mla_w_kvb', 'v_mla_q_nope_norm', 'v_mla_q_rope_norm', 'v_mla_k_nope_norm', 'v_mla_k_rope_norm', 'v_pool_w', 'v_pool_scale', 'v_swa_q_norm', 'v_swa_k_norm', 'v_swa_sink', 'v_na_q_norm', 'v_na_k_norm', 'v_na_rpb', 'v_ffn_w_up', 'v_ffn_conv_w', 'v_ffn_conv_b', 'v_ffn_w_down']
TWIN_OUTPUTS = ['loss', 'grad_x', 'grad_c_ctx', 'grad_w_mod', 'grad_b_mod', 'grad_g_mix', 'grad_g_ffn', 'grad_w_in', 'grad_w_out', 'grad_mla_q_a_norm', 'grad_mla_w_qb', 'grad_mla_kv_a_norm', 'grad_mla_w_kvb', 'grad_mla_q_nope_norm', 'grad_mla_q_rope_norm', 'grad_mla_k_nope_norm', 'grad_mla_k_rope_norm', 'grad_pool_w', 'grad_pool_scale', 'grad_swa_q_norm', 'grad_swa_k_norm', 'grad_swa_sink', 'grad_na_q_norm', 'grad_na_k_norm', 'grad_na_rpb', 'grad_ffn_w_up', 'grad_ffn_conv_w', 'grad_ffn_conv_b', 'grad_ffn_w_down', 'delta_c_ctx', 'delta_w_mod', 'delta_b_mod', 'delta_g_mix', 'delta_g_ffn', 'delta_w_in', 'delta_w_out', 'delta_mla_q_a_norm', 'delta_mla_w_qb', 'delta_mla_kv_a_norm', 'delta_mla_w_kvb', 'delta_mla_q_nope_norm', 'delta_mla_q_rope_norm', 'delta_mla_k_nope_norm', 'delta_mla_k_rope_norm', 'delta_pool_w', 'delta_pool_scale', 'delta_swa_q_norm', 'delta_swa_k_norm', 'delta_swa_sink', 'delta_na_q_norm', 'delta_na_k_norm', 'delta_na_rpb', 'delta_ffn_w_up', 'delta_ffn_conv_w', 'delta_ffn_conv_b', 'delta_ffn_w_down', 'new_m_c_ctx', 'new_m_w_mod', 'new_m_b_mod', 'new_m_g_mix', 'new_m_g_ffn', 'new_m_w_in', 'new_m_w_out', 'new_m_mla_q_a_norm', 'new_m_mla_w_qb', 'new_m_mla_kv_a_norm', 'new_m_mla_w_kvb', 'new_m_mla_q_nope_norm', 'new_m_mla_q_rope_norm', 'new_m_mla_k_nope_norm', 'new_m_mla_k_rope_norm', 'new_m_pool_w', 'new_m_pool_scale', 'new_m_swa_q_norm', 'new_m_swa_k_norm', 'new_m_swa_sink', 'new_m_na_q_norm', 'new_m_na_k_norm', 'new_m_na_rpb', 'new_m_ffn_w_up', 'new_m_ffn_conv_w', 'new_m_ffn_conv_b', 'new_m_ffn_w_down', 'new_v_c_ctx', 'new_v_w_mod', 'new_v_b_mod', 'new_v_g_mix', 'new_v_g_ffn', 'new_v_w_in', 'new_v_w_out', 'new_v_mla_q_a_norm', 'new_v_mla_w_qb', 'new_v_mla_kv_a_norm', 'new_v_mla_w_kvb', 'new_v_mla_q_nope_norm', 'new_v_mla_q_rope_norm', 'new_v_mla_k_nope_norm', 'new_v_mla_k_rope_norm', 'new_v_pool_w', 'new_v_pool_scale', 'new_v_swa_q_norm', 'new_v_swa_k_norm', 'new_v_swa_sink', 'new_v_na_q_norm', 'new_v_na_k_norm', 'new_v_na_rpb', 'new_v_ffn_w_up', 'new_v_ffn_conv_w', 'new_v_ffn_conv_b', 'new_v_ffn_w_down']
TWIN_LEAF_KINDS = {'loss': 'loss', 'grad_x': 'grad_x', 'grad_c_ctx': 'grad_w', 'grad_w_mod': 'grad_w', 'grad_b_mod': 'grad_w', 'grad_g_mix': 'grad_w', 'grad_g_ffn': 'grad_w', 'grad_w_in': 'grad_w', 'grad_w_out': 'grad_w', 'grad_mla_q_a_norm': 'grad_w', 'grad_mla_w_qb': 'grad_w', 'grad_mla_kv_a_norm': 'grad_w', 'grad_mla_w_kvb': 'grad_w', 'grad_mla_q_nope_norm': 'grad_w', 'grad_mla_q_rope_norm': 'grad_w', 'grad_mla_k_nope_norm': 'grad_w', 'grad_mla_k_rope_norm': 'grad_w', 'grad_pool_w': 'grad_w', 'grad_pool_scale': 'grad_w', 'grad_swa_q_norm': 'grad_w', 'grad_swa_k_norm': 'grad_w', 'grad_swa_sink': 'grad_w', 'grad_na_q_norm': 'grad_w', 'grad_na_k_norm': 'grad_w', 'grad_na_rpb': 'grad_w', 'grad_ffn_w_up': 'grad_w', 'grad_ffn_conv_w': 'grad_w', 'grad_ffn_conv_b': 'grad_w', 'grad_ffn_w_down': 'grad_w', 'delta_c_ctx': 'delta_w', 'delta_w_mod': 'delta_w', 'delta_b_mod': 'delta_w', 'delta_g_mix': 'delta_w', 'delta_g_ffn': 'delta_w', 'delta_w_in': 'delta_w', 'delta_w_out': 'delta_w', 'delta_mla_q_a_norm': 'delta_w', 'delta_mla_w_qb': 'delta_w', 'delta_mla_kv_a_norm': 'delta_w', 'delta_mla_w_kvb': 'delta_w', 'delta_mla_q_nope_norm': 'delta_w', 'delta_mla_q_rope_norm': 'delta_w', 'delta_mla_k_nope_norm': 'delta_w', 'delta_mla_k_rope_norm': 'delta_w', 'delta_pool_w': 'delta_w', 'delta_pool_scale': 'delta_w', 'delta_swa_q_norm': 'delta_w', 'delta_swa_k_norm': 'delta_w', 'delta_swa_sink': 'delta_w', 'delta_na_q_norm': 'delta_w', 'delta_na_k_norm': 'delta_w', 'delta_na_rpb': 'delta_w', 'delta_ffn_w_up': 'delta_w', 'delta_ffn_conv_w': 'delta_w', 'delta_ffn_conv_b': 'delta_w', 'delta_ffn_w_down': 'delta_w', 'new_m_c_ctx': 'new_m', 'new_m_w_mod': 'new_m', 'new_m_b_mod': 'new_m', 'new_m_g_mix': 'new_m', 'new_m_g_ffn': 'new_m', 'new_m_w_in': 'new_m', 'new_m_w_out': 'new_m', 'new_m_mla_q_a_norm': 'new_m', 'new_m_mla_w_qb': 'new_m', 'new_m_mla_kv_a_norm': 'new_m', 'new_m_mla_w_kvb': 'new_m', 'new_m_mla_q_nope_norm': 'new_m', 'new_m_mla_q_rope_norm': 'new_m', 'new_m_mla_k_nope_norm': 'new_m', 'new_m_mla_k_rope_norm': 'new_m', 'new_m_pool_w': 'new_m', 'new_m_pool_scale': 'new_m', 'new_m_swa_q_norm': 'new_m', 'new_m_swa_k_norm': 'new_m', 'new_m_swa_sink': 'new_m', 'new_m_na_q_norm': 'new_m', 'new_m_na_k_norm': 'new_m', 'new_m_na_rpb': 'new_m', 'new_m_ffn_w_up': 'new_m', 'new_m_ffn_conv_w': 'new_m', 'new_m_ffn_conv_b': 'new_m', 'new_m_ffn_w_down': 'new_m', 'new_v_c_ctx': 'new_v', 'new_v_w_mod': 'new_v', 'new_v_b_mod': 'new_v', 'new_v_g_mix': 'new_v', 'new_v_g_ffn': 'new_v', 'new_v_w_in': 'new_v', 'new_v_w_out': 'new_v', 'new_v_mla_q_a_norm': 'new_v', 'new_v_mla_w_qb': 'new_v', 'new_v_mla_kv_a_norm': 'new_v', 'new_v_mla_w_kvb': 'new_v', 'new_v_mla_q_nope_norm': 'new_v', 'new_v_mla_q_rope_norm': 'new_v', 'new_v_mla_k_nope_norm': 'new_v', 'new_v_mla_k_rope_norm': 'new_v', 'new_v_pool_w': 'new_v', 'new_v_pool_scale': 'new_v', 'new_v_swa_q_norm': 'new_v', 'new_v_swa_k_norm': 'new_v', 'new_v_swa_sink': 'new_v', 'new_v_na_q_norm': 'new_v', 'new_v_na_k_norm': 'new_v', 'new_v_na_rpb': 'new_v', 'new_v_ffn_w_up': 'new_v', 'new_v_ffn_conv_w': 'new_v', 'new_v_ffn_conv_b': 'new_v', 'new_v_ffn_w_down': 'new_v'}


def _forward(args):
    return _fwd_reference(*[args[k] for k in FWD_PARAMS])


def _output_shape():
    def fwd():
        inp = _fwd_setup_inputs(0)
        return _fwd_reference(*[inp[k] for k in FWD_PARAMS])
    out = _jax.eval_shape(fwd)
    return out.shape, out.dtype

N_MICROBATCH = 1
ADAM_LR = 0.001
ADAM_B1 = 0.9
ADAM_B2 = 0.999
ADAM_EPS = 1e-08
ADAM_WD = 0.01
ADAM_STEP = 10
PER_EXAMPLE_BATCH_AXIS = {'x': 0, 'c': 0, 'ctx': 0, 'loss_target': 0}
SHARED_INPUTS = []
_WEIGHT_DTYPES = {'c_ctx': _jnp.float32, 'w_mod': _jnp.float32, 'b_mod': _jnp.float32, 'g_mix': _jnp.float32, 'g_ffn': _jnp.float32, 'w_in': _jnp.float32, 'w_out': _jnp.float32, 'mla_q_a_norm': _jnp.float32, 'mla_w_qb': _jnp.float32, 'mla_kv_a_norm': _jnp.float32, 'mla_w_kvb': _jnp.float32, 'mla_q_nope_norm': _jnp.float32, 'mla_q_rope_norm': _jnp.float32, 'mla_k_nope_norm': _jnp.float32, 'mla_k_rope_norm': _jnp.float32, 'pool_w': _jnp.float32, 'pool_scale': _jnp.float32, 'swa_q_norm': _jnp.float32, 'swa_k_norm': _jnp.float32, 'swa_sink': _jnp.float32, 'na_q_norm': _jnp.float32, 'na_k_norm': _jnp.float32, 'na_rpb': _jnp.float32, 'ffn_w_up': _jnp.float32, 'ffn_conv_w': _jnp.float32, 'ffn_conv_b': _jnp.float32, 'ffn_w_down': _jnp.float32}
MOMENT_SCALE = {'c_ctx': 1.745626e-01, 'w_mod': 3.335011e-01, 'b_mod': 8.995278e-01, 'g_mix': 2.997351e-01, 'g_ffn': 1.600779e+00, 'w_in': 6.539519e-02, 'w_out': 6.915782e-02, 'mla_q_a_norm': 6.111653e-03, 'mla_w_qb': 5.282066e-03, 'mla_kv_a_norm': 2.299244e-01, 'mla_w_kvb': 5.850505e-02, 'mla_q_nope_norm': 1.794397e-02, 'mla_q_rope_norm': 8.204224e-03, 'mla_k_nope_norm': 1.796868e-02, 'mla_k_rope_norm': 8.236867e-03, 'pool_w': 1.131965e-01, 'pool_scale': 1.264862e+00, 'swa_q_norm': 7.270521e-02, 'swa_k_norm': 7.232797e-02, 'swa_sink': 6.386459e-03, 'na_q_norm': 9.705978e-02, 'na_k_norm': 9.589195e-02, 'na_rpb': 1.550092e-03, 'ffn_w_up': 4.540339e-02, 'ffn_conv_w': 2.416812e-01, 'ffn_conv_b': 2.016557e-01, 'ffn_w_down': 4.196733e-02}


def _to_microbatches(a, axis):
    t = _jnp.moveaxis(a, axis, 0)
    t = t.reshape((N_MICROBATCH, t.shape[0] // N_MICROBATCH) + t.shape[1:])
    return _jnp.moveaxis(t, 1, axis + 1)


def setup_inputs(seed: int = 0) -> dict:
    inp = _fwd_setup_inputs(seed)
    key = _jax.random.fold_in(_jax.random.key(seed), 7919)
    shape, _ = _output_shape()
    out = dict(inp)
    out["loss_target"] = _jax.random.normal(_jax.random.fold_in(key, 0), shape, _jnp.float32)
    for i, name in enumerate(TWIN_WEIGHTS):
        w = inp[name].astype(_jnp.float32)
        if MOMENT_SCALE is None:
            s = _jnp.sqrt(_jnp.mean(_jnp.square(w)) + 1e-30)
        else:
            s = MOMENT_SCALE[name]
        km, kv = _jax.random.split(_jax.random.fold_in(key, i + 1))
        out[name] = w
        out["m_" + name] = s * _jax.random.normal(km, w.shape, _jnp.float32)
        out["v_" + name] = (s * s) * _jax.random.uniform(kv, w.shape, _jnp.float32, 0.5, 1.5)
    if N_MICROBATCH > 1:
        for name, axis in PER_EXAMPLE_BATCH_AXIS.items():
            out[name] = _to_microbatches(out[name], axis)
    return {'x': out['x'], 'c': out['c'], 'ctx': out['ctx'], 'c_ctx': out['c_ctx'], 'w_mod': out['w_mod'], 'b_mod': out['b_mod'], 'g_mix': out['g_mix'], 'g_ffn': out['g_ffn'], 'w_in': out['w_in'], 'w_out': out['w_out'], 'mla_q_a_norm': out['mla_q_a_norm'], 'mla_w_qb': out['mla_w_qb'], 'mla_kv_a_norm': out['mla_kv_a_norm'], 'mla_w_kvb': out['mla_w_kvb'], 'mla_q_nope_norm': out['mla_q_nope_norm'], 'mla_q_rope_norm': out['mla_q_rope_norm'], 'mla_k_nope_norm': out['mla_k_nope_norm'], 'mla_k_rope_norm': out['mla_k_rope_norm'], 'pool_w': out['pool_w'], 'pool_scale': out['pool_scale'], 'swa_q_norm': out['swa_q_norm'], 'swa_k_norm': out['swa_k_norm'], 'swa_sink': out['swa_sink'], 'na_q_norm': out['na_q_norm'], 'na_k_norm': out['na_k_norm'], 'na_rpb': out['na_rpb'], 'ffn_w_up': out['ffn_w_up'], 'ffn_conv_w': out['ffn_conv_w'], 'ffn_conv_b': out['ffn_conv_b'], 'ffn_w_down': out['ffn_w_down'], 'loss_target': out['loss_target'], 'm_c_ctx': out['m_c_ctx'], 'm_w_mod': out['m_w_mod'], 'm_b_mod': out['m_b_mod'], 'm_g_mix': out['m_g_mix'], 'm_g_ffn': out['m_g_ffn'], 'm_w_in': out['m_w_in'], 'm_w_out': out['m_w_out'], 'm_mla_q_a_norm': out['m_mla_q_a_norm'], 'm_mla_w_qb': out['m_mla_w_qb'], 'm_mla_kv_a_norm': out['m_mla_kv_a_norm'], 'm_mla_w_kvb': out['m_mla_w_kvb'], 'm_mla_q_nope_norm': out['m_mla_q_nope_norm'], 'm_mla_q_rope_norm': out['m_mla_q_rope_norm'], 'm_mla_k_nope_norm': out['m_mla_k_nope_norm'], 'm_mla_k_rope_norm': out['m_mla_k_rope_norm'], 'm_pool_w': out['m_pool_w'], 'm_pool_scale': out['m_pool_scale'], 'm_swa_q_norm': out['m_swa_q_norm'], 'm_swa_k_norm': out['m_swa_k_norm'], 'm_swa_sink': out['m_swa_sink'], 'm_na_q_norm': out['m_na_q_norm'], 'm_na_k_norm': out['m_na_k_norm'], 'm_na_rpb': out['m_na_rpb'], 'm_ffn_w_up': out['m_ffn_w_up'], 'm_ffn_conv_w': out['m_ffn_conv_w'], 'm_ffn_conv_b': out['m_ffn_conv_b'], 'm_ffn_w_down': out['m_ffn_w_down'], 'v_c_ctx': out['v_c_ctx'], 'v_w_mod': out['v_w_mod'], 'v_b_mod': out['v_b_mod'], 'v_g_mix': out['v_g_mix'], 'v_g_ffn': out['v_g_ffn'], 'v_w_in': out['v_w_in'], 'v_w_out': out['v_w_out'], 'v_mla_q_a_norm': out['v_mla_q_a_norm'], 'v_mla_w_qb': out['v_mla_w_qb'], 'v_mla_kv_a_norm': out['v_mla_kv_a_norm'], 'v_mla_w_kvb': out['v_mla_w_kvb'], 'v_mla_q_nope_norm': out['v_mla_q_nope_norm'], 'v_mla_q_rope_norm': out['v_mla_q_rope_norm'], 'v_mla_k_nope_norm': out['v_mla_k_nope_norm'], 'v_mla_k_rope_norm': out['v_mla_k_rope_norm'], 'v_pool_w': out['v_pool_w'], 'v_pool_scale': out['v_pool_scale'], 'v_swa_q_norm': out['v_swa_q_norm'], 'v_swa_k_norm': out['v_swa_k_norm'], 'v_swa_sink': out['v_swa_sink'], 'v_na_q_norm': out['v_na_q_norm'], 'v_na_k_norm': out['v_na_k_norm'], 'v_na_rpb': out['v_na_rpb'], 'v_ffn_w_up': out['v_ffn_w_up'], 'v_ffn_conv_w': out['v_ffn_conv_w'], 'v_ffn_conv_b': out['v_ffn_conv_b'], 'v_ffn_w_down': out['v_ffn_w_down']}


def _loss(weights, diff, rest, loss_target):
    with _jax.named_scope("forward"):
        args = {**rest, TWIN_DIFF_INPUT: diff, **{k: w.astype(_WEIGHT_DTYPES[k]) for k, w in weights.items()}}
        y = _forward(args)
    with _jax.named_scope("loss_head"):
        err = _jnp.square(y.astype(_jnp.float32) - loss_target)
        return 0.5 * _jnp.sum(_jnp.mean(err, axis=-1)) if err.ndim else 0.5 * err


def _adamw(w, g, m, v):
    m = ADAM_B1 * m + (1.0 - ADAM_B1) * g
    v = ADAM_B2 * v + (1.0 - ADAM_B2) * _jnp.square(g)
    m_hat = m / (1.0 - ADAM_B1 ** ADAM_STEP)
    v_hat = v / (1.0 - ADAM_B2 ** ADAM_STEP)
    delta = -ADAM_LR * (m_hat / (_jnp.sqrt(v_hat) + ADAM_EPS) + ADAM_WD * w)
    return delta, m, v


def reference(x, c, ctx, c_ctx, w_mod, b_mod, g_mix, g_ffn, w_in, w_out, mla_q_a_norm, mla_w_qb, mla_kv_a_norm, mla_w_kvb, mla_q_nope_norm, mla_q_rope_norm, mla_k_nope_norm, mla_k_rope_norm, pool_w, pool_scale, swa_q_norm, swa_k_norm, swa_sink, na_q_norm, na_k_norm, na_rpb, ffn_w_up, ffn_conv_w, ffn_conv_b, ffn_w_down, loss_target, m_c_ctx, m_w_mod, m_b_mod, m_g_mix, m_g_ffn, m_w_in, m_w_out, m_mla_q_a_norm, m_mla_w_qb, m_mla_kv_a_norm, m_mla_w_kvb, m_mla_q_nope_norm, m_mla_q_rope_norm, m_mla_k_nope_norm, m_mla_k_rope_norm, m_pool_w, m_pool_scale, m_swa_q_norm, m_swa_k_norm, m_swa_sink, m_na_q_norm, m_na_k_norm, m_na_rpb, m_ffn_w_up, m_ffn_conv_w, m_ffn_conv_b, m_ffn_w_down, v_c_ctx, v_w_mod, v_b_mod, v_g_mix, v_g_ffn, v_w_in, v_w_out, v_mla_q_a_norm, v_mla_w_qb, v_mla_kv_a_norm, v_mla_w_kvb, v_mla_q_nope_norm, v_mla_q_rope_norm, v_mla_k_nope_norm, v_mla_k_rope_norm, v_pool_w, v_pool_scale, v_swa_q_norm, v_swa_k_norm, v_swa_sink, v_na_q_norm, v_na_k_norm, v_na_rpb, v_ffn_w_up, v_ffn_conv_w, v_ffn_conv_b, v_ffn_w_down):
    given = dict(x=x, c=c, ctx=ctx, c_ctx=c_ctx, w_mod=w_mod, b_mod=b_mod, g_mix=g_mix, g_ffn=g_ffn, w_in=w_in, w_out=w_out, mla_q_a_norm=mla_q_a_norm, mla_w_qb=mla_w_qb, mla_kv_a_norm=mla_kv_a_norm, mla_w_kvb=mla_w_kvb, mla_q_nope_norm=mla_q_nope_norm, mla_q_rope_norm=mla_q_rope_norm, mla_k_nope_norm=mla_k_nope_norm, mla_k_rope_norm=mla_k_rope_norm, pool_w=pool_w, pool_scale=pool_scale, swa_q_norm=swa_q_norm, swa_k_norm=swa_k_norm, swa_sink=swa_sink, na_q_norm=na_q_norm, na_k_norm=na_k_norm, na_rpb=na_rpb, ffn_w_up=ffn_w_up, ffn_conv_w=ffn_conv_w, ffn_conv_b=ffn_conv_b, ffn_w_down=ffn_w_down, loss_target=loss_target, m_c_ctx=m_c_ctx, m_w_mod=m_w_mod, m_b_mod=m_b_mod, m_g_mix=m_g_mix, m_g_ffn=m_g_ffn, m_w_in=m_w_in, m_w_out=m_w_out, m_mla_q_a_norm=m_mla_q_a_norm, m_mla_w_qb=m_mla_w_qb, m_mla_kv_a_norm=m_mla_kv_a_norm, m_mla_w_kvb=m_mla_w_kvb, m_mla_q_nope_norm=m_mla_q_nope_norm, m_mla_q_rope_norm=m_mla_q_rope_norm, m_mla_k_nope_norm=m_mla_k_nope_norm, m_mla_k_rope_norm=m_mla_k_rope_norm, m_pool_w=m_pool_w, m_pool_scale=m_pool_scale, m_swa_q_norm=m_swa_q_norm, m_swa_k_norm=m_swa_k_norm, m_swa_sink=m_swa_sink, m_na_q_norm=m_na_q_norm, m_na_k_norm=m_na_k_norm, m_na_rpb=m_na_rpb, m_ffn_w_up=m_ffn_w_up, m_ffn_conv_w=m_ffn_conv_w, m_ffn_conv_b=m_ffn_conv_b, m_ffn_w_down=m_ffn_w_down, v_c_ctx=v_c_ctx, v_w_mod=v_w_mod, v_b_mod=v_b_mod, v_g_mix=v_g_mix, v_g_ffn=v_g_ffn, v_w_in=v_w_in, v_w_out=v_w_out, v_mla_q_a_norm=v_mla_q_a_norm, v_mla_w_qb=v_mla_w_qb, v_mla_kv_a_norm=v_mla_kv_a_norm, v_mla_w_kvb=v_mla_w_kvb, v_mla_q_nope_norm=v_mla_q_nope_norm, v_mla_q_rope_norm=v_mla_q_rope_norm, v_mla_k_nope_norm=v_mla_k_nope_norm, v_mla_k_rope_norm=v_mla_k_rope_norm, v_pool_w=v_pool_w, v_pool_scale=v_pool_scale, v_swa_q_norm=v_swa_q_norm, v_swa_k_norm=v_swa_k_norm, v_swa_sink=v_swa_sink, v_na_q_norm=v_na_q_norm, v_na_k_norm=v_na_k_norm, v_na_rpb=v_na_rpb, v_ffn_w_up=v_ffn_w_up, v_ffn_conv_w=v_ffn_conv_w, v_ffn_conv_b=v_ffn_conv_b, v_ffn_w_down=v_ffn_w_down)
    weights = {n: given[n] for n in TWIN_WEIGHTS}
    shared = {n: given[n] for n in SHARED_INPUTS}
    per_example = {n: given[n] for n in ['x', 'c', 'ctx']}
    grad_fn = _jax.value_and_grad(_loss, argnums=(0, 1))

    def one_microbatch(ex, loss_target):
        ex = dict(ex)
        diff = ex.pop(TWIN_DIFF_INPUT)
        return grad_fn(weights, diff, {**shared, **ex}, loss_target)

    if N_MICROBATCH == 1:
        loss, (grad_w, grad_x) = one_microbatch(per_example, given["loss_target"])
    else:
        def body(carry, xs):
            loss_sum, grad_sum = carry
            l_k, (gw_k, gx_k) = one_microbatch(xs[0], xs[1])
            with _jax.named_scope("update"):
                return (loss_sum + l_k, _jax.tree.map(_jnp.add, grad_sum, gw_k)), gx_k

        init = (_jnp.zeros((), _jnp.float32), _jax.tree.map(_jnp.zeros_like, weights))
        (loss, grad_w), grad_x = _jax.lax.scan(body, init, (per_example, given["loss_target"]))
    with _jax.named_scope("update"):
        delta_w, new_m, new_v = {}, {}, {}
        for n in TWIN_WEIGHTS:
            delta_w[n], new_m[n], new_v[n] = _adamw(weights[n], grad_w[n], given["m_" + n], given["v_" + n])
    return (loss, grad_x, *[grad_w[n] for n in TWIN_WEIGHTS], *[delta_w[n] for n in TWIN_WEIGHTS],
            *[new_m[n] for n in TWIN_WEIGHTS], *[new_v[n] for n in TWIN_WEIGHTS])
```

```python
import functools

import jax
import jax.numpy as jnp
from jax import lax
from jax.experimental import pallas as pl
from jax.experimental.pallas import tpu as pltpu

F32 = jnp.float32
BF16 = jnp.bfloat16
I32 = jnp.int32

DEPTH = 4
GRID_W = 64
ROPE_BASE = 10000.0
EPS = 1e-6
NEG = -1e30
MLA_SCALE = 192.0 ** -0.5
HD_SCALE = 64.0 ** -0.5
NA_KROWS = 12
SWA_KEYS = 512
P_COLS = 3648
PW = 3840
TM = 256
HALO = 8
ADAM_LR, ADAM_B1, ADAM_B2, ADAM_EPS, ADAM_WD, ADAM_STEP = 0.001, 0.9, 0.999, 1e-08, 0.01, 10
VMEM_LIMIT = 56 * 1024 * 1024
MESH = pl.DeviceIdType.MESH


def _pick(n, cands):
    for c in cands:
        if n % c == 0:
            return c
    return n


def _params(sem=None):
    return pltpu.CompilerParams(dimension_semantics=sem, vmem_limit_bytes=VMEM_LIMIT)


@jax.custom_vjp
def _bdot(a, b):
    return jnp.dot(a.astype(BF16), b.astype(BF16), preferred_element_type=F32)


def _bdot_fwd(a, b):
    return _bdot(a, b), (a.astype(BF16), b.astype(BF16))


def _bdot_bwd(res, g):
    a, b = res
    gb = g.astype(BF16)
    da = lax.dot_general(gb, b, (((1,), (1,)), ((), ())), preferred_element_type=F32)
    db = lax.dot_general(a, gb, (((0,), (0,)), ((), ())), preferred_element_type=F32)
    return da, db


_bdot.defvjp(_bdot_fwd, _bdot_bwd)


@jax.custom_vjp
def _bdot_nt(a, b):
    return lax.dot_general(a.astype(BF16), b.astype(BF16), (((1,), (1,)), ((), ())), preferred_element_type=F32)


def _bdot_nt_fwd(a, b):
    return _bdot_nt(a, b), (a.astype(BF16), b.astype(BF16))


def _bdot_nt_bwd(res, g):
    a, b = res
    gb = g.astype(BF16)
    da = jnp.dot(gb, b, preferred_element_type=F32)
    db = lax.dot_general(gb, a, (((0,), (0,)), ((), ())), preferred_element_type=F32)
    return da, db


_bdot_nt.defvjp(_bdot_nt_fwd, _bdot_nt_bwd)


def _rms(x, g):
    return x * lax.rsqrt(jnp.mean(x * x, axis=-1, keepdims=True) + EPS) * g


def _rope(x, cos, sin):
    xr = jnp.concatenate([-x[:, 16:32], x[:, 0:16], -x[:, 48:64], x[:, 32:48]], axis=-1)
    return x * cos + xr * sin


def _sel(is_ctx, mod, row):
    return jnp.where(is_ctx, mod[1, row:row + 1, :], mod[0, row:row + 1, :])


def _mm(a, b, mode, out_dtype, name, a_split=False, b_split=False, o_split=False):
    def dims(x, split):
        return (x.shape[1], 2 * x.shape[2]) if split else x.shape

    ar, ac = dims(a, a_split)
    br, bc = dims(b, b_split)
    if mode == "nn":
        m, k, n = ar, ac, bc
        assert br == k
    elif mode == "nt":
        m, k, n = ar, ac, br
        assert bc == k
    else:
        k, m, n = ar, ac, bc
        assert br == k
    rows = (1088, 1024, 640, 512, 256, 128, 64, 32, 16)
    cols = (1280, 1024, 768, 512, 384, 256, 128)
    tm = _pick(m, rows if mode != "tn" else cols)
    tn = _pick(n // 2 if (o_split or (b_split and mode != "nt")) else n, cols)
    if mode == "tn":
        tk = _pick(k, rows)
    else:
        tk = _pick(k // 2 if (a_split or (b_split and mode == "nt")) else k, cols)
    nk = k // tk

    def spec(split, tr, tc, ncols, ridx, cidx):
        if not split:
            return pl.BlockSpec((tr, tc), lambda i, j, kk: (ridx(i, j, kk), cidx(i, j, kk)))
        nh = (ncols // 2) // tc
        return pl.BlockSpec((None, tr, tc), lambda i, j, kk: (cidx(i, j, kk) // nh, ridx(i, j, kk), cidx(i, j, kk) % nh))

    gi = lambda i, j, kk: i
    gj = lambda i, j, kk: j
    gk = lambda i, j, kk: kk
    if mode == "tn":
        a_spec = spec(a_split, tk, tm, ac, gk, gi)
    else:
        a_spec = spec(a_split, tm, tk, ac, gi, gk)
    if mode == "nt":
        b_spec = spec(b_split, tn, tk, bc, gj, gk)
    else:
        b_spec = spec(b_split, tk, tn, bc, gk, gj)
    o_spec = spec(o_split, tm, tn, n, gi, gj)
    dn = {"nn": (((1,), (0,)), ((), ())), "nt": (((1,), (1,)), ((), ())), "tn": (((0,), (0,)), ((), ()))}[mode]

    def body(a_ref, b_ref, o_ref, acc_ref):
        kk = pl.program_id(2)

        @pl.when(kk == 0)
        def _():
            acc_ref[...] = jnp.zeros_like(acc_ref)

        acc_ref[...] += lax.dot_general(a_ref[...], b_ref[...], dn, preferred_element_type=F32)

        @pl.when(kk == nk - 1)
        def _():
            o_ref[...] = acc_ref[...].astype(o_ref.dtype)

    oshape = (2, m, n // 2) if o_split else (m, n)
    return pl.pallas_call(
        body, name=name, grid=(m // tm, n // tn, nk), in_specs=[a_spec, b_spec], out_specs=o_spec,
        out_shape=jax.ShapeDtypeStruct(oshape, out_dtype), scratch_shapes=[pltpu.VMEM((tm, tn), F32)],
        compiler_params=_params(("parallel", "parallel", "arbitrary")))(a, b)


def _mm_exact(a, b, name):
    def body(a_ref, b_ref, o_ref):
        o_ref[...] = jnp.dot(a_ref[...], b_ref[...], preferred_element_type=F32, precision=lax.Precision.HIGHEST)

    return pl.pallas_call(body, name=name, out_shape=jax.ShapeDtypeStruct((a.shape[0], b.shape[1]), F32),
                          compiler_params=_params())(a, b)


def _row_spec(width, col=0):
    return pl.BlockSpec((TM, width), lambda i: (i, col))


def _full_spec(shape):
    nd = len(shape)
    return pl.BlockSpec(shape, lambda *_: (0,) * nd)


def _normmod_fn(x, g, mod, is_ctx, row):
    return _rms(x, g) * (1.0 + _sel(is_ctx, mod, row + 1)) + _sel(is_ctx, mod, row)


def _normmod_fwd(x, g, mod, row, nbl, name):
    r, d = x.shape

    def body(x_ref, g_ref, mod_ref, h_ref):
        is_ctx = pl.program_id(0) >= nbl
        h_ref[...] = _normmod_fn(x_ref[...], g_ref[...], mod_ref[...], is_ctx, row).astype(BF16)

    return pl.pallas_call(
        body, name=name, grid=(r // TM,), in_specs=[_row_spec(d), _full_spec((1, d)), _full_spec(mod.shape)],
        out_specs=_row_spec(d), out_shape=jax.ShapeDtypeStruct((r, d), BF16), compiler_params=_params(("parallel",)))(x, g, mod)


def _normmod_bwd(x, g, mod, dh, dx_in, row, nbl, name):
    r, d = x.shape

    def body(x_ref, g_ref, mod_ref, dh_ref, dxin_ref, dx_ref, dg_ref, dmod_ref):
        i = pl.program_id(0)
        is_ctx = i >= nbl

        @pl.when(i == 0)
        def _():
            dg_ref[...] = jnp.zeros_like(dg_ref)
            dmod_ref[...] = jnp.zeros_like(dmod_ref)

        _, vjp = jax.vjp(lambda xx, gg, mm: _normmod_fn(xx, gg, mm, is_ctx, row), x_ref[...], g_ref[...], mod_ref[...])
        dx, dg, dmod = vjp(dh_ref[...])
        dx_ref[...] = dxin_ref[...] + dx
        dg_ref[...] += dg
        dmod_ref[...] += dmod

    return pl.pallas_call(
        body, name=name, grid=(r // TM,),
        in_specs=[_row_spec(d), _full_spec((1, d)), _full_spec(mod.shape), _row_spec(d), _row_spec(d)],
        out_specs=[_row_spec(d), _full_spec((1, d)), _full_spec(mod.shape)],
        out_shape=[jax.ShapeDtypeStruct((r, d), F32), jax.ShapeDtypeStruct((1, d), F32), jax.ShapeDtypeStruct(mod.shape, F32)],
        compiler_params=_params(("arbitrary",)))(x, g, mod, dh, dx_in)


def _resid_fwd(x, y, mod, row, nbl, name):
    r, d = x.shape

    def body(x_ref, y_ref, mod_ref, o_ref):
        is_ctx = pl.program_id(0) >= nbl
        o_ref[...] = x_ref[...] + _sel(is_ctx, mod_ref[...], row) * y_ref[...]

    return pl.pallas_call(
        body, name=name, grid=(r // TM,), in_specs=[_row_spec(d), _row_spec(d), _full_spec(mod.shape)],
        out_specs=_row_spec(d), out_shape=jax.ShapeDtypeStruct((r, d), F32), compiler_params=_params(("parallel",)))(x, y, mod)


def _resid_bwd(dx, y, mod, row, nbl, name):
    r, d = dx.shape

    def body(dx_ref, y_ref, mod_ref, dy_ref, dmod_ref):
        i = pl.program_id(0)
        is_ctx = i >= nbl

        @pl.when(i == 0)
        def _():
            dmod_ref[...] = jnp.zeros_like(dmod_ref)

        dxv = dx_ref[...]
        dy_ref[...] = (_sel(is_ctx, mod_ref[...], row) * dxv).astype(BF16)
        dgate = jnp.sum(dxv * y_ref[...], axis=0, keepdims=True)

        @pl.when(is_ctx)
        def _():
            dmod_ref[1, row:row + 1, :] += dgate

        @pl.when(jnp.logical_not(is_ctx))
        def _():
            dmod_ref[0, row:row + 1, :] += dgate

    return pl.pallas_call(
        body, name=name, grid=(r // TM,), in_specs=[_row_spec(d), _row_spec(d), _full_spec(mod.shape)],
        out_specs=[_row_spec(d), _full_spec(mod.shape)],
        out_shape=[jax.ShapeDtypeStruct((r, d), BF16), jax.ShapeDtypeStruct(mod.shape, F32)],
        compiler_params=_params(("arbitrary",)))(dx, y, mod)


def _loss_kernel(x, target, nbl, name):
    r, d = x.shape

    def body(x_ref, t_ref, loss_ref, dx_ref):
        i = pl.program_id(0)

        @pl.when(i == 0)
        def _():
            loss_ref[...] = jnp.zeros_like(loss_ref)

        @pl.when(i < nbl)
        def _():
            e = x_ref[...] - t_ref[...]
            dx_ref[...] = e / d
            loss_ref[...] += 0.5 * jnp.sum(jnp.mean(e * e, axis=-1, keepdims=True), axis=0, keepdims=True)

        @pl.when(i >= nbl)
        def _():
            dx_ref[...] = jnp.zeros_like(dx_ref)

    return pl.pallas_call(
        body, name=name, grid=(r // TM,),
        in_specs=[_row_spec(d), pl.BlockSpec((TM, d), lambda i: (jnp.minimum(i, nbl - 1), 0))],
        out_specs=[_full_spec((1, 1)), _row_spec(d)],
        out_shape=[jax.ShapeDtypeStruct((1, 1), F32), jax.ShapeDtypeStruct((r, d), F32)],
        compiler_params=_params(("arbitrary",)))(x, target)


SP_QA, SP_KVA, SP_QN, SP_QR, SP_KN, SP_KR, SP_SQ, SP_SK, SP_NQ, SP_NK, SP_SINK = range(11)
C_CQ, C_CKV, C_KR, C_POOL, C_SQ, C_SK, C_SV, C_NQ, C_NK, C_NV = 0, 512, 768, 896, 1408, 1920, 2048, 2176, 2688, 3200


def _prep_fn(p, tab, sp, wqb, wkvb):
    cos, sin = tab[:, 0:64], tab[:, 64:128]
    q = _bdot(_rms(p[:, C_CQ:C_CQ + 512], sp[SP_QA:SP_QA + 1, 0:512]), wqb)
    kv = _bdot(_rms(p[:, C_CKV:C_CKV + 256], sp[SP_KVA:SP_KVA + 1, 0:256]), wkvb)
    krr = _rope(_rms(p[:, C_KR:C_KR + 64], sp[SP_KR:SP_KR + 1, 0:64]), cos, sin)
    zero = jnp.zeros_like(krr)
    aq, ak, av = [], [], []
    for h in range(4):
        qn = _rms(q[:, 128 * h:128 * h + 128], sp[SP_QN:SP_QN + 1, 0:128])
        qr = _rope(_rms(q[:, 512 + 64 * h:576 + 64 * h], sp[SP_QR:SP_QR + 1, 0:64]), cos, sin)
        kn = _rms(kv[:, 256 * h:256 * h + 128], sp[SP_KN:SP_KN + 1, 0:128])
        aq += [qn, qr, zero]
        ak += [kn, krr, zero]
        av.append(kv[:, 256 * h + 128:256 * h + 256])
    cq = [_rope(_rms(p[:, C_SQ + 64 * h:C_SQ + 64 * h + 64], sp[SP_SQ:SP_SQ + 1, 0:64]), cos, sin) for h in range(8)]
    ck = [_rope(_rms(p[:, C_SK + 64 * h:C_SK + 64 * h + 64], sp[SP_SK:SP_SK + 1, 0:64]), cos, sin) for h in range(2)]
    dq = [_rms(p[:, C_NQ + 64 * h:C_NQ + 64 * h + 64], sp[SP_NQ:SP_NQ + 1, 0:64]) for h in range(8)]
    dk = [_rms(p[:, C_NK + 64 * h:C_NK + 64 * h + 64], sp[SP_NK:SP_NK + 1, 0:64]) for h in range(8)]
    cat = lambda xs: jnp.concatenate(xs, axis=-1)
    return (cat(aq), cat(ak), cat(av), cat(cq), cat(ck), p[:, C_SV:C_SV + 128], cat(dq), cat(dk), p[:, C_NV:C_NV + 512])


PREP_WIDTHS = (1024, 1024, 512, 512, 128, 128, 512, 512, 512)


def _prep_fwd(p, tab, sp, wqb, wkvb, name):
    r = p.shape[0]

    def body(p_ref, tab_ref, sp_ref, wqb_ref, wkvb_ref, *outs):
        res = _prep_fn(p_ref[...], tab_ref[...], sp_ref[...], wqb_ref[...].astype(F32), wkvb_ref[...].astype(F32))
        for o_ref, v in zip(outs, res):
            o_ref[...] = v.astype(BF16)

    return pl.pallas_call(
        body, name=name, grid=(r // TM,),
        in_specs=[_row_spec(PW), _row_spec(128), _full_spec(sp.shape), _full_spec(wqb.shape), _full_spec(wkvb.shape)],
        out_specs=[_row_spec(w) for w in PREP_WIDTHS],
        out_shape=[jax.ShapeDtypeStruct((r, w), BF16) for w in PREP_WIDTHS],
        compiler_params=_params(("parallel",)))(p, tab, sp, wqb, wkvb)


def _prep_bwd(p, tab, sp, wqb, wkvb, cots, dpool, name):
    r = p.shape[0]

    def body(p_ref, tab_ref, sp_ref, wqb_ref, wkvb_ref, *rest):
        cot_refs, dpool_ref = rest[:9], rest[9]
        dp_ref, dsp_ref, dwqb_ref, dwkvb_ref = rest[10:]
        i = pl.program_id(0)

        @pl.when(i == 0)
        def _():
            dsp_ref[...] = jnp.zeros_like(dsp_ref)
            dwqb_ref[...] = jnp.zeros_like(dwqb_ref)
            dwkvb_ref[...] = jnp.zeros_like(dwkvb_ref)

        tab = tab_ref[...]
        _, vjp = jax.vjp(lambda pp, ss, wq, wk: _prep_fn(pp, tab, ss, wq, wk),
                         p_ref[...], sp_ref[...], wqb_ref[...].astype(F32), wkvb_ref[...].astype(F32))
        dp, dsp, dwq, dwk = vjp(tuple(c[...] for c in cot_refs))
        dp_ref[...] = dp.astype(BF16)
        dp_ref[:, C_POOL:C_POOL + 512] = dpool_ref[...].astype(BF16)
        dsp_ref[...] += dsp
        dwqb_ref[...] += dwq
        dwkvb_ref[...] += dwk

    return pl.pallas_call(
        body, name=name, grid=(r // TM,),
        in_specs=[_row_spec(PW), _row_spec(128), _full_spec(sp.shape), _full_spec(wqb.shape), _full_spec(wkvb.shape)]
        + [_row_spec(w) for w in PREP_WIDTHS] + [_row_spec(512)],
        out_specs=[_row_spec(PW), _full_spec(sp.shape), _full_spec(wqb.shape), _full_spec(wkvb.shape)],
        out_shape=[jax.ShapeDtypeStruct((r, PW), BF16), jax.ShapeDtypeStruct(sp.shape, F32),
                   jax.ShapeDtypeStruct(wqb.shape, F32), jax.ShapeDtypeStruct(wkvb.shape, F32)],
        compiler_params=_params(("arbitrary",)))(p, tab, sp, wqb, wkvb, *cots, dpool)


def _mla_probs(q, k, is_ctx, n_lat):
    s = lax.dot_general(q, k, (((1,), (1,)), ((), ())), preferred_element_type=F32) * MLA_SCALE
    kid = lax.broadcasted_iota(I32, s.shape, 1)
    s = jnp.where(jnp.logical_and(is_ctx, kid < n_lat), NEG, s)
    e = jnp.exp(s - jnp.max(s, axis=-1, keepdims=True))
    return e / jnp.sum(e, axis=-1, keepdims=True)


def _mla_fwd(aq, ak, av, n_lat, name):
    r = aq.shape[0]
    nbl = n_lat // TM

    def body(q_ref, k_ref, v_ref, o_ref):
        p = _mla_probs(q_ref[...], k_ref[...], pl.program_id(1) >= nbl, n_lat)
        o_ref[...] = jnp.dot(p.astype(BF16), v_ref[...], preferred_element_type=F32).astype(BF16)

    return pl.pallas_call(
        body, name=name, grid=(4, r // TM),
        in_specs=[pl.BlockSpec((TM, 256), lambda h, i: (i, h)), pl.BlockSpec((r, 256), lambda h, i: (0, h)),
                  pl.BlockSpec((r, 128), lambda h, i: (0, h))],
        out_specs=pl.BlockSpec((TM, 128), lambda h, i: (i, h)),
        out_shape=jax.ShapeDtypeStruct((r, 512), BF16), compiler_params=_params(("parallel", "parallel")))(aq, ak, av)


def _mla_bwd(aq, ak, av, dmix, n_lat, name):
    r = aq.shape[0]
    nbl = n_lat // TM

    def body(q_ref, k_ref, v_ref, do_ref, dq_ref, dk_ref, dv_ref):
        i = pl.program_id(1)

        @pl.when(i == 0)
        def _():
            dk_ref[...] = jnp.zeros_like(dk_ref)
            dv_ref[...] = jnp.zeros_like(dv_ref)

        q, k, v = q_ref[...], k_ref[...], v_ref[...]
        dob = do_ref[...].astype(BF16)
        p = _mla_probs(q, k, i >= nbl, n_lat)
        dv_ref[...] += lax.dot_general(p.astype(BF16), dob, (((0,), (0,)), ((), ())), preferred_element_type=F32)
        dp = lax.dot_general(dob, v, (((1,), (1,)), ((), ())), preferred_element_type=F32)
        ds = (p * (dp - jnp.sum(dp * p, axis=-1, keepdims=True)) * MLA_SCALE).astype(BF16)
        dq_ref[...] = jnp.dot(ds, k, preferred_element_type=F32)
        dk_ref[...] += lax.dot_general(ds, q, (((0,), (0,)), ((), ())), preferred_element_type=F32)

    return pl.pallas_call(
        body, name=name, grid=(4, r // TM),
        in_specs=[pl.BlockSpec((TM, 256), lambda h, i: (i, h)), pl.BlockSpec((r, 256), lambda h, i: (0, h)),
                  pl.BlockSpec((r, 128), lambda h, i: (0, h)), pl.BlockSpec((TM, 128), lambda h, i: (i, h))],
        out_specs=[pl.BlockSpec((TM, 256), lambda h, i: (i, h)), pl.BlockSpec((r, 256), lambda h, i: (0, h)),
                   pl.BlockSpec((r, 128), lambda h, i: (0, h))],
        out_shape=[jax.ShapeDtypeStruct((r, 1024), F32), jax.ShapeDtypeStruct((r, 1024), F32), jax.ShapeDtypeStruct((r, 512), F32)],
        compiler_params=_params(("parallel", "arbitrary")))(aq, ak, av, dmix)


def _pool_fn(ext, w, sc, gid0, grp, is_ctx, n_lat, r_all):
    gid = gid0 + lax.broadcasted_iota(I32, (TM + 2 * HALO, 1), 0)
    lo = jnp.where(is_ctx, n_lat, 0)
    hi = jnp.where(is_ctx, r_all, n_lat)
    z = jnp.where(jnp.logical_and(gid >= lo, gid < hi), ext, 0.0)
    w2 = jnp.roll(z, 1, axis=0) + z
    w4 = jnp.roll(w2, 1, axis=0) + jnp.roll(w2, -1, axis=0)
    w8 = jnp.roll(w4, 2, axis=0) + jnp.roll(w4, -2, axis=0)
    w16 = jnp.roll(w8, 4, axis=0) + jnp.roll(w8, -4, axis=0)
    win = jnp.where(grp == 0, w2, jnp.where(grp == 1, w4, jnp.where(grp == 2, w8, w16)))
    half = jnp.left_shift(1, grp)
    cnt = jnp.maximum(jnp.minimum(gid + half, hi) - jnp.maximum(gid - half, lo), 1).astype(F32)
    d = (win / cnt - ext)[HALO:HALO + TM]
    return _bdot(d, w) * sc


def _pool_ext(u_ref, i, r_all):
    s0 = pl.multiple_of(jnp.maximum(i * TM - HALO, 0), HALO)
    s2 = pl.multiple_of(jnp.minimum(i * TM + TM, r_all - HALO), HALO)
    ext = jnp.concatenate([u_ref[pl.ds(s0, HALO), :], u_ref[pl.ds(pl.multiple_of(i * TM, TM), TM), :], u_ref[pl.ds(s2, HALO), :]], axis=0)
    return ext, s0, s2


def _pool_specs(r):
    return [pl.BlockSpec((r, 128), lambda g, i: (0, C_POOL // 128 + g)), pl.BlockSpec((None, 128, 128), lambda g, i: (g, 0, 0)),
            pl.BlockSpec((None, 1, 128), lambda g, i: (g, 0, 0))]


def _pool_fwd(p, pool_w, pool_sc, n_lat, name):
    r = p.shape[0]
    nbl = n_lat // TM

    def body(u_ref, w_ref, sc_ref, o_ref):
        g, i = pl.program_id(0), pl.program_id(1)
        ext, _, _ = _pool_ext(u_ref, i, r)
        o_ref[...] = _pool_fn(ext, w_ref[...], sc_ref[...], i * TM - HALO, g, i >= nbl, n_lat, r).astype(BF16)

    return pl.pallas_call(
        body, name=name, grid=(4, r // TM), in_specs=_pool_specs(r), out_specs=pl.BlockSpec((TM, 128), lambda g, i: (i, g)),
        out_shape=jax.ShapeDtypeStruct((r, 512), BF16), compiler_params=_params(("parallel", "parallel")))(p, pool_w, pool_sc)


def _pool_bwd(p, pool_w, pool_sc, dmix, n_lat, name):
    r = p.shape[0]
    nbl = n_lat // TM

    def body(u_ref, w_ref, sc_ref, do_ref, du_ref, dw_ref, dsc_ref):
        g, i = pl.program_id(0), pl.program_id(1)

        @pl.when(i == 0)
        def _():
            du_ref[...] = jnp.zeros_like(du_ref)
            dw_ref[...] = jnp.zeros_like(dw_ref)
            dsc_ref[...] = jnp.zeros_like(dsc_ref)

        ext, s0, s2 = _pool_ext(u_ref, i, r)
        _, vjp = jax.vjp(lambda e, w, s: _pool_fn(e, w, s, i * TM - HALO, g, i >= nbl, n_lat, r), ext, w_ref[...], sc_ref[...])
        dext, dw, dsc = vjp(do_ref[...])
        du_ref[pl.ds(s0, HALO), :] += dext[0:HALO]
        du_ref[pl.ds(pl.multiple_of(i * TM, TM), TM), :] += dext[HALO:HALO + TM]
        du_ref[pl.ds(s2, HALO), :] += dext[HALO + TM:]
        dw_ref[...] += dw
        dsc_ref[...] += dsc

    return pl.pallas_call(
        body, name=name, grid=(4, r // TM), in_specs=_pool_specs(r) + [pl.BlockSpec((TM, 128), lambda g, i: (i, 4 + g))],
        out_specs=[pl.BlockSpec((r, 128), lambda g, i: (0, g)), pl.BlockSpec((None, 128, 128), lambda g, i: (g, 0, 0)),
                   pl.BlockSpec((None, 1, 128), lambda g, i: (g, 0, 0))],
        out_shape=[jax.ShapeDtypeStruct((r, 512), F32), jax.ShapeDtypeStruct((4, 128, 128), F32), jax.ShapeDtypeStruct((4, 1, 128), F32)],
        compiler_params=_params(("parallel", "arbitrary")))(p, pool_w, pool_sc, dmix)


def _softmax_parts(parts, extra=None):
    m = functools.reduce(jnp.maximum, [jnp.max(s, axis=-1, keepdims=True) for s in parts])
    if extra is not None:
        m = jnp.maximum(m, extra)
    m = lax.stop_gradient(m)
    es = [jnp.exp(s - m) for s in parts]
    den = functools.reduce(jnp.add, [jnp.sum(e, axis=-1, keepdims=True) for e in es])
    if extra is not None:
        den = den + jnp.exp(extra - m)
    return [e / den for e in es]


def _swa_fn(q4, kw, vw, kc, vc, sink4, qpos0, kpos0, is_ctx):
    qs = jnp.concatenate([q4[:, 64 * g:64 * g + 64] for g in range(4)], axis=0)
    s_loc = _bdot_nt(qs, kw) * HD_SCALE
    s_ctx = _bdot_nt(qs, kc) * HD_SCALE
    qpos = qpos0 + jnp.bitwise_and(lax.broadcasted_iota(I32, s_loc.shape, 0), TM - 1)
    kpos = kpos0 + lax.broadcasted_iota(I32, s_loc.shape, 1)
    valid = jnp.logical_and(jnp.abs(kpos - qpos) <= 128, jnp.logical_not(is_ctx))
    s_loc = jnp.where(valid, s_loc, NEG)
    sink = jnp.concatenate([jnp.broadcast_to(sink4[:, g:g + 1], (TM, 1)) for g in range(4)], axis=0)
    p_loc, p_ctx = _softmax_parts([s_loc, s_ctx], sink)
    o = _bdot(p_loc, vw) + _bdot(p_ctx, vc)
    return jnp.concatenate([o[TM * g:TM * (g + 1)] for g in range(4)], axis=1)


def _swa_window(i, n_lat):
    return pl.multiple_of(jnp.clip(i * TM - 128, 0, n_lat - SWA_KEYS), 128)


def _swa_fwd(cq, ck, cv, sp, n_lat, name):
    r = cq.shape[0]
    nbl = n_lat // TM

    def body(q_ref, k_ref, v_ref, sp_ref, o_ref):
        i = pl.program_id(0)
        k0 = _swa_window(i, n_lat)
        kw, vw = k_ref[pl.ds(k0, SWA_KEYS), :].astype(F32), v_ref[pl.ds(k0, SWA_KEYS), :].astype(F32)
        kc, vc = k_ref[pl.ds(n_lat, r - n_lat), :].astype(F32), v_ref[pl.ds(n_lat, r - n_lat), :].astype(F32)
        for j in range(2):
            c = slice(64 * j, 64 * j + 64)
            o = _swa_fn(q_ref[:, 256 * j:256 * j + 256].astype(F32), kw[:, c], vw[:, c], kc[:, c], vc[:, c],
                        sp_ref[SP_SINK:SP_SINK + 1, 4 * j:4 * j + 4], i * TM, k0, i >= nbl)
            o_ref[:, 256 * j:256 * j + 256] = o.astype(BF16)

    return pl.pallas_call(
        body, name=name, grid=(r // TM,),
        in_specs=[_row_spec(512), _full_spec((r, 128)), _full_spec((r, 128)), _full_spec(sp.shape)],
        out_specs=_row_spec(512), out_shape=jax.ShapeDtypeStruct((r, 512), BF16), compiler_params=_params(("parallel",)))(cq, ck, cv, sp)


def _swa_bwd(cq, ck, cv, sp, dmix, n_lat, name):
    r = cq.shape[0]
    nbl = n_lat // TM
    nc = r - n_lat

    def body(q_ref, k_ref, v_ref, sp_ref, do_ref, dq_ref, dk_ref, dv_ref, dsp_ref):
        i = pl.program_id(0)

        @pl.when(i == 0)
        def _():
            dk_ref[...] = jnp.zeros_like(dk_ref)
            dv_ref[...] = jnp.zeros_like(dv_ref)
            dsp_ref[...] = jnp.zeros_like(dsp_ref)

        k0 = _swa_window(i, n_lat)
        kw, vw = k_ref[pl.ds(k0, SWA_KEYS), :].astype(F32), v_ref[pl.ds(k0, SWA_KEYS), :].astype(F32)
        kc, vc = k_ref[pl.ds(n_lat, nc), :].astype(F32), v_ref[pl.ds(n_lat, nc), :].astype(F32)
        dkw, dvw, dkc, dvc, dsk = [], [], [], [], []
        for j in range(2):
            c = slice(64 * j, 64 * j + 64)
            _, vjp = jax.vjp(lambda q4, a, b, cc, d, s: _swa_fn(q4, a, b, cc, d, s, i * TM, k0, i >= nbl),
                             q_ref[:, 256 * j:256 * j + 256].astype(F32), kw[:, c], vw[:, c], kc[:, c], vc[:, c],
                             sp_ref[SP_SINK:SP_SINK + 1, 4 * j:4 * j + 4])
            dq4, a, b, cc, d, s = vjp(do_ref[:, 256 * j:256 * j + 256])
            dq_ref[:, 256 * j:256 * j + 256] = dq4
            dkw.append(a), dvw.append(b), dkc.append(cc), dvc.append(d), dsk.append(s)
        cat = lambda xs: jnp.concatenate(xs, axis=1)
        dk_ref[pl.ds(k0, SWA_KEYS), :] += cat(dkw)
        dv_ref[pl.ds(k0, SWA_KEYS), :] += cat(dvw)
        dk_ref[pl.ds(n_lat, nc), :] += cat(dkc)
        dv_ref[pl.ds(n_lat, nc), :] += cat(dvc)
        dsp_ref[SP_SINK:SP_SINK + 1, 0:8] += cat(dsk)

    return pl.pallas_call(
        body, name=name, grid=(r // TM,),
        in_specs=[_row_spec(512), _full_spec((r, 128)), _full_spec((r, 128)), _full_spec(sp.shape), _row_spec(512, 2)],
        out_specs=[_row_spec(512), _full_spec((r, 128)), _full_spec((r, 128)), _full_spec(sp.shape)],
        out_shape=[jax.ShapeDtypeStruct((r, 512), F32), jax.ShapeDtypeStruct((r, 128), F32), jax.ShapeDtypeStruct((r, 128), F32),
                   jax.ShapeDtypeStruct(sp.shape, F32)],
        compiler_params=_params(("arbitrary",)))(cq, ck, cv, sp, dmix)


def _na_fn(q, kw, vw, kc, vc, bias, valid):
    s_loc = jnp.where(valid, _bdot_nt(q, kw) * HD_SCALE + bias, NEG)
    s_ctx = _bdot_nt(q, kc) * HD_SCALE
    p_loc, p_ctx = _softmax_parts([s_loc, s_ctx])
    return _bdot(p_loc, vw) + _bdot(p_ctx, vc)


def _na_geometry(i, n_lat, is_ctx):
    rows = n_lat // GRID_W
    qrow0 = i * (TM // GRID_W)
    krow0 = jnp.clip(qrow0 - 4, 0, rows - NA_KROWS)
    nk = NA_KROWS * GRID_W
    tq = i * TM + lax.broadcasted_iota(I32, (TM, nk), 0)
    tk = krow0 * GRID_W + lax.broadcasted_iota(I32, (TM, nk), 1)
    qr, qc = jnp.right_shift(tq, 6), jnp.bitwise_and(tq, GRID_W - 1)
    kr, kc = jnp.right_shift(tk, 6), jnp.bitwise_and(tk, GRID_W - 1)
    r_lo = jnp.clip(qr - 4, 0, rows - 8)
    c_lo = jnp.clip(qc - 8, 0, GRID_W - 16)
    valid = (kr >= r_lo) & (kr < r_lo + 8) & (kc >= c_lo) & (kc < c_lo + 16) & jnp.logical_not(is_ctx)
    dr = [[jnp.clip(krow0 + kj - (qrow0 + qi) + 7, 0, 14) for kj in range(NA_KROWS)] for qi in range(TM // GRID_W)]
    return pl.multiple_of(krow0 * GRID_W, GRID_W), valid, dr


def _na_bias(t1_ref, hh, dr):
    return jnp.concatenate([jnp.concatenate([t1_ref[hh, dr[qi][kj]] for kj in range(NA_KROWS)], axis=1)
                            for qi in range(TM // GRID_W)], axis=0)


def _na_specs(r):
    return [pl.BlockSpec((TM, 128), lambda pr, i: (i, pr)), pl.BlockSpec((r, 128), lambda pr, i: (0, pr)),
            pl.BlockSpec((r, 128), lambda pr, i: (0, pr)), pl.BlockSpec((2, 16, GRID_W, GRID_W), lambda pr, i: (pr, 0, 0, 0))]


def _na_fwd(dq, dk, dv, t1, n_lat, name):
    r = dq.shape[0]
    nbl = n_lat // TM
    nc = r - n_lat
    nk = NA_KROWS * GRID_W

    def body(q_ref, k_ref, v_ref, t1_ref, o_ref):
        i = pl.program_id(1)
        k0, valid, dr = _na_geometry(jnp.minimum(i, nbl - 1), n_lat, i >= nbl)
        kw, vw = k_ref[pl.ds(k0, nk), :].astype(F32), v_ref[pl.ds(k0, nk), :].astype(F32)
        kc, vc = k_ref[pl.ds(n_lat, nc), :].astype(F32), v_ref[pl.ds(n_lat, nc), :].astype(F32)
        for hh in range(2):
            c = slice(64 * hh, 64 * hh + 64)
            o = _na_fn(q_ref[:, c].astype(F32), kw[:, c], vw[:, c], kc[:, c], vc[:, c], _na_bias(t1_ref, hh, dr), valid)
            o_ref[:, c] = o.astype(BF16)

    return pl.pallas_call(
        body, name=name, grid=(4, r // TM), in_specs=_na_specs(r), out_specs=pl.BlockSpec((TM, 128), lambda pr, i: (i, pr)),
        out_shape=jax.ShapeDtypeStruct((r, 512), BF16), compiler_params=_params(("parallel", "parallel")))(dq, dk, dv, t1)


def _na_bwd(dq, dk, dv, t1, dmix, n_lat, name):
    r = dq.shape[0]
    nbl = n_lat // TM
    nc = r - n_lat
    nk = NA_KROWS * GRID_W

    def body(q_ref, k_ref, v_ref, t1_ref, do_ref, dq_ref, dk_ref, dv_ref, dt1_ref):
        i = pl.program_id(1)

        @pl.when(i == 0)
        def _():
            dk_ref[...] = jnp.zeros_like(dk_ref)
            dv_ref[...] = jnp.zeros_like(dv_ref)
            dt1_ref[...] = jnp.zeros_like(dt1_ref)

        k0, valid, dr = _na_geometry(jnp.minimum(i, nbl - 1), n_lat, i >= nbl)
        kw, vw = k_ref[pl.ds(k0, nk), :].astype(F32), v_ref[pl.ds(k0, nk), :].astype(F32)
        kc, vc = k_ref[pl.ds(n_lat, nc), :].astype(F32), v_ref[pl.ds(n_lat, nc), :].astype(F32)
        dkw, dvw, dkc, dvc = [], [], [], []
        for hh in range(2):
            c = slice(64 * hh, 64 * hh + 64)
            _, vjp = jax.vjp(lambda q, a, b, cc, d, bb: _na_fn(q, a, b, cc, d, bb, valid),
                             q_ref[:, c].astype(F32), kw[:, c], vw[:, c], kc[:, c], vc[:, c], _na_bias(t1_ref, hh, dr))
            dqh, a, b, cc, d, dbias = vjp(do_ref[:, c])
            dq_ref[:, c] = dqh
            dkw.append(a), dvw.append(b), dkc.append(cc), dvc.append(d)
            for qi in range(TM // GRID_W):
                for kj in range(NA_KROWS):
                    dt1_ref[hh, dr[qi][kj]] += dbias[GRID_W * qi:GRID_W * (qi + 1), GRID_W * kj:GRID_W * (kj + 1)]
        cat = lambda xs: jnp.concatenate(xs, axis=1)
        dk_ref[pl.ds(k0, nk), :] += cat(dkw)
        dv_ref[pl.ds(k0, nk), :] += cat(dvw)
        dk_ref[pl.ds(n_lat, nc), :] += cat(dkc)
        dv_ref[pl.ds(n_lat, nc), :] += cat(dvc)

    return pl.pallas_call(
        body, name=name, grid=(4, r // TM), in_specs=_na_specs(r) + [pl.BlockSpec((TM, 128), lambda pr, i: (i, 12 + pr))],
        out_specs=[pl.BlockSpec((TM, 128), lambda pr, i: (i, pr)), pl.BlockSpec((r, 128), lambda pr, i: (0, pr)),
                   pl.BlockSpec((r, 128), lambda pr, i: (0, pr)), pl.BlockSpec((2, 16, GRID_W, GRID_W), lambda pr, i: (pr, 0, 0, 0))],
        out_shape=[jax.ShapeDtypeStruct((r, 512), F32)] * 3 + [jax.ShapeDtypeStruct((8, 16, GRID_W, GRID_W), F32)],
        compiler_params=_params(("parallel", "arbitrary")))(dq, dk, dv, t1, dmix)


def _conv_ext(main_ref, prev_ref, next_ref):
    return jnp.concatenate([prev_ref[...], main_ref[...], next_ref[...]], axis=0)


def _conv_masks(i, nbl, n_lat, r_all):
    gid = i * TM - HALO + lax.broadcasted_iota(I32, (TM + 2 * HALO, 1), 0)
    is_ctx = i >= nbl
    lo = jnp.where(is_ctx, n_lat, 0)
    hi = jnp.where(is_ctx, r_all, n_lat)
    return jnp.logical_and(gid >= lo, gid < hi), gid - 1 >= lo, gid + 1 < hi


def _conv_apply(ext, w, b, has_up, has_dn):
    up = jnp.where(has_up, jnp.roll(ext, 1, axis=0), 0.0)
    dn = jnp.where(has_dn, jnp.roll(ext, -1, axis=0), 0.0)
    return up * w[0:1] + ext * w[1:2] + dn * w[2:3] + b, up, dn


def _conv_in_specs(tc, r):
    nb8 = TM // HALO
    last8 = r // HALO - 1

    def trio(half):
        return [pl.BlockSpec((None, TM, tc), lambda j, i: (half, i, j)),
                pl.BlockSpec((None, HALO, tc), lambda j, i: (half, jnp.maximum(i * nb8 - 1, 0), j)),
                pl.BlockSpec((None, HALO, tc), lambda j, i: (half, jnp.minimum((i + 1) * nb8, last8), j))]

    wb = [pl.BlockSpec((None, 3, tc), lambda j, i: (0, 0, j)), pl.BlockSpec((None, 3, tc), lambda j, i: (1, 0, j)),
          pl.BlockSpec((None, 1, tc), lambda j, i: (0, 0, j)), pl.BlockSpec((None, 1, tc), lambda j, i: (1, 0, j))]
    return trio(0) + trio(1) + wb


def _convgate_fwd(a3, cw, cb, n_lat, name):
    _, r, ff = a3.shape
    nbl = n_lat // TM
    tc = _pick(ff, (512, 256, 128))

    def body(g_ref, gp_ref, gn_ref, v_ref, vp_ref, vn_ref, wg_ref, wv_ref, bg_ref, bv_ref, u_ref):
        _, has_up, has_dn = _conv_masks(pl.program_id(1), nbl, n_lat, r)
        gg, _, _ = _conv_apply(_conv_ext(g_ref, gp_ref, gn_ref), wg_ref[...], bg_ref[...], has_up, has_dn)
        gv, _, _ = _conv_apply(_conv_ext(v_ref, vp_ref, vn_ref), wv_ref[...], bv_ref[...], has_up, has_dn)
        u_ref[...] = (jax.nn.silu(gg) * gv)[HALO:HALO + TM].astype(BF16)

    return pl.pallas_call(
        body, name=name, grid=(ff // tc, r // TM), in_specs=_conv_in_specs(tc, r),
        out_specs=pl.BlockSpec((TM, tc), lambda j, i: (i, j)), out_shape=jax.ShapeDtypeStruct((r, ff), BF16),
        compiler_params=_params(("parallel", "parallel")))(a3, a3, a3, a3, a3, a3, cw, cw, cb, cb)


def _convgate_bwd(a3, cw, cb, du, n_lat, name):
    _, r, ff = a3.shape
    nbl = n_lat // TM
    tc = _pick(ff, (512, 256, 128))
    nb8 = TM // HALO
    last8 = r // HALO - 1

    def body(g_ref, gp_ref, gn_ref, v_ref, vp_ref, vn_ref, wg_ref, wv_ref, bg_ref, bv_ref, du_ref, dup_ref, dun_ref,
             da_ref, dcw_ref, dcb_ref):
        i = pl.program_id(1)

        @pl.when(i == 0)
        def _():
            dcw_ref[...] = jnp.zeros_like(dcw_ref)
            dcb_ref[...] = jnp.zeros_like(dcb_ref)

        inb, has_up, has_dn = _conv_masks(i, nbl, n_lat, r)
        wg, wv = wg_ref[...], wv_ref[...]
        eg, ev = _conv_ext(g_ref, gp_ref, gn_ref), _conv_ext(v_ref, vp_ref, vn_ref)
        gg, ug, dg_ = _conv_apply(eg, wg, bg_ref[...], has_up, has_dn)
        gv, uv, dv_ = _conv_apply(ev, wv, bv_ref[...], has_up, has_dn)
        due = jnp.where(inb, _conv_ext(du_ref, dup_ref, dun_ref), 0.0)
        sg = jax.nn.sigmoid(gg)
        dgg = due * gv * (sg * (1.0 + gg * (1.0 - sg)))
        dgv = due * (gg * sg)
        main = slice(HALO, HALO + TM)
        for h, (dgx, w, ex, upx, dnx) in enumerate(((dgg, wg, eg, ug, dg_), (dgv, wv, ev, uv, dv_))):
            da = dgx * w[1:2] + jnp.roll(dgx, -1, axis=0) * w[0:1] + jnp.roll(dgx, 1, axis=0) * w[2:3]
            da_ref[h] = da[main].astype(BF16)
            dm = dgx[main]
            dcw_ref[h, 0:1, :] += jnp.sum(dm * upx[main], axis=0, keepdims=True)
            dcw_ref[h, 1:2, :] += jnp.sum(dm * ex[main], axis=0, keepdims=True)
            dcw_ref[h, 2:3, :] += jnp.sum(dm * dnx[main], axis=0, keepdims=True)
            dcb_ref[h] += jnp.sum(dm, axis=0, keepdims=True)

    du_specs = [pl.BlockSpec((TM, tc), lambda j, i: (i, j)),
                pl.BlockSpec((HALO, tc), lambda j, i: (jnp.maximum(i * nb8 - 1, 0), j)),
                pl.BlockSpec((HALO, tc), lambda j, i: (jnp.minimum((i + 1) * nb8, last8), j))]
    return pl.pallas_call(
        body, name=name, grid=(ff // tc, r // TM), in_specs=_conv_in_specs(tc, r) + du_specs,
        out_specs=[pl.BlockSpec((2, TM, tc), lambda j, i: (0, i, j)), pl.BlockSpec((2, 3, tc), lambda j, i: (0, 0, j)),
                   pl.BlockSpec((2, 1, tc), lambda j, i: (0, 0, j))],
        out_shape=[jax.ShapeDtypeStruct((2, r, ff), BF16), jax.ShapeDtypeStruct((2, 3, ff), F32), jax.ShapeDtypeStruct((2, 1, ff), F32)],
        compiler_params=_params(("parallel", "arbitrary")))(a3, a3, a3, a3, a3, a3, cw, cw, cb, cb, du, du, du)


def _layer_fwd(x, w, l, tab, n_lat):
    nbl = n_lat // TM
    mod = w["mods"][l]
    h1 = _normmod_fwd(x, w["g_mix"][l], mod, 0, nbl, "normmod_fwd")
    p = _mm(h1, w["w_in"][l], "nn", F32, "mm_in")
    qkv = _prep_fwd(p, tab, w["sp"][l], w["w_qb"][l], w["w_kvb"][l], "prep_fwd")
    oa = _mla_fwd(qkv[0], qkv[1], qkv[2], n_lat, "mla_fwd")
    ob = _pool_fwd(p, w["pool_w"][l], w["pool_sc"][l], n_lat, "pool_fwd")
    oc = _swa_fwd(qkv[3], qkv[4], qkv[5], w["sp"][l], n_lat, "swa_fwd")
    od = _na_fwd(qkv[6], qkv[7], qkv[8], w["t1"][l], n_lat, "na_fwd")
    mix = jnp.concatenate([oa, ob, oc, od], axis=1)
    y = _mm(mix, w["w_out"][l], "nn", F32, "mm_out")
    x1 = _resid_fwd(x, y, mod, 2, nbl, "resid_fwd")
    h2 = _normmod_fwd(x1, w["g_ffn"][l], mod, 3, nbl, "normmod_fwd")
    a3 = _mm(h2, w["w_up"][l], "nn", F32, "mm_up", o_split=True)
    u = _convgate_fwd(a3, w["conv_w"][l], w["conv_b"][l], n_lat, "convgate_fwd")
    y2 = _mm(u, w["w_down"][l], "nn", F32, "mm_down")
    x2 = _resid_fwd(x1, y2, mod, 5, nbl, "resid_fwd")
    return x2, dict(x=x, h1=h1, p=p, qkv=qkv, mix=mix, y=y, x1=x1, h2=h2, a3=a3, u=u, y2=y2)


def _layer_bwd(dx, s, w, l, tab, n_lat):
    nbl = n_lat // TM
    mod = w["mods"][l]
    g = {}
    dy2, dmod_a = _resid_bwd(dx, s["y2"], mod, 5, nbl, "resid_bwd")
    g["w_down"] = _mm(s["u"], dy2, "tn", F32, "mm_dwdown")
    du = _mm(dy2, w["w_down"][l], "nt", F32, "mm_du")
    da3, g["conv_w"], g["conv_b"] = _convgate_bwd(s["a3"], w["conv_w"][l], w["conv_b"][l], du, n_lat, "convgate_bwd")
    g["w_up"] = _mm(s["h2"], da3, "tn", F32, "mm_dwup", b_split=True)
    dh2 = _mm(da3, w["w_up"][l], "nt", F32, "mm_dh2", a_split=True)
    dx1, g["g_ffn"], dmod_b = _normmod_bwd(s["x1"], w["g_ffn"][l], mod, dh2, dx, 3, nbl, "normmod_bwd")
    dy, dmod_c = _resid_bwd(dx1, s["y"], mod, 2, nbl, "resid_bwd")
    g["w_out"] = _mm(s["mix"], dy, "tn", F32, "mm_dwout")
    dmix = _mm(dy, w["w_out"][l], "nt", F32, "mm_dmix")
    qkv = s["qkv"]
    daq, dak, dav = _mla_bwd(qkv[0], qkv[1], qkv[2], dmix, n_lat, "mla_bwd")
    dpool, g["pool_w"], g["pool_sc"] = _pool_bwd(s["p"], w["pool_w"][l], w["pool_sc"][l], dmix, n_lat, "pool_bwd")
    dcq, dck, dcv, dsp_c = _swa_bwd(qkv[3], qkv[4], qkv[5], w["sp"][l], dmix, n_lat, "swa_bwd")
    ddq, ddk, ddv, g["t1"] = _na_bwd(qkv[6], qkv[7], qkv[8], w["t1"][l], dmix, n_lat, "na_bwd")
    dp, dsp_p, g["w_qb"], g["w_kvb"] = _prep_bwd(s["p"], tab, w["sp"][l], w["w_qb"][l], w["w_kvb"][l],
                                                (daq, dak, dav, dcq, dck, dcv, ddq, ddk, ddv), dpool, "prep_bwd")
    g["sp"] = dsp_c + dsp_p
    g["w_in"] = _mm(s["h1"], dp, "tn", F32, "mm_dwin")
    dh1 = _mm(dp, w["w_in"][l], "nt", F32, "mm_dh1")
    dx0, g["g_mix"], dmod_d = _normmod_bwd(s["x"], w["g_mix"][l], mod, dh1, dx1, 0, nbl, "normmod_bwd")
    g["mods"] = dmod_a + dmod_b + dmod_c + dmod_d
    return dx0, g


def _local_step(x_all, target, w, tab, n_lat):
    saved = []
    x = x_all
    for l in range(DEPTH):
        x, s = _layer_fwd(x, w, l, tab, n_lat)
        saved.append(s)
    loss, dx = _loss_kernel(x, target, n_lat // TM, "loss")
    grads = [None] * DEPTH
    for l in reversed(range(DEPTH)):
        dx, grads[l] = _layer_bwd(dx, saved[l], w, l, tab, n_lat)
    return loss[0, 0], dx, grads


def _pad_cols(a, widths):
    parts, o = [], 0
    for take, pad in widths:
        parts.append(a[..., o:o + take])
        if pad:
            parts.append(jnp.zeros(a.shape[:-1] + (pad,), a.dtype))
        o += take
    return jnp.concatenate(parts, axis=-1)


def _w_in_layout(w_in):
    return _pad_cols(w_in, [(832, 64), (P_COLS - 832, PW - P_COLS - 64)])


def _w_in_unlayout(g):
    return jnp.concatenate([g[..., 0:832], g[..., 896:896 + P_COLS - 832]], axis=-1)


def _w_qb_layout(w):
    s = w.reshape(w.shape[:-1] + (4, 192))
    return jnp.concatenate([s[..., 0:128].reshape(w.shape[:-1] + (512,)), s[..., 128:192].reshape(w.shape[:-1] + (256,))], axis=-1)


def _w_qb_unlayout(g):
    n = g[..., 0:512].reshape(g.shape[:-1] + (4, 128))
    r = g[..., 512:768].reshape(g.shape[:-1] + (4, 64))
    return jnp.concatenate([n, r], axis=-1).reshape(g.shape[:-1] + (768,))


SP_SLOTS = (("mla_q_a_norm", 512), ("mla_kv_a_norm", 256), ("mla_q_nope_norm", 128), ("mla_q_rope_norm", 64),
            ("mla_k_nope_norm", 128), ("mla_k_rope_norm", 64), ("swa_q_norm", 64), ("swa_k_norm", 64),
            ("na_q_norm", 64), ("na_k_norm", 64), ("swa_sink", 8))


def _sp_pack(small):
    rows = [jnp.pad(small[k], ((0, 0), (0, 512 - n))) for k, n in SP_SLOTS]
    rows += [jnp.zeros_like(rows[0])] * (16 - len(rows))
    return jnp.stack(rows, axis=1)


def _sp_unpack(sp):
    return {k: sp[:, i, 0:n] for i, (k, n) in enumerate(SP_SLOTS)}


def _rpb_onehot():
    qc = lax.broadcasted_iota(I32, (GRID_W, GRID_W), 0)
    kc = lax.broadcasted_iota(I32, (GRID_W, GRID_W), 1)
    dc = (jnp.clip(kc - qc, -15, 15) + 15).reshape(1, GRID_W * GRID_W)
    return (lax.broadcasted_iota(I32, (32, GRID_W * GRID_W), 0) == dc).astype(F32)


def _rpb_expand(rpb):
    l = rpb.shape[0]
    flat = jnp.pad(rpb, ((0, 0), (0, 0), (0, 1), (0, 1))).reshape(l * 128, 32)
    return _mm_exact(flat, _rpb_onehot(), "rpb_expand").reshape(l, 8, 16, GRID_W, GRID_W)


def _rpb_fold(dt1):
    l = dt1.shape[0]
    g = _mm_exact(dt1.reshape(l * 128, GRID_W * GRID_W), _rpb_onehot().T, "rpb_fold")
    return g.reshape(l, 8, 16, 32)[:, :, 0:15, 0:31]


def _rope_table(n_lat, n_ctx):
    t = jnp.arange(n_lat)
    inv = ROPE_BASE ** (-jnp.arange(0, 32, 2, dtype=F32) / 32)
    ar = (t // GRID_W).astype(F32)[:, None] * inv
    ac = (t % GRID_W).astype(F32)[:, None] * inv
    cos = jnp.concatenate([jnp.cos(ar), jnp.cos(ar), jnp.cos(ac), jnp.cos(ac)], axis=-1)
    sin = jnp.concatenate([jnp.sin(ar), jnp.sin(ar), jnp.sin(ac), jnp.sin(ac)], axis=-1)
    tab = jnp.concatenate([cos, sin], axis=-1)
    ident = jnp.concatenate([jnp.ones((n_ctx, 64), F32), jnp.zeros((n_ctx, 64), F32)], axis=-1)
    return jnp.concatenate([tab, ident], axis=0)


def _kernel_weights(full, mods):
    l = full["g_mix"].shape[0]
    ff = full["ffn_w_down"].shape[1]
    return dict(
        mods=mods,
        g_mix=full["g_mix"][:, None, :], g_ffn=full["g_ffn"][:, None, :],
        w_in=_w_in_layout(full["w_in"]).astype(BF16), w_out=full["w_out"].astype(BF16),
        w_up=full["ffn_w_up"].astype(BF16), w_down=full["ffn_w_down"].astype(BF16),
        w_qb=_w_qb_layout(full["mla_w_qb"]).astype(BF16), w_kvb=full["mla_w_kvb"].astype(BF16),
        sp=_sp_pack(full), pool_w=full["pool_w"], pool_sc=full["pool_scale"].reshape(l, 4, 1, 128),
        t1=_rpb_expand(full["na_rpb"]),
        conv_w=full["ffn_conv_w"].reshape(l, 3, 2, ff).transpose(0, 2, 1, 3),
        conv_b=full["ffn_conv_b"].reshape(l, 2, 1, ff))


def _reference_grads(grads):
    st = lambda k: jnp.stack([g[k] for g in grads], axis=0)
    l = len(grads)
    out = dict(
        g_mix=st("g_mix")[:, 0], g_ffn=st("g_ffn")[:, 0], w_in=_w_in_unlayout(st("w_in")), w_out=st("w_out"),
        ffn_w_up=st("w_up"), ffn_w_down=st("w_down"), mla_w_qb=_w_qb_unlayout(st("w_qb")), mla_w_kvb=st("w_kvb"),
        pool_w=st("pool_w"), pool_scale=st("pool_sc").reshape(l, 512), na_rpb=_rpb_fold(st("t1")),
        ffn_conv_w=st("conv_w").transpose(0, 2, 1, 3).reshape(l, 3, -1), ffn_conv_b=st("conv_b").reshape(l, -1),
        mods=st("mods"))
    out.update(_sp_unpack(st("sp")))
    return out


ANY = pl.BlockSpec(memory_space=pl.ANY)


def _flip(x, y, j):
    return (1 - x if j >> 1 else x), (1 - y if j & 1 else y)


def _comm_call(name, ins, out_shapes, n_copies, plan):
    n_in, n_out = len(ins), len(out_shapes)

    def body(*refs):
        in_refs, out_refs = refs[:n_in], refs[n_in:n_in + n_out]
        ssem, rsem = refs[n_in + n_out:]
        pos = (lax.axis_index("x"), lax.axis_index("y"), lax.axis_index("c"))
        copies = plan(in_refs, out_refs, pos)
        assert len(copies) == n_copies
        descs = []
        for i, (src, dst, peer) in enumerate(copies):
            if peer is None:
                d = pltpu.make_async_copy(src, dst, ssem.at[i])
            else:
                d = pltpu.make_async_remote_copy(src_ref=src, dst_ref=dst, send_sem=ssem.at[i], recv_sem=rsem.at[i],
                                                 device_id=peer, device_id_type=MESH)
            d.start()
            descs.append(d)
        for d in descs:
            d.wait()

    return pl.pallas_call(
        body, name=name, in_specs=[ANY] * n_in, out_specs=[ANY] * n_out, out_shape=list(out_shapes),
        scratch_shapes=[pltpu.SemaphoreType.DMA((n_copies,)), pltpu.SemaphoreType.DMA((n_copies,))])(*ins)


def _pair_up(xs, name):
    def plan(ins, outs, pos):
        x, y, c = pos
        cps = []
        for i_ref, o_ref in zip(ins, outs):
            cps.append((i_ref, o_ref.at[c], None))
            cps.append((i_ref, o_ref.at[c], (x, y, 1 - c)))
        return cps

    return _comm_call(name, xs, [jax.ShapeDtypeStruct((2,) + a.shape, a.dtype) for a in xs], 2 * len(xs), plan)


def _pair_layers(groups, name):
    flat = [a for g in groups for a in g]

    def plan(ins, outs, pos):
        x, y, c = pos
        cps, n = [], 0
        for g, o_ref in zip(groups, outs):
            for l in range(len(g)):
                cps.append((ins[n], o_ref.at[l, c], None))
                cps.append((ins[n], o_ref.at[l, c], (x, y, 1 - c)))
                n += 1
        return cps

    shapes = [jax.ShapeDtypeStruct((len(g), 2) + g[0].shape, g[0].dtype) for g in groups]
    return _comm_call(name, flat, shapes, 2 * len(flat), plan)


def _chip_gather(xs, kinds, name):
    def dst(o_ref, kind, k, x_shape):
        if kind == "lead":
            return o_ref.at[k]
        if kind == "axis1":
            return o_ref.at[:, k]
        w = x_shape[-1]
        idx = (slice(None),) * (len(x_shape) - 1) + (pl.ds(pl.multiple_of(k * w, 128), w),)
        return o_ref.at[idx]

    def plan(ins, outs, pos):
        x, y, c = pos
        k = 2 * x + y
        cps = []
        for i_ref, o_ref, kind, a in zip(ins, outs, kinds, xs):
            cps.append((i_ref, dst(o_ref, kind, k, a.shape), None))
            for j in (1, 2, 3):
                tx, ty = _flip(x, y, j)
                cps.append((i_ref, dst(o_ref, kind, k, a.shape), (tx, ty, c)))
        return cps

    def oshape(a, kind):
        if kind == "lead":
            return (4,) + a.shape
        if kind == "axis1":
            return (a.shape[0], 4) + a.shape[1:]
        return a.shape[:-1] + (4 * a.shape[-1],)

    return _comm_call(name, xs, [jax.ShapeDtypeStruct(oshape(a, kd), a.dtype) for a, kd in zip(xs, kinds)], 4 * len(xs), plan)


def _chip_scatter(xs, name):
    def plan(ins, outs, pos):
        x, y, c = pos
        k = 2 * x + y
        cps = []
        for i_ref, o_ref in zip(ins, outs):
            cps.append((i_ref.at[k], o_ref.at[k], None))
            for j in (1, 2, 3):
                tx, ty = _flip(x, y, j)
                cps.append((i_ref.at[2 * tx + ty], o_ref.at[k], (tx, ty, c)))
        return cps

    return _comm_call(name, xs, [jax.ShapeDtypeStruct(a.shape, a.dtype) for a in xs], 4 * len(xs), plan)


def _half_swap(xs, kinds, name):
    def plan(ins, outs, pos):
        x, y, c = pos
        cps = []
        for i_ref, o_ref, kind, a in zip(ins, outs, kinds, xs):
            h = a.shape[-2] // 2
            rows = pl.ds(pl.multiple_of((1 - c) * h, 8), h)
            src = i_ref.at[:, rows, :] if kind == "cm" else i_ref.at[rows, :]
            cps.append((src, o_ref, (x, y, 1 - c)))
        return cps

    shapes = [jax.ShapeDtypeStruct(a.shape[:-2] + (a.shape[-2] // 2, a.shape[-1]), a.dtype) for a in xs]
    return _comm_call(name, xs, shapes, len(xs), plan)


def _row_tile(rows, cols, budget=1 << 20):
    for t in (2048, 1024, 512, 256, 128, 64, 32, 16, 8):
        if rows % t == 0 and t * cols * 4 <= budget:
            return t
    return rows


def _half_add(g, recv, kind, cvec, name):
    if kind == "cm":
        _, h2, w = g.shape
        h = h2 // 2
        tr = _row_tile(h, w)
        nb = h // tr
        g_spec = pl.BlockSpec((None, tr, w), lambda kk, i, c_ref: (kk, i + c_ref[0] * nb, 0))
        r_spec = pl.BlockSpec((None, tr, w), lambda kk, i, c_ref: (kk, i, 0))
    else:
        h2, w4 = g.shape
        h, w = h2 // 2, w4 // 4
        tr = _row_tile(h, w)
        nb = h // tr
        g_spec = pl.BlockSpec((tr, w), lambda kk, i, c_ref: (i + c_ref[0] * nb, kk))
        r_spec = pl.BlockSpec((tr, w), lambda kk, i, c_ref: (i, kk))

    def body(c_ref, g_ref, r_ref, o_ref):
        o_ref[...] = (g_ref[...] + r_ref[...]).astype(BF16)

    return pl.pallas_call(
        body, name=name,
        grid_spec=pltpu.PrefetchScalarGridSpec(num_scalar_prefetch=1, grid=(4, nb), in_specs=[g_spec, r_spec],
                                               out_specs=pl.BlockSpec((None, tr, w), lambda kk, i, c_ref: (kk, i, 0))),
        out_shape=jax.ShapeDtypeStruct((4, h, w), BF16), compiler_params=_params(("parallel", "parallel")))(cvec, g, recv)


def _sum_lead(x, name):
    n, rows, w = x.shape
    tr = _row_tile(rows, w, (1 << 21) // n)

    def body(x_ref, o_ref):
        acc = x_ref[0].astype(F32)
        for j in range(1, n):
            acc = acc + x_ref[j].astype(F32)
        o_ref[...] = acc

    return pl.pallas_call(
        body, name=name, grid=(rows // tr,), in_specs=[pl.BlockSpec((n, tr, w), lambda i: (0, i, 0))],
        out_specs=pl.BlockSpec((tr, w), lambda i: (i, 0)), out_shape=jax.ShapeDtypeStruct((rows, w), F32),
        compiler_params=_params(("parallel",)))(x)


def _adamw(w, g, m, v, name):
    shape = w.shape
    cols = shape[-1]
    rows = w.size // cols
    tr = _row_tile(rows, cols, 1 << 19)

    def body(w_ref, g_ref, m_ref, v_ref, d_ref, mo_ref, vo_ref):
        gv = g_ref[...]
        mn = ADAM_B1 * m_ref[...] + (1.0 - ADAM_B1) * gv
        vn = ADAM_B2 * v_ref[...] + (1.0 - ADAM_B2) * jnp.square(gv)
        m_hat = mn / (1.0 - ADAM_B1 ** ADAM_STEP)
        v_hat = vn / (1.0 - ADAM_B2 ** ADAM_STEP)
        d_ref[...] = -ADAM_LR * (m_hat / (jnp.sqrt(v_hat) + ADAM_EPS) + ADAM_WD * w_ref[...])
        mo_ref[...] = mn
        vo_ref[...] = vn

    spec = pl.BlockSpec((tr, cols), lambda i: (i, 0))
    outs = pl.pallas_call(
        body, name=name, grid=(rows // tr,), in_specs=[spec] * 4, out_specs=[spec] * 3,
        out_shape=[jax.ShapeDtypeStruct((rows, cols), F32)] * 3,
        compiler_params=_params(("parallel",)))(*[a.reshape(rows, cols) for a in (w, g, m, v)])
    return [o.reshape(shape) for o in outs]


def _silu_grad(x):
    s = jax.nn.sigmoid(x)
    return s * (1.0 + x * (1.0 - s))


def _mod_fwd(cs16, w_mod, b_sh, name):
    l, d, wc = w_mod.shape
    tn = _pick(wc, (512, 384, 256, 128))

    def body(c_ref, w_ref, b_ref, o_ref):
        a = jax.nn.silu(c_ref[...]).astype(BF16)
        o_ref[...] = jnp.dot(a, w_ref[...].astype(BF16), preferred_element_type=F32) + b_ref[...]

    return pl.pallas_call(
        body, name=name, grid=(l, wc // tn),
        in_specs=[_full_spec((16, d)), pl.BlockSpec((None, d, tn), lambda i, j: (i, 0, j)), pl.BlockSpec((None, 1, tn), lambda i, j: (i, 0, j))],
        out_specs=pl.BlockSpec((None, 16, tn), lambda i, j: (i, 0, j)), out_shape=jax.ShapeDtypeStruct((l, 16, wc), F32),
        compiler_params=_params(("parallel", "parallel")))(cs16, w_mod, b_sh)


def _mod_dw(cs16, dm_sh, name):
    l, _, wc = dm_sh.shape
    d = cs16.shape[1]
    tr = _pick(d, (512, 256, 128))
    tc = _pick(wc, (512, 384, 256, 128))

    def body(c_ref, dm_ref, o_ref):
        a = jax.nn.silu(c_ref[...]).astype(BF16)
        o_ref[...] = lax.dot_general(a, dm_ref[...].astype(BF16), (((0,), (0,)), ((), ())), preferred_element_type=F32)

    return pl.pallas_call(
        body, name=name, grid=(l, d // tr, wc // tc),
        in_specs=[pl.BlockSpec((16, tr), lambda i, r, j: (0, r)), pl.BlockSpec((None, 16, tc), lambda i, r, j: (i, 0, j))],
        out_specs=pl.BlockSpec((None, tr, tc), lambda i, r, j: (i, r, j)), out_shape=jax.ShapeDtypeStruct((l, d, wc), F32),
        compiler_params=_params(("parallel", "parallel", "parallel")))(cs16, dm_sh)


def _mod_dc(dm_sh, w_mod, c_ctx, name):
    l, d, wc = w_mod.shape
    tk = _pick(wc, (512, 384, 256, 128))
    nk = wc // tk

    def body(dm_ref, w_ref, c_ref, o_ref, acc_ref):
        i, j = pl.program_id(0), pl.program_id(1)

        @pl.when(jnp.logical_and(i == 0, j == 0))
        def _():
            acc_ref[...] = jnp.zeros_like(acc_ref)

        acc_ref[...] += lax.dot_general(dm_ref[...].astype(BF16), w_ref[...].astype(BF16), (((1,), (1,)), ((), ())),
                                        preferred_element_type=F32)

        @pl.when(jnp.logical_and(i == l - 1, j == nk - 1))
        def _():
            mine = jnp.where(lax.axis_index("c") == 0, 1.0, 0.0)
            o_ref[...] = acc_ref[8:9, :] * _silu_grad(c_ref[...]) * mine

    return pl.pallas_call(
        body, name=name, grid=(l, nk),
        in_specs=[pl.BlockSpec((None, 16, tk), lambda i, j: (i, 0, j)), pl.BlockSpec((None, d, tk), lambda i, j: (i, 0, j)), _full_spec((1, d))],
        out_specs=_full_spec((1, d)), out_shape=jax.ShapeDtypeStruct((1, d), F32), scratch_shapes=[pltpu.VMEM((16, d), F32)],
        compiler_params=_params(("arbitrary", "arbitrary")))(dm_sh, w_mod, c_ctx)


def _dmod_assemble(gath, name):
    _, l, _, w = gath.shape
    gath = gath.transpose(1, 2, 0, 3)
    tc = _pick(w, (2048, 1024, 512, 256, 128))

    def body(lat_ref, ctx_ref, o_ref, b_ref):
        ctx = ctx_ref[0:1, :]
        for dev in range(1, 8):
            ctx = ctx + ctx_ref[dev:dev + 1, :]
        lat = lat_ref[...]
        o_ref[0:8, :] = lat
        o_ref[8:9, :] = ctx
        o_ref[9:16, :] = jnp.zeros((7, tc), F32)
        b_ref[...] = jnp.sum(lat, axis=0, keepdims=True) + ctx

    return pl.pallas_call(
        body, name=name, grid=(l, w // tc),
        in_specs=[pl.BlockSpec((None, None, 8, tc), lambda i, j: (i, 0, 0, j)), pl.BlockSpec((None, None, 8, tc), lambda i, j: (i, 1, 0, j))],
        out_specs=[pl.BlockSpec((None, 16, tc), lambda i, j: (i, 0, j)), pl.BlockSpec((None, 1, tc), lambda i, j: (i, 0, j))],
        out_shape=[jax.ShapeDtypeStruct((l, 16, w), F32), jax.ShapeDtypeStruct((l, 1, w), F32)],
        compiler_params=_params(("parallel", "parallel")))(gath, gath)


SMALL = ("c_ctx", "g_mix", "g_ffn", "mla_q_a_norm", "mla_kv_a_norm", "mla_q_nope_norm", "mla_q_rope_norm", "mla_k_nope_norm",
         "mla_k_rope_norm", "pool_w", "pool_scale", "swa_q_norm", "swa_k_norm", "swa_sink", "na_q_norm", "na_k_norm", "na_rpb",
         "ffn_conv_b")
PACK_W = 512
PACK_Q = 8 * PACK_W


def _pack(arrs):
    flat = []
    for a in arrs:
        f = a.reshape(-1)
        flat.append(jnp.pad(f, (0, (-f.size) % PACK_Q)))
    return jnp.concatenate(flat).reshape(-1, PACK_W)


def _unpack(packed, shapes):
    flat, out, o = packed.reshape(-1), [], 0
    for s in shapes:
        n = 1
        for dim in s:
            n *= dim
        out.append(flat[o:o + n].reshape(s))
        o += n + (-n) % PACK_Q
    return out


def _all_sum(p, name):
    pair = _pair_up([p], name + "_pair")[0]
    chip = _sum_lead(pair, name + "_sum2")
    return _sum_lead(_chip_gather([chip], ["lead"], name + "_gather")[0], name + "_sum4")


WEIGHTS = ("c_ctx", "w_mod", "b_mod", "g_mix", "g_ffn", "w_in", "w_out", "mla_q_a_norm", "mla_w_qb", "mla_kv_a_norm", "mla_w_kvb",
           "mla_q_nope_norm", "mla_q_rope_norm", "mla_k_nope_norm", "mla_k_rope_norm", "pool_w", "pool_scale", "swa_q_norm",
           "swa_k_norm", "swa_sink", "na_q_norm", "na_k_norm", "na_rpb", "ffn_w_up", "ffn_conv_w", "ffn_conv_b", "ffn_w_down")
BIG = ("w_in", "mla_w_qb", "w_out", "ffn_w_down", "ffn_w_up", "mla_w_kvb")
BIG_KINDS = ("cm", "cm", "cm", "cm", "lb", "lb")


def _step(a):
    x, c, ctx = a["x"], a["c"], a["ctx"]
    n_lat, d = x.shape[1], x.shape[2]
    n_ctx = ctx.shape[1]
    l = DEPTH
    px, py, pc = lax.axis_index("x"), lax.axis_index("y"), lax.axis_index("c")
    chip = 2 * px + py
    cvec = pc.reshape(1).astype(I32)

    c_all = _chip_gather(_pair_up([c], "c_pair"), ["lead"], "c_gather")[0].reshape(8, d)
    cs16 = jnp.concatenate([c_all, a["c_ctx"][None, :], jnp.zeros((7, d), F32)], axis=0)
    wc = a["w_mod"].shape[-1]
    b_sh = lax.dynamic_slice_in_dim(a["b_mod"], chip * wc, wc, axis=1)[:, None, :]
    mod_sh = _mod_fwd(cs16, a["w_mod"], b_sh, "mod_fwd")
    mod_all, conv_w_full = _chip_gather([mod_sh, a["ffn_conv_w"]], ["lead", "lane"], "mod_gather")
    mod_all = mod_all.transpose(1, 2, 0, 3).reshape(l, 16, 4 * wc)
    mods = jnp.stack([lax.dynamic_index_in_dim(mod_all, 2 * chip + pc, axis=1, keepdims=False), mod_all[:, 8]], axis=1)
    mods = mods.reshape(l, 2, 6, d)

    own = 2 * pc
    shards = [lax.dynamic_slice_in_dim(a[k].astype(BF16), own, 2, axis=0) for k in BIG]
    gathered = _chip_gather(shards, ["axis1", "axis1", "axis1", "axis1", "lane", "lane"], "w_gather")
    paired = _pair_up(gathered, "w_pair")
    w_in_g, w_qb_g, w_out_g, w_down_g, w_up_g, w_kvb_g = [p.reshape((l,) + p.shape[2:]) for p in paired]
    full = {k: a[k] for k in SMALL if k != "c_ctx"}
    full.update(
        w_in=w_in_g.transpose(0, 2, 1, 3).reshape(l, d, P_COLS), mla_w_qb=w_qb_g.transpose(0, 2, 1, 3).reshape(l, 512, 768),
        w_out=w_out_g.reshape(l, -1, d), ffn_w_down=w_down_g.reshape(l, -1, d), ffn_w_up=w_up_g, mla_w_kvb=w_kvb_g,
        ffn_conv_w=conv_w_full)
    w = _kernel_weights(full, mods)
    tab = _rope_table(n_lat, n_ctx)

    x_all = jnp.concatenate([x[0], ctx[0]], axis=0)
    loss, dx, grads = _local_step(x_all, a["loss_target"][0], w, tab, n_lat)
    loss = lax.psum(loss, ("x", "y", "c"))

    sums = []
    for li in range(l):
        g = grads[li]
        ops = [_w_in_unlayout(g["w_in"]).reshape(d, 4, -1).transpose(1, 0, 2),
               _w_qb_unlayout(g["w_qb"]).reshape(512, 4, 192).transpose(1, 0, 2),
               g["w_out"].reshape(4, -1, d), g["w_down"].reshape(4, -1, d), g["w_up"], g["w_kvb"]]
        recv = _half_swap(ops, BIG_KINDS, "g_swap")
        halves = [_half_add(o, r, kd, cvec, "g_half_add") for o, r, kd in zip(ops, recv, BIG_KINDS)]
        landed = _chip_scatter(halves, "g_scatter")
        sums.append([_sum_lead(s, "g_sum4") for s in landed])
    g_big = _pair_layers([[sums[li][j] for li in range(l)] for j in range(len(BIG))], "g_pair")
    g_out = {k: g.reshape(a[k].shape) for k, g in zip(BIG, g_big)}

    dmods = jnp.stack([grads[li]["mods"] for li in range(l)], axis=0).reshape(l, 2, 6 * d)
    dm_gath = _chip_gather(_pair_up([dmods], "dmod_pair"), ["lead"], "dmod_gather")[0].reshape(8, l, 2, 6 * d)
    dmod_all, g_b_mod = _dmod_assemble(dm_gath, "dmod_assemble")
    dm_sh = lax.dynamic_slice_in_dim(dmod_all, chip * wc, wc, axis=2)
    g_out["w_mod"] = _mod_dw(cs16, dm_sh, "mod_dw")
    g_out["b_mod"] = g_b_mod.reshape(l, 6 * d)
    g_c_ctx = _mod_dc(dm_sh, a["w_mod"], a["c_ctx"][None, :], "mod_dc")

    rg = _reference_grads(grads)
    rg["c_ctx"] = g_c_ctx[0]
    packed = _all_sum(_pack([rg[k] for k in SMALL] + [rg["ffn_conv_w"]]), "small")
    small_g = _unpack(packed, [a[k].shape for k in SMALL] + [rg["ffn_conv_w"].shape])
    for k, g in zip(SMALL, small_g[:-1]):
        g_out[k] = g
    cw = a["ffn_conv_w"].shape[-1]
    g_out["ffn_conv_w"] = lax.dynamic_slice_in_dim(small_g[-1], chip * cw, cw, axis=2)

    upd = {}
    pk = lambda pre: _pack([a[pre + k] for k in SMALL])
    outs = _adamw(pk(""), _pack([g_out[k] for k in SMALL]), pk("m_"), pk("v_"), "adamw_small")
    for o, kind in zip(outs, ("delta", "m", "v")):
        for k, val in zip(SMALL, _unpack(o, [a[k].shape for k in SMALL])):
            upd[kind, k] = val
    for k in WEIGHTS:
        if k not in SMALL:
            outs = _adamw(a[k], g_out[k], a["m_" + k], a["v_" + k], "adamw_" + k)
            for o, kind in zip(outs, ("delta", "m", "v")):
                upd[kind, k] = o
    grad_x = dx[0:n_lat].reshape(x.shape)
    return (loss, grad_x, *[g_out[k] for k in WEIGHTS], *[upd["delta", k] for k in WEIGHTS],
            *[upd["m", k] for k in WEIGHTS], *[upd["v", k] for k in WEIGHTS])


def kernel(x, c, ctx, c_ctx, w_mod, b_mod, g_mix, g_ffn, w_in, w_out, mla_q_a_norm, mla_w_qb, mla_kv_a_norm, mla_w_kvb, mla_q_nope_norm, mla_q_rope_norm, mla_k_nope_norm, mla_k_rope_norm, pool_w, pool_scale, swa_q_norm, swa_k_norm, swa_sink, na_q_norm, na_k_norm, na_rpb, ffn_w_up, ffn_conv_w, ffn_conv_b, ffn_w_down, loss_target, m_c_ctx, m_w_mod, m_b_mod, m_g_mix, m_g_ffn, m_w_in, m_w_out, m_mla_q_a_norm, m_mla_w_qb, m_mla_kv_a_norm, m_mla_w_kvb, m_mla_q_nope_norm, m_mla_q_rope_norm, m_mla_k_nope_norm, m_mla_k_rope_norm, m_pool_w, m_pool_scale, m_swa_q_norm, m_swa_k_norm, m_swa_sink, m_na_q_norm, m_na_k_norm, m_na_rpb, m_ffn_w_up, m_ffn_conv_w, m_ffn_conv_b, m_ffn_w_down, v_c_ctx, v_w_mod, v_b_mod, v_g_mix, v_g_ffn, v_w_in, v_w_out, v_mla_q_a_norm, v_mla_w_qb, v_mla_kv_a_norm, v_mla_w_kvb, v_mla_q_nope_norm, v_mla_q_rope_norm, v_mla_k_nope_norm, v_mla_k_rope_norm, v_pool_w, v_pool_scale, v_swa_q_norm, v_swa_k_norm, v_swa_sink, v_na_q_norm, v_na_k_norm, v_na_rpb, v_ffn_w_up, v_ffn_conv_w, v_ffn_conv_b, v_ffn_w_down):
    return _step(dict(locals()))
```

```python
import functools

import jax
import jax.numpy as jnp
from jax import lax
from jax.experimental import pallas as pl
from jax.experimental.pallas import tpu as pltpu

F32 = jnp.float32
BF16 = jnp.bfloat16
I32 = jnp.int32

DEPTH = 4
GRID_W = 64
ROPE_BASE = 10000.0
EPS = 1e-6
NEG = -1e30
MLA_SCALE = 192.0 ** -0.5
HD_SCALE = 64.0 ** -0.5
NA_KROWS = 12
SWA_KEYS = 512
P_COLS = 3648
PW = 3840
TM = 256
HALO = 8
ADAM_LR, ADAM_B1, ADAM_B2, ADAM_EPS, ADAM_WD, ADAM_STEP = 0.001, 0.9, 0.999, 1e-08, 0.01, 10
VMEM_LIMIT = 56 * 1024 * 1024
MESH = pl.DeviceIdType.MESH


def _pick(n, cands):
    for c in cands:
        if n % c == 0:
            return c
    return n


def _params(sem=None):
    return pltpu.CompilerParams(dimension_semantics=sem, vmem_limit_bytes=VMEM_LIMIT)


@jax.custom_vjp
def _bdot(a, b):
    return jnp.dot(a.astype(BF16), b.astype(BF16), preferred_element_type=F32)


def _bdot_fwd(a, b):
    return _bdot(a, b), (a.astype(BF16), b.astype(BF16))


def _bdot_bwd(res, g):
    a, b = res
    gb = g.astype(BF16)
    da = lax.dot_general(gb, b, (((1,), (1,)), ((), ())), preferred_element_type=F32)
    db = lax.dot_general(a, gb, (((0,), (0,)), ((), ())), preferred_element_type=F32)
    return da, db


_bdot.defvjp(_bdot_fwd, _bdot_bwd)


@jax.custom_vjp
def _bdot_nt(a, b):
    return lax.dot_general(a.astype(BF16), b.astype(BF16), (((1,), (1,)), ((), ())), preferred_element_type=F32)


def _bdot_nt_fwd(a, b):
    return _bdot_nt(a, b), (a.astype(BF16), b.astype(BF16))


def _bdot_nt_bwd(res, g):
    a, b = res
    gb = g.astype(BF16)
    da = jnp.dot(gb, b, preferred_element_type=F32)
    db = lax.dot_general(gb, a, (((0,), (0,)), ((), ())), preferred_element_type=F32)
    return da, db


_bdot_nt.defvjp(_bdot_nt_fwd, _bdot_nt_bwd)


def _rms(x, g):
    return x * lax.rsqrt(jnp.mean(x * x, axis=-1, keepdims=True) + EPS) * g


def _rope(x, cos, sin):
    xr = jnp.concatenate([-x[:, 16:32], x[:, 0:16], -x[:, 48:64], x[:, 32:48]], axis=-1)
    return x * cos + xr * sin


def _sel(is_ctx, mod, row):
    return jnp.where(is_ctx, mod[1, row:row + 1, :], mod[0, row:row + 1, :])


MM_VMEM_BUDGET = 40 * 1024 * 1024
HBM_BYTES_PER_STEP = 1 << 20


def _mm_tiles(m, n, k, mode, osize, n_unit, k_unit):
    lanes = (3840, 2816, 2048, 1280, 1024, 768, 512, 256)
    subl = (4352, 2176, 1088, 1024, 640, 544, 512, 256)
    tms = [c for c in (lanes if mode == "tn" else subl) if m % c == 0] or [m]
    tns = [c for c in lanes if n_unit % c == 0] or [n_unit]
    tks = [c for c in (subl if mode == "tn" else lanes) if k_unit % c == 0]
    if k_unit == k:
        tks = [k] + tks
    best = None
    for tm in tms:
        for tn in tns:
            for tk in tks:
                nk = k // tk
                vmem = 4 * (tm * tk + tk * tn) + 2 * tm * tn * osize + tm * tn * 4
                if vmem > MM_VMEM_BUDGET:
                    continue
                a_reads = 1 if nk == 1 else n // tn
                b_reads = 1 if (nk == 1 and n == tn) else m // tm
                steps = (m // tm) * (n // tn) * nk
                cost = (2 * m * k * a_reads + 2 * k * n * b_reads + m * n * osize + steps * HBM_BYTES_PER_STEP
                        + (12 * m * n * nk if nk > 1 else 0))
                if best is None or cost < best[0]:
                    best = (cost, tm, tn, tk)
    return best[1:]


def _mm(a, b, mode, out_dtype, name, a_split=False, b_split=False, o_split=False):
    def dims(x, split):
        return (x.shape[1], 2 * x.shape[2]) if split else x.shape

    ar, ac = dims(a, a_split)
    br, bc = dims(b, b_split)
    if mode == "nn":
        m, k, n = ar, ac, bc
        assert br == k
    elif mode == "nt":
        m, k, n = ar, ac, br
        assert bc == k
    else:
        k, m, n = ar, ac, bc
        assert br == k
    n_unit = n // 2 if (o_split or (b_split and mode != "nt")) else n
    k_unit = k // 2 if (mode != "tn" and (a_split or (b_split and mode == "nt"))) else k
    tm, tn, tk = _mm_tiles(m, n, k, mode, jnp.dtype(out_dtype).itemsize, n_unit, k_unit)
    nk = k // tk

    def spec(split, tr, tc, ncols, ridx, cidx):
        if not split:
            return pl.BlockSpec((tr, tc), lambda i, j, kk: (ridx(i, j, kk), cidx(i, j, kk)))
        nh = (ncols // 2) // tc
        return pl.BlockSpec((None, tr, tc), lambda i, j, kk: (cidx(i, j, kk) // nh, ridx(i, j, kk), cidx(i, j, kk) % nh))

    gi = lambda i, j, kk: i
    gj = lambda i, j, kk: j
    gk = lambda i, j, kk: kk
    if mode == "tn":
        a_spec = spec(a_split, tk, tm, ac, gk, gi)
    else:
        a_spec = spec(a_split, tm, tk, ac, gi, gk)
    if mode == "nt":
        b_spec = spec(b_split, tn, tk, bc, gj, gk)
    else:
        b_spec = spec(b_split, tk, tn, bc, gk, gj)
    o_spec = spec(o_split, tm, tn, n, gi, gj)
    dn = {"nn": (((1,), (0,)), ((), ())), "nt": (((1,), (1,)), ((), ())), "tn": (((0,), (0,)), ((), ()))}[mode]

    def body(a_ref, b_ref, o_ref, acc_ref):
        kk = pl.program_id(2)

        @pl.when(kk == 0)
        def _():
            acc_ref[...] = jnp.zeros_like(acc_ref)

        acc_ref[...] += lax.dot_general(a_ref[...], b_ref[...], dn, preferred_element_type=F32)

        @pl.when(kk == nk - 1)
        def _():
            o_ref[...] = acc_ref[...].astype(o_ref.dtype)

    def body_whole_k(a_ref, b_ref, o_ref):
        o_ref[...] = lax.dot_general(a_ref[...], b_ref[...], dn, preferred_element_type=F32).astype(o_ref.dtype)

    oshape = (2, m, n // 2) if o_split else (m, n)
    return pl.pallas_call(
        body if nk > 1 else body_whole_k, name=name, grid=(m // tm, n // tn, nk), in_specs=[a_spec, b_spec], out_specs=o_spec,
        out_shape=jax.ShapeDtypeStruct(oshape, out_dtype), scratch_shapes=[pltpu.VMEM((tm, tn), F32)] if nk > 1 else [],
        compiler_params=_params(("parallel", "parallel", "arbitrary")))(a, b)


def _mm_exact(a, b, name):
    def body(a_ref, b_ref, o_ref):
        o_ref[...] = jnp.dot(a_ref[...], b_ref[...], preferred_element_type=F32, precision=lax.Precision.HIGHEST)

    return pl.pallas_call(body, name=name, out_shape=jax.ShapeDtypeStruct((a.shape[0], b.shape[1]), F32),
                          compiler_params=_params())(a, b)


def _row_spec(width, col=0):
    return pl.BlockSpec((TM, width), lambda i: (i, col))


def _full_spec(shape):
    nd = len(shape)
    return pl.BlockSpec(shape, lambda *_: (0,) * nd)


def _normmod_fn(x, g, mod, is_ctx, row):
    return _rms(x, g) * (1.0 + _sel(is_ctx, mod, row + 1)) + _sel(is_ctx, mod, row)


def _normmod_fwd(x, g, mod, row, nbl, name):
    r, d = x.shape

    def body(x_ref, g_ref, mod_ref, h_ref):
        is_ctx = pl.program_id(0) >= nbl
        h_ref[...] = _normmod_fn(x_ref[...], g_ref[...], mod_ref[...], is_ctx, row).astype(BF16)

    return pl.pallas_call(
        body, name=name, grid=(r // TM,), in_specs=[_row_spec(d), _full_spec((1, d)), _full_spec(mod.shape)],
        out_specs=_row_spec(d), out_shape=jax.ShapeDtypeStruct((r, d), BF16), compiler_params=_params(("parallel",)))(x, g, mod)


def _normmod_bwd(x, g, mod, dh, dx_in, row, nbl, name):
    r, d = x.shape

    def body(x_ref, g_ref, mod_ref, dh_ref, dxin_ref, dx_ref, dg_ref, dmod_ref):
        i = pl.program_id(0)
        is_ctx = i >= nbl

        @pl.when(i == 0)
        def _():
            dg_ref[...] = jnp.zeros_like(dg_ref)
            dmod_ref[...] = jnp.zeros_like(dmod_ref)

        _, vjp = jax.vjp(lambda xx, gg, mm: _normmod_fn(xx, gg, mm, is_ctx, row), x_ref[...], g_ref[...], mod_ref[...])
        dx, dg, dmod = vjp(dh_ref[...])
        dx_ref[...] = dxin_ref[...] + dx
        dg_ref[...] += dg
        dmod_ref[...] += dmod

    return pl.pallas_call(
        body, name=name, grid=(r // TM,),
        in_specs=[_row_spec(d), _full_spec((1, d)), _full_spec(mod.shape), _row_spec(d), _row_spec(d)],
        out_specs=[_row_spec(d), _full_spec((1, d)), _full_spec(mod.shape)],
        out_shape=[jax.ShapeDtypeStruct((r, d), F32), jax.ShapeDtypeStruct((1, d), F32), jax.ShapeDtypeStruct(mod.shape, F32)],
        compiler_params=_params(("arbitrary",)))(x, g, mod, dh, dx_in)


def _resid_fwd(x, y, mod, row, nbl, name):
    r, d = x.shape

    def body(x_ref, y_ref, mod_ref, o_ref):
        is_ctx = pl.program_id(0) >= nbl
        o_ref[...] = x_ref[...] + _sel(is_ctx, mod_ref[...], row) * y_ref[...]

    return pl.pallas_call(
        body, name=name, grid=(r // TM,), in_specs=[_row_spec(d), _row_spec(d), _full_spec(mod.shape)],
        out_specs=_row_spec(d), out_shape=jax.ShapeDtypeStruct((r, d), F32), compiler_params=_params(("parallel",)))(x, y, mod)


def _resid_bwd(dx, y, mod, row, nbl, name):
    r, d = dx.shape

    def body(dx_ref, y_ref, mod_ref, dy_ref, dmod_ref):
        i = pl.program_id(0)
        is_ctx = i >= nbl

        @pl.when(i == 0)
        def _():
            dmod_ref[...] = jnp.zeros_like(dmod_ref)

        dxv = dx_ref[...]
        dy_ref[...] = (_sel(is_ctx, mod_ref[...], row) * dxv).astype(BF16)
        dgate = jnp.sum(dxv * y_ref[...], axis=0, keepdims=True)

        @pl.when(is_ctx)
        def _():
            dmod_ref[1, row:row + 1, :] += dgate

        @pl.when(jnp.logical_not(is_ctx))
        def _():
            dmod_ref[0, row:row + 1, :] += dgate

    return pl.pallas_call(
        body, name=name, grid=(r // TM,), in_specs=[_row_spec(d), _row_spec(d), _full_spec(mod.shape)],
        out_specs=[_row_spec(d), _full_spec(mod.shape)],
        out_shape=[jax.ShapeDtypeStruct((r, d), BF16), jax.ShapeDtypeStruct(mod.shape, F32)],
        compiler_params=_params(("arbitrary",)))(dx, y, mod)


def _loss_kernel(x, target, nbl, name):
    r, d = x.shape

    def body(x_ref, t_ref, loss_ref, dx_ref):
        i = pl.program_id(0)

        @pl.when(i == 0)
        def _():
            loss_ref[...] = jnp.zeros_like(loss_ref)

        @pl.when(i < nbl)
        def _():
            e = x_ref[...] - t_ref[...]
            dx_ref[...] = e / d
            loss_ref[...] += 0.5 * jnp.sum(jnp.mean(e * e, axis=-1, keepdims=True), axis=0, keepdims=True)

        @pl.when(i >= nbl)
        def _():
            dx_ref[...] = jnp.zeros_like(dx_ref)

    return pl.pallas_call(
        body, name=name, grid=(r // TM,),
        in_specs=[_row_spec(d), pl.BlockSpec((TM, d), lambda i: (jnp.minimum(i, nbl - 1), 0))],
        out_specs=[_full_spec((1, 1)), _row_spec(d)],
        out_shape=[jax.ShapeDtypeStruct((1, 1), F32), jax.ShapeDtypeStruct((r, d), F32)],
        compiler_params=_params(("arbitrary",)))(x, target)


SP_QA, SP_KVA, SP_QN, SP_QR, SP_KN, SP_KR, SP_SQ, SP_SK, SP_NQ, SP_NK, SP_SINK = range(11)
C_CQ, C_CKV, C_KR, C_POOL, C_SQ, C_SK, C_SV, C_NQ, C_NK, C_NV = 0, 512, 768, 896, 1408, 1920, 2048, 2176, 2688, 3200


def _prep_fn(p, tab, sp, wqb, wkvb):
    cos, sin = tab[:, 0:64], tab[:, 64:128]
    q = _bdot(_rms(p[:, C_CQ:C_CQ + 512], sp[SP_QA:SP_QA + 1, 0:512]), wqb)
    kv = _bdot(_rms(p[:, C_CKV:C_CKV + 256], sp[SP_KVA:SP_KVA + 1, 0:256]), wkvb)
    krr = _rope(_rms(p[:, C_KR:C_KR + 64], sp[SP_KR:SP_KR + 1, 0:64]), cos, sin)
    zero = jnp.zeros_like(krr)
    aq, ak, av = [], [], []
    for h in range(4):
        qn = _rms(q[:, 128 * h:128 * h + 128], sp[SP_QN:SP_QN + 1, 0:128])
        qr = _rope(_rms(q[:, 512 + 64 * h:576 + 64 * h], sp[SP_QR:SP_QR + 1, 0:64]), cos, sin)
        kn = _rms(kv[:, 256 * h:256 * h + 128], sp[SP_KN:SP_KN + 1, 0:128])
        aq += [qn, qr, zero]
        ak += [kn, krr, zero]
        av.append(kv[:, 256 * h + 128:256 * h + 256])
    cq = [_rope(_rms(p[:, C_SQ + 64 * h:C_SQ + 64 * h + 64], sp[SP_SQ:SP_SQ + 1, 0:64]), cos, sin) for h in range(8)]
    ck = [_rope(_rms(p[:, C_SK + 64 * h:C_SK + 64 * h + 64], sp[SP_SK:SP_SK + 1, 0:64]), cos, sin) for h in range(2)]
    dq = [_rms(p[:, C_NQ + 64 * h:C_NQ + 64 * h + 64], sp[SP_NQ:SP_NQ + 1, 0:64]) for h in range(8)]
    dk = [_rms(p[:, C_NK + 64 * h:C_NK + 64 * h + 64], sp[SP_NK:SP_NK + 1, 0:64]) for h in range(8)]
    cat = lambda xs: jnp.concatenate(xs, axis=-1)
    return (cat(aq), cat(ak), cat(av), cat(cq), cat(ck), p[:, C_SV:C_SV + 128], cat(dq), cat(dk), p[:, C_NV:C_NV + 512])


PREP_WIDTHS = (1024, 1024, 512, 512, 128, 128, 512, 512, 512)


def _prep_fwd(p, tab, sp, wqb, wkvb, name):
    r = p.shape[0]

    def body(p_ref, tab_ref, sp_ref, wqb_ref, wkvb_ref, *outs):
        res = _prep_fn(p_ref[...], tab_ref[...], sp_ref[...], wqb_ref[...].astype(F32), wkvb_ref[...].astype(F32))
        for o_ref, v in zip(outs, res):
            o_ref[...] = v.astype(BF16)

    return pl.pallas_call(
        body, name=name, grid=(r // TM,),
        in_specs=[_row_spec(PW), _row_spec(128), _full_spec(sp.shape), _full_spec(wqb.shape), _full_spec(wkvb.shape)],
        out_specs=[_row_spec(w) for w in PREP_WIDTHS],
        out_shape=[jax.ShapeDtypeStruct((r, w), BF16) for w in PREP_WIDTHS],
        compiler_params=_params(("parallel",)))(p, tab, sp, wqb, wkvb)


def _prep_bwd(p, tab, sp, wqb, wkvb, cots, dpool, name):
    r = p.shape[0]

    def body(p_ref, tab_ref, sp_ref, wqb_ref, wkvb_ref, *rest):
        cot_refs, dpool_ref = rest[:9], rest[9]
        dp_ref, dsp_ref, dwqb_ref, dwkvb_ref = rest[10:]
        i = pl.program_id(0)

        @pl.when(i == 0)
        def _():
            dsp_ref[...] = jnp.zeros_like(dsp_ref)
            dwqb_ref[...] = jnp.zeros_like(dwqb_ref)
            dwkvb_ref[...] = jnp.zeros_like(dwkvb_ref)

        tab = tab_ref[...]
        _, vjp = jax.vjp(lambda pp, ss, wq, wk: _prep_fn(pp, tab, ss, wq, wk),
                         p_ref[...], sp_ref[...], wqb_ref[...].astype(F32), wkvb_ref[...].astype(F32))
        dp, dsp, dwq, dwk = vjp(tuple(c[...] for c in cot_refs))
        dp_ref[...] = dp.astype(BF16)
        dp_ref[:, C_POOL:C_POOL + 512] = dpool_ref[...].astype(BF16)
        dsp_ref[...] += dsp
        dwqb_ref[...] += dwq
        dwkvb_ref[...] += dwk

    return pl.pallas_call(
        body, name=name, grid=(r // TM,),
        in_specs=[_row_spec(PW), _row_spec(128), _full_spec(sp.shape), _full_spec(wqb.shape), _full_spec(wkvb.shape)]
        + [_row_spec(w) for w in PREP_WIDTHS] + [_row_spec(512)],
        out_specs=[_row_spec(PW), _full_spec(sp.shape), _full_spec(wqb.shape), _full_spec(wkvb.shape)],
        out_shape=[jax.ShapeDtypeStruct((r, PW), BF16), jax.ShapeDtypeStruct(sp.shape, F32),
                   jax.ShapeDtypeStruct(wqb.shape, F32), jax.ShapeDtypeStruct(wkvb.shape, F32)],
        compiler_params=_params(("arbitrary",)))(p, tab, sp, wqb, wkvb, *cots, dpool)


def _mla_probs(q, k, is_ctx, n_lat):
    s = lax.dot_general(q, k, (((1,), (1,)), ((), ())), preferred_element_type=F32) * MLA_SCALE
    kid = lax.broadcasted_iota(I32, s.shape, 1)
    s = jnp.where(jnp.logical_and(is_ctx, kid < n_lat), NEG, s)
    e = jnp.exp(s - jnp.max(s, axis=-1, keepdims=True))
    return e / jnp.sum(e, axis=-1, keepdims=True)


def _mla_fwd(aq, ak, av, n_lat, name):
    r = aq.shape[0]
    nbl = n_lat // TM

    def body(q_ref, k_ref, v_ref, o_ref):
        p = _mla_probs(q_ref[...], k_ref[...], pl.program_id(1) >= nbl, n_lat)
        o_ref[...] = jnp.dot(p.astype(BF16), v_ref[...], preferred_element_type=F32).astype(BF16)

    return pl.pallas_call(
        body, name=name, grid=(4, r // TM),
        in_specs=[pl.BlockSpec((TM, 256), lambda h, i: (i, h)), pl.BlockSpec((r, 256), lambda h, i: (0, h)),
                  pl.BlockSpec((r, 128), lambda h, i: (0, h))],
        out_specs=pl.BlockSpec((TM, 128), lambda h, i: (i, h)),
        out_shape=jax.ShapeDtypeStruct((r, 512), BF16), compiler_params=_params(("parallel", "parallel")))(aq, ak, av)


def _mla_bwd(aq, ak, av, dmix, n_lat, name):
    r = aq.shape[0]
    nbl = n_lat // TM

    def body(q_ref, k_ref, v_ref, do_ref, dq_ref, dk_ref, dv_ref):
        i = pl.program_id(1)

        @pl.when(i == 0)
        def _():
            dk_ref[...] = jnp.zeros_like(dk_ref)
            dv_ref[...] = jnp.zeros_like(dv_ref)

        q, k, v = q_ref[...], k_ref[...], v_ref[...]
        dob = do_ref[...].astype(BF16)
        p = _mla_probs(q, k, i >= nbl, n_lat)
        dv_ref[...] += lax.dot_general(p.astype(BF16), dob, (((0,), (0,)), ((), ())), preferred_element_type=F32)
        dp = lax.dot_general(dob, v, (((1,), (1,)), ((), ())), preferred_element_type=F32)
        ds = (p * (dp - jnp.sum(dp * p, axis=-1, keepdims=True)) * MLA_SCALE).astype(BF16)
        dq_ref[...] = jnp.dot(ds, k, preferred_element_type=F32)
        dk_ref[...] += lax.dot_general(ds, q, (((0,), (0,)), ((), ())), preferred_element_type=F32)

    return pl.pallas_call(
        body, name=name, grid=(4, r // TM),
        in_specs=[pl.BlockSpec((TM, 256), lambda h, i: (i, h)), pl.BlockSpec((r, 256), lambda h, i: (0, h)),
                  pl.BlockSpec((r, 128), lambda h, i: (0, h)), pl.BlockSpec((TM, 128), lambda h, i: (i, h))],
        out_specs=[pl.BlockSpec((TM, 256), lambda h, i: (i, h)), pl.BlockSpec((r, 256), lambda h, i: (0, h)),
                   pl.BlockSpec((r, 128), lambda h, i: (0, h))],
        out_shape=[jax.ShapeDtypeStruct((r, 1024), F32), jax.ShapeDtypeStruct((r, 1024), F32), jax.ShapeDtypeStruct((r, 512), F32)],
        compiler_params=_params(("parallel", "arbitrary")))(aq, ak, av, dmix)


def _pool_fn(ext, w, sc, gid0, grp, is_ctx, n_lat, r_all):
    gid = gid0 + lax.broadcasted_iota(I32, (TM + 2 * HALO, 1), 0)
    lo = jnp.where(is_ctx, n_lat, 0)
    hi = jnp.where(is_ctx, r_all, n_lat)
    z = jnp.where(jnp.logical_and(gid >= lo, gid < hi), ext, 0.0)
    w2 = jnp.roll(z, 1, axis=0) + z
    w4 = jnp.roll(w2, 1, axis=0) + jnp.roll(w2, -1, axis=0)
    w8 = jnp.roll(w4, 2, axis=0) + jnp.roll(w4, -2, axis=0)
    w16 = jnp.roll(w8, 4, axis=0) + jnp.roll(w8, -4, axis=0)
    win = jnp.where(grp == 0, w2, jnp.where(grp == 1, w4, jnp.where(grp == 2, w8, w16)))
    half = jnp.left_shift(1, grp)
    cnt = jnp.maximum(jnp.minimum(gid + half, hi) - jnp.maximum(gid - half, lo), 1).astype(F32)
    d = (win / cnt - ext)[HALO:HALO + TM]
    return _bdot(d, w) * sc


def _pool_ext(u_ref, i, r_all):
    s0 = pl.multiple_of(jnp.maximum(i * TM - HALO, 0), HALO)
    s2 = pl.multiple_of(jnp.minimum(i * TM + TM, r_all - HALO), HALO)
    ext = jnp.concatenate([u_ref[pl.ds(s0, HALO), :], u_ref[pl.ds(pl.multiple_of(i * TM, TM), TM), :], u_ref[pl.ds(s2, HALO), :]], axis=0)
    return ext, s0, s2


def _pool_specs(r):
    return [pl.BlockSpec((r, 128), lambda g, i: (0, C_POOL // 128 + g)), pl.BlockSpec((None, 128, 128), lambda g, i: (g, 0, 0)),
            pl.BlockSpec((None, 1, 128), lambda g, i: (g, 0, 0))]


def _pool_fwd(p, pool_w, pool_sc, n_lat, name):
    r = p.shape[0]
    nbl = n_lat // TM

    def body(u_ref, w_ref, sc_ref, o_ref):
        g, i = pl.program_id(0), pl.program_id(1)
        ext, _, _ = _pool_ext(u_ref, i, r)
        o_ref[...] = _pool_fn(ext, w_ref[...], sc_ref[...], i * TM - HALO, g, i >= nbl, n_lat, r).astype(BF16)

    return pl.pallas_call(
        body, name=name, grid=(4, r // TM), in_specs=_pool_specs(r), out_specs=pl.BlockSpec((TM, 128), lambda g, i: (i, g)),
        out_shape=jax.ShapeDtypeStruct((r, 512), BF16), compiler_params=_params(("parallel", "parallel")))(p, pool_w, pool_sc)


def _pool_bwd(p, pool_w, pool_sc, dmix, n_lat, name):
    r = p.shape[0]
    nbl = n_lat // TM

    def body(u_ref, w_ref, sc_ref, do_ref, du_ref, dw_ref, dsc_ref):
        g, i = pl.program_id(0), pl.program_id(1)

        @pl.when(i == 0)
        def _():
            du_ref[...] = jnp.zeros_like(du_ref)
            dw_ref[...] = jnp.zeros_like(dw_ref)
            dsc_ref[...] = jnp.zeros_like(dsc_ref)

        ext, s0, s2 = _pool_ext(u_ref, i, r)
        _, vjp = jax.vjp(lambda e, w, s: _pool_fn(e, w, s, i * TM - HALO, g, i >= nbl, n_lat, r), ext, w_ref[...], sc_ref[...])
        dext, dw, dsc = vjp(do_ref[...])
        du_ref[pl.ds(s0, HALO), :] += dext[0:HALO]
        du_ref[pl.ds(pl.multiple_of(i * TM, TM), TM), :] += dext[HALO:HALO + TM]
        du_ref[pl.ds(s2, HALO), :] += dext[HALO + TM:]
        dw_ref[...] += dw
        dsc_ref[...] += dsc

    return pl.pallas_call(
        body, name=name, grid=(4, r // TM), in_specs=_pool_specs(r) + [pl.BlockSpec((TM, 128), lambda g, i: (i, 4 + g))],
        out_specs=[pl.BlockSpec((r, 128), lambda g, i: (0, g)), pl.BlockSpec((None, 128, 128), lambda g, i: (g, 0, 0)),
                   pl.BlockSpec((None, 1, 128), lambda g, i: (g, 0, 0))],
        out_shape=[jax.ShapeDtypeStruct((r, 512), F32), jax.ShapeDtypeStruct((4, 128, 128), F32), jax.ShapeDtypeStruct((4, 1, 128), F32)],
        compiler_params=_params(("parallel", "arbitrary")))(p, pool_w, pool_sc, dmix)


def _softmax_parts(parts, extra=None):
    m = functools.reduce(jnp.maximum, [jnp.max(s, axis=-1, keepdims=True) for s in parts])
    if extra is not None:
        m = jnp.maximum(m, extra)
    m = lax.stop_gradient(m)
    es = [jnp.exp(s - m) for s in parts]
    den = functools.reduce(jnp.add, [jnp.sum(e, axis=-1, keepdims=True) for e in es])
    if extra is not None:
        den = den + jnp.exp(extra - m)
    return [e / den for e in es]


def _swa_fn(q4, kw, vw, kc, vc, sink4, qpos0, kpos0, is_ctx):
    qs = jnp.concatenate([q4[:, 64 * g:64 * g + 64] for g in range(4)], axis=0)
    s_loc = _bdot_nt(qs, kw) * HD_SCALE
    s_ctx = _bdot_nt(qs, kc) * HD_SCALE
    qpos = qpos0 + jnp.bitwise_and(lax.broadcasted_iota(I32, s_loc.shape, 0), TM - 1)
    kpos = kpos0 + lax.broadcasted_iota(I32, s_loc.shape, 1)
    valid = jnp.logical_and(jnp.abs(kpos - qpos) <= 128, jnp.logical_not(is_ctx))
    s_loc = jnp.where(valid, s_loc, NEG)
    sink = jnp.concatenate([jnp.broadcast_to(sink4[:, g:g + 1], (TM, 1)) for g in range(4)], axis=0)
    p_loc, p_ctx = _softmax_parts([s_loc, s_ctx], sink)
    o = _bdot(p_loc, vw) + _bdot(p_ctx, vc)
    return jnp.concatenate([o[TM * g:TM * (g + 1)] for g in range(4)], axis=1)


def _swa_window(i, n_lat):
    return pl.multiple_of(jnp.clip(i * TM - 128, 0, n_lat - SWA_KEYS), 128)


def _swa_fwd(cq, ck, cv, sp, n_lat, name):
    r = cq.shape[0]
    nbl = n_lat // TM

    def body(q_ref, k_ref, v_ref, sp_ref, o_ref):
        i = pl.program_id(0)
        k0 = _swa_window(i, n_lat)
        kw, vw = k_ref[pl.ds(k0, SWA_KEYS), :].astype(F32), v_ref[pl.ds(k0, SWA_KEYS), :].astype(F32)
        kc, vc = k_ref[pl.ds(n_lat, r - n_lat), :].astype(F32), v_ref[pl.ds(n_lat, r - n_lat), :].astype(F32)
        for j in range(2):
            c = slice(64 * j, 64 * j + 64)
            o = _swa_fn(q_ref[:, 256 * j:256 * j + 256].astype(F32), kw[:, c], vw[:, c], kc[:, c], vc[:, c],
                        sp_ref[SP_SINK:SP_SINK + 1, 4 * j:4 * j + 4], i * TM, k0, i >= nbl)
            o_ref[:, 256 * j:256 * j + 256] = o.astype(BF16)

    return pl.pallas_call(
        body, name=name, grid=(r // TM,),
        in_specs=[_row_spec(512), _full_spec((r, 128)), _full_spec((r, 128)), _full_spec(sp.shape)],
        out_specs=_row_spec(512), out_shape=jax.ShapeDtypeStruct((r, 512), BF16), compiler_params=_params(("parallel",)))(cq, ck, cv, sp)


def _swa_bwd(cq, ck, cv, sp, dmix, n_lat, name):
    r = cq.shape[0]
    nbl = n_lat // TM
    nc = r - n_lat

    def body(q_ref, k_ref, v_ref, sp_ref, do_ref, dq_ref, dk_ref, dv_ref, dsp_ref):
        i = pl.program_id(0)

        @pl.when(i == 0)
        def _():
            dk_ref[...] = jnp.zeros_like(dk_ref)
            dv_ref[...] = jnp.zeros_like(dv_ref)
            dsp_ref[...] = jnp.zeros_like(dsp_ref)

        k0 = _swa_window(i, n_lat)
        kw, vw = k_ref[pl.ds(k0, SWA_KEYS), :].astype(F32), v_ref[pl.ds(k0, SWA_KEYS), :].astype(F32)
        kc, vc = k_ref[pl.ds(n_lat, nc), :].astype(F32), v_ref[pl.ds(n_lat, nc), :].astype(F32)
        dkw, dvw, dkc, dvc, dsk = [], [], [], [], []
        for j in range(2):
            c = slice(64 * j, 64 * j + 64)
            _, vjp = jax.vjp(lambda q4, a, b, cc, d, s: _swa_fn(q4, a, b, cc, d, s, i * TM, k0, i >= nbl),
                             q_ref[:, 256 * j:256 * j + 256].astype(F32), kw[:, c], vw[:, c], kc[:, c], vc[:, c],
                             sp_ref[SP_SINK:SP_SINK + 1, 4 * j:4 * j + 4])
            dq4, a, b, cc, d, s = vjp(do_ref[:, 256 * j:256 * j + 256])
            dq_ref[:, 256 * j:256 * j + 256] = dq4
            dkw.append(a), dvw.append(b), dkc.append(cc), dvc.append(d), dsk.append(s)
        cat = lambda xs: jnp.concatenate(xs, axis=1)
        dk_ref[pl.ds(k0, SWA_KEYS), :] += cat(dkw)
        dv_ref[pl.ds(k0, SWA_KEYS), :] += cat(dvw)
        dk_ref[pl.ds(n_lat, nc), :] += cat(dkc)
        dv_ref[pl.ds(n_lat, nc), :] += cat(dvc)
        dsp_ref[SP_SINK:SP_SINK + 1, 0:8] += cat(dsk)

    return pl.pallas_call(
        body, name=name, grid=(r // TM,),
        in_specs=[_row_spec(512), _full_spec((r, 128)), _full_spec((r, 128)), _full_spec(sp.shape), _row_spec(512, 2)],
        out_specs=[_row_spec(512), _full_spec((r, 128)), _full_spec((r, 128)), _full_spec(sp.shape)],
        out_shape=[jax.ShapeDtypeStruct((r, 512), F32), jax.ShapeDtypeStruct((r, 128), F32), jax.ShapeDtypeStruct((r, 128), F32),
                   jax.ShapeDtypeStruct(sp.shape, F32)],
        compiler_params=_params(("arbitrary",)))(cq, ck, cv, sp, dmix)


def _na_fn(q, kw, vw, kc, vc, bias, valid):
    s_loc = jnp.where(valid, _bdot_nt(q, kw) * HD_SCALE + bias, NEG)
    s_ctx = _bdot_nt(q, kc) * HD_SCALE
    p_loc, p_ctx = _softmax_parts([s_loc, s_ctx])
    return _bdot(p_loc, vw) + _bdot(p_ctx, vc)


def _na_geometry(i, n_lat, is_ctx):
    rows = n_lat // GRID_W
    qrow0 = i * (TM // GRID_W)
    krow0 = jnp.clip(qrow0 - 4, 0, rows - NA_KROWS)
    nk = NA_KROWS * GRID_W
    tq = i * TM + lax.broadcasted_iota(I32, (TM, nk), 0)
    tk = krow0 * GRID_W + lax.broadcasted_iota(I32, (TM, nk), 1)
    qr, qc = jnp.right_shift(tq, 6), jnp.bitwise_and(tq, GRID_W - 1)
    kr, kc = jnp.right_shift(tk, 6), jnp.bitwise_and(tk, GRID_W - 1)
    r_lo = jnp.clip(qr - 4, 0, rows - 8)
    c_lo = jnp.clip(qc - 8, 0, GRID_W - 16)
    valid = (kr >= r_lo) & (kr < r_lo + 8) & (kc >= c_lo) & (kc < c_lo + 16) & jnp.logical_not(is_ctx)
    dr = [[jnp.clip(krow0 + kj - (qrow0 + qi) + 7, 0, 14) for kj in range(NA_KROWS)] for qi in range(TM // GRID_W)]
    return pl.multiple_of(krow0 * GRID_W, GRID_W), valid, dr


def _na_bias(t1_ref, hh, dr):
    return jnp.concatenate([jnp.concatenate([t1_ref[hh, dr[qi][kj]] for kj in range(NA_KROWS)], axis=1)
                            for qi in range(TM // GRID_W)], axis=0)


def _na_specs(r):
    return [pl.BlockSpec((TM, 128), lambda pr, i: (i, pr)), pl.BlockSpec((r, 128), lambda pr, i: (0, pr)),
            pl.BlockSpec((r, 128), lambda pr, i: (0, pr)), pl.BlockSpec((2, 16, GRID_W, GRID_W), lambda pr, i: (pr, 0, 0, 0))]


def _na_fwd(dq, dk, dv, t1, n_lat, name):
    r = dq.shape[0]
    nbl = n_lat // TM
    nc = r - n_lat
    nk = NA_KROWS * GRID_W

    def body(q_ref, k_ref, v_ref, t1_ref, o_ref):
        i = pl.program_id(1)
        k0, valid, dr = _na_geometry(jnp.minimum(i, nbl - 1), n_lat, i >= nbl)
        kw, vw = k_ref[pl.ds(k0, nk), :].astype(F32), v_ref[pl.ds(k0, nk), :].astype(F32)
        kc, vc = k_ref[pl.ds(n_lat, nc), :].astype(F32), v_ref[pl.ds(n_lat, nc), :].astype(F32)
        for hh in range(2):
            c = slice(64 * hh, 64 * hh + 64)
            o = _na_fn(q_ref[:, c].astype(F32), kw[:, c], vw[:, c], kc[:, c], vc[:, c], _na_bias(t1_ref, hh, dr), valid)
            o_ref[:, c] = o.astype(BF16)

    return pl.pallas_call(
        body, name=name, grid=(4, r // TM), in_specs=_na_specs(r), out_specs=pl.BlockSpec((TM, 128), lambda pr, i: (i, pr)),
        out_shape=jax.ShapeDtypeStruct((r, 512), BF16), compiler_params=_params(("parallel", "parallel")))(dq, dk, dv, t1)


def _na_bwd(dq, dk, dv, t1, dmix, n_lat, name):
    r = dq.shape[0]
    nbl = n_lat // TM
    nc = r - n_lat
    nk = NA_KROWS * GRID_W

    def body(q_ref, k_ref, v_ref, t1_ref, do_ref, dq_ref, dk_ref, dv_ref, dt1_ref):
        i = pl.program_id(1)

        @pl.when(i == 0)
        def _():
            dk_ref[...] = jnp.zeros_like(dk_ref)
            dv_ref[...] = jnp.zeros_like(dv_ref)
            dt1_ref[...] = jnp.zeros_like(dt1_ref)

        k0, valid, dr = _na_geometry(jnp.minimum(i, nbl - 1), n_lat, i >= nbl)
        kw, vw = k_ref[pl.ds(k0, nk), :].astype(F32), v_ref[pl.ds(k0, nk), :].astype(F32)
        kc, vc = k_ref[pl.ds(n_lat, nc), :].astype(F32), v_ref[pl.ds(n_lat, nc), :].astype(F32)
        dkw, dvw, dkc, dvc = [], [], [], []
        for hh in range(2):
            c = slice(64 * hh, 64 * hh + 64)
            _, vjp = jax.vjp(lambda q, a, b, cc, d, bb: _na_fn(q, a, b, cc, d, bb, valid),
                             q_ref[:, c].astype(F32), kw[:, c], vw[:, c], kc[:, c], vc[:, c], _na_bias(t1_ref, hh, dr))
            dqh, a, b, cc, d, dbias = vjp(do_ref[:, c])
            dq_ref[:, c] = dqh
            dkw.append(a), dvw.append(b), dkc.append(cc), dvc.append(d)
            for qi in range(TM // GRID_W):
                for kj in range(NA_KROWS):
                    dt1_ref[hh, dr[qi][kj]] += dbias[GRID_W * qi:GRID_W * (qi + 1), GRID_W * kj:GRID_W * (kj + 1)]
        cat = lambda xs: jnp.concatenate(xs, axis=1)
        dk_ref[pl.ds(k0, nk), :] += cat(dkw)
        dv_ref[pl.ds(k0, nk), :] += cat(dvw)
        dk_ref[pl.ds(n_lat, nc), :] += cat(dkc)
        dv_ref[pl.ds(n_lat, nc), :] += cat(dvc)

    return pl.pallas_call(
        body, name=name, grid=(4, r // TM), in_specs=_na_specs(r) + [pl.BlockSpec((TM, 128), lambda pr, i: (i, 12 + pr))],
        out_specs=[pl.BlockSpec((TM, 128), lambda pr, i: (i, pr)), pl.BlockSpec((r, 128), lambda pr, i: (0, pr)),
                   pl.BlockSpec((r, 128), lambda pr, i: (0, pr)), pl.BlockSpec((2, 16, GRID_W, GRID_W), lambda pr, i: (pr, 0, 0, 0))],
        out_shape=[jax.ShapeDtypeStruct((r, 512), F32)] * 3 + [jax.ShapeDtypeStruct((8, 16, GRID_W, GRID_W), F32)],
        compiler_params=_params(("parallel", "arbitrary")))(dq, dk, dv, t1, dmix)


def _conv_ext(main_ref, prev_ref, next_ref):
    return jnp.concatenate([prev_ref[...], main_ref[...], next_ref[...]], axis=0)


def _conv_masks(i, nbl, n_lat, r_all):
    gid = i * TM - HALO + lax.broadcasted_iota(I32, (TM + 2 * HALO, 1), 0)
    is_ctx = i >= nbl
    lo = jnp.where(is_ctx, n_lat, 0)
    hi = jnp.where(is_ctx, r_all, n_lat)
    return jnp.logical_and(gid >= lo, gid < hi), gid - 1 >= lo, gid + 1 < hi


def _conv_apply(ext, w, b, has_up, has_dn):
    up = jnp.where(has_up, jnp.roll(ext, 1, axis=0), 0.0)
    dn = jnp.where(has_dn, jnp.roll(ext, -1, axis=0), 0.0)
    return up * w[0:1] + ext * w[1:2] + dn * w[2:3] + b, up, dn


def _conv_in_specs(tc, r):
    nb8 = TM // HALO
    last8 = r // HALO - 1

    def trio(half):
        return [pl.BlockSpec((None, TM, tc), lambda j, i: (half, i, j)),
                pl.BlockSpec((None, HALO, tc), lambda j, i: (half, jnp.maximum(i * nb8 - 1, 0), j)),
                pl.BlockSpec((None, HALO, tc), lambda j, i: (half, jnp.minimum((i + 1) * nb8, last8), j))]

    wb = [pl.BlockSpec((None, 3, tc), lambda j, i: (0, 0, j)), pl.BlockSpec((None, 3, tc), lambda j, i: (1, 0, j)),
          pl.BlockSpec((None, 1, tc), lambda j, i: (0, 0, j)), pl.BlockSpec((None, 1, tc), lambda j, i: (1, 0, j))]
    return trio(0) + trio(1) + wb


def _convgate_fwd(a3, cw, cb, n_lat, name):
    _, r, ff = a3.shape
    nbl = n_lat // TM
    tc = _pick(ff, (512, 256, 128))

    def body(g_ref, gp_ref, gn_ref, v_ref, vp_ref, vn_ref, wg_ref, wv_ref, bg_ref, bv_ref, u_ref):
        _, has_up, has_dn = _conv_masks(pl.program_id(1), nbl, n_lat, r)
        gg, _, _ = _conv_apply(_conv_ext(g_ref, gp_ref, gn_ref), wg_ref[...], bg_ref[...], has_up, has_dn)
        gv, _, _ = _conv_apply(_conv_ext(v_ref, vp_ref, vn_ref), wv_ref[...], bv_ref[...], has_up, has_dn)
        u_ref[...] = (jax.nn.silu(gg) * gv)[HALO:HALO + TM].astype(BF16)

    return pl.pallas_call(
        body, name=name, grid=(ff // tc, r // TM), in_specs=_conv_in_specs(tc, r),
        out_specs=pl.BlockSpec((TM, tc), lambda j, i: (i, j)), out_shape=jax.ShapeDtypeStruct((r, ff), BF16),
        compiler_params=_params(("parallel", "parallel")))(a3, a3, a3, a3, a3, a3, cw, cw, cb, cb)


def _convgate_bwd(a3, cw, cb, du, n_lat, name):
    _, r, ff = a3.shape
    nbl = n_lat // TM
    tc = _pick(ff, (512, 256, 128))
    nb8 = TM // HALO
    last8 = r // HALO - 1

    def body(g_ref, gp_ref, gn_ref, v_ref, vp_ref, vn_ref, wg_ref, wv_ref, bg_ref, bv_ref, du_ref, dup_ref, dun_ref,
             da_ref, dcw_ref, dcb_ref):
        i = pl.program_id(1)

        @pl.when(i == 0)
        def _():
            dcw_ref[...] = jnp.zeros_like(dcw_ref)
            dcb_ref[...] = jnp.zeros_like(dcb_ref)

        inb, has_up, has_dn = _conv_masks(i, nbl, n_lat, r)
        wg, wv = wg_ref[...], wv_ref[...]
        eg, ev = _conv_ext(g_ref, gp_ref, gn_ref), _conv_ext(v_ref, vp_ref, vn_ref)
        gg, ug, dg_ = _conv_apply(eg, wg, bg_ref[...], has_up, has_dn)
        gv, uv, dv_ = _conv_apply(ev, wv, bv_ref[...], has_up, has_dn)
        due = jnp.where(inb, _conv_ext(du_ref, dup_ref, dun_ref), 0.0)
        sg = jax.nn.sigmoid(gg)
        dgg = due * gv * (sg * (1.0 + gg * (1.0 - sg)))
        dgv = due * (gg * sg)
        main = slice(HALO, HALO + TM)
        for h, (dgx, w, ex, upx, dnx) in enumerate(((dgg, wg, eg, ug, dg_), (dgv, wv, ev, uv, dv_))):
            da = dgx * w[1:2] + jnp.roll(dgx, -1, axis=0) * w[0:1] + jnp.roll(dgx, 1, axis=0) * w[2:3]
            da_ref[h] = da[main].astype(BF16)
            dm = dgx[main]
            dcw_ref[h, 0:1, :] += jnp.sum(dm * upx[main], axis=0, keepdims=True)
            dcw_ref[h, 1:2, :] += jnp.sum(dm * ex[main], axis=0, keepdims=True)
            dcw_ref[h, 2:3, :] += jnp.sum(dm * dnx[main], axis=0, keepdims=True)
            dcb_ref[h] += jnp.sum(dm, axis=0, keepdims=True)

    du_specs = [pl.BlockSpec((TM, tc), lambda j, i: (i, j)),
                pl.BlockSpec((HALO, tc), lambda j, i: (jnp.maximum(i * nb8 - 1, 0), j)),
                pl.BlockSpec((HALO, tc), lambda j, i: (jnp.minimum((i + 1) * nb8, last8), j))]
    return pl.pallas_call(
        body, name=name, grid=(ff // tc, r // TM), in_specs=_conv_in_specs(tc, r) + du_specs,
        out_specs=[pl.BlockSpec((2, TM, tc), lambda j, i: (0, i, j)), pl.BlockSpec((2, 3, tc), lambda j, i: (0, 0, j)),
                   pl.BlockSpec((2, 1, tc), lambda j, i: (0, 0, j))],
        out_shape=[jax.ShapeDtypeStruct((2, r, ff), BF16), jax.ShapeDtypeStruct((2, 3, ff), F32), jax.ShapeDtypeStruct((2, 1, ff), F32)],
        compiler_params=_params(("parallel", "arbitrary")))(a3, a3, a3, a3, a3, a3, cw, cw, cb, cb, du, du, du)


def _layer_fwd(x, w, l, tab, n_lat):
    nbl = n_lat // TM
    mod = w["mods"][l]
    h1 = _normmod_fwd(x, w["g_mix"][l], mod, 0, nbl, "normmod_fwd")
    p = _mm(h1, w["w_in"][l], "nn", F32, "mm_in")
    qkv = _prep_fwd(p, tab, w["sp"][l], w["w_qb"][l], w["w_kvb"][l], "prep_fwd")
    oa = _mla_fwd(qkv[0], qkv[1], qkv[2], n_lat, "mla_fwd")
    ob = _pool_fwd(p, w["pool_w"][l], w["pool_sc"][l], n_lat, "pool_fwd")
    oc = _swa_fwd(qkv[3], qkv[4], qkv[5], w["sp"][l], n_lat, "swa_fwd")
    od = _na_fwd(qkv[6], qkv[7], qkv[8], w["t1"][l], n_lat, "na_fwd")
    mix = jnp.concatenate([oa, ob, oc, od], axis=1)
    y = _mm(mix, w["w_out"][l], "nn", F32, "mm_out")
    x1 = _resid_fwd(x, y, mod, 2, nbl, "resid_fwd")
    h2 = _normmod_fwd(x1, w["g_ffn"][l], mod, 3, nbl, "normmod_fwd")
    a3 = _mm(h2, w["w_up"][l], "nn", F32, "mm_up", o_split=True)
    u = _convgate_fwd(a3, w["conv_w"][l], w["conv_b"][l], n_lat, "convgate_fwd")
    y2 = _mm(u, w["w_down"][l], "nn", F32, "mm_down")
    x2 = _resid_fwd(x1, y2, mod, 5, nbl, "resid_fwd")
    return x2, dict(x=x, h1=h1, p=p, qkv=qkv, mix=mix, y=y, x1=x1, h2=h2, a3=a3, u=u, y2=y2)


def _layer_bwd(dx, s, w, l, tab, n_lat):
    nbl = n_lat // TM
    mod = w["mods"][l]
    g = {}
    dy2, dmod_a = _resid_bwd(dx, s["y2"], mod, 5, nbl, "resid_bwd")
    g["w_down"] = _mm(s["u"], dy2, "tn", F32, "mm_dwdown")
    du = _mm(dy2, w["w_down"][l], "nt", F32, "mm_du")
    da3, g["conv_w"], g["conv_b"] = _convgate_bwd(s["a3"], w["conv_w"][l], w["conv_b"][l], du, n_lat, "convgate_bwd")
    g["w_up"] = _mm(s["h2"], da3, "tn", F32, "mm_dwup", b_split=True)
    dh2 = _mm(da3, w["w_up"][l], "nt", F32, "mm_dh2", a_split=True)
    dx1, g["g_ffn"], dmod_b = _normmod_bwd(s["x1"], w["g_ffn"][l], mod, dh2, dx, 3, nbl, "normmod_bwd")
    dy, dmod_c = _resid_bwd(dx1, s["y"], mod, 2, nbl, "resid_bwd")
    g["w_out"] = _mm(s["mix"], dy, "tn", F32, "mm_dwout")
    dmix = _mm(dy, w["w_out"][l], "nt", F32, "mm_dmix")
    qkv = s["qkv"]
    daq, dak, dav = _mla_bwd(qkv[0], qkv[1], qkv[2], dmix, n_lat, "mla_bwd")
    dpool, g["pool_w"], g["pool_sc"] = _pool_bwd(s["p"], w["pool_w"][l], w["pool_sc"][l], dmix, n_lat, "pool_bwd")
    dcq, dck, dcv, dsp_c = _swa_bwd(qkv[3], qkv[4], qkv[5], w["sp"][l], dmix, n_lat, "swa_bwd")
    ddq, ddk, ddv, g["t1"] = _na_bwd(qkv[6], qkv[7], qkv[8], w["t1"][l], dmix, n_lat, "na_bwd")
    dp, dsp_p, g["w_qb"], g["w_kvb"] = _prep_bwd(s["p"], tab, w["sp"][l], w["w_qb"][l], w["w_kvb"][l],
                                                (daq, dak, dav, dcq, dck, dcv, ddq, ddk, ddv), dpool, "prep_bwd")
    g["sp"] = dsp_c + dsp_p
    g["w_in"] = _mm(s["h1"], dp, "tn", F32, "mm_dwin")
    dh1 = _mm(dp, w["w_in"][l], "nt", F32, "mm_dh1")
    dx0, g["g_mix"], dmod_d = _normmod_bwd(s["x"], w["g_mix"][l], mod, dh1, dx1, 0, nbl, "normmod_bwd")
    g["mods"] = dmod_a + dmod_b + dmod_c + dmod_d
    return dx0, g


def _local_step(x_all, target, w, tab, n_lat):
    saved = []
    x = x_all
    for l in range(DEPTH):
        x, s = _layer_fwd(x, w, l, tab, n_lat)
        saved.append(s)
    loss, dx = _loss_kernel(x, target, n_lat // TM, "loss")
    grads = [None] * DEPTH
    for l in reversed(range(DEPTH)):
        dx, grads[l] = _layer_bwd(dx, saved[l], w, l, tab, n_lat)
    return loss[0, 0], dx, grads


def _pad_cols(a, widths):
    parts, o = [], 0
    for take, pad in widths:
        parts.append(a[..., o:o + take])
        if pad:
            parts.append(jnp.zeros(a.shape[:-1] + (pad,), a.dtype))
        o += take
    return jnp.concatenate(parts, axis=-1)


def _w_in_layout(w_in):
    return _pad_cols(w_in, [(832, 64), (P_COLS - 832, PW - P_COLS - 64)])


def _w_in_unlayout(g):
    return jnp.concatenate([g[..., 0:832], g[..., 896:896 + P_COLS - 832]], axis=-1)


def _w_qb_layout(w):
    s = w.reshape(w.shape[:-1] + (4, 192))
    return jnp.concatenate([s[..., 0:128].reshape(w.shape[:-1] + (512,)), s[..., 128:192].reshape(w.shape[:-1] + (256,))], axis=-1)


def _w_qb_unlayout(g):
    n = g[..., 0:512].reshape(g.shape[:-1] + (4, 128))
    r = g[..., 512:768].reshape(g.shape[:-1] + (4, 64))
    return jnp.concatenate([n, r], axis=-1).reshape(g.shape[:-1] + (768,))


SP_SLOTS = (("mla_q_a_norm", 512), ("mla_kv_a_norm", 256), ("mla_q_nope_norm", 128), ("mla_q_rope_norm", 64),
            ("mla_k_nope_norm", 128), ("mla_k_rope_norm", 64), ("swa_q_norm", 64), ("swa_k_norm", 64),
            ("na_q_norm", 64), ("na_k_norm", 64), ("swa_sink", 8))


def _sp_pack(small):
    rows = [jnp.pad(small[k], ((0, 0), (0, 512 - n))) for k, n in SP_SLOTS]
    rows += [jnp.zeros_like(rows[0])] * (16 - len(rows))
    return jnp.stack(rows, axis=1)


def _sp_unpack(sp):
    return {k: sp[:, i, 0:n] for i, (k, n) in enumerate(SP_SLOTS)}


def _rpb_onehot():
    qc = lax.broadcasted_iota(I32, (GRID_W, GRID_W), 0)
    kc = lax.broadcasted_iota(I32, (GRID_W, GRID_W), 1)
    dc = (jnp.clip(kc - qc, -15, 15) + 15).reshape(1, GRID_W * GRID_W)
    return (lax.broadcasted_iota(I32, (32, GRID_W * GRID_W), 0) == dc).astype(F32)


def _rpb_expand(rpb):
    l = rpb.shape[0]
    flat = jnp.pad(rpb, ((0, 0), (0, 0), (0, 1), (0, 1))).reshape(l * 128, 32)
    return _mm_exact(flat, _rpb_onehot(), "rpb_expand").reshape(l, 8, 16, GRID_W, GRID_W)


def _rpb_fold(dt1):
    l = dt1.shape[0]
    g = _mm_exact(dt1.reshape(l * 128, GRID_W * GRID_W), _rpb_onehot().T, "rpb_fold")
    return g.reshape(l, 8, 16, 32)[:, :, 0:15, 0:31]


def _rope_table(n_lat, n_ctx):
    t = jnp.arange(n_lat)
    inv = ROPE_BASE ** (-jnp.arange(0, 32, 2, dtype=F32) / 32)
    ar = (t // GRID_W).astype(F32)[:, None] * inv
    ac = (t % GRID_W).astype(F32)[:, None] * inv
    cos = jnp.concatenate([jnp.cos(ar), jnp.cos(ar), jnp.cos(ac), jnp.cos(ac)], axis=-1)
    sin = jnp.concatenate([jnp.sin(ar), jnp.sin(ar), jnp.sin(ac), jnp.sin(ac)], axis=-1)
    tab = jnp.concatenate([cos, sin], axis=-1)
    ident = jnp.concatenate([jnp.ones((n_ctx, 64), F32), jnp.zeros((n_ctx, 64), F32)], axis=-1)
    return jnp.concatenate([tab, ident], axis=0)


def _kernel_weights(full, mods):
    l = full["g_mix"].shape[0]
    ff = full["ffn_w_down"].shape[1]
    return dict(
        mods=mods,
        g_mix=full["g_mix"][:, None, :], g_ffn=full["g_ffn"][:, None, :],
        w_in=_w_in_layout(full["w_in"]).astype(BF16), w_out=full["w_out"].astype(BF16),
        w_up=full["ffn_w_up"].astype(BF16), w_down=full["ffn_w_down"].astype(BF16),
        w_qb=_w_qb_layout(full["mla_w_qb"]).astype(BF16), w_kvb=full["mla_w_kvb"].astype(BF16),
        sp=_sp_pack(full), pool_w=full["pool_w"], pool_sc=full["pool_scale"].reshape(l, 4, 1, 128),
        t1=_rpb_expand(full["na_rpb"]),
        conv_w=full["ffn_conv_w"].reshape(l, 3, 2, ff).transpose(0, 2, 1, 3),
        conv_b=full["ffn_conv_b"].reshape(l, 2, 1, ff))


def _reference_grads(grads):
    st = lambda k: jnp.stack([g[k] for g in grads], axis=0)
    l = len(grads)
    out = dict(
        g_mix=st("g_mix")[:, 0], g_ffn=st("g_ffn")[:, 0], w_in=_w_in_unlayout(st("w_in")), w_out=st("w_out"),
        ffn_w_up=st("w_up"), ffn_w_down=st("w_down"), mla_w_qb=_w_qb_unlayout(st("w_qb")), mla_w_kvb=st("w_kvb"),
        pool_w=st("pool_w"), pool_scale=st("pool_sc").reshape(l, 512), na_rpb=_rpb_fold(st("t1")),
        ffn_conv_w=st("conv_w").transpose(0, 2, 1, 3).reshape(l, 3, -1), ffn_conv_b=st("conv_b").reshape(l, -1),
        mods=st("mods"))
    out.update(_sp_unpack(st("sp")))
    return out


ANY = pl.BlockSpec(memory_space=pl.ANY)


def _flip(x, y, j):
    return (1 - x if j >> 1 else x), (1 - y if j & 1 else y)


def _comm_call(name, ins, out_shapes, n_copies, plan, aliases=None):
    n_in, n_out = len(ins), len(out_shapes)

    def body(*refs):
        in_refs, out_refs = refs[:n_in], refs[n_in:n_in + n_out]
        ssem, rsem = refs[n_in + n_out:]
        pos = (lax.axis_index("x"), lax.axis_index("y"), lax.axis_index("c"))
        copies = plan(in_refs, out_refs, pos)
        assert len(copies) == n_copies
        descs = []
        for i, (src, dst, peer) in enumerate(copies):
            if peer is None:
                d = pltpu.make_async_copy(src, dst, ssem.at[i])
            else:
                d = pltpu.make_async_remote_copy(src_ref=src, dst_ref=dst, send_sem=ssem.at[i], recv_sem=rsem.at[i],
                                                 device_id=peer, device_id_type=MESH)
            d.start()
            descs.append(d)
        for d in descs:
            d.wait()

    return pl.pallas_call(
        body, name=name, in_specs=[ANY] * n_in, out_specs=[ANY] * n_out, out_shape=list(out_shapes),
        input_output_aliases=aliases or {},
        scratch_shapes=[pltpu.SemaphoreType.DMA((n_copies,)), pltpu.SemaphoreType.DMA((n_copies,))])(*ins)


def _sib_fill(bufs, part, name):
    def plan(ins, outs, pos):
        x, y, c = pos
        return [(o_ref.at[part(c)], o_ref.at[part(c)], (x, y, 1 - c)) for o_ref in outs]

    shapes = [jax.ShapeDtypeStruct(b.shape, b.dtype) for b in bufs]
    return _comm_call(name, bufs, shapes, len(bufs), plan, aliases={i: i for i in range(len(bufs))})


def _pair_up(xs, name):
    def plan(ins, outs, pos):
        x, y, c = pos
        cps = []
        for i_ref, o_ref in zip(ins, outs):
            cps.append((i_ref, o_ref.at[c], None))
            cps.append((i_ref, o_ref.at[c], (x, y, 1 - c)))
        return cps

    return _comm_call(name, xs, [jax.ShapeDtypeStruct((2,) + a.shape, a.dtype) for a in xs], 2 * len(xs), plan)


def _chip_gather(xs, kinds, name, layers=None):
    def dst(o_ref, kind, k, c, x_shape):
        lead = slice(None) if layers is None else pl.ds(2 * c, 2)
        if kind == "lead":
            return o_ref.at[k]
        if kind == "axis1":
            return o_ref.at[lead, k]
        w = x_shape[-1]
        idx = (lead,) + (slice(None),) * (len(x_shape) - 2) + (pl.ds(pl.multiple_of(k * w, 128), w),)
        return o_ref.at[idx]

    def plan(ins, outs, pos):
        x, y, c = pos
        k = 2 * x + y
        cps = []
        for i_ref, o_ref, kind, a in zip(ins, outs, kinds, xs):
            cps.append((i_ref, dst(o_ref, kind, k, c, a.shape), None))
            for j in (1, 2, 3):
                tx, ty = _flip(x, y, j)
                cps.append((i_ref, dst(o_ref, kind, k, c, a.shape), (tx, ty, c)))
        return cps

    def oshape(a, kind):
        lead = a.shape[0] if layers is None else layers
        if kind == "lead":
            return (4,) + a.shape
        if kind == "axis1":
            return (lead, 4) + a.shape[1:]
        return (lead,) + a.shape[1:-1] + (4 * a.shape[-1],)

    return _comm_call(name, xs, [jax.ShapeDtypeStruct(oshape(a, kd), a.dtype) for a, kd in zip(xs, kinds)], 4 * len(xs), plan)


def _chip_scatter(xs, name):
    def plan(ins, outs, pos):
        x, y, c = pos
        k = 2 * x + y
        cps = []
        for i_ref, o_ref in zip(ins, outs):
            cps.append((i_ref.at[k], o_ref.at[k], None))
            for j in (1, 2, 3):
                tx, ty = _flip(x, y, j)
                cps.append((i_ref.at[2 * tx + ty], o_ref.at[k], (tx, ty, c)))
        return cps

    return _comm_call(name, xs, [jax.ShapeDtypeStruct(a.shape, a.dtype) for a in xs], 4 * len(xs), plan)


def _half_swap(xs, kinds, name):
    def plan(ins, outs, pos):
        x, y, c = pos
        cps = []
        for i_ref, o_ref, kind, a in zip(ins, outs, kinds, xs):
            h = a.shape[-2] // 2
            rows = pl.ds(pl.multiple_of((1 - c) * h, 8), h)
            src = i_ref.at[:, rows, :] if kind == "cm" else i_ref.at[rows, :]
            cps.append((src, o_ref, (x, y, 1 - c)))
        return cps

    shapes = [jax.ShapeDtypeStruct(a.shape[:-2] + (a.shape[-2] // 2, a.shape[-1]), a.dtype) for a in xs]
    return _comm_call(name, xs, shapes, len(xs), plan)


def _row_tile(rows, cols, budget=1 << 20):
    for t in (2048, 1024, 512, 256, 128, 64, 32, 16, 8):
        if rows % t == 0 and t * cols * 4 <= budget:
            return t
    return rows


def _half_add(g, recv, kind, cvec, name):
    if kind == "cm":
        _, h2, w = g.shape
        h = h2 // 2
        tr = _row_tile(h, w)
        nb = h // tr
        g_spec = pl.BlockSpec((None, tr, w), lambda kk, i, c_ref: (kk, i + c_ref[0] * nb, 0))
        r_spec = pl.BlockSpec((None, tr, w), lambda kk, i, c_ref: (kk, i, 0))
    else:
        h2, w4 = g.shape
        h, w = h2 // 2, w4 // 4
        tr = _row_tile(h, w)
        nb = h // tr
        g_spec = pl.BlockSpec((tr, w), lambda kk, i, c_ref: (i + c_ref[0] * nb, kk))
        r_spec = pl.BlockSpec((tr, w), lambda kk, i, c_ref: (i, kk))

    def body(c_ref, g_ref, r_ref, o_ref):
        o_ref[...] = (g_ref[...] + r_ref[...]).astype(BF16)

    return pl.pallas_call(
        body, name=name,
        grid_spec=pltpu.PrefetchScalarGridSpec(num_scalar_prefetch=1, grid=(4, nb), in_specs=[g_spec, r_spec],
                                               out_specs=pl.BlockSpec((None, tr, w), lambda kk, i, c_ref: (kk, i, 0))),
        out_shape=jax.ShapeDtypeStruct((4, h, w), BF16), compiler_params=_params(("parallel", "parallel")))(cvec, g, recv)


def _sum_lead(x, name):
    n, rows, w = x.shape
    tr = _row_tile(rows, w, (1 << 21) // n)

    def body(x_ref, o_ref):
        acc = x_ref[0].astype(F32)
        for j in range(1, n):
            acc = acc + x_ref[j].astype(F32)
        o_ref[...] = acc

    return pl.pallas_call(
        body, name=name, grid=(rows // tr,), in_specs=[pl.BlockSpec((n, tr, w), lambda i: (0, i, 0))],
        out_specs=pl.BlockSpec((tr, w), lambda i: (i, 0)), out_shape=jax.ShapeDtypeStruct((rows, w), F32),
        compiler_params=_params(("parallel",)))(x)


def _sum_into(x, buf, layer, cvec, layers, name):
    n, rows, w = x.shape
    tr = _row_tile(rows, w, (1 << 21) // n)

    def body(c_ref, x_ref, *refs):
        o_ref = refs[-1]
        acc = x_ref[0].astype(F32)
        for j in range(1, n):
            acc = acc + x_ref[j].astype(F32)
        o_ref[...] = acc

    in_specs = [pl.BlockSpec((n, tr, w), lambda i, c_ref: (0, i, 0))]
    args = [cvec, x]
    if buf is not None:
        in_specs.append(ANY)
        args.append(buf)
    return pl.pallas_call(
        body, name=name,
        grid_spec=pltpu.PrefetchScalarGridSpec(
            num_scalar_prefetch=1, grid=(rows // tr,), in_specs=in_specs,
            out_specs=pl.BlockSpec((None, None, tr, w), lambda i, c_ref: (layer, c_ref[0], i, 0))),
        out_shape=jax.ShapeDtypeStruct((layers, 2, rows, w), F32), input_output_aliases={} if buf is None else {2: 0},
        compiler_params=_params(("arbitrary",)))(*args)


def _adamw(w, g, m, v, name):
    shape = w.shape
    cols = shape[-1]
    rows = w.size // cols
    tr = _row_tile(rows, cols, 1 << 19)

    def body(w_ref, g_ref, m_ref, v_ref, d_ref, mo_ref, vo_ref):
        gv = g_ref[...]
        mn = ADAM_B1 * m_ref[...] + (1.0 - ADAM_B1) * gv
        vn = ADAM_B2 * v_ref[...] + (1.0 - ADAM_B2) * jnp.square(gv)
        m_hat = mn / (1.0 - ADAM_B1 ** ADAM_STEP)
        v_hat = vn / (1.0 - ADAM_B2 ** ADAM_STEP)
        d_ref[...] = -ADAM_LR * (m_hat / (jnp.sqrt(v_hat) + ADAM_EPS) + ADAM_WD * w_ref[...])
        mo_ref[...] = mn
        vo_ref[...] = vn

    spec = pl.BlockSpec((tr, cols), lambda i: (i, 0))
    outs = pl.pallas_call(
        body, name=name, grid=(rows // tr,), in_specs=[spec] * 4, out_specs=[spec] * 3,
        out_shape=[jax.ShapeDtypeStruct((rows, cols), F32)] * 3,
        compiler_params=_params(("parallel",)))(*[a.reshape(rows, cols) for a in (w, g, m, v)])
    return [o.reshape(shape) for o in outs]


def _silu_grad(x):
    s = jax.nn.sigmoid(x)
    return s * (1.0 + x * (1.0 - s))


def _mod_fwd(cs16, w_mod, b_sh, name):
    l, d, wc = w_mod.shape
    tn = _pick(wc, (512, 384, 256, 128))

    def body(c_ref, w_ref, b_ref, o_ref):
        a = jax.nn.silu(c_ref[...]).astype(BF16)
        o_ref[...] = jnp.dot(a, w_ref[...].astype(BF16), preferred_element_type=F32) + b_ref[...]

    return pl.pallas_call(
        body, name=name, grid=(l, wc // tn),
        in_specs=[_full_spec((16, d)), pl.BlockSpec((None, d, tn), lambda i, j: (i, 0, j)), pl.BlockSpec((None, 1, tn), lambda i, j: (i, 0, j))],
        out_specs=pl.BlockSpec((None, 16, tn), lambda i, j: (i, 0, j)), out_shape=jax.ShapeDtypeStruct((l, 16, wc), F32),
        compiler_params=_params(("parallel", "parallel")))(cs16, w_mod, b_sh)


def _mod_dw(cs16, dm_sh, name):
    l, _, wc = dm_sh.shape
    d = cs16.shape[1]
    tr = _pick(d, (512, 256, 128))
    tc = _pick(wc, (512, 384, 256, 128))

    def body(c_ref, dm_ref, o_ref):
        a = jax.nn.silu(c_ref[...]).astype(BF16)
        o_ref[...] = lax.dot_general(a, dm_ref[...].astype(BF16), (((0,), (0,)), ((), ())), preferred_element_type=F32)

    return pl.pallas_call(
        body, name=name, grid=(l, d // tr, wc // tc),
        in_specs=[pl.BlockSpec((16, tr), lambda i, r, j: (0, r)), pl.BlockSpec((None, 16, tc), lambda i, r, j: (i, 0, j))],
        out_specs=pl.BlockSpec((None, tr, tc), lambda i, r, j: (i, r, j)), out_shape=jax.ShapeDtypeStruct((l, d, wc), F32),
        compiler_params=_params(("parallel", "parallel", "parallel")))(cs16, dm_sh)


def _mod_dc(dm_sh, w_mod, c_ctx, name):
    l, d, wc = w_mod.shape
    tk = _pick(wc, (512, 384, 256, 128))
    nk = wc // tk

    def body(dm_ref, w_ref, c_ref, o_ref, acc_ref):
        i, j = pl.program_id(0), pl.program_id(1)

        @pl.when(jnp.logical_and(i == 0, j == 0))
        def _():
            acc_ref[...] = jnp.zeros_like(acc_ref)

        acc_ref[...] += lax.dot_general(dm_ref[...].astype(BF16), w_ref[...].astype(BF16), (((1,), (1,)), ((), ())),
                                        preferred_element_type=F32)

        @pl.when(jnp.logical_and(i == l - 1, j == nk - 1))
        def _():
            mine = jnp.where(lax.axis_index("c") == 0, 1.0, 0.0)
            o_ref[...] = acc_ref[8:9, :] * _silu_grad(c_ref[...]) * mine

    return pl.pallas_call(
        body, name=name, grid=(l, nk),
        in_specs=[pl.BlockSpec((None, 16, tk), lambda i, j: (i, 0, j)), pl.BlockSpec((None, d, tk), lambda i, j: (i, 0, j)), _full_spec((1, d))],
        out_specs=_full_spec((1, d)), out_shape=jax.ShapeDtypeStruct((1, d), F32), scratch_shapes=[pltpu.VMEM((16, d), F32)],
        compiler_params=_params(("arbitrary", "arbitrary")))(dm_sh, w_mod, c_ctx)


def _dmod_assemble(gath, name):
    _, l, _, w = gath.shape
    gath = gath.transpose(1, 2, 0, 3)
    tc = _pick(w, (2048, 1024, 512, 256, 128))

    def body(lat_ref, ctx_ref, o_ref, b_ref):
        ctx = ctx_ref[0:1, :]
        for dev in range(1, 8):
            ctx = ctx + ctx_ref[dev:dev + 1, :]
        lat = lat_ref[...]
        o_ref[0:8, :] = lat
        o_ref[8:9, :] = ctx
        o_ref[9:16, :] = jnp.zeros((7, tc), F32)
        b_ref[...] = jnp.sum(lat, axis=0, keepdims=True) + ctx

    return pl.pallas_call(
        body, name=name, grid=(l, w // tc),
        in_specs=[pl.BlockSpec((None, None, 8, tc), lambda i, j: (i, 0, 0, j)), pl.BlockSpec((None, None, 8, tc), lambda i, j: (i, 1, 0, j))],
        out_specs=[pl.BlockSpec((None, 16, tc), lambda i, j: (i, 0, j)), pl.BlockSpec((None, 1, tc), lambda i, j: (i, 0, j))],
        out_shape=[jax.ShapeDtypeStruct((l, 16, w), F32), jax.ShapeDtypeStruct((l, 1, w), F32)],
        compiler_params=_params(("parallel", "parallel")))(gath, gath)


SMALL = ("c_ctx", "g_mix", "g_ffn", "mla_q_a_norm", "mla_kv_a_norm", "mla_q_nope_norm", "mla_q_rope_norm", "mla_k_nope_norm",
         "mla_k_rope_norm", "pool_w", "pool_scale", "swa_q_norm", "swa_k_norm", "swa_sink", "na_q_norm", "na_k_norm", "na_rpb",
         "ffn_conv_b")
PACK_W = 512
PACK_Q = 8 * PACK_W


def _pack(arrs):
    flat = []
    for a in arrs:
        f = a.reshape(-1)
        flat.append(jnp.pad(f, (0, (-f.size) % PACK_Q)))
    return jnp.concatenate(flat).reshape(-1, PACK_W)


def _unpack(packed, shapes):
    flat, out, o = packed.reshape(-1), [], 0
    for s in shapes:
        n = 1
        for dim in s:
            n *= dim
        out.append(flat[o:o + n].reshape(s))
        o += n + (-n) % PACK_Q
    return out


def _all_sum(p, name):
    pair = _pair_up([p], name + "_pair")[0]
    chip = _sum_lead(pair, name + "_sum2")
    return _sum_lead(_chip_gather([chip], ["lead"], name + "_gather")[0], name + "_sum4")


WEIGHTS = ("c_ctx", "w_mod", "b_mod", "g_mix", "g_ffn", "w_in", "w_out", "mla_q_a_norm", "mla_w_qb", "mla_kv_a_norm", "mla_w_kvb",
           "mla_q_nope_norm", "mla_q_rope_norm", "mla_k_nope_norm", "mla_k_rope_norm", "pool_w", "pool_scale", "swa_q_norm",
           "swa_k_norm", "swa_sink", "na_q_norm", "na_k_norm", "na_rpb", "ffn_w_up", "ffn_conv_w", "ffn_conv_b", "ffn_w_down")
BIG = ("w_in", "mla_w_qb", "w_out", "ffn_w_down", "ffn_w_up", "mla_w_kvb")
BIG_KINDS = ("cm", "cm", "cm", "cm", "lb", "lb")


def _step(a):
    x, c, ctx = a["x"], a["c"], a["ctx"]
    n_lat, d = x.shape[1], x.shape[2]
    n_ctx = ctx.shape[1]
    l = DEPTH
    px, py, pc = lax.axis_index("x"), lax.axis_index("y"), lax.axis_index("c")
    chip = 2 * px + py
    cvec = pc.reshape(1).astype(I32)

    c_all = _chip_gather(_pair_up([c], "c_pair"), ["lead"], "c_gather")[0].reshape(8, d)
    cs16 = jnp.concatenate([c_all, a["c_ctx"][None, :], jnp.zeros((7, d), F32)], axis=0)
    wc = a["w_mod"].shape[-1]
    b_sh = lax.dynamic_slice_in_dim(a["b_mod"], chip * wc, wc, axis=1)[:, None, :]
    mod_sh = _mod_fwd(cs16, a["w_mod"], b_sh, "mod_fwd")
    mod_all, conv_w_full = _chip_gather([mod_sh, a["ffn_conv_w"]], ["lead", "lane"], "mod_gather")
    mod_all = mod_all.transpose(1, 2, 0, 3).reshape(l, 16, 4 * wc)
    mods = jnp.stack([lax.dynamic_index_in_dim(mod_all, 2 * chip + pc, axis=1, keepdims=False), mod_all[:, 8]], axis=1)
    mods = mods.reshape(l, 2, 6, d)

    own = 2 * pc
    shards = [lax.dynamic_slice_in_dim(a[k].astype(BF16), own, 2, axis=0) for k in BIG]
    gathered = _chip_gather(shards, ["axis1", "axis1", "axis1", "axis1", "lane", "lane"], "w_gather", layers=l)
    w_in_g, w_qb_g, w_out_g, w_down_g, w_up_g, w_kvb_g = _sib_fill(gathered, lambda cc: pl.ds(2 * cc, 2), "w_pair")
    full = {k: a[k] for k in SMALL if k != "c_ctx"}
    full.update(
        w_in=w_in_g.transpose(0, 2, 1, 3).reshape(l, d, P_COLS), mla_w_qb=w_qb_g.transpose(0, 2, 1, 3).reshape(l, 512, 768),
        w_out=w_out_g.reshape(l, -1, d), ffn_w_down=w_down_g.reshape(l, -1, d), ffn_w_up=w_up_g, mla_w_kvb=w_kvb_g,
        ffn_conv_w=conv_w_full)
    w = _kernel_weights(full, mods)
    tab = _rope_table(n_lat, n_ctx)

    x_all = jnp.concatenate([x[0], ctx[0]], axis=0)
    loss, dx, grads = _local_step(x_all, a["loss_target"][0], w, tab, n_lat)
    loss = lax.psum(loss, ("x", "y", "c"))

    g_big = [None] * len(BIG)
    for li in range(l):
        g = grads[li]
        ops = [_w_in_unlayout(g["w_in"]).reshape(d, 4, -1).transpose(1, 0, 2),
               _w_qb_unlayout(g["w_qb"]).reshape(512, 4, 192).transpose(1, 0, 2),
               g["w_out"].reshape(4, -1, d), g["w_down"].reshape(4, -1, d), g["w_up"], g["w_kvb"]]
        recv = _half_swap(ops, BIG_KINDS, "g_swap")
        halves = [_half_add(o, r, kd, cvec, "g_half_add") for o, r, kd in zip(ops, recv, BIG_KINDS)]
        landed = _chip_scatter(halves, "g_scatter")
        g_big = [_sum_into(s, buf, li, cvec, l, "g_sum4") for s, buf in zip(landed, g_big)]
    g_big = _sib_fill(g_big, lambda cc: (slice(None), cc), "g_pair")
    g_out = {k: g.reshape(a[k].shape) for k, g in zip(BIG, g_big)}

    dmods = jnp.stack([grads[li]["mods"] for li in range(l)], axis=0).reshape(l, 2, 6 * d)
    dm_gath = _chip_gather(_pair_up([dmods], "dmod_pair"), ["lead"], "dmod_gather")[0].reshape(8, l, 2, 6 * d)
    dmod_all, g_b_mod = _dmod_assemble(dm_gath, "dmod_assemble")
    dm_sh = lax.dynamic_slice_in_dim(dmod_all, chip * wc, wc, axis=2)
    g_out["w_mod"] = _mod_dw(cs16, dm_sh, "mod_dw")
    g_out["b_mod"] = g_b_mod.reshape(l, 6 * d)
    g_c_ctx = _mod_dc(dm_sh, a["w_mod"], a["c_ctx"][None, :], "mod_dc")

    rg = _reference_grads(grads)
    rg["c_ctx"] = g_c_ctx[0]
    packed = _all_sum(_pack([rg[k] for k in SMALL] + [rg["ffn_conv_w"]]), "small")
    small_g = _unpack(packed, [a[k].shape for k in SMALL] + [rg["ffn_conv_w"].shape])
    for k, g in zip(SMALL, small_g[:-1]):
        g_out[k] = g
    cw = a["ffn_conv_w"].shape[-1]
    g_out["ffn_conv_w"] = lax.dynamic_slice_in_dim(small_g[-1], chip * cw, cw, axis=2)

    upd = {}
    pk = lambda pre: _pack([a[pre + k] for k in SMALL])
    outs = _adamw(pk(""), _pack([g_out[k] for k in SMALL]), pk("m_"), pk("v_"), "adamw_small")
    for o, kind in zip(outs, ("delta", "m", "v")):
        for k, val in zip(SMALL, _unpack(o, [a[k].shape for k in SMALL])):
            upd[kind, k] = val
    for k in WEIGHTS:
        if k not in SMALL:
            outs = _adamw(a[k], g_out[k], a["m_" + k], a["v_" + k], "adamw_" + k)
            for o, kind in zip(outs, ("delta", "m", "v")):
                upd[kind, k] = o
    grad_x = dx[0:n_lat].reshape(x.shape)
    return (loss, grad_x, *[g_out[k] for k in WEIGHTS], *[upd["delta", k] for k in WEIGHTS],
            *[upd["m", k] for k in WEIGHTS], *[upd["v", k] for k in WEIGHTS])


def kernel(x, c, ctx, c_ctx, w_mod, b_mod, g_mix, g_ffn, w_in, w_out, mla_q_a_norm, mla_w_qb, mla_kv_a_norm, mla_w_kvb, mla_q_nope_norm, mla_q_rope_norm, mla_k_nope_norm, mla_k_rope_norm, pool_w, pool_scale, swa_q_norm, swa_k_norm, swa_sink, na_q_norm, na_k_norm, na_rpb, ffn_w_up, ffn_conv_w, ffn_conv_b, ffn_w_down, loss_target, m_c_ctx, m_w_mod, m_b_mod, m_g_mix, m_g_ffn, m_w_in, m_w_out, m_mla_q_a_norm, m_mla_w_qb, m_mla_kv_a_norm, m_mla_w_kvb, m_mla_q_nope_norm, m_mla_q_rope_norm, m_mla_k_nope_norm, m_mla_k_rope_norm, m_pool_w, m_pool_scale, m_swa_q_norm, m_swa_k_norm, m_swa_sink, m_na_q_norm, m_na_k_norm, m_na_rpb, m_ffn_w_up, m_ffn_conv_w, m_ffn_conv_b, m_ffn_w_down, v_c_ctx, v_w_mod, v_b_mod, v_g_mix, v_g_ffn, v_w_in, v_w_out, v_mla_q_a_norm, v_mla_w_qb, v_mla_kv_a_norm, v_mla_w_kvb, v_mla_q_nope_norm, v_mla_q_rope_norm, v_mla_k_nope_norm, v_mla_k_rope_norm, v_pool_w, v_pool_scale, v_swa_q_norm, v_swa_k_norm, v_swa_sink, v_na_q_norm, v_na_k_norm, v_na_rpb, v_ffn_w_up, v_ffn_conv_w, v_ffn_conv_b, v_ffn_w_down):
    return _step(dict(locals()))
```

```python
import functools

import jax
import jax.numpy as jnp
from jax import lax
from jax.experimental import pallas as pl
from jax.experimental.pallas import tpu as pltpu

F32 = jnp.float32
BF16 = jnp.bfloat16
I32 = jnp.int32

DEPTH = 4
GRID_W = 64
ROPE_BASE = 10000.0
EPS = 1e-6
NEG = -1e30
MLA_SCALE = 192.0 ** -0.5
HD_SCALE = 64.0 ** -0.5
NA_KROWS = 12
SWA_KEYS = 512
P_COLS = 3648
PW = 3840
TM = 256
HALO = 8
ADAM_LR, ADAM_B1, ADAM_B2, ADAM_EPS, ADAM_WD, ADAM_STEP = 0.001, 0.9, 0.999, 1e-08, 0.01, 10
VMEM_LIMIT = 56 * 1024 * 1024
MESH = pl.DeviceIdType.MESH


def _pick(n, cands):
    for c in cands:
        if n % c == 0:
            return c
    return n


def _params(sem=None):
    return pltpu.CompilerParams(dimension_semantics=sem, vmem_limit_bytes=VMEM_LIMIT)


@jax.custom_vjp
def _bdot(a, b):
    return jnp.dot(a.astype(BF16), b.astype(BF16), preferred_element_type=F32)


def _bdot_fwd(a, b):
    return _bdot(a, b), (a.astype(BF16), b.astype(BF16))


def _bdot_bwd(res, g):
    a, b = res
    gb = g.astype(BF16)
    da = lax.dot_general(gb, b, (((1,), (1,)), ((), ())), preferred_element_type=F32)
    db = lax.dot_general(a, gb, (((0,), (0,)), ((), ())), preferred_element_type=F32)
    return da, db


_bdot.defvjp(_bdot_fwd, _bdot_bwd)


@jax.custom_vjp
def _bdot_nt(a, b):
    return lax.dot_general(a.astype(BF16), b.astype(BF16), (((1,), (1,)), ((), ())), preferred_element_type=F32)


def _bdot_nt_fwd(a, b):
    return _bdot_nt(a, b), (a.astype(BF16), b.astype(BF16))


def _bdot_nt_bwd(res, g):
    a, b = res
    gb = g.astype(BF16)
    da = jnp.dot(gb, b, preferred_element_type=F32)
    db = lax.dot_general(gb, a, (((0,), (0,)), ((), ())), preferred_element_type=F32)
    return da, db


_bdot_nt.defvjp(_bdot_nt_fwd, _bdot_nt_bwd)


def _rms(x, g):
    return x * lax.rsqrt(jnp.mean(x * x, axis=-1, keepdims=True) + EPS) * g


def _rope(x, cos, sin):
    xr = jnp.concatenate([-x[:, 16:32], x[:, 0:16], -x[:, 48:64], x[:, 32:48]], axis=-1)
    return x * cos + xr * sin


def _sel(is_ctx, mod, row):
    return jnp.where(is_ctx, mod[1, row:row + 1, :], mod[0, row:row + 1, :])


MM_VMEM_BUDGET = 40 * 1024 * 1024
HBM_BYTES_PER_STEP = 1 << 20


def _mm_tiles(m, n, k, mode, osize, n_unit, k_unit):
    lanes = (3840, 2816, 2048, 1280, 1024, 768, 512, 256)
    subl = (4352, 2176, 1088, 1024, 640, 544, 512, 256)
    tms = [c for c in (lanes if mode == "tn" else subl) if m % c == 0] or [m]
    tns = [c for c in lanes if n_unit % c == 0] or [n_unit]
    tks = [c for c in (subl if mode == "tn" else lanes) if k_unit % c == 0]
    if k_unit == k:
        tks = [k] + tks
    best = None
    for tm in tms:
        for tn in tns:
            for tk in tks:
                nk = k // tk
                vmem = 4 * (tm * tk + tk * tn) + 2 * tm * tn * osize + tm * tn * 4
                if vmem > MM_VMEM_BUDGET:
                    continue
                a_reads = 1 if nk == 1 else n // tn
                b_reads = 1 if (nk == 1 and n == tn) else m // tm
                steps = (m // tm) * (n // tn) * nk
                cost = (2 * m * k * a_reads + 2 * k * n * b_reads + m * n * osize + steps * HBM_BYTES_PER_STEP
                        + (12 * m * n * nk if nk > 1 else 0))
                if best is None or cost < best[0]:
                    best = (cost, tm, tn, tk)
    return best[1:]


def _mm(a, b, mode, out_dtype, name, a_split=False, b_split=False, o_split=False):
    def dims(x, split):
        return (x.shape[1], 2 * x.shape[2]) if split else x.shape

    ar, ac = dims(a, a_split)
    br, bc = dims(b, b_split)
    if mode == "nn":
        m, k, n = ar, ac, bc
        assert br == k
    elif mode == "nt":
        m, k, n = ar, ac, br
        assert bc == k
    else:
        k, m, n = ar, ac, bc
        assert br == k
    n_unit = n // 2 if (o_split or (b_split and mode != "nt")) else n
    k_unit = k // 2 if (mode != "tn" and (a_split or (b_split and mode == "nt"))) else k
    tm, tn, tk = _mm_tiles(m, n, k, mode, jnp.dtype(out_dtype).itemsize, n_unit, k_unit)
    nk = k // tk

    def spec(split, tr, tc, ncols, ridx, cidx):
        if not split:
            return pl.BlockSpec((tr, tc), lambda i, j, kk: (ridx(i, j, kk), cidx(i, j, kk)))
        nh = (ncols // 2) // tc
        return pl.BlockSpec((None, tr, tc), lambda i, j, kk: (cidx(i, j, kk) // nh, ridx(i, j, kk), cidx(i, j, kk) % nh))

    gi = lambda i, j, kk: i
    gj = lambda i, j, kk: j
    gk = lambda i, j, kk: kk
    if mode == "tn":
        a_spec = spec(a_split, tk, tm, ac, gk, gi)
    else:
        a_spec = spec(a_split, tm, tk, ac, gi, gk)
    if mode == "nt":
        b_spec = spec(b_split, tn, tk, bc, gj, gk)
    else:
        b_spec = spec(b_split, tk, tn, bc, gk, gj)
    o_spec = spec(o_split, tm, tn, n, gi, gj)
    dn = {"nn": (((1,), (0,)), ((), ())), "nt": (((1,), (1,)), ((), ())), "tn": (((0,), (0,)), ((), ()))}[mode]

    def body(a_ref, b_ref, o_ref, acc_ref):
        kk = pl.program_id(2)

        @pl.when(kk == 0)
        def _():
            acc_ref[...] = jnp.zeros_like(acc_ref)

        acc_ref[...] += lax.dot_general(a_ref[...], b_ref[...], dn, preferred_element_type=F32)

        @pl.when(kk == nk - 1)
        def _():
            o_ref[...] = acc_ref[...].astype(o_ref.dtype)

    def body_whole_k(a_ref, b_ref, o_ref):
        o_ref[...] = lax.dot_general(a_ref[...], b_ref[...], dn, preferred_element_type=F32).astype(o_ref.dtype)

    oshape = (2, m, n // 2) if o_split else (m, n)
    return pl.pallas_call(
        body if nk > 1 else body_whole_k, name=name, grid=(m // tm, n // tn, nk), in_specs=[a_spec, b_spec], out_specs=o_spec,
        out_shape=jax.ShapeDtypeStruct(oshape, out_dtype), scratch_shapes=[pltpu.VMEM((tm, tn), F32)] if nk > 1 else [],
        compiler_params=_params(("parallel", "parallel", "arbitrary")))(a, b)


def _mm_exact(a, b, name):
    def body(a_ref, b_ref, o_ref):
        o_ref[...] = jnp.dot(a_ref[...], b_ref[...], preferred_element_type=F32, precision=lax.Precision.HIGHEST)

    return pl.pallas_call(body, name=name, out_shape=jax.ShapeDtypeStruct((a.shape[0], b.shape[1]), F32),
                          compiler_params=_params())(a, b)


def _row_spec(width, col=0):
    return pl.BlockSpec((TM, width), lambda i: (i, col))


def _full_spec(shape):
    nd = len(shape)
    return pl.BlockSpec(shape, lambda *_: (0,) * nd)


def _normmod_fn(x, g, mod, is_ctx, row):
    return _rms(x, g) * (1.0 + _sel(is_ctx, mod, row + 1)) + _sel(is_ctx, mod, row)


def _normmod_fwd(x, g, mod, row, nbl, name):
    r, d = x.shape

    def body(x_ref, g_ref, mod_ref, h_ref):
        is_ctx = pl.program_id(0) >= nbl
        h_ref[...] = _normmod_fn(x_ref[...], g_ref[...], mod_ref[...], is_ctx, row).astype(BF16)

    return pl.pallas_call(
        body, name=name, grid=(r // TM,), in_specs=[_row_spec(d), _full_spec((1, d)), _full_spec(mod.shape)],
        out_specs=_row_spec(d), out_shape=jax.ShapeDtypeStruct((r, d), BF16), compiler_params=_params(("parallel",)))(x, g, mod)


def _normmod_bwd(x, g, mod, dh, dx_in, row, nbl, name):
    r, d = x.shape

    def body(x_ref, g_ref, mod_ref, dh_ref, dxin_ref, dx_ref, dg_ref, dmod_ref):
        i = pl.program_id(0)
        is_ctx = i >= nbl

        @pl.when(i == 0)
        def _():
            dg_ref[...] = jnp.zeros_like(dg_ref)
            dmod_ref[...] = jnp.zeros_like(dmod_ref)

        _, vjp = jax.vjp(lambda xx, gg, mm: _normmod_fn(xx, gg, mm, is_ctx, row), x_ref[...], g_ref[...], mod_ref[...])
        dx, dg, dmod = vjp(dh_ref[...])
        dx_ref[...] = dxin_ref[...] + dx
        dg_ref[...] += dg
        dmod_ref[...] += dmod

    return pl.pallas_call(
        body, name=name, grid=(r // TM,),
        in_specs=[_row_spec(d), _full_spec((1, d)), _full_spec(mod.shape), _row_spec(d), _row_spec(d)],
        out_specs=[_row_spec(d), _full_spec((1, d)), _full_spec(mod.shape)],
        out_shape=[jax.ShapeDtypeStruct((r, d), F32), jax.ShapeDtypeStruct((1, d), F32), jax.ShapeDtypeStruct(mod.shape, F32)],
        compiler_params=_params(("arbitrary",)))(x, g, mod, dh, dx_in)


def _resid_fwd(x, y, mod, row, nbl, name):
    r, d = x.shape

    def body(x_ref, y_ref, mod_ref, o_ref):
        is_ctx = pl.program_id(0) >= nbl
        o_ref[...] = x_ref[...] + _sel(is_ctx, mod_ref[...], row) * y_ref[...]

    return pl.pallas_call(
        body, name=name, grid=(r // TM,), in_specs=[_row_spec(d), _row_spec(d), _full_spec(mod.shape)],
        out_specs=_row_spec(d), out_shape=jax.ShapeDtypeStruct((r, d), F32), compiler_params=_params(("parallel",)))(x, y, mod)


def _resid_bwd(dx, y, mod, row, nbl, name):
    r, d = dx.shape

    def body(dx_ref, y_ref, mod_ref, dy_ref, dmod_ref):
        i = pl.program_id(0)
        is_ctx = i >= nbl

        @pl.when(i == 0)
        def _():
            dmod_ref[...] = jnp.zeros_like(dmod_ref)

        dxv = dx_ref[...]
        dy_ref[...] = (_sel(is_ctx, mod_ref[...], row) * dxv).astype(BF16)
        dgate = jnp.sum(dxv * y_ref[...], axis=0, keepdims=True)

        @pl.when(is_ctx)
        def _():
            dmod_ref[1, row:row + 1, :] += dgate

        @pl.when(jnp.logical_not(is_ctx))
        def _():
            dmod_ref[0, row:row + 1, :] += dgate

    return pl.pallas_call(
        body, name=name, grid=(r // TM,), in_specs=[_row_spec(d), _row_spec(d), _full_spec(mod.shape)],
        out_specs=[_row_spec(d), _full_spec(mod.shape)],
        out_shape=[jax.ShapeDtypeStruct((r, d), BF16), jax.ShapeDtypeStruct(mod.shape, F32)],
        compiler_params=_params(("arbitrary",)))(dx, y, mod)


def _loss_kernel(x, target, nbl, name):
    r, d = x.shape

    def body(x_ref, t_ref, loss_ref, dx_ref):
        i = pl.program_id(0)

        @pl.when(i == 0)
        def _():
            loss_ref[...] = jnp.zeros_like(loss_ref)

        @pl.when(i < nbl)
        def _():
            e = x_ref[...] - t_ref[...]
            dx_ref[...] = e / d
            loss_ref[...] += 0.5 * jnp.sum(jnp.mean(e * e, axis=-1, keepdims=True), axis=0, keepdims=True)

        @pl.when(i >= nbl)
        def _():
            dx_ref[...] = jnp.zeros_like(dx_ref)

    return pl.pallas_call(
        body, name=name, grid=(r // TM,),
        in_specs=[_row_spec(d), pl.BlockSpec((TM, d), lambda i: (jnp.minimum(i, nbl - 1), 0))],
        out_specs=[_full_spec((1, 1)), _row_spec(d)],
        out_shape=[jax.ShapeDtypeStruct((1, 1), F32), jax.ShapeDtypeStruct((r, d), F32)],
        compiler_params=_params(("arbitrary",)))(x, target)


SP_QA, SP_KVA, SP_QN, SP_QR, SP_KN, SP_KR, SP_SQ, SP_SK, SP_NQ, SP_NK, SP_SINK = range(11)
C_CQ, C_CKV, C_KR, C_POOL, C_SQ, C_SK, C_SV, C_NQ, C_NK, C_NV = 0, 512, 768, 896, 1408, 1920, 2048, 2176, 2688, 3200


def _prep_fn(p, tab, sp, wqb, wkvb):
    cos, sin = tab[:, 0:64], tab[:, 64:128]
    q = _bdot(_rms(p[:, C_CQ:C_CQ + 512], sp[SP_QA:SP_QA + 1, 0:512]), wqb)
    kv = _bdot(_rms(p[:, C_CKV:C_CKV + 256], sp[SP_KVA:SP_KVA + 1, 0:256]), wkvb)
    krr = _rope(_rms(p[:, C_KR:C_KR + 64], sp[SP_KR:SP_KR + 1, 0:64]), cos, sin)
    zero = jnp.zeros_like(krr)
    aq, ak, av = [], [], []
    for h in range(4):
        qn = _rms(q[:, 128 * h:128 * h + 128], sp[SP_QN:SP_QN + 1, 0:128])
        qr = _rope(_rms(q[:, 512 + 64 * h:576 + 64 * h], sp[SP_QR:SP_QR + 1, 0:64]), cos, sin)
        kn = _rms(kv[:, 256 * h:256 * h + 128], sp[SP_KN:SP_KN + 1, 0:128])
        aq += [qn, qr, zero]
        ak += [kn, krr, zero]
        av.append(kv[:, 256 * h + 128:256 * h + 256])
    cq = [_rope(_rms(p[:, C_SQ + 64 * h:C_SQ + 64 * h + 64], sp[SP_SQ:SP_SQ + 1, 0:64]), cos, sin) for h in range(8)]
    ck = [_rope(_rms(p[:, C_SK + 64 * h:C_SK + 64 * h + 64], sp[SP_SK:SP_SK + 1, 0:64]), cos, sin) for h in range(2)]
    dq = [_rms(p[:, C_NQ + 64 * h:C_NQ + 64 * h + 64], sp[SP_NQ:SP_NQ + 1, 0:64]) for h in range(8)]
    dk = [_rms(p[:, C_NK + 64 * h:C_NK + 64 * h + 64], sp[SP_NK:SP_NK + 1, 0:64]) for h in range(8)]
    cat = lambda xs: jnp.concatenate(xs, axis=-1)
    return (cat(aq), cat(ak), cat(av), cat(cq), cat(ck), p[:, C_SV:C_SV + 128], cat(dq), cat(dk), p[:, C_NV:C_NV + 512])


PREP_WIDTHS = (1024, 1024, 512, 512, 128, 128, 512, 512, 512)


def _prep_fwd(p, tab, sp, wqb, wkvb, name):
    r = p.shape[0]

    def body(p_ref, tab_ref, sp_ref, wqb_ref, wkvb_ref, *outs):
        res = _prep_fn(p_ref[...], tab_ref[...], sp_ref[...], wqb_ref[...].astype(F32), wkvb_ref[...].astype(F32))
        for o_ref, v in zip(outs, res):
            o_ref[...] = v.astype(BF16)

    return pl.pallas_call(
        body, name=name, grid=(r // TM,),
        in_specs=[_row_spec(PW), _row_spec(128), _full_spec(sp.shape), _full_spec(wqb.shape), _full_spec(wkvb.shape)],
        out_specs=[_row_spec(w) for w in PREP_WIDTHS],
        out_shape=[jax.ShapeDtypeStruct((r, w), BF16) for w in PREP_WIDTHS],
        compiler_params=_params(("parallel",)))(p, tab, sp, wqb, wkvb)


def _prep_bwd(p, tab, sp, wqb, wkvb, cots, dpool, name):
    r = p.shape[0]

    def body(p_ref, tab_ref, sp_ref, wqb_ref, wkvb_ref, *rest):
        cot_refs, dpool_ref = rest[:9], rest[9]
        dp_ref, dsp_ref, dwqb_ref, dwkvb_ref = rest[10:]
        i = pl.program_id(0)

        @pl.when(i == 0)
        def _():
            dsp_ref[...] = jnp.zeros_like(dsp_ref)
            dwqb_ref[...] = jnp.zeros_like(dwqb_ref)
            dwkvb_ref[...] = jnp.zeros_like(dwkvb_ref)

        tab = tab_ref[...]
        _, vjp = jax.vjp(lambda pp, ss, wq, wk: _prep_fn(pp, tab, ss, wq, wk),
                         p_ref[...], sp_ref[...], wqb_ref[...].astype(F32), wkvb_ref[...].astype(F32))
        dp, dsp, dwq, dwk = vjp(tuple(c[...] for c in cot_refs))
        dp_ref[...] = dp.astype(BF16)
        dp_ref[:, C_POOL:C_POOL + 512] = dpool_ref[...].astype(BF16)
        dsp_ref[...] += dsp
        dwqb_ref[...] += dwq
        dwkvb_ref[...] += dwk

    return pl.pallas_call(
        body, name=name, grid=(r // TM,),
        in_specs=[_row_spec(PW), _row_spec(128), _full_spec(sp.shape), _full_spec(wqb.shape), _full_spec(wkvb.shape)]
        + [_row_spec(w) for w in PREP_WIDTHS] + [_row_spec(512)],
        out_specs=[_row_spec(PW), _full_spec(sp.shape), _full_spec(wqb.shape), _full_spec(wkvb.shape)],
        out_shape=[jax.ShapeDtypeStruct((r, PW), BF16), jax.ShapeDtypeStruct(sp.shape, F32),
                   jax.ShapeDtypeStruct(wqb.shape, F32), jax.ShapeDtypeStruct(wkvb.shape, F32)],
        compiler_params=_params(("arbitrary",)))(p, tab, sp, wqb, wkvb, *cots, dpool)


def _mla_probs(q, k, is_ctx, n_lat):
    s = lax.dot_general(q, k, (((1,), (1,)), ((), ())), preferred_element_type=F32) * MLA_SCALE
    kid = lax.broadcasted_iota(I32, s.shape, 1)
    s = jnp.where(jnp.logical_and(is_ctx, kid < n_lat), NEG, s)
    e = jnp.exp(s - jnp.max(s, axis=-1, keepdims=True))
    return e / jnp.sum(e, axis=-1, keepdims=True)


def _mla_fwd(aq, ak, av, n_lat, name):
    r = aq.shape[0]
    nbl = n_lat // TM

    def body(q_ref, k_ref, v_ref, o_ref):
        p = _mla_probs(q_ref[...], k_ref[...], pl.program_id(1) >= nbl, n_lat)
        o_ref[...] = jnp.dot(p.astype(BF16), v_ref[...], preferred_element_type=F32).astype(BF16)

    return pl.pallas_call(
        body, name=name, grid=(4, r // TM),
        in_specs=[pl.BlockSpec((TM, 256), lambda h, i: (i, h)), pl.BlockSpec((r, 256), lambda h, i: (0, h)),
                  pl.BlockSpec((r, 128), lambda h, i: (0, h))],
        out_specs=pl.BlockSpec((TM, 128), lambda h, i: (i, h)),
        out_shape=jax.ShapeDtypeStruct((r, 512), BF16), compiler_params=_params(("parallel", "parallel")))(aq, ak, av)


def _mla_bwd(aq, ak, av, dmix, n_lat, name):
    r = aq.shape[0]
    nbl = n_lat // TM

    def body(q_ref, k_ref, v_ref, do_ref, dq_ref, dk_ref, dv_ref):
        i = pl.program_id(1)

        @pl.when(i == 0)
        def _():
            dk_ref[...] = jnp.zeros_like(dk_ref)
            dv_ref[...] = jnp.zeros_like(dv_ref)

        q, k, v = q_ref[...], k_ref[...], v_ref[...]
        dob = do_ref[...].astype(BF16)
        p = _mla_probs(q, k, i >= nbl, n_lat)
        dv_ref[...] += lax.dot_general(p.astype(BF16), dob, (((0,), (0,)), ((), ())), preferred_element_type=F32)
        dp = lax.dot_general(dob, v, (((1,), (1,)), ((), ())), preferred_element_type=F32)
        ds = (p * (dp - jnp.sum(dp * p, axis=-1, keepdims=True)) * MLA_SCALE).astype(BF16)
        dq_ref[...] = jnp.dot(ds, k, preferred_element_type=F32)
        dk_ref[...] += lax.dot_general(ds, q, (((0,), (0,)), ((), ())), preferred_element_type=F32)

    return pl.pallas_call(
        body, name=name, grid=(4, r // TM),
        in_specs=[pl.BlockSpec((TM, 256), lambda h, i: (i, h)), pl.BlockSpec((r, 256), lambda h, i: (0, h)),
                  pl.BlockSpec((r, 128), lambda h, i: (0, h)), pl.BlockSpec((TM, 128), lambda h, i: (i, h))],
        out_specs=[pl.BlockSpec((TM, 256), lambda h, i: (i, h)), pl.BlockSpec((r, 256), lambda h, i: (0, h)),
                   pl.BlockSpec((r, 128), lambda h, i: (0, h))],
        out_shape=[jax.ShapeDtypeStruct((r, 1024), F32), jax.ShapeDtypeStruct((r, 1024), F32), jax.ShapeDtypeStruct((r, 512), F32)],
        compiler_params=_params(("parallel", "arbitrary")))(aq, ak, av, dmix)


def _pool_fn(ext, w, sc, gid0, grp, is_ctx, n_lat, r_all):
    gid = gid0 + lax.broadcasted_iota(I32, (TM + 2 * HALO, 1), 0)
    lo = jnp.where(is_ctx, n_lat, 0)
    hi = jnp.where(is_ctx, r_all, n_lat)
    z = jnp.where(jnp.logical_and(gid >= lo, gid < hi), ext, 0.0)
    w2 = jnp.roll(z, 1, axis=0) + z
    w4 = jnp.roll(w2, 1, axis=0) + jnp.roll(w2, -1, axis=0)
    w8 = jnp.roll(w4, 2, axis=0) + jnp.roll(w4, -2, axis=0)
    w16 = jnp.roll(w8, 4, axis=0) + jnp.roll(w8, -4, axis=0)
    win = jnp.where(grp == 0, w2, jnp.where(grp == 1, w4, jnp.where(grp == 2, w8, w16)))
    half = jnp.left_shift(1, grp)
    cnt = jnp.maximum(jnp.minimum(gid + half, hi) - jnp.maximum(gid - half, lo), 1).astype(F32)
    d = (win / cnt - ext)[HALO:HALO + TM]
    return _bdot(d, w) * sc


def _pool_ext(u_ref, i, r_all):
    s0 = pl.multiple_of(jnp.maximum(i * TM - HALO, 0), HALO)
    s2 = pl.multiple_of(jnp.minimum(i * TM + TM, r_all - HALO), HALO)
    ext = jnp.concatenate([u_ref[pl.ds(s0, HALO), :], u_ref[pl.ds(pl.multiple_of(i * TM, TM), TM), :], u_ref[pl.ds(s2, HALO), :]], axis=0)
    return ext, s0, s2


def _pool_specs(r):
    return [pl.BlockSpec((r, 128), lambda g, i: (0, C_POOL // 128 + g)), pl.BlockSpec((None, 128, 128), lambda g, i: (g, 0, 0)),
            pl.BlockSpec((None, 1, 128), lambda g, i: (g, 0, 0))]


def _pool_fwd(p, pool_w, pool_sc, n_lat, name):
    r = p.shape[0]
    nbl = n_lat // TM

    def body(u_ref, w_ref, sc_ref, o_ref):
        g, i = pl.program_id(0), pl.program_id(1)
        ext, _, _ = _pool_ext(u_ref, i, r)
        o_ref[...] = _pool_fn(ext, w_ref[...], sc_ref[...], i * TM - HALO, g, i >= nbl, n_lat, r).astype(BF16)

    return pl.pallas_call(
        body, name=name, grid=(4, r // TM), in_specs=_pool_specs(r), out_specs=pl.BlockSpec((TM, 128), lambda g, i: (i, g)),
        out_shape=jax.ShapeDtypeStruct((r, 512), BF16), compiler_params=_params(("parallel", "parallel")))(p, pool_w, pool_sc)


def _pool_bwd(p, pool_w, pool_sc, dmix, n_lat, name):
    r = p.shape[0]
    nbl = n_lat // TM

    def body(u_ref, w_ref, sc_ref, do_ref, du_ref, dw_ref, dsc_ref):
        g, i = pl.program_id(0), pl.program_id(1)

        @pl.when(i == 0)
        def _():
            du_ref[...] = jnp.zeros_like(du_ref)
            dw_ref[...] = jnp.zeros_like(dw_ref)
            dsc_ref[...] = jnp.zeros_like(dsc_ref)

        ext, s0, s2 = _pool_ext(u_ref, i, r)
        _, vjp = jax.vjp(lambda e, w, s: _pool_fn(e, w, s, i * TM - HALO, g, i >= nbl, n_lat, r), ext, w_ref[...], sc_ref[...])
        dext, dw, dsc = vjp(do_ref[...])
        du_ref[pl.ds(s0, HALO), :] += dext[0:HALO]
        du_ref[pl.ds(pl.multiple_of(i * TM, TM), TM), :] += dext[HALO:HALO + TM]
        du_ref[pl.ds(s2, HALO), :] += dext[HALO + TM:]
        dw_ref[...] += dw
        dsc_ref[...] += dsc

    return pl.pallas_call(
        body, name=name, grid=(4, r // TM), in_specs=_pool_specs(r) + [pl.BlockSpec((TM, 128), lambda g, i: (i, 4 + g))],
        out_specs=[pl.BlockSpec((r, 128), lambda g, i: (0, g)), pl.BlockSpec((None, 128, 128), lambda g, i: (g, 0, 0)),
                   pl.BlockSpec((None, 1, 128), lambda g, i: (g, 0, 0))],
        out_shape=[jax.ShapeDtypeStruct((r, 512), F32), jax.ShapeDtypeStruct((4, 128, 128), F32), jax.ShapeDtypeStruct((4, 1, 128), F32)],
        compiler_params=_params(("parallel", "arbitrary")))(p, pool_w, pool_sc, dmix)


def _softmax_parts(parts, extra=None):
    m = functools.reduce(jnp.maximum, [jnp.max(s, axis=-1, keepdims=True) for s in parts])
    if extra is not None:
        m = jnp.maximum(m, extra)
    m = lax.stop_gradient(m)
    es = [jnp.exp(s - m) for s in parts]
    den = functools.reduce(jnp.add, [jnp.sum(e, axis=-1, keepdims=True) for e in es])
    if extra is not None:
        den = den + jnp.exp(extra - m)
    return [e / den for e in es]


def _swa_fn(q4, kw, vw, kc, vc, sink4, qpos0, kpos0, is_ctx):
    qs = jnp.concatenate([q4[:, 64 * g:64 * g + 64] for g in range(4)], axis=0)
    s_loc = _bdot_nt(qs, kw) * HD_SCALE
    s_ctx = _bdot_nt(qs, kc) * HD_SCALE
    qpos = qpos0 + jnp.bitwise_and(lax.broadcasted_iota(I32, s_loc.shape, 0), TM - 1)
    kpos = kpos0 + lax.broadcasted_iota(I32, s_loc.shape, 1)
    valid = jnp.logical_and(jnp.abs(kpos - qpos) <= 128, jnp.logical_not(is_ctx))
    s_loc = jnp.where(valid, s_loc, NEG)
    sink = jnp.concatenate([jnp.broadcast_to(sink4[:, g:g + 1], (TM, 1)) for g in range(4)], axis=0)
    p_loc, p_ctx = _softmax_parts([s_loc, s_ctx], sink)
    o = _bdot(p_loc, vw) + _bdot(p_ctx, vc)
    return jnp.concatenate([o[TM * g:TM * (g + 1)] for g in range(4)], axis=1)


def _swa_window(i, n_lat):
    return pl.multiple_of(jnp.clip(i * TM - 128, 0, n_lat - SWA_KEYS), 128)


def _swa_fwd(cq, ck, cv, sp, n_lat, name):
    r = cq.shape[0]
    nbl = n_lat // TM

    def body(q_ref, k_ref, v_ref, sp_ref, o_ref):
        i = pl.program_id(0)
        k0 = _swa_window(i, n_lat)
        kw, vw = k_ref[pl.ds(k0, SWA_KEYS), :].astype(F32), v_ref[pl.ds(k0, SWA_KEYS), :].astype(F32)
        kc, vc = k_ref[pl.ds(n_lat, r - n_lat), :].astype(F32), v_ref[pl.ds(n_lat, r - n_lat), :].astype(F32)
        for j in range(2):
            c = slice(64 * j, 64 * j + 64)
            o = _swa_fn(q_ref[:, 256 * j:256 * j + 256].astype(F32), kw[:, c], vw[:, c], kc[:, c], vc[:, c],
                        sp_ref[SP_SINK:SP_SINK + 1, 4 * j:4 * j + 4], i * TM, k0, i >= nbl)
            o_ref[:, 256 * j:256 * j + 256] = o.astype(BF16)

    return pl.pallas_call(
        body, name=name, grid=(r // TM,),
        in_specs=[_row_spec(512), _full_spec((r, 128)), _full_spec((r, 128)), _full_spec(sp.shape)],
        out_specs=_row_spec(512), out_shape=jax.ShapeDtypeStruct((r, 512), BF16), compiler_params=_params(("parallel",)))(cq, ck, cv, sp)


def _swa_bwd(cq, ck, cv, sp, dmix, n_lat, name):
    r = cq.shape[0]
    nbl = n_lat // TM
    nc = r - n_lat

    def body(q_ref, k_ref, v_ref, sp_ref, do_ref, dq_ref, dk_ref, dv_ref, dsp_ref):
        i = pl.program_id(0)

        @pl.when(i == 0)
        def _():
            dk_ref[...] = jnp.zeros_like(dk_ref)
            dv_ref[...] = jnp.zeros_like(dv_ref)
            dsp_ref[...] = jnp.zeros_like(dsp_ref)

        k0 = _swa_window(i, n_lat)
        kw, vw = k_ref[pl.ds(k0, SWA_KEYS), :].astype(F32), v_ref[pl.ds(k0, SWA_KEYS), :].astype(F32)
        kc, vc = k_ref[pl.ds(n_lat, nc), :].astype(F32), v_ref[pl.ds(n_lat, nc), :].astype(F32)
        dkw, dvw, dkc, dvc, dsk = [], [], [], [], []
        for j in range(2):
            c = slice(64 * j, 64 * j + 64)
            _, vjp = jax.vjp(lambda q4, a, b, cc, d, s: _swa_fn(q4, a, b, cc, d, s, i * TM, k0, i >= nbl),
                             q_ref[:, 256 * j:256 * j + 256].astype(F32), kw[:, c], vw[:, c], kc[:, c], vc[:, c],
                             sp_ref[SP_SINK:SP_SINK + 1, 4 * j:4 * j + 4])
            dq4, a, b, cc, d, s = vjp(do_ref[:, 256 * j:256 * j + 256])
            dq_ref[:, 256 * j:256 * j + 256] = dq4
            dkw.append(a), dvw.append(b), dkc.append(cc), dvc.append(d), dsk.append(s)
        cat = lambda xs: jnp.concatenate(xs, axis=1)
        dk_ref[pl.ds(k0, SWA_KEYS), :] += cat(dkw)
        dv_ref[pl.ds(k0, SWA_KEYS), :] += cat(dvw)
        dk_ref[pl.ds(n_lat, nc), :] += cat(dkc)
        dv_ref[pl.ds(n_lat, nc), :] += cat(dvc)
        dsp_ref[SP_SINK:SP_SINK + 1, 0:8] += cat(dsk)

    return pl.pallas_call(
        body, name=name, grid=(r // TM,),
        in_specs=[_row_spec(512), _full_spec((r, 128)), _full_spec((r, 128)), _full_spec(sp.shape), _row_spec(512, 2)],
        out_specs=[_row_spec(512), _full_spec((r, 128)), _full_spec((r, 128)), _full_spec(sp.shape)],
        out_shape=[jax.ShapeDtypeStruct((r, 512), F32), jax.ShapeDtypeStruct((r, 128), F32), jax.ShapeDtypeStruct((r, 128), F32),
                   jax.ShapeDtypeStruct(sp.shape, F32)],
        compiler_params=_params(("arbitrary",)))(cq, ck, cv, sp, dmix)


def _na_fn(q, kw, vw, kc, vc, bias, valid):
    s_loc = jnp.where(valid, _bdot_nt(q, kw) * HD_SCALE + bias, NEG)
    s_ctx = _bdot_nt(q, kc) * HD_SCALE
    p_loc, p_ctx = _softmax_parts([s_loc, s_ctx])
    return _bdot(p_loc, vw) + _bdot(p_ctx, vc)


def _na_geometry(i, n_lat, is_ctx):
    rows = n_lat // GRID_W
    qrow0 = i * (TM // GRID_W)
    krow0 = jnp.clip(qrow0 - 4, 0, rows - NA_KROWS)
    nk = NA_KROWS * GRID_W
    tq = i * TM + lax.broadcasted_iota(I32, (TM, nk), 0)
    tk = krow0 * GRID_W + lax.broadcasted_iota(I32, (TM, nk), 1)
    qr, qc = jnp.right_shift(tq, 6), jnp.bitwise_and(tq, GRID_W - 1)
    kr, kc = jnp.right_shift(tk, 6), jnp.bitwise_and(tk, GRID_W - 1)
    r_lo = jnp.clip(qr - 4, 0, rows - 8)
    c_lo = jnp.clip(qc - 8, 0, GRID_W - 16)
    valid = (kr >= r_lo) & (kr < r_lo + 8) & (kc >= c_lo) & (kc < c_lo + 16) & jnp.logical_not(is_ctx)
    dr = [[jnp.clip(krow0 + kj - (qrow0 + qi) + 7, 0, 14) for kj in range(NA_KROWS)] for qi in range(TM // GRID_W)]
    return pl.multiple_of(krow0 * GRID_W, GRID_W), valid, dr


def _na_bias(t1_ref, hh, dr):
    return jnp.concatenate([jnp.concatenate([t1_ref[hh, dr[qi][kj]] for kj in range(NA_KROWS)], axis=1)
                            for qi in range(TM // GRID_W)], axis=0)


def _na_specs(r):
    return [pl.BlockSpec((TM, 128), lambda pr, i: (i, pr)), pl.BlockSpec((r, 128), lambda pr, i: (0, pr)),
            pl.BlockSpec((r, 128), lambda pr, i: (0, pr)), pl.BlockSpec((2, 16, GRID_W, GRID_W), lambda pr, i: (pr, 0, 0, 0))]


def _na_fwd(dq, dk, dv, t1, n_lat, name):
    r = dq.shape[0]
    nbl = n_lat // TM
    nc = r - n_lat
    nk = NA_KROWS * GRID_W

    def body(q_ref, k_ref, v_ref, t1_ref, o_ref):
        i = pl.program_id(1)
        k0, valid, dr = _na_geometry(jnp.minimum(i, nbl - 1), n_lat, i >= nbl)
        kw, vw = k_ref[pl.ds(k0, nk), :].astype(F32), v_ref[pl.ds(k0, nk), :].astype(F32)
        kc, vc = k_ref[pl.ds(n_lat, nc), :].astype(F32), v_ref[pl.ds(n_lat, nc), :].astype(F32)
        for hh in range(2):
            c = slice(64 * hh, 64 * hh + 64)
            o = _na_fn(q_ref[:, c].astype(F32), kw[:, c], vw[:, c], kc[:, c], vc[:, c], _na_bias(t1_ref, hh, dr), valid)
            o_ref[:, c] = o.astype(BF16)

    return pl.pallas_call(
        body, name=name, grid=(4, r // TM), in_specs=_na_specs(r), out_specs=pl.BlockSpec((TM, 128), lambda pr, i: (i, pr)),
        out_shape=jax.ShapeDtypeStruct((r, 512), BF16), compiler_params=_params(("parallel", "parallel")))(dq, dk, dv, t1)


def _na_bwd(dq, dk, dv, t1, dmix, n_lat, name):
    r = dq.shape[0]
    nbl = n_lat // TM
    nc = r - n_lat
    nk = NA_KROWS * GRID_W

    def body(q_ref, k_ref, v_ref, t1_ref, do_ref, dq_ref, dk_ref, dv_ref, dt1_ref):
        i = pl.program_id(1)

        @pl.when(i == 0)
        def _():
            dk_ref[...] = jnp.zeros_like(dk_ref)
            dv_ref[...] = jnp.zeros_like(dv_ref)
            dt1_ref[...] = jnp.zeros_like(dt1_ref)

        k0, valid, dr = _na_geometry(jnp.minimum(i, nbl - 1), n_lat, i >= nbl)
        kw, vw = k_ref[pl.ds(k0, nk), :].astype(F32), v_ref[pl.ds(k0, nk), :].astype(F32)
        kc, vc = k_ref[pl.ds(n_lat, nc), :].astype(F32), v_ref[pl.ds(n_lat, nc), :].astype(F32)
        dkw, dvw, dkc, dvc = [], [], [], []
        for hh in range(2):
            c = slice(64 * hh, 64 * hh + 64)
            _, vjp = jax.vjp(lambda q, a, b, cc, d, bb: _na_fn(q, a, b, cc, d, bb, valid),
                             q_ref[:, c].astype(F32), kw[:, c], vw[:, c], kc[:, c], vc[:, c], _na_bias(t1_ref, hh, dr))
            dqh, a, b, cc, d, dbias = vjp(do_ref[:, c])
            dq_ref[:, c] = dqh
            dkw.append(a), dvw.append(b), dkc.append(cc), dvc.append(d)
            for qi in range(TM // GRID_W):
                for kj in range(NA_KROWS):
                    dt1_ref[hh, dr[qi][kj]] += dbias[GRID_W * qi:GRID_W * (qi + 1), GRID_W * kj:GRID_W * (kj + 1)]
        cat = lambda xs: jnp.concatenate(xs, axis=1)
        dk_ref[pl.ds(k0, nk), :] += cat(dkw)
        dv_ref[pl.ds(k0, nk), :] += cat(dvw)
        dk_ref[pl.ds(n_lat, nc), :] += cat(dkc)
        dv_ref[pl.ds(n_lat, nc), :] += cat(dvc)

    return pl.pallas_call(
        body, name=name, grid=(4, r // TM), in_specs=_na_specs(r) + [pl.BlockSpec((TM, 128), lambda pr, i: (i, 12 + pr))],
        out_specs=[pl.BlockSpec((TM, 128), lambda pr, i: (i, pr)), pl.BlockSpec((r, 128), lambda pr, i: (0, pr)),
                   pl.BlockSpec((r, 128), lambda pr, i: (0, pr)), pl.BlockSpec((2, 16, GRID_W, GRID_W), lambda pr, i: (pr, 0, 0, 0))],
        out_shape=[jax.ShapeDtypeStruct((r, 512), F32)] * 3 + [jax.ShapeDtypeStruct((8, 16, GRID_W, GRID_W), F32)],
        compiler_params=_params(("parallel", "arbitrary")))(dq, dk, dv, t1, dmix)


def _conv_ext(main_ref, prev_ref, next_ref, edges):
    prev_ok, next_ok = edges
    return jnp.concatenate([jnp.where(prev_ok, prev_ref[...], 0.0), main_ref[...], jnp.where(next_ok, next_ref[...], 0.0)], axis=0)


def _conv_edges(i, nbl, nb):
    return jnp.logical_and(i != 0, i != nbl), jnp.logical_and(i != nbl - 1, i != nb - 1)


def _conv_apply(ext, w, b):
    up = jnp.roll(ext, 1, axis=0)
    dn = jnp.roll(ext, -1, axis=0)
    return up * w[0:1] + ext * w[1:2] + dn * w[2:3] + b, up, dn


def _conv_in_specs(tc, r):
    nb8 = TM // HALO
    last8 = r // HALO - 1

    def trio(half):
        return [pl.BlockSpec((None, TM, tc), lambda j, i: (half, i, j)),
                pl.BlockSpec((None, HALO, tc), lambda j, i: (half, jnp.maximum(i * nb8 - 1, 0), j)),
                pl.BlockSpec((None, HALO, tc), lambda j, i: (half, jnp.minimum((i + 1) * nb8, last8), j))]

    wb = [pl.BlockSpec((None, 3, tc), lambda j, i: (0, 0, j)), pl.BlockSpec((None, 3, tc), lambda j, i: (1, 0, j)),
          pl.BlockSpec((None, 1, tc), lambda j, i: (0, 0, j)), pl.BlockSpec((None, 1, tc), lambda j, i: (1, 0, j))]
    return trio(0) + trio(1) + wb


def _convgate_fwd(a3, cw, cb, n_lat, name):
    _, r, ff = a3.shape
    nbl = n_lat // TM
    tc = _pick(ff, (512, 256, 128))

    def body(g_ref, gp_ref, gn_ref, v_ref, vp_ref, vn_ref, wg_ref, wv_ref, bg_ref, bv_ref, u_ref):
        edges = _conv_edges(pl.program_id(1), nbl, r // TM)
        gg, _, _ = _conv_apply(_conv_ext(g_ref, gp_ref, gn_ref, edges), wg_ref[...], bg_ref[...])
        gv, _, _ = _conv_apply(_conv_ext(v_ref, vp_ref, vn_ref, edges), wv_ref[...], bv_ref[...])
        u_ref[...] = (jax.nn.silu(gg[HALO:HALO + TM]) * gv[HALO:HALO + TM]).astype(BF16)

    return pl.pallas_call(
        body, name=name, grid=(ff // tc, r // TM), in_specs=_conv_in_specs(tc, r),
        out_specs=pl.BlockSpec((TM, tc), lambda j, i: (i, j)), out_shape=jax.ShapeDtypeStruct((r, ff), BF16),
        compiler_params=_params(("parallel", "parallel")))(a3, a3, a3, a3, a3, a3, cw, cw, cb, cb)


def _convgate_bwd(a3, cw, cb, du, n_lat, name):
    _, r, ff = a3.shape
    nbl = n_lat // TM
    tc = _pick(ff, (512, 256, 128))
    nb8 = TM // HALO
    last8 = r // HALO - 1

    def body(g_ref, gp_ref, gn_ref, v_ref, vp_ref, vn_ref, wg_ref, wv_ref, bg_ref, bv_ref, du_ref, dup_ref, dun_ref,
             da_ref, dcw_ref, dcb_ref):
        i = pl.program_id(1)

        @pl.when(i == 0)
        def _():
            dcw_ref[...] = jnp.zeros_like(dcw_ref)
            dcb_ref[...] = jnp.zeros_like(dcb_ref)

        edges = _conv_edges(i, nbl, r // TM)
        wg, wv = wg_ref[...], wv_ref[...]
        eg, ev = _conv_ext(g_ref, gp_ref, gn_ref, edges), _conv_ext(v_ref, vp_ref, vn_ref, edges)
        gg, ug, dg_ = _conv_apply(eg, wg, bg_ref[...])
        gv, uv, dv_ = _conv_apply(ev, wv, bv_ref[...])
        due = _conv_ext(du_ref, dup_ref, dun_ref, edges)
        sg = jax.nn.sigmoid(gg)
        dgg = due * gv * (sg * (1.0 + gg * (1.0 - sg)))
        dgv = due * (gg * sg)
        main = slice(HALO, HALO + TM)
        for h, (dgx, w, ex, upx, dnx) in enumerate(((dgg, wg, eg, ug, dg_), (dgv, wv, ev, uv, dv_))):
            da = dgx * w[1:2] + jnp.roll(dgx, -1, axis=0) * w[0:1] + jnp.roll(dgx, 1, axis=0) * w[2:3]
            da_ref[h] = da[main].astype(BF16)
            dm = dgx[main]
            dcw_ref[h, 0:1, :] += jnp.sum(dm * upx[main], axis=0, keepdims=True)
            dcw_ref[h, 1:2, :] += jnp.sum(dm * ex[main], axis=0, keepdims=True)
            dcw_ref[h, 2:3, :] += jnp.sum(dm * dnx[main], axis=0, keepdims=True)
            dcb_ref[h] += jnp.sum(dm, axis=0, keepdims=True)

    du_specs = [pl.BlockSpec((TM, tc), lambda j, i: (i, j)),
                pl.BlockSpec((HALO, tc), lambda j, i: (jnp.maximum(i * nb8 - 1, 0), j)),
                pl.BlockSpec((HALO, tc), lambda j, i: (jnp.minimum((i + 1) * nb8, last8), j))]
    return pl.pallas_call(
        body, name=name, grid=(ff // tc, r // TM), in_specs=_conv_in_specs(tc, r) + du_specs,
        out_specs=[pl.BlockSpec((2, TM, tc), lambda j, i: (0, i, j)), pl.BlockSpec((2, 3, tc), lambda j, i: (0, 0, j)),
                   pl.BlockSpec((2, 1, tc), lambda j, i: (0, 0, j))],
        out_shape=[jax.ShapeDtypeStruct((2, r, ff), BF16), jax.ShapeDtypeStruct((2, 3, ff), F32), jax.ShapeDtypeStruct((2, 1, ff), F32)],
        compiler_params=_params(("parallel", "arbitrary")))(a3, a3, a3, a3, a3, a3, cw, cw, cb, cb, du, du, du)


def _layer_fwd(x, w, l, tab, n_lat):
    nbl = n_lat // TM
    mod = w["mods"][l]
    h1 = _normmod_fwd(x, w["g_mix"][l], mod, 0, nbl, "normmod_fwd")
    p = _mm(h1, w["w_in"][l], "nn", F32, "mm_in")
    qkv = _prep_fwd(p, tab, w["sp"][l], w["w_qb"][l], w["w_kvb"][l], "prep_fwd")
    oa = _mla_fwd(qkv[0], qkv[1], qkv[2], n_lat, "mla_fwd")
    ob = _pool_fwd(p, w["pool_w"][l], w["pool_sc"][l], n_lat, "pool_fwd")
    oc = _swa_fwd(qkv[3], qkv[4], qkv[5], w["sp"][l], n_lat, "swa_fwd")
    od = _na_fwd(qkv[6], qkv[7], qkv[8], w["t1"][l], n_lat, "na_fwd")
    mix = jnp.concatenate([oa, ob, oc, od], axis=1)
    y = _mm(mix, w["w_out"][l], "nn", F32, "mm_out")
    x1 = _resid_fwd(x, y, mod, 2, nbl, "resid_fwd")
    h2 = _normmod_fwd(x1, w["g_ffn"][l], mod, 3, nbl, "normmod_fwd")
    a3 = _mm(h2, w["w_up"][l], "nn", F32, "mm_up", o_split=True)
    u = _convgate_fwd(a3, w["conv_w"][l], w["conv_b"][l], n_lat, "convgate_fwd")
    y2 = _mm(u, w["w_down"][l], "nn", F32, "mm_down")
    x2 = _resid_fwd(x1, y2, mod, 5, nbl, "resid_fwd")
    return x2, dict(x=x, h1=h1, p=p, qkv=qkv, mix=mix, y=y, x1=x1, h2=h2, a3=a3, u=u, y2=y2)


def _layer_bwd(dx, s, w, l, tab, n_lat):
    nbl = n_lat // TM
    mod = w["mods"][l]
    g = {}
    dy2, dmod_a = _resid_bwd(dx, s["y2"], mod, 5, nbl, "resid_bwd")
    g["w_down"] = _mm(s["u"], dy2, "tn", F32, "mm_dwdown")
    du = _mm(dy2, w["w_down"][l], "nt", F32, "mm_du")
    da3, g["conv_w"], g["conv_b"] = _convgate_bwd(s["a3"], w["conv_w"][l], w["conv_b"][l], du, n_lat, "convgate_bwd")
    g["w_up"] = _mm(s["h2"], da3, "tn", F32, "mm_dwup", b_split=True)
    dh2 = _mm(da3, w["w_up"][l], "nt", F32, "mm_dh2", a_split=True)
    dx1, g["g_ffn"], dmod_b = _normmod_bwd(s["x1"], w["g_ffn"][l], mod, dh2, dx, 3, nbl, "normmod_bwd")
    dy, dmod_c = _resid_bwd(dx1, s["y"], mod, 2, nbl, "resid_bwd")
    g["w_out"] = _mm(s["mix"], dy, "tn", F32, "mm_dwout")
    dmix = _mm(dy, w["w_out"][l], "nt", F32, "mm_dmix")
    qkv = s["qkv"]
    daq, dak, dav = _mla_bwd(qkv[0], qkv[1], qkv[2], dmix, n_lat, "mla_bwd")
    dpool, g["pool_w"], g["pool_sc"] = _pool_bwd(s["p"], w["pool_w"][l], w["pool_sc"][l], dmix, n_lat, "pool_bwd")
    dcq, dck, dcv, dsp_c = _swa_bwd(qkv[3], qkv[4], qkv[5], w["sp"][l], dmix, n_lat, "swa_bwd")
    ddq, ddk, ddv, g["t1"] = _na_bwd(qkv[6], qkv[7], qkv[8], w["t1"][l], dmix, n_lat, "na_bwd")
    dp, dsp_p, g["w_qb"], g["w_kvb"] = _prep_bwd(s["p"], tab, w["sp"][l], w["w_qb"][l], w["w_kvb"][l],
                                                (daq, dak, dav, dcq, dck, dcv, ddq, ddk, ddv), dpool, "prep_bwd")
    g["sp"] = dsp_c + dsp_p
    g["w_in"] = _mm(s["h1"], dp, "tn", F32, "mm_dwin")
    dh1 = _mm(dp, w["w_in"][l], "nt", F32, "mm_dh1")
    dx0, g["g_mix"], dmod_d = _normmod_bwd(s["x"], w["g_mix"][l], mod, dh1, dx1, 0, nbl, "normmod_bwd")
    g["mods"] = dmod_a + dmod_b + dmod_c + dmod_d
    return dx0, g


def _local_step(x_all, target, w, tab, n_lat):
    saved = []
    x = x_all
    for l in range(DEPTH):
        x, s = _layer_fwd(x, w, l, tab, n_lat)
        saved.append(s)
    loss, dx = _loss_kernel(x, target, n_lat // TM, "loss")
    grads = [None] * DEPTH
    for l in reversed(range(DEPTH)):
        dx, grads[l] = _layer_bwd(dx, saved[l], w, l, tab, n_lat)
    return loss[0, 0], dx, grads


def _pad_cols(a, widths):
    parts, o = [], 0
    for take, pad in widths:
        parts.append(a[..., o:o + take])
        if pad:
            parts.append(jnp.zeros(a.shape[:-1] + (pad,), a.dtype))
        o += take
    return jnp.concatenate(parts, axis=-1)


def _w_in_layout(w_in):
    return _pad_cols(w_in, [(832, 64), (P_COLS - 832, PW - P_COLS - 64)])


def _w_in_unlayout(g):
    return jnp.concatenate([g[..., 0:832], g[..., 896:896 + P_COLS - 832]], axis=-1)


def _w_qb_layout(w):
    s = w.reshape(w.shape[:-1] + (4, 192))
    return jnp.concatenate([s[..., 0:128].reshape(w.shape[:-1] + (512,)), s[..., 128:192].reshape(w.shape[:-1] + (256,))], axis=-1)


def _w_qb_unlayout(g):
    n = g[..., 0:512].reshape(g.shape[:-1] + (4, 128))
    r = g[..., 512:768].reshape(g.shape[:-1] + (4, 64))
    return jnp.concatenate([n, r], axis=-1).reshape(g.shape[:-1] + (768,))


SP_SLOTS = (("mla_q_a_norm", 512), ("mla_kv_a_norm", 256), ("mla_q_nope_norm", 128), ("mla_q_rope_norm", 64),
            ("mla_k_nope_norm", 128), ("mla_k_rope_norm", 64), ("swa_q_norm", 64), ("swa_k_norm", 64),
            ("na_q_norm", 64), ("na_k_norm", 64), ("swa_sink", 8))


def _sp_pack(small):
    rows = [jnp.pad(small[k], ((0, 0), (0, 512 - n))) for k, n in SP_SLOTS]
    rows += [jnp.zeros_like(rows[0])] * (16 - len(rows))
    return jnp.stack(rows, axis=1)


def _sp_unpack(sp):
    return {k: sp[:, i, 0:n] for i, (k, n) in enumerate(SP_SLOTS)}


def _rpb_onehot():
    qc = lax.broadcasted_iota(I32, (GRID_W, GRID_W), 0)
    kc = lax.broadcasted_iota(I32, (GRID_W, GRID_W), 1)
    dc = (jnp.clip(kc - qc, -15, 15) + 15).reshape(1, GRID_W * GRID_W)
    return (lax.broadcasted_iota(I32, (32, GRID_W * GRID_W), 0) == dc).astype(F32)


def _rpb_expand(rpb):
    l = rpb.shape[0]
    flat = jnp.pad(rpb, ((0, 0), (0, 0), (0, 1), (0, 1))).reshape(l * 128, 32)
    return _mm_exact(flat, _rpb_onehot(), "rpb_expand").reshape(l, 8, 16, GRID_W, GRID_W)


def _rpb_fold(dt1):
    l = dt1.shape[0]
    g = _mm_exact(dt1.reshape(l * 128, GRID_W * GRID_W), _rpb_onehot().T, "rpb_fold")
    return g.reshape(l, 8, 16, 32)[:, :, 0:15, 0:31]


def _rope_table(n_lat, n_ctx):
    t = jnp.arange(n_lat)
    inv = ROPE_BASE ** (-jnp.arange(0, 32, 2, dtype=F32) / 32)
    ar = (t // GRID_W).astype(F32)[:, None] * inv
    ac = (t % GRID_W).astype(F32)[:, None] * inv
    cos = jnp.concatenate([jnp.cos(ar), jnp.cos(ar), jnp.cos(ac), jnp.cos(ac)], axis=-1)
    sin = jnp.concatenate([jnp.sin(ar), jnp.sin(ar), jnp.sin(ac), jnp.sin(ac)], axis=-1)
    tab = jnp.concatenate([cos, sin], axis=-1)
    ident = jnp.concatenate([jnp.ones((n_ctx, 64), F32), jnp.zeros((n_ctx, 64), F32)], axis=-1)
    return jnp.concatenate([tab, ident], axis=0)


def _small_weights(full, mods):
    l = full["g_mix"].shape[0]
    ff = full["ffn_conv_b"].shape[1] // 2
    return dict(
        mods=mods, g_mix=full["g_mix"][:, None, :], g_ffn=full["g_ffn"][:, None, :],
        sp=_sp_pack(full), pool_w=full["pool_w"], pool_sc=full["pool_scale"].reshape(l, 4, 1, 128),
        t1=_rpb_expand(full["na_rpb"]),
        conv_w=full["ffn_conv_w"].reshape(l, 3, 2, ff).transpose(0, 2, 1, 3),
        conv_b=full["ffn_conv_b"].reshape(l, 2, 1, ff))


def _kernel_weights(full, mods):
    w = _small_weights(full, mods)
    w.update(w_in=_w_in_layout(full["w_in"]).astype(BF16), w_out=full["w_out"].astype(BF16),
             w_up=full["ffn_w_up"].astype(BF16), w_down=full["ffn_w_down"].astype(BF16),
             w_qb=_w_qb_layout(full["mla_w_qb"]).astype(BF16), w_kvb=full["mla_w_kvb"].astype(BF16))
    return w


def _reference_grads(grads, big=True):
    st = lambda k: jnp.stack([g[k] for g in grads], axis=0)
    l = len(grads)
    out = dict(
        g_mix=st("g_mix")[:, 0], g_ffn=st("g_ffn")[:, 0],
        pool_w=st("pool_w"), pool_scale=st("pool_sc").reshape(l, 512), na_rpb=_rpb_fold(st("t1")),
        ffn_conv_w=st("conv_w").transpose(0, 2, 1, 3).reshape(l, 3, -1), ffn_conv_b=st("conv_b").reshape(l, -1),
        mods=st("mods"))
    if big:
        out.update(w_in=_w_in_unlayout(st("w_in")), w_out=st("w_out"), ffn_w_up=st("w_up"), ffn_w_down=st("w_down"),
                   mla_w_qb=_w_qb_unlayout(st("w_qb")), mla_w_kvb=st("w_kvb"))
    out.update(_sp_unpack(st("sp")))
    return out


ANY = pl.BlockSpec(memory_space=pl.ANY)


def _flip(x, y, j):
    return (1 - x if j >> 1 else x), (1 - y if j & 1 else y)


def _comm_call(name, ins, out_shapes, n_copies, plan, aliases=None):
    n_in, n_out = len(ins), len(out_shapes)

    def body(*refs):
        in_refs, out_refs = refs[:n_in], refs[n_in:n_in + n_out]
        ssem, rsem = refs[n_in + n_out:]
        pos = (lax.axis_index("x"), lax.axis_index("y"), lax.axis_index("c"))
        copies = plan(in_refs, out_refs, pos)
        assert len(copies) == n_copies
        descs = []
        for i, (src, dst, peer) in enumerate(copies):
            if peer is None:
                d = pltpu.make_async_copy(src, dst, ssem.at[i])
            else:
                d = pltpu.make_async_remote_copy(src_ref=src, dst_ref=dst, send_sem=ssem.at[i], recv_sem=rsem.at[i],
                                                 device_id=peer, device_id_type=MESH)
            d.start()
            descs.append(d)
        for d in descs:
            d.wait()

    return pl.pallas_call(
        body, name=name, in_specs=[ANY] * n_in, out_specs=[ANY] * n_out, out_shape=list(out_shapes),
        input_output_aliases=aliases or {},
        scratch_shapes=[pltpu.SemaphoreType.DMA((n_copies,)), pltpu.SemaphoreType.DMA((n_copies,))])(*ins)


def _sib_fill(bufs, part, name):
    def plan(ins, outs, pos):
        x, y, c = pos
        return [(o_ref.at[part(c)], o_ref.at[part(c)], (x, y, 1 - c)) for o_ref in outs]

    shapes = [jax.ShapeDtypeStruct(b.shape, b.dtype) for b in bufs]
    return _comm_call(name, bufs, shapes, len(bufs), plan, aliases={i: i for i in range(len(bufs))})


HBM = pl.BlockSpec(memory_space=pltpu.HBM)
SEM = pl.BlockSpec(memory_space=pltpu.SEMAPHORE)
DATAFLOW = pltpu.SideEffectType.DATAFLOW_SIDE_EFFECTING


def _remote_start(name, bufs, n_copies, plan, after):
    nb = len(bufs)

    def body(*refs):
        ssem, rsem, token = refs[nb + 1], refs[nb + 2], refs[-1]
        pos = (lax.axis_index("x"), lax.axis_index("y"), lax.axis_index("c"))
        copies = plan(refs[:nb], pos)
        assert len(copies) == n_copies
        for i, (src, dst, peer) in enumerate(copies):
            pltpu.make_async_remote_copy(src_ref=src, dst_ref=dst, send_sem=ssem.at[i], recv_sem=rsem.at[i],
                                         device_id=peer, device_id_type=MESH).start()
        token[...] = jnp.zeros_like(token)

    outs = pl.pallas_call(
        body, name=name,
        out_shape=(pltpu.SemaphoreType.DMA((n_copies,)), pltpu.SemaphoreType.DMA((n_copies,)),
                   *[pltpu.HBM(b.shape, b.dtype) for b in bufs], jax.ShapeDtypeStruct((8, 128), F32)),
        in_specs=[HBM] * nb + [ANY], out_specs=(SEM, SEM, *[HBM] * nb, pl.BlockSpec(memory_space=pltpu.VMEM)),
        input_output_aliases={i: 2 + i for i in range(nb)},
        compiler_params=pltpu.CompilerParams(has_side_effects=DATAFLOW),
    )(*[pltpu.with_memory_space_constraint(b, pltpu.HBM) for b in bufs], after)
    return outs[0], outs[1], list(outs[2:2 + nb]), outs[-1]


def _remote_wait(name, ssem, rsem, bufs, n_copies, plan, after):
    nb = len(bufs)

    def body(*refs):
        ssem_ref, rsem_ref = refs[nb], refs[nb + 1]
        pos = (lax.axis_index("x"), lax.axis_index("y"), lax.axis_index("c"))
        copies = plan(refs[:nb], pos)
        assert len(copies) == n_copies
        for i, (src, dst, peer) in enumerate(copies):
            cp = pltpu.make_async_remote_copy(src_ref=src, dst_ref=dst, send_sem=ssem_ref.at[i], recv_sem=rsem_ref.at[i],
                                              device_id=peer, device_id_type=MESH)
            cp.wait_send()
            cp.wait_recv()

    outs = pl.pallas_call(
        body, name=name, out_shape=tuple(pltpu.HBM(b.shape, b.dtype) for b in bufs),
        in_specs=[HBM] * nb + [SEM, SEM, ANY], out_specs=tuple([HBM] * nb), input_output_aliases={i: i for i in range(nb)},
        compiler_params=pltpu.CompilerParams(has_side_effects=DATAFLOW),
    )(*bufs, ssem, rsem, after)
    return list(outs)


BIG_GATHER = ("axis1", "axis1", "axis1", "axis1", "lane", "lane")


def _w_gather_plan(shapes):
    def plan(refs, pos):
        x, y, c = pos
        k = 2 * x + y
        cps = []
        for s_ref, l_ref, kind, shp in zip(refs[:6], refs[6:], BIG_GATHER, shapes):
            h, w = shp[0] // 2, shp[1]
            rows = pl.ds(pl.multiple_of(c * h, 16), h)
            dst = l_ref.at[k, rows, :] if kind == "axis1" else l_ref.at[rows, pl.ds(pl.multiple_of(k * w, 128), w)]
            for j in (1, 2, 3):
                tx, ty = _flip(x, y, j)
                cps.append((s_ref.at[rows, :], dst, (tx, ty, c)))
        return cps

    return plan


def _w_fill(lands, shapes, name):
    def plan(ins, outs, pos):
        x, y, c = pos
        cps = []
        for o_ref, kind, shp in zip(outs, BIG_GATHER, shapes):
            h = shp[0] // 2
            rows = pl.ds(pl.multiple_of(c * h, 16), h)
            part = o_ref.at[:, rows, :] if kind == "axis1" else o_ref.at[rows, :]
            cps.append((part, part, (x, y, 1 - c)))
        return cps

    return _comm_call(name, lands, [jax.ShapeDtypeStruct(b.shape, b.dtype) for b in lands], len(lands), plan,
                      aliases={i: i for i in range(len(lands))})


def _g_scatter_plan(refs, pos):
    x, y, c = pos
    cps = []
    for s_ref, l_ref in zip(refs[:6], refs[6:]):
        for j in (1, 2, 3):
            tx, ty = _flip(x, y, j)
            cps.append((s_ref.at[j], l_ref.at[j - 1], (tx, ty, c)))
    return cps


def _pair_up(xs, name):
    def plan(ins, outs, pos):
        x, y, c = pos
        cps = []
        for i_ref, o_ref in zip(ins, outs):
            cps.append((i_ref, o_ref.at[c], None))
            cps.append((i_ref, o_ref.at[c], (x, y, 1 - c)))
        return cps

    return _comm_call(name, xs, [jax.ShapeDtypeStruct((2,) + a.shape, a.dtype) for a in xs], 2 * len(xs), plan)


def _chip_gather(xs, kinds, name):
    def dst(o_ref, kind, k, x_shape):
        if kind == "lead":
            return o_ref.at[k]
        w = x_shape[-1]
        return o_ref.at[(slice(None),) * (len(x_shape) - 1) + (pl.ds(pl.multiple_of(k * w, 128), w),)]

    def plan(ins, outs, pos):
        x, y, c = pos
        k = 2 * x + y
        cps = []
        for i_ref, o_ref, kind, a in zip(ins, outs, kinds, xs):
            cps.append((i_ref, dst(o_ref, kind, k, a.shape), None))
            for j in (1, 2, 3):
                tx, ty = _flip(x, y, j)
                cps.append((i_ref, dst(o_ref, kind, k, a.shape), (tx, ty, c)))
        return cps

    def oshape(a, kind):
        return (4,) + a.shape if kind == "lead" else a.shape[:-1] + (4 * a.shape[-1],)

    return _comm_call(name, xs, [jax.ShapeDtypeStruct(oshape(a, kd), a.dtype) for a, kd in zip(xs, kinds)], 4 * len(xs), plan)


def _half_swap(xs, kinds, name):
    def plan(ins, outs, pos):
        x, y, c = pos
        cps = []
        for i_ref, o_ref, kind, a in zip(ins, outs, kinds, xs):
            h = a.shape[-2] // 2
            rows = pl.ds(pl.multiple_of((1 - c) * h, 8), h)
            src = i_ref.at[:, rows, :] if kind == "cm" else i_ref.at[rows, :]
            cps.append((src, o_ref, (x, y, 1 - c)))
        return cps

    shapes = [jax.ShapeDtypeStruct(a.shape[:-2] + (a.shape[-2] // 2, a.shape[-1]), a.dtype) for a in xs]
    return _comm_call(name, xs, shapes, len(xs), plan)


def _row_tile(rows, cols, budget=1 << 20):
    for t in (2048, 1024, 512, 256, 128, 64, 32, 16, 8):
        if rows % t == 0 and t * cols * 4 <= budget:
            return t
    return rows


def _half_add(g, recv, kind, pvec, name):
    def chip(j, x_ref, y_ref):
        jx, jy = j // 2, j % 2
        return 2 * (x_ref[0] + jx - 2 * x_ref[0] * jx) + (y_ref[0] + jy - 2 * y_ref[0] * jy)

    if kind == "cm":
        _, h2, w = g.shape
        h = h2 // 2
        tr = _row_tile(h, w)
        nb = h // tr
        g_spec = pl.BlockSpec((None, tr, w), lambda j, i, c_ref, x_ref, y_ref: (chip(j, x_ref, y_ref), i + c_ref[0] * nb, 0))
        r_spec = pl.BlockSpec((None, tr, w), lambda j, i, c_ref, x_ref, y_ref: (chip(j, x_ref, y_ref), i, 0))
    else:
        h2, w4 = g.shape
        h, w = h2 // 2, w4 // 4
        tr = _row_tile(h, w)
        nb = h // tr
        g_spec = pl.BlockSpec((tr, w), lambda j, i, c_ref, x_ref, y_ref: (i + c_ref[0] * nb, chip(j, x_ref, y_ref)))
        r_spec = pl.BlockSpec((tr, w), lambda j, i, c_ref, x_ref, y_ref: (i, chip(j, x_ref, y_ref)))

    def body(c_ref, x_ref, y_ref, g_ref, r_ref, o_ref):
        o_ref[...] = (g_ref[...] + r_ref[...]).astype(BF16)

    return pl.pallas_call(
        body, name=name,
        grid_spec=pltpu.PrefetchScalarGridSpec(
            num_scalar_prefetch=3, grid=(4, nb), in_specs=[g_spec, r_spec],
            out_specs=pl.BlockSpec((None, tr, w), lambda j, i, c_ref, x_ref, y_ref: (j, i, 0))),
        out_shape=jax.ShapeDtypeStruct((4, h, w), BF16), compiler_params=_params(("parallel", "parallel")))(*pvec, g, recv)


def _sum_lead(x, name):
    n, rows, w = x.shape
    tr = _row_tile(rows, w, (1 << 21) // n)

    def body(x_ref, o_ref):
        acc = x_ref[0].astype(F32)
        for j in range(1, n):
            acc = acc + x_ref[j].astype(F32)
        o_ref[...] = acc

    return pl.pallas_call(
        body, name=name, grid=(rows // tr,), in_specs=[pl.BlockSpec((n, tr, w), lambda i: (0, i, 0))],
        out_specs=pl.BlockSpec((tr, w), lambda i: (i, 0)), out_shape=jax.ShapeDtypeStruct((rows, w), F32),
        compiler_params=_params(("parallel",)))(x)


def _sum_into(own, landed, buf, layer, cvec, layers, name):
    n, rows, w = landed.shape
    tr = _row_tile(rows, w, 1 << 19)

    def body(c_ref, own_ref, x_ref, *refs):
        o_ref = refs[-1]
        acc = own_ref[...].astype(F32)
        for j in range(n):
            acc = acc + x_ref[j].astype(F32)
        o_ref[...] = acc

    in_specs = [pl.BlockSpec((None, tr, w), lambda i, c_ref: (0, i, 0)), pl.BlockSpec((n, tr, w), lambda i, c_ref: (0, i, 0))]
    args = [cvec, own, landed]
    if buf is not None:
        in_specs.append(ANY)
        args.append(buf)
    return pl.pallas_call(
        body, name=name,
        grid_spec=pltpu.PrefetchScalarGridSpec(
            num_scalar_prefetch=1, grid=(rows // tr,), in_specs=in_specs,
            out_specs=pl.BlockSpec((None, None, tr, w), lambda i, c_ref: (layer, c_ref[0], i, 0))),
        out_shape=jax.ShapeDtypeStruct((layers, 2, rows, w), F32), input_output_aliases={} if buf is None else {3: 0},
        compiler_params=_params(("arbitrary",)))(*args)


def _adamw(w, g, m, v, name):
    shape = w.shape
    cols = shape[-1]
    rows = w.size // cols
    tr = _row_tile(rows, cols, 1 << 19)

    def body(w_ref, g_ref, m_ref, v_ref, d_ref, mo_ref, vo_ref):
        gv = g_ref[...]
        mn = ADAM_B1 * m_ref[...] + (1.0 - ADAM_B1) * gv
        vn = ADAM_B2 * v_ref[...] + (1.0 - ADAM_B2) * jnp.square(gv)
        m_hat = mn / (1.0 - ADAM_B1 ** ADAM_STEP)
        v_hat = vn / (1.0 - ADAM_B2 ** ADAM_STEP)
        d_ref[...] = -ADAM_LR * (m_hat / (jnp.sqrt(v_hat) + ADAM_EPS) + ADAM_WD * w_ref[...])
        mo_ref[...] = mn
        vo_ref[...] = vn

    spec = pl.BlockSpec((tr, cols), lambda i: (i, 0))
    outs = pl.pallas_call(
        body, name=name, grid=(rows // tr,), in_specs=[spec] * 4, out_specs=[spec] * 3,
        out_shape=[jax.ShapeDtypeStruct((rows, cols), F32)] * 3,
        compiler_params=_params(("parallel",)))(*[a.reshape(rows, cols) for a in (w, g, m, v)])
    return [o.reshape(shape) for o in outs]


def _silu_grad(x):
    s = jax.nn.sigmoid(x)
    return s * (1.0 + x * (1.0 - s))


def _mod_fwd(cs16, w_mod, b_sh, name):
    l, d, wc = w_mod.shape
    tn = _pick(wc, (512, 384, 256, 128))

    def body(c_ref, w_ref, b_ref, o_ref):
        a = jax.nn.silu(c_ref[...]).astype(BF16)
        o_ref[...] = jnp.dot(a, w_ref[...].astype(BF16), preferred_element_type=F32) + b_ref[...]

    return pl.pallas_call(
        body, name=name, grid=(l, wc // tn),
        in_specs=[_full_spec((16, d)), pl.BlockSpec((None, d, tn), lambda i, j: (i, 0, j)), pl.BlockSpec((None, 1, tn), lambda i, j: (i, 0, j))],
        out_specs=pl.BlockSpec((None, 16, tn), lambda i, j: (i, 0, j)), out_shape=jax.ShapeDtypeStruct((l, 16, wc), F32),
        compiler_params=_params(("parallel", "parallel")))(cs16, w_mod, b_sh)


def _mod_dw(cs16, dm_sh, name):
    l, _, wc = dm_sh.shape
    d = cs16.shape[1]
    tr = _pick(d, (512, 256, 128))
    tc = _pick(wc, (512, 384, 256, 128))

    def body(c_ref, dm_ref, o_ref):
        a = jax.nn.silu(c_ref[...]).astype(BF16)
        o_ref[...] = lax.dot_general(a, dm_ref[...].astype(BF16), (((0,), (0,)), ((), ())), preferred_element_type=F32)

    return pl.pallas_call(
        body, name=name, grid=(l, d // tr, wc // tc),
        in_specs=[pl.BlockSpec((16, tr), lambda i, r, j: (0, r)), pl.BlockSpec((None, 16, tc), lambda i, r, j: (i, 0, j))],
        out_specs=pl.BlockSpec((None, tr, tc), lambda i, r, j: (i, r, j)), out_shape=jax.ShapeDtypeStruct((l, d, wc), F32),
        compiler_params=_params(("parallel", "parallel", "parallel")))(cs16, dm_sh)


def _mod_dc(dm_sh, w_mod, c_ctx, name):
    l, d, wc = w_mod.shape
    tk = _pick(wc, (512, 384, 256, 128))
    nk = wc // tk

    def body(dm_ref, w_ref, c_ref, o_ref, acc_ref):
        i, j = pl.program_id(0), pl.program_id(1)

        @pl.when(jnp.logical_and(i == 0, j == 0))
        def _():
            acc_ref[...] = jnp.zeros_like(acc_ref)

        acc_ref[...] += lax.dot_general(dm_ref[...].astype(BF16), w_ref[...].astype(BF16), (((1,), (1,)), ((), ())),
                                        preferred_element_type=F32)

        @pl.when(jnp.logical_and(i == l - 1, j == nk - 1))
        def _():
            mine = jnp.where(lax.axis_index("c") == 0, 1.0, 0.0)
            o_ref[...] = acc_ref[8:9, :] * _silu_grad(c_ref[...]) * mine

    return pl.pallas_call(
        body, name=name, grid=(l, nk),
        in_specs=[pl.BlockSpec((None, 16, tk), lambda i, j: (i, 0, j)), pl.BlockSpec((None, d, tk), lambda i, j: (i, 0, j)), _full_spec((1, d))],
        out_specs=_full_spec((1, d)), out_shape=jax.ShapeDtypeStruct((1, d), F32), scratch_shapes=[pltpu.VMEM((16, d), F32)],
        compiler_params=_params(("arbitrary", "arbitrary")))(dm_sh, w_mod, c_ctx)


def _dmod_assemble(gath, name):
    _, l, _, w = gath.shape
    gath = gath.transpose(1, 2, 0, 3)
    tc = _pick(w, (2048, 1024, 512, 256, 128))

    def body(lat_ref, ctx_ref, o_ref, b_ref):
        ctx = ctx_ref[0:1, :]
        for dev in range(1, 8):
            ctx = ctx + ctx_ref[dev:dev + 1, :]
        lat = lat_ref[...]
        o_ref[0:8, :] = lat
        o_ref[8:9, :] = ctx
        o_ref[9:16, :] = jnp.zeros((7, tc), F32)
        b_ref[...] = jnp.sum(lat, axis=0, keepdims=True) + ctx

    return pl.pallas_call(
        body, name=name, grid=(l, w // tc),
        in_specs=[pl.BlockSpec((None, None, 8, tc), lambda i, j: (i, 0, 0, j)), pl.BlockSpec((None, None, 8, tc), lambda i, j: (i, 1, 0, j))],
        out_specs=[pl.BlockSpec((None, 16, tc), lambda i, j: (i, 0, j)), pl.BlockSpec((None, 1, tc), lambda i, j: (i, 0, j))],
        out_shape=[jax.ShapeDtypeStruct((l, 16, w), F32), jax.ShapeDtypeStruct((l, 1, w), F32)],
        compiler_params=_params(("parallel", "parallel")))(gath, gath)


SMALL = ("c_ctx", "g_mix", "g_ffn", "mla_q_a_norm", "mla_kv_a_norm", "mla_q_nope_norm", "mla_q_rope_norm", "mla_k_nope_norm",
         "mla_k_rope_norm", "pool_w", "pool_scale", "swa_q_norm", "swa_k_norm", "swa_sink", "na_q_norm", "na_k_norm", "na_rpb",
         "ffn_conv_b")
PACK_W = 512
PACK_Q = 8 * PACK_W


def _pack(arrs):
    flat = []
    for a in arrs:
        f = a.reshape(-1)
        flat.append(jnp.pad(f, (0, (-f.size) % PACK_Q)))
    return jnp.concatenate(flat).reshape(-1, PACK_W)


def _unpack(packed, shapes):
    flat, out, o = packed.reshape(-1), [], 0
    for s in shapes:
        n = 1
        for dim in s:
            n *= dim
        out.append(flat[o:o + n].reshape(s))
        o += n + (-n) % PACK_Q
    return out


def _all_sum(p, name):
    pair = _pair_up([p], name + "_pair")[0]
    chip = _sum_lead(pair, name + "_sum2")
    return _sum_lead(_chip_gather([chip], ["lead"], name + "_gather")[0], name + "_sum4")


WEIGHTS = ("c_ctx", "w_mod", "b_mod", "g_mix", "g_ffn", "w_in", "w_out", "mla_q_a_norm", "mla_w_qb", "mla_kv_a_norm", "mla_w_kvb",
           "mla_q_nope_norm", "mla_q_rope_norm", "mla_k_nope_norm", "mla_k_rope_norm", "pool_w", "pool_scale", "swa_q_norm",
           "swa_k_norm", "swa_sink", "na_q_norm", "na_k_norm", "na_rpb", "ffn_w_up", "ffn_conv_w", "ffn_conv_b", "ffn_w_down")
BIG = ("w_in", "mla_w_qb", "w_out", "ffn_w_down", "ffn_w_up", "mla_w_kvb")
BIG_KINDS = ("cm", "cm", "cm", "cm", "lb", "lb")


def _step(a):
    x, c, ctx = a["x"], a["c"], a["ctx"]
    n_lat, d = x.shape[1], x.shape[2]
    n_ctx = ctx.shape[1]
    l = DEPTH
    px, py, pc = lax.axis_index("x"), lax.axis_index("y"), lax.axis_index("c")
    chip = 2 * px + py
    cvec = pc.reshape(1).astype(I32)

    c_all = _chip_gather(_pair_up([c], "c_pair"), ["lead"], "c_gather")[0].reshape(8, d)
    cs16 = jnp.concatenate([c_all, a["c_ctx"][None, :], jnp.zeros((7, d), F32)], axis=0)
    wc = a["w_mod"].shape[-1]
    b_sh = lax.dynamic_slice_in_dim(a["b_mod"], chip * wc, wc, axis=1)[:, None, :]
    mod_sh = _mod_fwd(cs16, a["w_mod"], b_sh, "mod_fwd")
    mod_all, conv_w_full = _chip_gather([mod_sh, a["ffn_conv_w"]], ["lead", "lane"], "mod_gather")
    mod_all = mod_all.transpose(1, 2, 0, 3).reshape(l, 16, 4 * wc)
    mods = jnp.stack([lax.dynamic_index_in_dim(mod_all, 2 * chip + pc, axis=1, keepdims=False), mod_all[:, 8]], axis=1)
    mods = mods.reshape(l, 2, 6, d)

    full = {k: a[k] for k in SMALL if k != "c_ctx"}
    full["ffn_conv_w"] = conv_w_full
    w = _small_weights(full, mods)
    w["mods"] = [w["mods"][li] for li in range(l)]
    w["g_mix"] = [w["g_mix"][li] for li in range(l)]
    for key in ("w_in", "w_qb", "w_out", "w_down", "w_up", "w_kvb"):
        w[key] = [None] * l
    tab = _rope_table(n_lat, n_ctx)

    bf = {k: a[k].astype(BF16) for k in BIG}

    def land_init(shard, kind):
        if kind == "axis1":
            return lax.dynamic_update_slice(jnp.zeros((4,) + shard.shape, BF16), shard[None], (chip, 0, 0))
        return lax.dynamic_update_slice(jnp.zeros((shard.shape[0], 4 * shard.shape[1]), BF16), shard, (0, chip * shard.shape[1]))

    def gather_start(li, after):
        shards = [bf[k][li] for k in BIG]
        shapes = [s.shape for s in shards]
        lands = [land_init(s, kd) for s, kd in zip(shards, BIG_GATHER)]
        return _remote_start(f"w_start_{li}", shards + lands, 18, _w_gather_plan(shapes), after) + (shapes,)

    def gather_finish(li, started, after):
        ssem, rsem, bufs, _, shapes = started
        bufs = _remote_wait(f"w_wait_{li}", ssem, rsem, bufs, 18, _w_gather_plan(shapes), after)
        w_in_g, w_qb_g, w_out_g, w_down_g, w_up_g, w_kvb_g = _w_fill(bufs[6:], shapes, "w_fill")
        w["w_in"][li] = _w_in_layout(w_in_g.transpose(1, 0, 2).reshape(d, P_COLS))
        w["w_qb"][li] = _w_qb_layout(w_qb_g.transpose(1, 0, 2).reshape(512, 768))
        w["w_out"][li], w["w_down"][li] = w_out_g.reshape(-1, d), w_down_g.reshape(-1, d)
        w["w_up"][li], w["w_kvb"][li] = w_up_g, w_kvb_g

    gather_finish(0, gather_start(0, mods), mods)
    xs = jnp.concatenate([x[0], ctx[0]], axis=0)
    saved = []
    for li in range(l):
        if li + 1 < l:
            started = gather_start(li + 1, w["w_kvb"][li])
            w["g_mix"][li] = w["g_mix"][li] + started[3][0:1, 0:1]
        xs, s = _layer_fwd(xs, w, li, tab, n_lat)
        saved.append(s)
        if li + 1 < l:
            gather_finish(li + 1, started, xs)
    loss, dx = _loss_kernel(xs, a["loss_target"][0], n_lat // TM, "loss")
    loss = lax.psum(loss[0, 0], ("x", "y", "c"))

    pvec = [p.reshape(1).astype(I32) for p in (pc, px, py)]
    grads = [None] * l
    g_big = [None] * len(BIG)

    def scatter_finish(li, started, after):
        ssem, rsem, bufs, _ = started
        bufs = _remote_wait(f"g_wait_{li}", ssem, rsem, bufs, 18, _g_scatter_plan, after)
        return [_sum_into(own, landed, buf, li, cvec, l, "g_sum4") for own, landed, buf in zip(bufs[:6], bufs[6:], g_big)]

    pending = None
    for li in reversed(range(l)):
        dx, grads[li] = _layer_bwd(dx, saved[li], w, li, tab, n_lat)
        if pending is not None:
            g_big = scatter_finish(li + 1, pending, dx)
        g = grads[li]
        ops = [_w_in_unlayout(g["w_in"]).reshape(d, 4, -1).transpose(1, 0, 2),
               _w_qb_unlayout(g["w_qb"]).reshape(512, 4, 192).transpose(1, 0, 2),
               g["w_out"].reshape(4, -1, d), g["w_down"].reshape(4, -1, d), g["w_up"], g["w_kvb"]]
        recv = _half_swap(ops, BIG_KINDS, "g_swap")
        halves = [_half_add(o, r, kd, pvec, "g_half_add") for o, r, kd in zip(ops, recv, BIG_KINDS)]
        lands = [jnp.zeros((3,) + h.shape[1:], BF16) for h in halves]
        pending = _remote_start(f"g_start_{li}", halves + lands, 18, _g_scatter_plan, halves[0])
        if li > 0:
            w["mods"][li - 1] = w["mods"][li - 1] + pending[3][0, 0]

    dmods = jnp.stack([grads[li]["mods"] for li in range(l)], axis=0).reshape(l, 2, 6 * d)
    dm_gath = _chip_gather(_pair_up([dmods], "dmod_pair"), ["lead"], "dmod_gather")[0].reshape(8, l, 2, 6 * d)
    dmod_all, g_b_mod = _dmod_assemble(dm_gath, "dmod_assemble")
    dm_sh = lax.dynamic_slice_in_dim(dmod_all, chip * wc, wc, axis=2)
    g_w_mod = _mod_dw(cs16, dm_sh, "mod_dw")
    g_c_ctx = _mod_dc(dm_sh, a["w_mod"], a["c_ctx"][None, :], "mod_dc")
    g_big = _sib_fill(scatter_finish(0, pending, g_w_mod), lambda cc: (slice(None), cc), "g_pair")
    g_out = {k: g.reshape(a[k].shape) for k, g in zip(BIG, g_big)}
    g_out["w_mod"] = g_w_mod
    g_out["b_mod"] = g_b_mod.reshape(l, 6 * d)

    rg = _reference_grads(grads, big=False)
    rg["c_ctx"] = g_c_ctx[0]
    packed = _all_sum(_pack([rg[k] for k in SMALL] + [rg["ffn_conv_w"]]), "small")
    small_g = _unpack(packed, [a[k].shape for k in SMALL] + [rg["ffn_conv_w"].shape])
    for k, g in zip(SMALL, small_g[:-1]):
        g_out[k] = g
    cw = a["ffn_conv_w"].shape[-1]
    g_out["ffn_conv_w"] = lax.dynamic_slice_in_dim(small_g[-1], chip * cw, cw, axis=2)

    upd = {}
    pk = lambda pre: _pack([a[pre + k] for k in SMALL])
    outs = _adamw(pk(""), _pack([g_out[k] for k in SMALL]), pk("m_"), pk("v_"), "adamw_small")
    for o, kind in zip(outs, ("delta", "m", "v")):
        for k, val in zip(SMALL, _unpack(o, [a[k].shape for k in SMALL])):
            upd[kind, k] = val
    for k in WEIGHTS:
        if k not in SMALL:
            outs = _adamw(a[k], g_out[k], a["m_" + k], a["v_" + k], "adamw_" + k)
            for o, kind in zip(outs, ("delta", "m", "v")):
                upd[kind, k] = o
    grad_x = dx[0:n_lat].reshape(x.shape)
    return (loss, grad_x, *[g_out[k] for k in WEIGHTS], *[upd["delta", k] for k in WEIGHTS],
            *[upd["m", k] for k in WEIGHTS], *[upd["v", k] for k in WEIGHTS])


def kernel(x, c, ctx, c_ctx, w_mod, b_mod, g_mix, g_ffn, w_in, w_out, mla_q_a_norm, mla_w_qb, mla_kv_a_norm, mla_w_kvb, mla_q_nope_norm, mla_q_rope_norm, mla_k_nope_norm, mla_k_rope_norm, pool_w, pool_scale, swa_q_norm, swa_k_norm, swa_sink, na_q_norm, na_k_norm, na_rpb, ffn_w_up, ffn_conv_w, ffn_conv_b, ffn_w_down, loss_target, m_c_ctx, m_w_mod, m_b_mod, m_g_mix, m_g_ffn, m_w_in, m_w_out, m_mla_q_a_norm, m_mla_w_qb, m_mla_kv_a_norm, m_mla_w_kvb, m_mla_q_nope_norm, m_mla_q_rope_norm, m_mla_k_nope_norm, m_mla_k_rope_norm, m_pool_w, m_pool_scale, m_swa_q_norm, m_swa_k_norm, m_swa_sink, m_na_q_norm, m_na_k_norm, m_na_rpb, m_ffn_w_up, m_ffn_conv_w, m_ffn_conv_b, m_ffn_w_down, v_c_ctx, v_w_mod, v_b_mod, v_g_mix, v_g_ffn, v_w_in, v_w_out, v_mla_q_a_norm, v_mla_w_qb, v_mla_kv_a_norm, v_mla_w_kvb, v_mla_q_nope_norm, v_mla_q_rope_norm, v_mla_k_nope_norm, v_mla_k_rope_norm, v_pool_w, v_pool_scale, v_swa_q_norm, v_swa_k_norm, v_swa_sink, v_na_q_norm, v_na_k_norm, v_na_rpb, v_ffn_w_up, v_ffn_conv_w, v_ffn_conv_b, v_ffn_w_down):
    return _step(dict(locals()))
```

```python
import functools

import jax
import jax.numpy as jnp
from jax import lax
from jax.experimental import pallas as pl
from jax.experimental.pallas import tpu as pltpu

F32 = jnp.float32
BF16 = jnp.bfloat16
I32 = jnp.int32

DEPTH = 4
GRID_W = 64
ROPE_BASE = 10000.0
EPS = 1e-6
NEG = -1e30
MLA_SCALE = 192.0 ** -0.5
HD_SCALE = 64.0 ** -0.5
NA_KROWS = 12
SWA_KEYS = 512
P_COLS = 3648
PW = 3840
TM = 256
HALO = 8
ADAM_LR, ADAM_B1, ADAM_B2, ADAM_EPS, ADAM_WD, ADAM_STEP = 0.001, 0.9, 0.999, 1e-08, 0.01, 10
VMEM_LIMIT = 56 * 1024 * 1024
MESH = pl.DeviceIdType.MESH


def _pick(n, cands):
    for c in cands:
        if n % c == 0:
            return c
    return n


def _params(sem=None):
    return pltpu.CompilerParams(dimension_semantics=sem, vmem_limit_bytes=VMEM_LIMIT)


@jax.custom_vjp
def _bdot(a, b):
    return jnp.dot(a.astype(BF16), b.astype(BF16), preferred_element_type=F32)


def _bdot_fwd(a, b):
    return _bdot(a, b), (a.astype(BF16), b.astype(BF16))


def _bdot_bwd(res, g):
    a, b = res
    gb = g.astype(BF16)
    da = lax.dot_general(gb, b, (((1,), (1,)), ((), ())), preferred_element_type=F32)
    db = lax.dot_general(a, gb, (((0,), (0,)), ((), ())), preferred_element_type=F32)
    return da, db


_bdot.defvjp(_bdot_fwd, _bdot_bwd)


@jax.custom_vjp
def _bdot_nt(a, b):
    return lax.dot_general(a.astype(BF16), b.astype(BF16), (((1,), (1,)), ((), ())), preferred_element_type=F32)


def _bdot_nt_fwd(a, b):
    return _bdot_nt(a, b), (a.astype(BF16), b.astype(BF16))


def _bdot_nt_bwd(res, g):
    a, b = res
    gb = g.astype(BF16)
    da = jnp.dot(gb, b, preferred_element_type=F32)
    db = lax.dot_general(gb, a, (((0,), (0,)), ((), ())), preferred_element_type=F32)
    return da, db


_bdot_nt.defvjp(_bdot_nt_fwd, _bdot_nt_bwd)


def _rms(x, g):
    return x * lax.rsqrt(jnp.mean(x * x, axis=-1, keepdims=True) + EPS) * g


def _rope(x, cos, sin):
    xr = jnp.concatenate([-x[:, 16:32], x[:, 0:16], -x[:, 48:64], x[:, 32:48]], axis=-1)
    return x * cos + xr * sin


def _sel(is_ctx, mod, row):
    return jnp.where(is_ctx, mod[1, row:row + 1, :], mod[0, row:row + 1, :])


MM_VMEM_BUDGET = 40 * 1024 * 1024
HBM_BYTES_PER_STEP = 1 << 20


def _mm_tiles(m, n, k, mode, osize, n_unit, k_unit):
    lanes = (3840, 2816, 2048, 1280, 1024, 768, 512, 256)
    subl = (4352, 2176, 1088, 1024, 640, 544, 512, 256)
    tms = [c for c in (lanes if mode == "tn" else subl) if m % c == 0] or [m]
    tns = [c for c in lanes if n_unit % c == 0] or [n_unit]
    tks = [c for c in (subl if mode == "tn" else lanes) if k_unit % c == 0]
    if k_unit == k:
        tks = [k] + tks
    best = None
    for tm in tms:
        for tn in tns:
            for tk in tks:
                nk = k // tk
                vmem = 4 * (tm * tk + tk * tn) + 2 * tm * tn * osize + tm * tn * 4
                if vmem > MM_VMEM_BUDGET:
                    continue
                a_reads = 1 if nk == 1 else n // tn
                b_reads = 1 if (nk == 1 and n == tn) else m // tm
                steps = (m // tm) * (n // tn) * nk
                cost = (2 * m * k * a_reads + 2 * k * n * b_reads + m * n * osize + steps * HBM_BYTES_PER_STEP
                        + (12 * m * n * nk if nk > 1 else 0))
                if best is None or cost < best[0]:
                    best = (cost, tm, tn, tk)
    return best[1:]


def _mm(a, b, mode, out_dtype, name, a_split=False, b_split=False, o_split=False):
    def dims(x, split):
        return (x.shape[1], 2 * x.shape[2]) if split else x.shape

    ar, ac = dims(a, a_split)
    br, bc = dims(b, b_split)
    if mode == "nn":
        m, k, n = ar, ac, bc
        assert br == k
    elif mode == "nt":
        m, k, n = ar, ac, br
        assert bc == k
    else:
        k, m, n = ar, ac, bc
        assert br == k
    n_unit = n // 2 if (o_split or (b_split and mode != "nt")) else n
    k_unit = k // 2 if (mode != "tn" and (a_split or (b_split and mode == "nt"))) else k
    tm, tn, tk = _mm_tiles(m, n, k, mode, jnp.dtype(out_dtype).itemsize, n_unit, k_unit)
    nk = k // tk

    def spec(split, tr, tc, ncols, ridx, cidx):
        if not split:
            return pl.BlockSpec((tr, tc), lambda i, j, kk: (ridx(i, j, kk), cidx(i, j, kk)))
        nh = (ncols // 2) // tc
        return pl.BlockSpec((None, tr, tc), lambda i, j, kk: (cidx(i, j, kk) // nh, ridx(i, j, kk), cidx(i, j, kk) % nh))

    gi = lambda i, j, kk: i
    gj = lambda i, j, kk: j
    gk = lambda i, j, kk: kk
    if mode == "tn":
        a_spec = spec(a_split, tk, tm, ac, gk, gi)
    else:
        a_spec = spec(a_split, tm, tk, ac, gi, gk)
    if mode == "nt":
        b_spec = spec(b_split, tn, tk, bc, gj, gk)
    else:
        b_spec = spec(b_split, tk, tn, bc, gk, gj)
    o_spec = spec(o_split, tm, tn, n, gi, gj)
    dn = {"nn": (((1,), (0,)), ((), ())), "nt": (((1,), (1,)), ((), ())), "tn": (((0,), (0,)), ((), ()))}[mode]

    def body(a_ref, b_ref, o_ref, acc_ref):
        kk = pl.program_id(2)

        @pl.when(kk == 0)
        def _():
            acc_ref[...] = jnp.zeros_like(acc_ref)

        acc_ref[...] += lax.dot_general(a_ref[...], b_ref[...], dn, preferred_element_type=F32)

        @pl.when(kk == nk - 1)
        def _():
            o_ref[...] = acc_ref[...].astype(o_ref.dtype)

    def body_whole_k(a_ref, b_ref, o_ref):
        o_ref[...] = lax.dot_general(a_ref[...], b_ref[...], dn, preferred_element_type=F32).astype(o_ref.dtype)

    oshape = (2, m, n // 2) if o_split else (m, n)
    return pl.pallas_call(
        body if nk > 1 else body_whole_k, name=name, grid=(m // tm, n // tn, nk), in_specs=[a_spec, b_spec], out_specs=o_spec,
        out_shape=jax.ShapeDtypeStruct(oshape, out_dtype), scratch_shapes=[pltpu.VMEM((tm, tn), F32)] if nk > 1 else [],
        compiler_params=_params(("parallel", "parallel", "arbitrary")))(a, b)


def _mm_exact(a, b, name):
    def body(a_ref, b_ref, o_ref):
        o_ref[...] = jnp.dot(a_ref[...], b_ref[...], preferred_element_type=F32, precision=lax.Precision.HIGHEST)

    return pl.pallas_call(body, name=name, out_shape=jax.ShapeDtypeStruct((a.shape[0], b.shape[1]), F32),
                          compiler_params=_params())(a, b)


def _row_spec(width, col=0):
    return pl.BlockSpec((TM, width), lambda i: (i, col))


def _full_spec(shape):
    nd = len(shape)
    return pl.BlockSpec(shape, lambda *_: (0,) * nd)


def _normmod_fn(x, g, mod, is_ctx, row):
    return _rms(x, g) * (1.0 + _sel(is_ctx, mod, row + 1)) + _sel(is_ctx, mod, row)


def _normmod_fwd(x, g, mod, row, nbl, name):
    r, d = x.shape

    def body(x_ref, g_ref, mod_ref, h_ref):
        is_ctx = pl.program_id(0) >= nbl
        h_ref[...] = _normmod_fn(x_ref[...], g_ref[...], mod_ref[...], is_ctx, row).astype(BF16)

    return pl.pallas_call(
        body, name=name, grid=(r // TM,), in_specs=[_row_spec(d), _full_spec((1, d)), _full_spec(mod.shape)],
        out_specs=_row_spec(d), out_shape=jax.ShapeDtypeStruct((r, d), BF16), compiler_params=_params(("parallel",)))(x, g, mod)


def _normmod_bwd(x, g, mod, dh, dx_in, row, nbl, name):
    r, d = x.shape

    def body(x_ref, g_ref, mod_ref, dh_ref, dxin_ref, dx_ref, dg_ref, dmod_ref):
        i = pl.program_id(0)
        is_ctx = i >= nbl

        @pl.when(i == 0)
        def _():
            dg_ref[...] = jnp.zeros_like(dg_ref)
            dmod_ref[...] = jnp.zeros_like(dmod_ref)

        _, vjp = jax.vjp(lambda xx, gg, mm: _normmod_fn(xx, gg, mm, is_ctx, row), x_ref[...], g_ref[...], mod_ref[...])
        dx, dg, dmod = vjp(dh_ref[...])
        dx_ref[...] = dxin_ref[...] + dx
        dg_ref[...] += dg
        dmod_ref[...] += dmod

    return pl.pallas_call(
        body, name=name, grid=(r // TM,),
        in_specs=[_row_spec(d), _full_spec((1, d)), _full_spec(mod.shape), _row_spec(d), _row_spec(d)],
        out_specs=[_row_spec(d), _full_spec((1, d)), _full_spec(mod.shape)],
        out_shape=[jax.ShapeDtypeStruct((r, d), F32), jax.ShapeDtypeStruct((1, d), F32), jax.ShapeDtypeStruct(mod.shape, F32)],
        compiler_params=_params(("arbitrary",)))(x, g, mod, dh, dx_in)


def _resid_fwd(x, y, mod, row, nbl, name):
    r, d = x.shape

    def body(x_ref, y_ref, mod_ref, o_ref):
        is_ctx = pl.program_id(0) >= nbl
        o_ref[...] = x_ref[...] + _sel(is_ctx, mod_ref[...], row) * y_ref[...]

    return pl.pallas_call(
        body, name=name, grid=(r // TM,), in_specs=[_row_spec(d), _row_spec(d), _full_spec(mod.shape)],
        out_specs=_row_spec(d), out_shape=jax.ShapeDtypeStruct((r, d), F32), compiler_params=_params(("parallel",)))(x, y, mod)


def _resid_bwd(dx, y, mod, row, nbl, name):
    r, d = dx.shape

    def body(dx_ref, y_ref, mod_ref, dy_ref, dmod_ref):
        i = pl.program_id(0)
        is_ctx = i >= nbl

        @pl.when(i == 0)
        def _():
            dmod_ref[...] = jnp.zeros_like(dmod_ref)

        dxv = dx_ref[...]
        dy_ref[...] = (_sel(is_ctx, mod_ref[...], row) * dxv).astype(BF16)
        dgate = jnp.sum(dxv * y_ref[...], axis=0, keepdims=True)

        @pl.when(is_ctx)
        def _():
            dmod_ref[1, row:row + 1, :] += dgate

        @pl.when(jnp.logical_not(is_ctx))
        def _():
            dmod_ref[0, row:row + 1, :] += dgate

    return pl.pallas_call(
        body, name=name, grid=(r // TM,), in_specs=[_row_spec(d), _row_spec(d), _full_spec(mod.shape)],
        out_specs=[_row_spec(d), _full_spec(mod.shape)],
        out_shape=[jax.ShapeDtypeStruct((r, d), BF16), jax.ShapeDtypeStruct(mod.shape, F32)],
        compiler_params=_params(("arbitrary",)))(dx, y, mod)


def _loss_kernel(x, target, nbl, name):
    r, d = x.shape

    def body(x_ref, t_ref, loss_ref, dx_ref):
        i = pl.program_id(0)

        @pl.when(i == 0)
        def _():
            loss_ref[...] = jnp.zeros_like(loss_ref)

        @pl.when(i < nbl)
        def _():
            e = x_ref[...] - t_ref[...]
            dx_ref[...] = e / d
            loss_ref[...] += 0.5 * jnp.sum(jnp.mean(e * e, axis=-1, keepdims=True), axis=0, keepdims=True)

        @pl.when(i >= nbl)
        def _():
            dx_ref[...] = jnp.zeros_like(dx_ref)

    return pl.pallas_call(
        body, name=name, grid=(r // TM,),
        in_specs=[_row_spec(d), pl.BlockSpec((TM, d), lambda i: (jnp.minimum(i, nbl - 1), 0))],
        out_specs=[_full_spec((1, 1)), _row_spec(d)],
        out_shape=[jax.ShapeDtypeStruct((1, 1), F32), jax.ShapeDtypeStruct((r, d), F32)],
        compiler_params=_params(("arbitrary",)))(x, target)


SP_QA, SP_KVA, SP_QN, SP_QR, SP_KN, SP_KR, SP_SQ, SP_SK, SP_NQ, SP_NK, SP_SINK = range(11)
C_CQ, C_CKV, C_KR, C_POOL, C_SQ, C_SK, C_SV, C_NQ, C_NK, C_NV = 0, 512, 768, 896, 1408, 1920, 2048, 2176, 2688, 3200


def _prep_fn(p, tab, sp, wqb, wkvb):
    cos, sin = tab[:, 0:64], tab[:, 64:128]
    q = _bdot(_rms(p[:, C_CQ:C_CQ + 512], sp[SP_QA:SP_QA + 1, 0:512]), wqb)
    kv = _bdot(_rms(p[:, C_CKV:C_CKV + 256], sp[SP_KVA:SP_KVA + 1, 0:256]), wkvb)
    krr = _rope(_rms(p[:, C_KR:C_KR + 64], sp[SP_KR:SP_KR + 1, 0:64]), cos, sin)
    zero = jnp.zeros_like(krr)
    aq, ak, av = [], [], []
    for h in range(4):
        qn = _rms(q[:, 128 * h:128 * h + 128], sp[SP_QN:SP_QN + 1, 0:128])
        qr = _rope(_rms(q[:, 512 + 64 * h:576 + 64 * h], sp[SP_QR:SP_QR + 1, 0:64]), cos, sin)
        kn = _rms(kv[:, 256 * h:256 * h + 128], sp[SP_KN:SP_KN + 1, 0:128])
        aq += [qn, qr, zero]
        ak += [kn, krr, zero]
        av.append(kv[:, 256 * h + 128:256 * h + 256])
    cq = [_rope(_rms(p[:, C_SQ + 64 * h:C_SQ + 64 * h + 64], sp[SP_SQ:SP_SQ + 1, 0:64]), cos, sin) for h in range(8)]
    ck = [_rope(_rms(p[:, C_SK + 64 * h:C_SK + 64 * h + 64], sp[SP_SK:SP_SK + 1, 0:64]), cos, sin) for h in range(2)]
    dq = [_rms(p[:, C_NQ + 64 * h:C_NQ + 64 * h + 64], sp[SP_NQ:SP_NQ + 1, 0:64]) for h in range(8)]
    dk = [_rms(p[:, C_NK + 64 * h:C_NK + 64 * h + 64], sp[SP_NK:SP_NK + 1, 0:64]) for h in range(8)]
    cat = lambda xs: jnp.concatenate(xs, axis=-1)
    return (cat(aq), cat(ak), cat(av), cat(cq), cat(ck), p[:, C_SV:C_SV + 128], cat(dq), cat(dk), p[:, C_NV:C_NV + 512])


PREP_WIDTHS = (1024, 1024, 512, 512, 128, 128, 512, 512, 512)


def _prep_fwd(p, tab, sp, wqb, wkvb, name):
    r = p.shape[0]

    def body(p_ref, tab_ref, sp_ref, wqb_ref, wkvb_ref, *outs):
        res = _prep_fn(p_ref[...], tab_ref[...], sp_ref[...], wqb_ref[...].astype(F32), wkvb_ref[...].astype(F32))
        for o_ref, v in zip(outs, res):
            o_ref[...] = v.astype(BF16)

    return pl.pallas_call(
        body, name=name, grid=(r // TM,),
        in_specs=[_row_spec(PW), _row_spec(128), _full_spec(sp.shape), _full_spec(wqb.shape), _full_spec(wkvb.shape)],
        out_specs=[_row_spec(w) for w in PREP_WIDTHS],
        out_shape=[jax.ShapeDtypeStruct((r, w), BF16) for w in PREP_WIDTHS],
        compiler_params=_params(("parallel",)))(p, tab, sp, wqb, wkvb)


def _prep_bwd(p, tab, sp, wqb, wkvb, cots, dpool, name):
    r = p.shape[0]

    def body(p_ref, tab_ref, sp_ref, wqb_ref, wkvb_ref, *rest):
        cot_refs, dpool_ref = rest[:9], rest[9]
        dp_ref, dsp_ref, dwqb_ref, dwkvb_ref = rest[10:]
        i = pl.program_id(0)

        @pl.when(i == 0)
        def _():
            dsp_ref[...] = jnp.zeros_like(dsp_ref)
            dwqb_ref[...] = jnp.zeros_like(dwqb_ref)
            dwkvb_ref[...] = jnp.zeros_like(dwkvb_ref)

        tab = tab_ref[...]
        _, vjp = jax.vjp(lambda pp, ss, wq, wk: _prep_fn(pp, tab, ss, wq, wk),
                         p_ref[...], sp_ref[...], wqb_ref[...].astype(F32), wkvb_ref[...].astype(F32))
        dp, dsp, dwq, dwk = vjp(tuple(c[...] for c in cot_refs))
        dp_ref[...] = dp.astype(BF16)
        dp_ref[:, C_POOL:C_POOL + 512] = dpool_ref[...].astype(BF16)
        dsp_ref[...] += dsp
        dwqb_ref[...] += dwq
        dwkvb_ref[...] += dwk

    return pl.pallas_call(
        body, name=name, grid=(r // TM,),
        in_specs=[_row_spec(PW), _row_spec(128), _full_spec(sp.shape), _full_spec(wqb.shape), _full_spec(wkvb.shape)]
        + [_row_spec(w) for w in PREP_WIDTHS] + [_row_spec(512)],
        out_specs=[_row_spec(PW), _full_spec(sp.shape), _full_spec(wqb.shape), _full_spec(wkvb.shape)],
        out_shape=[jax.ShapeDtypeStruct((r, PW), BF16), jax.ShapeDtypeStruct(sp.shape, F32),
                   jax.ShapeDtypeStruct(wqb.shape, F32), jax.ShapeDtypeStruct(wkvb.shape, F32)],
        compiler_params=_params(("arbitrary",)))(p, tab, sp, wqb, wkvb, *cots, dpool)


def _mla_probs(q, k, is_ctx, n_lat):
    s = lax.dot_general(q, k, (((1,), (1,)), ((), ())), preferred_element_type=F32) * MLA_SCALE
    kid = lax.broadcasted_iota(I32, (1, s.shape[1]), 1)
    s = s + jnp.where(jnp.logical_and(is_ctx, kid < n_lat), NEG, 0.0)
    e = jnp.exp(s - jnp.max(s, axis=-1, keepdims=True))
    return e * (1.0 / jnp.sum(e, axis=-1, keepdims=True))


def _mla_fwd(aq, ak, av, n_lat, name):
    r = aq.shape[0]
    nbl = n_lat // TM

    def body(q_ref, k_ref, v_ref, o_ref):
        p = _mla_probs(q_ref[...], k_ref[...], pl.program_id(1) >= nbl, n_lat)
        o_ref[...] = jnp.dot(p.astype(BF16), v_ref[...], preferred_element_type=F32).astype(BF16)

    return pl.pallas_call(
        body, name=name, grid=(4, r // TM),
        in_specs=[pl.BlockSpec((TM, 256), lambda h, i: (i, h)), pl.BlockSpec((r, 256), lambda h, i: (0, h)),
                  pl.BlockSpec((r, 128), lambda h, i: (0, h))],
        out_specs=pl.BlockSpec((TM, 128), lambda h, i: (i, h)),
        out_shape=jax.ShapeDtypeStruct((r, 512), BF16), compiler_params=_params(("parallel", "parallel")))(aq, ak, av)


def _mla_bwd(aq, ak, av, dmix, n_lat, name):
    r = aq.shape[0]
    nbl = n_lat // TM

    def body(q_ref, k_ref, v_ref, do_ref, dq_ref, dk_ref, dv_ref):
        i = pl.program_id(1)

        @pl.when(i == 0)
        def _():
            dk_ref[...] = jnp.zeros_like(dk_ref)
            dv_ref[...] = jnp.zeros_like(dv_ref)

        q, k, v = q_ref[...], k_ref[...], v_ref[...]
        dob = do_ref[...].astype(BF16)
        p = _mla_probs(q, k, i >= nbl, n_lat)
        dv_ref[...] += lax.dot_general(p.astype(BF16), dob, (((0,), (0,)), ((), ())), preferred_element_type=F32)
        dp = lax.dot_general(dob, v, (((1,), (1,)), ((), ())), preferred_element_type=F32)
        ds = (p * (dp - jnp.sum(dp * p, axis=-1, keepdims=True)) * MLA_SCALE).astype(BF16)
        dq_ref[...] = jnp.dot(ds, k, preferred_element_type=F32)
        dk_ref[...] += lax.dot_general(ds, q, (((0,), (0,)), ((), ())), preferred_element_type=F32)

    return pl.pallas_call(
        body, name=name, grid=(4, r // TM),
        in_specs=[pl.BlockSpec((TM, 256), lambda h, i: (i, h)), pl.BlockSpec((r, 256), lambda h, i: (0, h)),
                  pl.BlockSpec((r, 128), lambda h, i: (0, h)), pl.BlockSpec((TM, 128), lambda h, i: (i, h))],
        out_specs=[pl.BlockSpec((TM, 256), lambda h, i: (i, h)), pl.BlockSpec((r, 256), lambda h, i: (0, h)),
                   pl.BlockSpec((r, 128), lambda h, i: (0, h))],
        out_shape=[jax.ShapeDtypeStruct((r, 1024), F32), jax.ShapeDtypeStruct((r, 1024), F32), jax.ShapeDtypeStruct((r, 512), F32)],
        compiler_params=_params(("parallel", "arbitrary")))(aq, ak, av, dmix)


def _pool_fn(ext, w, sc, gid0, grp, is_ctx, n_lat, r_all):
    gid = gid0 + lax.broadcasted_iota(I32, (TM + 2 * HALO, 1), 0)
    lo = jnp.where(is_ctx, n_lat, 0)
    hi = jnp.where(is_ctx, r_all, n_lat)
    z = jnp.where(jnp.logical_and(gid >= lo, gid < hi), ext, 0.0)
    w2 = jnp.roll(z, 1, axis=0) + z
    w4 = jnp.roll(w2, 1, axis=0) + jnp.roll(w2, -1, axis=0)
    w8 = jnp.roll(w4, 2, axis=0) + jnp.roll(w4, -2, axis=0)
    w16 = jnp.roll(w8, 4, axis=0) + jnp.roll(w8, -4, axis=0)
    win = jnp.where(grp == 0, w2, jnp.where(grp == 1, w4, jnp.where(grp == 2, w8, w16)))
    half = jnp.left_shift(1, grp)
    cnt = jnp.maximum(jnp.minimum(gid + half, hi) - jnp.maximum(gid - half, lo), 1).astype(F32)
    d = (win / cnt - ext)[HALO:HALO + TM]
    return _bdot(d, w) * sc


def _pool_ext(u_ref, i, r_all):
    s0 = pl.multiple_of(jnp.maximum(i * TM - HALO, 0), HALO)
    s2 = pl.multiple_of(jnp.minimum(i * TM + TM, r_all - HALO), HALO)
    ext = jnp.concatenate([u_ref[pl.ds(s0, HALO), :], u_ref[pl.ds(pl.multiple_of(i * TM, TM), TM), :], u_ref[pl.ds(s2, HALO), :]], axis=0)
    return ext, s0, s2


def _pool_specs(r):
    return [pl.BlockSpec((r, 128), lambda g, i: (0, C_POOL // 128 + g)), pl.BlockSpec((None, 128, 128), lambda g, i: (g, 0, 0)),
            pl.BlockSpec((None, 1, 128), lambda g, i: (g, 0, 0))]


def _pool_fwd(p, pool_w, pool_sc, n_lat, name):
    r = p.shape[0]
    nbl = n_lat // TM

    def body(u_ref, w_ref, sc_ref, o_ref):
        g, i = pl.program_id(0), pl.program_id(1)
        ext, _, _ = _pool_ext(u_ref, i, r)
        o_ref[...] = _pool_fn(ext, w_ref[...], sc_ref[...], i * TM - HALO, g, i >= nbl, n_lat, r).astype(BF16)

    return pl.pallas_call(
        body, name=name, grid=(4, r // TM), in_specs=_pool_specs(r), out_specs=pl.BlockSpec((TM, 128), lambda g, i: (i, g)),
        out_shape=jax.ShapeDtypeStruct((r, 512), BF16), compiler_params=_params(("parallel", "parallel")))(p, pool_w, pool_sc)


def _pool_bwd(p, pool_w, pool_sc, dmix, n_lat, name):
    r = p.shape[0]
    nbl = n_lat // TM

    def body(u_ref, w_ref, sc_ref, do_ref, du_ref, dw_ref, dsc_ref):
        g, i = pl.program_id(0), pl.program_id(1)

        @pl.when(i == 0)
        def _():
            du_ref[...] = jnp.zeros_like(du_ref)
            dw_ref[...] = jnp.zeros_like(dw_ref)
            dsc_ref[...] = jnp.zeros_like(dsc_ref)

        ext, s0, s2 = _pool_ext(u_ref, i, r)
        _, vjp = jax.vjp(lambda e, w, s: _pool_fn(e, w, s, i * TM - HALO, g, i >= nbl, n_lat, r), ext, w_ref[...], sc_ref[...])
        dext, dw, dsc = vjp(do_ref[...])
        du_ref[pl.ds(s0, HALO), :] += dext[0:HALO]
        du_ref[pl.ds(pl.multiple_of(i * TM, TM), TM), :] += dext[HALO:HALO + TM]
        du_ref[pl.ds(s2, HALO), :] += dext[HALO + TM:]
        dw_ref[...] += dw
        dsc_ref[...] += dsc

    return pl.pallas_call(
        body, name=name, grid=(4, r // TM), in_specs=_pool_specs(r) + [pl.BlockSpec((TM, 128), lambda g, i: (i, 4 + g))],
        out_specs=[pl.BlockSpec((r, 128), lambda g, i: (0, g)), pl.BlockSpec((None, 128, 128), lambda g, i: (g, 0, 0)),
                   pl.BlockSpec((None, 1, 128), lambda g, i: (g, 0, 0))],
        out_shape=[jax.ShapeDtypeStruct((r, 512), F32), jax.ShapeDtypeStruct((4, 128, 128), F32), jax.ShapeDtypeStruct((4, 1, 128), F32)],
        compiler_params=_params(("parallel", "arbitrary")))(p, pool_w, pool_sc, dmix)


def _softmax_parts(parts, extra=None):
    m = functools.reduce(jnp.maximum, [jnp.max(s, axis=-1, keepdims=True) for s in parts])
    if extra is not None:
        m = jnp.maximum(m, extra)
    m = lax.stop_gradient(m)
    es = [jnp.exp(s - m) for s in parts]
    den = functools.reduce(jnp.add, [jnp.sum(e, axis=-1, keepdims=True) for e in es])
    if extra is not None:
        den = den + jnp.exp(extra - m)
    inv = 1.0 / den
    return [e * inv for e in es]


def _swa_band(qpos0, kpos0, is_ctx):
    qpos = qpos0 + lax.broadcasted_iota(I32, (TM, SWA_KEYS), 0)
    kpos = kpos0 + lax.broadcasted_iota(I32, (TM, SWA_KEYS), 1)
    valid = jnp.logical_and(jnp.abs(kpos - qpos) <= 128, jnp.logical_not(is_ctx))
    band = jnp.where(valid, 0.0, NEG)
    return jnp.concatenate([band] * 4, axis=0)


def _swa_fn(q4, kw, vw, kc, vc, sink4, band):
    qs = jnp.concatenate([q4[:, 64 * g:64 * g + 64] for g in range(4)], axis=0)
    s_loc = _bdot_nt(qs, kw) * HD_SCALE + band
    s_ctx = _bdot_nt(qs, kc) * HD_SCALE
    sink = jnp.concatenate([jnp.broadcast_to(sink4[:, g:g + 1], (TM, 1)) for g in range(4)], axis=0)
    p_loc, p_ctx = _softmax_parts([s_loc, s_ctx], sink)
    o = _bdot(p_loc, vw) + _bdot(p_ctx, vc)
    return jnp.concatenate([o[TM * g:TM * (g + 1)] for g in range(4)], axis=1)


def _swa_window(i, n_lat):
    return pl.multiple_of(jnp.clip(i * TM - 128, 0, n_lat - SWA_KEYS), 128)


def _swa_fwd(cq, ck, cv, sp, n_lat, name):
    r = cq.shape[0]
    nbl = n_lat // TM

    def body(q_ref, k_ref, v_ref, sp_ref, o_ref):
        i = pl.program_id(0)
        k0 = _swa_window(i, n_lat)
        kw, vw = k_ref[pl.ds(k0, SWA_KEYS), :].astype(F32), v_ref[pl.ds(k0, SWA_KEYS), :].astype(F32)
        kc, vc = k_ref[pl.ds(n_lat, r - n_lat), :].astype(F32), v_ref[pl.ds(n_lat, r - n_lat), :].astype(F32)
        band = _swa_band(i * TM, k0, i >= nbl)
        for j in range(2):
            c = slice(64 * j, 64 * j + 64)
            o = _swa_fn(q_ref[:, 256 * j:256 * j + 256].astype(F32), kw[:, c], vw[:, c], kc[:, c], vc[:, c],
                        sp_ref[SP_SINK:SP_SINK + 1, 4 * j:4 * j + 4], band)
            o_ref[:, 256 * j:256 * j + 256] = o.astype(BF16)

    return pl.pallas_call(
        body, name=name, grid=(r // TM,),
        in_specs=[_row_spec(512), _full_spec((r, 128)), _full_spec((r, 128)), _full_spec(sp.shape)],
        out_specs=_row_spec(512), out_shape=jax.ShapeDtypeStruct((r, 512), BF16), compiler_params=_params(("parallel",)))(cq, ck, cv, sp)


def _swa_bwd(cq, ck, cv, sp, dmix, n_lat, name):
    r = cq.shape[0]
    nbl = n_lat // TM
    nc = r - n_lat

    def body(q_ref, k_ref, v_ref, sp_ref, do_ref, dq_ref, dk_ref, dv_ref, dsp_ref):
        i = pl.program_id(0)

        @pl.when(i == 0)
        def _():
            dk_ref[...] = jnp.zeros_like(dk_ref)
            dv_ref[...] = jnp.zeros_like(dv_ref)
            dsp_ref[...] = jnp.zeros_like(dsp_ref)

        k0 = _swa_window(i, n_lat)
        kw, vw = k_ref[pl.ds(k0, SWA_KEYS), :].astype(F32), v_ref[pl.ds(k0, SWA_KEYS), :].astype(F32)
        kc, vc = k_ref[pl.ds(n_lat, nc), :].astype(F32), v_ref[pl.ds(n_lat, nc), :].astype(F32)
        dkw, dvw, dkc, dvc, dsk = [], [], [], [], []
        band = _swa_band(i * TM, k0, i >= nbl)
        for j in range(2):
            c = slice(64 * j, 64 * j + 64)
            _, vjp = jax.vjp(lambda q4, a, b, cc, d, s: _swa_fn(q4, a, b, cc, d, s, band),
                             q_ref[:, 256 * j:256 * j + 256].astype(F32), kw[:, c], vw[:, c], kc[:, c], vc[:, c],
                             sp_ref[SP_SINK:SP_SINK + 1, 4 * j:4 * j + 4])
            dq4, a, b, cc, d, s = vjp(do_ref[:, 256 * j:256 * j + 256])
            dq_ref[:, 256 * j:256 * j + 256] = dq4
            dkw.append(a), dvw.append(b), dkc.append(cc), dvc.append(d), dsk.append(s)
        cat = lambda xs: jnp.concatenate(xs, axis=1)
        dk_ref[pl.ds(k0, SWA_KEYS), :] += cat(dkw)
        dv_ref[pl.ds(k0, SWA_KEYS), :] += cat(dvw)
        dk_ref[pl.ds(n_lat, nc), :] += cat(dkc)
        dv_ref[pl.ds(n_lat, nc), :] += cat(dvc)
        dsp_ref[SP_SINK:SP_SINK + 1, 0:8] += cat(dsk)

    return pl.pallas_call(
        body, name=name, grid=(r // TM,),
        in_specs=[_row_spec(512), _full_spec((r, 128)), _full_spec((r, 128)), _full_spec(sp.shape), _row_spec(512, 2)],
        out_specs=[_row_spec(512), _full_spec((r, 128)), _full_spec((r, 128)), _full_spec(sp.shape)],
        out_shape=[jax.ShapeDtypeStruct((r, 512), F32), jax.ShapeDtypeStruct((r, 128), F32), jax.ShapeDtypeStruct((r, 128), F32),
                   jax.ShapeDtypeStruct(sp.shape, F32)],
        compiler_params=_params(("arbitrary",)))(cq, ck, cv, sp, dmix)


def _na_fn(q, kw, vw, kc, vc, bias):
    s_loc = _bdot_nt(q, kw) * HD_SCALE + bias
    s_ctx = _bdot_nt(q, kc) * HD_SCALE
    p_loc, p_ctx = _softmax_parts([s_loc, s_ctx])
    return _bdot(p_loc, vw) + _bdot(p_ctx, vc)


NA_MASKED = 15


def _na_geometry(i, n_lat, is_ctx):
    rows = n_lat // GRID_W
    qrow0 = i * (TM // GRID_W)
    krow0 = jnp.clip(qrow0 - 4, 0, rows - NA_KROWS)
    dr = []
    for qi in range(TM // GRID_W):
        r_lo = jnp.clip(qrow0 + qi - 4, 0, rows - 8)
        row = []
        for kj in range(NA_KROWS):
            kr = krow0 + kj
            ok = jnp.logical_and(jnp.logical_and(kr >= r_lo, kr < r_lo + 8), jnp.logical_not(is_ctx))
            row.append(jnp.where(ok, kr - (qrow0 + qi) + 7, NA_MASKED))
        dr.append(row)
    return pl.multiple_of(krow0 * GRID_W, GRID_W), dr


def _na_bias(t1_ref, hh, dr):
    return jnp.concatenate([jnp.concatenate([t1_ref[hh, dr[qi][kj]] for kj in range(NA_KROWS)], axis=1)
                            for qi in range(TM // GRID_W)], axis=0)


def _na_specs(r):
    return [pl.BlockSpec((TM, 128), lambda pr, i: (i, pr)), pl.BlockSpec((r, 128), lambda pr, i: (0, pr)),
            pl.BlockSpec((r, 128), lambda pr, i: (0, pr)), pl.BlockSpec((2, 16, GRID_W, GRID_W), lambda pr, i: (pr, 0, 0, 0))]


def _na_fwd(dq, dk, dv, t1, n_lat, name):
    r = dq.shape[0]
    nbl = n_lat // TM
    nc = r - n_lat
    nk = NA_KROWS * GRID_W

    def body(q_ref, k_ref, v_ref, t1_ref, o_ref):
        i = pl.program_id(1)
        k0, dr = _na_geometry(jnp.minimum(i, nbl - 1), n_lat, i >= nbl)
        kw, vw = k_ref[pl.ds(k0, nk), :].astype(F32), v_ref[pl.ds(k0, nk), :].astype(F32)
        kc, vc = k_ref[pl.ds(n_lat, nc), :].astype(F32), v_ref[pl.ds(n_lat, nc), :].astype(F32)
        for hh in range(2):
            c = slice(64 * hh, 64 * hh + 64)
            o = _na_fn(q_ref[:, c].astype(F32), kw[:, c], vw[:, c], kc[:, c], vc[:, c], _na_bias(t1_ref, hh, dr))
            o_ref[:, c] = o.astype(BF16)

    return pl.pallas_call(
        body, name=name, grid=(4, r // TM), in_specs=_na_specs(r), out_specs=pl.BlockSpec((TM, 128), lambda pr, i: (i, pr)),
        out_shape=jax.ShapeDtypeStruct((r, 512), BF16), compiler_params=_params(("parallel", "parallel")))(dq, dk, dv, t1)


def _na_bwd(dq, dk, dv, t1, dmix, n_lat, name):
    r = dq.shape[0]
    nbl = n_lat // TM
    nc = r - n_lat
    nk = NA_KROWS * GRID_W

    def body(q_ref, k_ref, v_ref, t1_ref, do_ref, dq_ref, dk_ref, dv_ref, dt1_ref):
        i = pl.program_id(1)

        @pl.when(i == 0)
        def _():
            dk_ref[...] = jnp.zeros_like(dk_ref)
            dv_ref[...] = jnp.zeros_like(dv_ref)
            dt1_ref[...] = jnp.zeros_like(dt1_ref)

        k0, dr = _na_geometry(jnp.minimum(i, nbl - 1), n_lat, i >= nbl)
        kw, vw = k_ref[pl.ds(k0, nk), :].astype(F32), v_ref[pl.ds(k0, nk), :].astype(F32)
        kc, vc = k_ref[pl.ds(n_lat, nc), :].astype(F32), v_ref[pl.ds(n_lat, nc), :].astype(F32)
        dkw, dvw, dkc, dvc = [], [], [], []
        for hh in range(2):
            c = slice(64 * hh, 64 * hh + 64)
            _, vjp = jax.vjp(_na_fn,
                             q_ref[:, c].astype(F32), kw[:, c], vw[:, c], kc[:, c], vc[:, c], _na_bias(t1_ref, hh, dr))
            dqh, a, b, cc, d, dbias = vjp(do_ref[:, c])
            dq_ref[:, c] = dqh
            dkw.append(a), dvw.append(b), dkc.append(cc), dvc.append(d)
            for qi in range(TM // GRID_W):
                for kj in range(NA_KROWS):
                    dt1_ref[hh, dr[qi][kj]] += dbias[GRID_W * qi:GRID_W * (qi + 1), GRID_W * kj:GRID_W * (kj + 1)]
        cat = lambda xs: jnp.concatenate(xs, axis=1)
        dk_ref[pl.ds(k0, nk), :] += cat(dkw)
        dv_ref[pl.ds(k0, nk), :] += cat(dvw)
        dk_ref[pl.ds(n_lat, nc), :] += cat(dkc)
        dv_ref[pl.ds(n_lat, nc), :] += cat(dvc)

    return pl.pallas_call(
        body, name=name, grid=(4, r // TM), in_specs=_na_specs(r) + [pl.BlockSpec((TM, 128), lambda pr, i: (i, 12 + pr))],
        out_specs=[pl.BlockSpec((TM, 128), lambda pr, i: (i, pr)), pl.BlockSpec((r, 128), lambda pr, i: (0, pr)),
                   pl.BlockSpec((r, 128), lambda pr, i: (0, pr)), pl.BlockSpec((2, 16, GRID_W, GRID_W), lambda pr, i: (pr, 0, 0, 0))],
        out_shape=[jax.ShapeDtypeStruct((r, 512), F32)] * 3 + [jax.ShapeDtypeStruct((8, 16, GRID_W, GRID_W), F32)],
        compiler_params=_params(("parallel", "arbitrary")))(dq, dk, dv, t1, dmix)


def _conv_ext(main_ref, prev_ref, next_ref, edges):
    prev_ok, next_ok = edges
    return jnp.concatenate([jnp.where(prev_ok, prev_ref[...], 0.0), main_ref[...], jnp.where(next_ok, next_ref[...], 0.0)], axis=0)


def _conv_edges(i, nbl, nb):
    return jnp.logical_and(i != 0, i != nbl), jnp.logical_and(i != nbl - 1, i != nb - 1)


def _conv_apply(ext, w, b):
    up = jnp.roll(ext, 1, axis=0)
    dn = jnp.roll(ext, -1, axis=0)
    return up * w[0:1] + ext * w[1:2] + dn * w[2:3] + b, up, dn


def _conv_in_specs(tc, r):
    nb8 = TM // HALO
    last8 = r // HALO - 1

    def trio(half):
        return [pl.BlockSpec((None, TM, tc), lambda j, i: (half, i, j)),
                pl.BlockSpec((None, HALO, tc), lambda j, i: (half, jnp.maximum(i * nb8 - 1, 0), j)),
                pl.BlockSpec((None, HALO, tc), lambda j, i: (half, jnp.minimum((i + 1) * nb8, last8), j))]

    wb = [pl.BlockSpec((None, 3, tc), lambda j, i: (0, 0, j)), pl.BlockSpec((None, 3, tc), lambda j, i: (1, 0, j)),
          pl.BlockSpec((None, 1, tc), lambda j, i: (0, 0, j)), pl.BlockSpec((None, 1, tc), lambda j, i: (1, 0, j))]
    return trio(0) + trio(1) + wb


def _convgate_fwd(a3, cw, cb, n_lat, name):
    _, r, ff = a3.shape
    nbl = n_lat // TM
    tc = _pick(ff, (512, 256, 128))

    def body(g_ref, gp_ref, gn_ref, v_ref, vp_ref, vn_ref, wg_ref, wv_ref, bg_ref, bv_ref, u_ref):
        edges = _conv_edges(pl.program_id(1), nbl, r // TM)
        gg, _, _ = _conv_apply(_conv_ext(g_ref, gp_ref, gn_ref, edges), wg_ref[...], bg_ref[...])
        gv, _, _ = _conv_apply(_conv_ext(v_ref, vp_ref, vn_ref, edges), wv_ref[...], bv_ref[...])
        u_ref[...] = (jax.nn.silu(gg[HALO:HALO + TM]) * gv[HALO:HALO + TM]).astype(BF16)

    return pl.pallas_call(
        body, name=name, grid=(ff // tc, r // TM), in_specs=_conv_in_specs(tc, r),
        out_specs=pl.BlockSpec((TM, tc), lambda j, i: (i, j)), out_shape=jax.ShapeDtypeStruct((r, ff), BF16),
        compiler_params=_params(("parallel", "parallel")))(a3, a3, a3, a3, a3, a3, cw, cw, cb, cb)


def _convgate_bwd(a3, cw, cb, du, n_lat, name):
    _, r, ff = a3.shape
    nbl = n_lat // TM
    tc = _pick(ff, (512, 256, 128))
    nb8 = TM // HALO
    last8 = r // HALO - 1

    def body(g_ref, gp_ref, gn_ref, v_ref, vp_ref, vn_ref, wg_ref, wv_ref, bg_ref, bv_ref, du_ref, dup_ref, dun_ref,
             da_ref, dcw_ref, dcb_ref):
        i = pl.program_id(1)

        @pl.when(i == 0)
        def _():
            dcw_ref[...] = jnp.zeros_like(dcw_ref)
            dcb_ref[...] = jnp.zeros_like(dcb_ref)

        edges = _conv_edges(i, nbl, r // TM)
        wg, wv = wg_ref[...], wv_ref[...]
        eg, ev = _conv_ext(g_ref, gp_ref, gn_ref, edges), _conv_ext(v_ref, vp_ref, vn_ref, edges)
        gg, ug, dg_ = _conv_apply(eg, wg, bg_ref[...])
        gv, uv, dv_ = _conv_apply(ev, wv, bv_ref[...])
        due = _conv_ext(du_ref, dup_ref, dun_ref, edges)
        sg = jax.nn.sigmoid(gg)
        dgg = due * gv * (sg * (1.0 + gg * (1.0 - sg)))
        dgv = due * (gg * sg)
        main = slice(HALO, HALO + TM)
        for h, (dgx, w, ex, upx, dnx) in enumerate(((dgg, wg, eg, ug, dg_), (dgv, wv, ev, uv, dv_))):
            da = dgx * w[1:2] + jnp.roll(dgx, -1, axis=0) * w[0:1] + jnp.roll(dgx, 1, axis=0) * w[2:3]
            da_ref[h] = da[main].astype(BF16)
            dm = dgx[main]
            dcw_ref[h, 0:1, :] += jnp.sum(dm * upx[main], axis=0, keepdims=True)
            dcw_ref[h, 1:2, :] += jnp.sum(dm * ex[main], axis=0, keepdims=True)
            dcw_ref[h, 2:3, :] += jnp.sum(dm * dnx[main], axis=0, keepdims=True)
            dcb_ref[h] += jnp.sum(dm, axis=0, keepdims=True)

    du_specs = [pl.BlockSpec((TM, tc), lambda j, i: (i, j)),
                pl.BlockSpec((HALO, tc), lambda j, i: (jnp.maximum(i * nb8 - 1, 0), j)),
                pl.BlockSpec((HALO, tc), lambda j, i: (jnp.minimum((i + 1) * nb8, last8), j))]
    return pl.pallas_call(
        body, name=name, grid=(ff // tc, r // TM), in_specs=_conv_in_specs(tc, r) + du_specs,
        out_specs=[pl.BlockSpec((2, TM, tc), lambda j, i: (0, i, j)), pl.BlockSpec((2, 3, tc), lambda j, i: (0, 0, j)),
                   pl.BlockSpec((2, 1, tc), lambda j, i: (0, 0, j))],
        out_shape=[jax.ShapeDtypeStruct((2, r, ff), BF16), jax.ShapeDtypeStruct((2, 3, ff), F32), jax.ShapeDtypeStruct((2, 1, ff), F32)],
        compiler_params=_params(("parallel", "arbitrary")))(a3, a3, a3, a3, a3, a3, cw, cw, cb, cb, du, du, du)


def _layer_fwd(x, w, l, tab, n_lat):
    nbl = n_lat // TM
    mod = w["mods"][l]
    h1 = _normmod_fwd(x, w["g_mix"][l], mod, 0, nbl, "normmod_fwd")
    p = _mm(h1, w["w_in"][l], "nn", F32, "mm_in")
    qkv = _prep_fwd(p, tab, w["sp"][l], w["w_qb"][l], w["w_kvb"][l], "prep_fwd")
    oa = _mla_fwd(qkv[0], qkv[1], qkv[2], n_lat, "mla_fwd")
    ob = _pool_fwd(p, w["pool_w"][l], w["pool_sc"][l], n_lat, "pool_fwd")
    oc = _swa_fwd(qkv[3], qkv[4], qkv[5], w["sp"][l], n_lat, "swa_fwd")
    od = _na_fwd(qkv[6], qkv[7], qkv[8], w["t1"][l], n_lat, "na_fwd")
    mix = jnp.concatenate([oa, ob, oc, od], axis=1)
    y = _mm(mix, w["w_out"][l], "nn", F32, "mm_out")
    x1 = _resid_fwd(x, y, mod, 2, nbl, "resid_fwd")
    h2 = _normmod_fwd(x1, w["g_ffn"][l], mod, 3, nbl, "normmod_fwd")
    a3 = _mm(h2, w["w_up"][l], "nn", F32, "mm_up", o_split=True)
    u = _convgate_fwd(a3, w["conv_w"][l], w["conv_b"][l], n_lat, "convgate_fwd")
    y2 = _mm(u, w["w_down"][l], "nn", F32, "mm_down")
    x2 = _resid_fwd(x1, y2, mod, 5, nbl, "resid_fwd")
    return x2, dict(x=x, h1=h1, p=p, qkv=qkv, mix=mix, y=y, x1=x1, h2=h2, a3=a3, u=u, y2=y2)


def _layer_bwd(dx, s, w, l, tab, n_lat):
    nbl = n_lat // TM
    mod = w["mods"][l]
    g = {}
    dy2, dmod_a = _resid_bwd(dx, s["y2"], mod, 5, nbl, "resid_bwd")
    g["w_down"] = _mm(s["u"], dy2, "tn", F32, "mm_dwdown")
    du = _mm(dy2, w["w_down"][l], "nt", F32, "mm_du")
    da3, g["conv_w"], g["conv_b"] = _convgate_bwd(s["a3"], w["conv_w"][l], w["conv_b"][l], du, n_lat, "convgate_bwd")
    g["w_up"] = _mm(s["h2"], da3, "tn", F32, "mm_dwup", b_split=True)
    dh2 = _mm(da3, w["w_up"][l], "nt", F32, "mm_dh2", a_split=True)
    dx1, g["g_ffn"], dmod_b = _normmod_bwd(s["x1"], w["g_ffn"][l], mod, dh2, dx, 3, nbl, "normmod_bwd")
    dy, dmod_c = _resid_bwd(dx1, s["y"], mod, 2, nbl, "resid_bwd")
    g["w_out"] = _mm(s["mix"], dy, "tn", F32, "mm_dwout")
    dmix = _mm(dy, w["w_out"][l], "nt", F32, "mm_dmix")
    qkv = s["qkv"]
    daq, dak, dav = _mla_bwd(qkv[0], qkv[1], qkv[2], dmix, n_lat, "mla_bwd")
    dpool, g["pool_w"], g["pool_sc"] = _pool_bwd(s["p"], w["pool_w"][l], w["pool_sc"][l], dmix, n_lat, "pool_bwd")
    dcq, dck, dcv, dsp_c = _swa_bwd(qkv[3], qkv[4], qkv[5], w["sp"][l], dmix, n_lat, "swa_bwd")
    ddq, ddk, ddv, g["t1"] = _na_bwd(qkv[6], qkv[7], qkv[8], w["t1"][l], dmix, n_lat, "na_bwd")
    dp, dsp_p, g["w_qb"], g["w_kvb"] = _prep_bwd(s["p"], tab, w["sp"][l], w["w_qb"][l], w["w_kvb"][l],
                                                (daq, dak, dav, dcq, dck, dcv, ddq, ddk, ddv), dpool, "prep_bwd")
    g["sp"] = dsp_c + dsp_p
    g["w_in"] = _mm(s["h1"], dp, "tn", F32, "mm_dwin")
    dh1 = _mm(dp, w["w_in"][l], "nt", F32, "mm_dh1")
    dx0, g["g_mix"], dmod_d = _normmod_bwd(s["x"], w["g_mix"][l], mod, dh1, dx1, 0, nbl, "normmod_bwd")
    g["mods"] = dmod_a + dmod_b + dmod_c + dmod_d
    return dx0, g


def _local_step(x_all, target, w, tab, n_lat):
    saved = []
    x = x_all
    for l in range(DEPTH):
        x, s = _layer_fwd(x, w, l, tab, n_lat)
        saved.append(s)
    loss, dx = _loss_kernel(x, target, n_lat // TM, "loss")
    grads = [None] * DEPTH
    for l in reversed(range(DEPTH)):
        dx, grads[l] = _layer_bwd(dx, saved[l], w, l, tab, n_lat)
    return loss[0, 0], dx, grads


def _pad_cols(a, widths):
    parts, o = [], 0
    for take, pad in widths:
        parts.append(a[..., o:o + take])
        if pad:
            parts.append(jnp.zeros(a.shape[:-1] + (pad,), a.dtype))
        o += take
    return jnp.concatenate(parts, axis=-1)


def _w_in_layout(w_in):
    return _pad_cols(w_in, [(832, 64), (P_COLS - 832, PW - P_COLS - 64)])


def _w_in_unlayout(g):
    return jnp.concatenate([g[..., 0:832], g[..., 896:896 + P_COLS - 832]], axis=-1)


def _w_qb_layout(w):
    s = w.reshape(w.shape[:-1] + (4, 192))
    return jnp.concatenate([s[..., 0:128].reshape(w.shape[:-1] + (512,)), s[..., 128:192].reshape(w.shape[:-1] + (256,))], axis=-1)


def _w_qb_unlayout(g):
    n = g[..., 0:512].reshape(g.shape[:-1] + (4, 128))
    r = g[..., 512:768].reshape(g.shape[:-1] + (4, 64))
    return jnp.concatenate([n, r], axis=-1).reshape(g.shape[:-1] + (768,))


SP_SLOTS = (("mla_q_a_norm", 512), ("mla_kv_a_norm", 256), ("mla_q_nope_norm", 128), ("mla_q_rope_norm", 64),
            ("mla_k_nope_norm", 128), ("mla_k_rope_norm", 64), ("swa_q_norm", 64), ("swa_k_norm", 64),
            ("na_q_norm", 64), ("na_k_norm", 64), ("swa_sink", 8))


def _sp_pack(small):
    rows = [jnp.pad(small[k], ((0, 0), (0, 512 - n))) for k, n in SP_SLOTS]
    rows += [jnp.zeros_like(rows[0])] * (16 - len(rows))
    return jnp.stack(rows, axis=1)


def _sp_unpack(sp):
    return {k: sp[:, i, 0:n] for i, (k, n) in enumerate(SP_SLOTS)}


def _rpb_onehot():
    qc = lax.broadcasted_iota(I32, (GRID_W, GRID_W), 0)
    kc = lax.broadcasted_iota(I32, (GRID_W, GRID_W), 1)
    dc = (jnp.clip(kc - qc, -15, 15) + 15).reshape(1, GRID_W * GRID_W)
    return (lax.broadcasted_iota(I32, (32, GRID_W * GRID_W), 0) == dc).astype(F32)


def _rpb_expand(rpb):
    l = rpb.shape[0]
    flat = jnp.pad(rpb, ((0, 0), (0, 0), (0, 1), (0, 1))).reshape(l * 128, 32)
    t1 = _mm_exact(flat, _rpb_onehot(), "rpb_expand").reshape(l, 8, 16, GRID_W, GRID_W)
    qc = lax.broadcasted_iota(I32, (GRID_W, GRID_W), 0)
    kc = lax.broadcasted_iota(I32, (GRID_W, GRID_W), 1)
    c_lo = jnp.clip(qc - 8, 0, GRID_W - 16)
    col_ok = jnp.logical_and(kc >= c_lo, kc < c_lo + 16)
    row_ok = lax.broadcasted_iota(I32, (16, 1, 1), 0) != NA_MASKED
    return jnp.where(jnp.logical_and(col_ok[None], row_ok), t1, NEG)


def _rpb_fold(dt1):
    l = dt1.shape[0]
    g = _mm_exact(dt1.reshape(l * 128, GRID_W * GRID_W), _rpb_onehot().T, "rpb_fold")
    return g.reshape(l, 8, 16, 32)[:, :, 0:15, 0:31]


def _rope_table(n_lat, n_ctx):
    t = jnp.arange(n_lat)
    inv = ROPE_BASE ** (-jnp.arange(0, 32, 2, dtype=F32) / 32)
    ar = (t // GRID_W).astype(F32)[:, None] * inv
    ac = (t % GRID_W).astype(F32)[:, None] * inv
    cos = jnp.concatenate([jnp.cos(ar), jnp.cos(ar), jnp.cos(ac), jnp.cos(ac)], axis=-1)
    sin = jnp.concatenate([jnp.sin(ar), jnp.sin(ar), jnp.sin(ac), jnp.sin(ac)], axis=-1)
    tab = jnp.concatenate([cos, sin], axis=-1)
    ident = jnp.concatenate([jnp.ones((n_ctx, 64), F32), jnp.zeros((n_ctx, 64), F32)], axis=-1)
    return jnp.concatenate([tab, ident], axis=0)


def _small_weights(full, mods):
    l = full["g_mix"].shape[0]
    ff = full["ffn_conv_b"].shape[1] // 2
    return dict(
        mods=mods, g_mix=full["g_mix"][:, None, :], g_ffn=full["g_ffn"][:, None, :],
        sp=_sp_pack(full), pool_w=full["pool_w"], pool_sc=full["pool_scale"].reshape(l, 4, 1, 128),
        t1=_rpb_expand(full["na_rpb"]),
        conv_w=full["ffn_conv_w"].reshape(l, 3, 2, ff).transpose(0, 2, 1, 3),
        conv_b=full["ffn_conv_b"].reshape(l, 2, 1, ff))


def _kernel_weights(full, mods):
    w = _small_weights(full, mods)
    w.update(w_in=_w_in_layout(full["w_in"]).astype(BF16), w_out=full["w_out"].astype(BF16),
             w_up=full["ffn_w_up"].astype(BF16), w_down=full["ffn_w_down"].astype(BF16),
             w_qb=_w_qb_layout(full["mla_w_qb"]).astype(BF16), w_kvb=full["mla_w_kvb"].astype(BF16))
    return w


def _reference_grads(grads, big=True):
    st = lambda k: jnp.stack([g[k] for g in grads], axis=0)
    l = len(grads)
    out = dict(
        g_mix=st("g_mix")[:, 0], g_ffn=st("g_ffn")[:, 0],
        pool_w=st("pool_w"), pool_scale=st("pool_sc").reshape(l, 512), na_rpb=_rpb_fold(st("t1")),
        ffn_conv_w=st("conv_w").transpose(0, 2, 1, 3).reshape(l, 3, -1), ffn_conv_b=st("conv_b").reshape(l, -1),
        mods=st("mods"))
    if big:
        out.update(w_in=_w_in_unlayout(st("w_in")), w_out=st("w_out"), ffn_w_up=st("w_up"), ffn_w_down=st("w_down"),
                   mla_w_qb=_w_qb_unlayout(st("w_qb")), mla_w_kvb=st("w_kvb"))
    out.update(_sp_unpack(st("sp")))
    return out


ANY = pl.BlockSpec(memory_space=pl.ANY)


def _flip(x, y, j):
    return (1 - x if j >> 1 else x), (1 - y if j & 1 else y)


def _comm_call(name, ins, out_shapes, n_copies, plan, aliases=None):
    n_in, n_out = len(ins), len(out_shapes)

    def body(*refs):
        in_refs, out_refs = refs[:n_in], refs[n_in:n_in + n_out]
        ssem, rsem = refs[n_in + n_out:]
        pos = (lax.axis_index("x"), lax.axis_index("y"), lax.axis_index("c"))
        copies = plan(in_refs, out_refs, pos)
        assert len(copies) == n_copies
        descs = []
        for i, (src, dst, peer) in enumerate(copies):
            if peer is None:
                d = pltpu.make_async_copy(src, dst, ssem.at[i])
            else:
                d = pltpu.make_async_remote_copy(src_ref=src, dst_ref=dst, send_sem=ssem.at[i], recv_sem=rsem.at[i],
                                                 device_id=peer, device_id_type=MESH)
            d.start()
            descs.append(d)
        for d in descs:
            d.wait()

    return pl.pallas_call(
        body, name=name, in_specs=[ANY] * n_in, out_specs=[ANY] * n_out, out_shape=list(out_shapes),
        input_output_aliases=aliases or {},
        scratch_shapes=[pltpu.SemaphoreType.DMA((n_copies,)), pltpu.SemaphoreType.DMA((n_copies,))])(*ins)


def _sib_fill(bufs, part, name):
    def plan(ins, outs, pos):
        x, y, c = pos
        return [(o_ref.at[part(c)], o_ref.at[part(c)], (x, y, 1 - c)) for o_ref in outs]

    shapes = [jax.ShapeDtypeStruct(b.shape, b.dtype) for b in bufs]
    return _comm_call(name, bufs, shapes, len(bufs), plan, aliases={i: i for i in range(len(bufs))})


HBM = pl.BlockSpec(memory_space=pltpu.HBM)
SEM = pl.BlockSpec(memory_space=pltpu.SEMAPHORE)
DATAFLOW = pltpu.SideEffectType.DATAFLOW_SIDE_EFFECTING


def _remote_start(name, bufs, n_copies, plan, after):
    nb = len(bufs)

    def body(*refs):
        ssem, rsem, token = refs[nb + 1], refs[nb + 2], refs[-1]
        pos = (lax.axis_index("x"), lax.axis_index("y"), lax.axis_index("c"))
        copies = plan(refs[:nb], pos)
        assert len(copies) == n_copies
        for i, (src, dst, peer) in enumerate(copies):
            pltpu.make_async_remote_copy(src_ref=src, dst_ref=dst, send_sem=ssem.at[i], recv_sem=rsem.at[i],
                                         device_id=peer, device_id_type=MESH).start()
        token[...] = jnp.zeros_like(token)

    outs = pl.pallas_call(
        body, name=name,
        out_shape=(pltpu.SemaphoreType.DMA((n_copies,)), pltpu.SemaphoreType.DMA((n_copies,)),
                   *[pltpu.HBM(b.shape, b.dtype) for b in bufs], jax.ShapeDtypeStruct((8, 128), F32)),
        in_specs=[HBM] * nb + [ANY], out_specs=(SEM, SEM, *[HBM] * nb, pl.BlockSpec(memory_space=pltpu.VMEM)),
        input_output_aliases={i: 2 + i for i in range(nb)},
        compiler_params=pltpu.CompilerParams(has_side_effects=DATAFLOW),
    )(*[pltpu.with_memory_space_constraint(b, pltpu.HBM) for b in bufs], after)
    return outs[0], outs[1], list(outs[2:2 + nb]), outs[-1]


def _remote_wait(name, ssem, rsem, bufs, n_copies, plan, after):
    nb = len(bufs)

    def body(*refs):
        ssem_ref, rsem_ref = refs[nb], refs[nb + 1]
        pos = (lax.axis_index("x"), lax.axis_index("y"), lax.axis_index("c"))
        copies = plan(refs[:nb], pos)
        assert len(copies) == n_copies
        for i, (src, dst, peer) in enumerate(copies):
            cp = pltpu.make_async_remote_copy(src_ref=src, dst_ref=dst, send_sem=ssem_ref.at[i], recv_sem=rsem_ref.at[i],
                                              device_id=peer, device_id_type=MESH)
            cp.wait_send()
            cp.wait_recv()

    outs = pl.pallas_call(
        body, name=name, out_shape=tuple(pltpu.HBM(b.shape, b.dtype) for b in bufs),
        in_specs=[HBM] * nb + [SEM, SEM, ANY], out_specs=tuple([HBM] * nb), input_output_aliases={i: i for i in range(nb)},
        compiler_params=pltpu.CompilerParams(has_side_effects=DATAFLOW),
    )(*bufs, ssem, rsem, after)
    return list(outs)


BIG_GATHER = ("axis1", "axis1", "axis1", "axis1", "lane", "lane")


def _place_own(shard, kind, xyvec, name):
    rows, cols = shard.shape
    tr = _row_tile(rows, cols)
    if kind == "axis1":
        shape = (4, rows, cols)
        o_spec = pl.BlockSpec((None, tr, cols), lambda i, x_ref, y_ref: (2 * x_ref[0] + y_ref[0], i, 0))
    else:
        shape = (rows, 4 * cols)
        o_spec = pl.BlockSpec((tr, cols), lambda i, x_ref, y_ref: (i, 2 * x_ref[0] + y_ref[0]))

    def body(x_ref, y_ref, s_ref, o_ref):
        o_ref[...] = s_ref[...]

    return pl.pallas_call(
        body, name=name,
        grid_spec=pltpu.PrefetchScalarGridSpec(num_scalar_prefetch=2, grid=(rows // tr,),
                                               in_specs=[pl.BlockSpec((tr, cols), lambda i, x_ref, y_ref: (i, 0))], out_specs=o_spec),
        out_shape=jax.ShapeDtypeStruct(shape, shard.dtype), compiler_params=_params(("parallel",)))(*xyvec, shard)


def _w_gather_plan(shapes):
    def plan(refs, pos):
        x, y, c = pos
        k = 2 * x + y
        cps = []
        for s_ref, l_ref, kind, shp in zip(refs[:6], refs[6:], BIG_GATHER, shapes):
            h, w = shp[0] // 2, shp[1]
            rows = pl.ds(pl.multiple_of(c * h, 16), h)
            dst = l_ref.at[k, rows, :] if kind == "axis1" else l_ref.at[rows, pl.ds(pl.multiple_of(k * w, 128), w)]
            for j in (1, 2, 3):
                tx, ty = _flip(x, y, j)
                cps.append((s_ref.at[rows, :], dst, (tx, ty, c)))
        return cps

    return plan


def _w_fill(lands, shapes, name):
    def plan(ins, outs, pos):
        x, y, c = pos
        cps = []
        for o_ref, kind, shp in zip(outs, BIG_GATHER, shapes):
            h = shp[0] // 2
            rows = pl.ds(pl.multiple_of(c * h, 16), h)
            part = o_ref.at[:, rows, :] if kind == "axis1" else o_ref.at[rows, :]
            cps.append((part, part, (x, y, 1 - c)))
        return cps

    return _comm_call(name, lands, [jax.ShapeDtypeStruct(b.shape, b.dtype) for b in lands], len(lands), plan,
                      aliases={i: i for i in range(len(lands))})


def _g_scatter_plan(refs, pos):
    x, y, c = pos
    cps = []
    for s_ref, l_ref in zip(refs[:6], refs[6:]):
        for j in (1, 2, 3):
            tx, ty = _flip(x, y, j)
            cps.append((s_ref.at[j], l_ref.at[j - 1], (tx, ty, c)))
    return cps


def _pair_up(xs, name):
    def plan(ins, outs, pos):
        x, y, c = pos
        cps = []
        for i_ref, o_ref in zip(ins, outs):
            cps.append((i_ref, o_ref.at[c], None))
            cps.append((i_ref, o_ref.at[c], (x, y, 1 - c)))
        return cps

    return _comm_call(name, xs, [jax.ShapeDtypeStruct((2,) + a.shape, a.dtype) for a in xs], 2 * len(xs), plan)


def _chip_gather(xs, kinds, name):
    def dst(o_ref, kind, k, x_shape):
        if kind == "lead":
            return o_ref.at[k]
        w = x_shape[-1]
        return o_ref.at[(slice(None),) * (len(x_shape) - 1) + (pl.ds(pl.multiple_of(k * w, 128), w),)]

    def plan(ins, outs, pos):
        x, y, c = pos
        k = 2 * x + y
        cps = []
        for i_ref, o_ref, kind, a in zip(ins, outs, kinds, xs):
            cps.append((i_ref, dst(o_ref, kind, k, a.shape), None))
            for j in (1, 2, 3):
                tx, ty = _flip(x, y, j)
                cps.append((i_ref, dst(o_ref, kind, k, a.shape), (tx, ty, c)))
        return cps

    def oshape(a, kind):
        return (4,) + a.shape if kind == "lead" else a.shape[:-1] + (4 * a.shape[-1],)

    return _comm_call(name, xs, [jax.ShapeDtypeStruct(oshape(a, kd), a.dtype) for a, kd in zip(xs, kinds)], 4 * len(xs), plan)


def _half_swap(xs, kinds, name):
    def plan(ins, outs, pos):
        x, y, c = pos
        cps = []
        for i_ref, o_ref, kind, a in zip(ins, outs, kinds, xs):
            h = a.shape[-2] // 2
            rows = pl.ds(pl.multiple_of((1 - c) * h, 8), h)
            src = i_ref.at[:, rows, :] if kind == "cm" else i_ref.at[rows, :]
            cps.append((src, o_ref, (x, y, 1 - c)))
        return cps

    shapes = [jax.ShapeDtypeStruct(a.shape[:-2] + (a.shape[-2] // 2, a.shape[-1]), a.dtype) for a in xs]
    return _comm_call(name, xs, shapes, len(xs), plan)


def _row_tile(rows, cols, budget=1 << 20):
    for t in (2048, 1024, 512, 256, 128, 64, 32, 16, 8):
        if rows % t == 0 and t * cols * 4 <= budget:
            return t
    return rows


def _half_add(g, recv, kind, pvec, name):
    def chip(j, x_ref, y_ref):
        jx, jy = j // 2, j % 2
        return 2 * (x_ref[0] + jx - 2 * x_ref[0] * jx) + (y_ref[0] + jy - 2 * y_ref[0] * jy)

    if kind == "cm":
        _, h2, w = g.shape
        h = h2 // 2
        tr = _row_tile(h, w)
        nb = h // tr
        g_spec = pl.BlockSpec((None, tr, w), lambda j, i, c_ref, x_ref, y_ref: (chip(j, x_ref, y_ref), i + c_ref[0] * nb, 0))
        r_spec = pl.BlockSpec((None, tr, w), lambda j, i, c_ref, x_ref, y_ref: (chip(j, x_ref, y_ref), i, 0))
    else:
        h2, w4 = g.shape
        h, w = h2 // 2, w4 // 4
        tr = _row_tile(h, w)
        nb = h // tr
        g_spec = pl.BlockSpec((tr, w), lambda j, i, c_ref, x_ref, y_ref: (i + c_ref[0] * nb, chip(j, x_ref, y_ref)))
        r_spec = pl.BlockSpec((tr, w), lambda j, i, c_ref, x_ref, y_ref: (i, chip(j, x_ref, y_ref)))

    def body(c_ref, x_ref, y_ref, g_ref, r_ref, o_ref):
        o_ref[...] = (g_ref[...] + r_ref[...]).astype(BF16)

    return pl.pallas_call(
        body, name=name,
        grid_spec=pltpu.PrefetchScalarGridSpec(
            num_scalar_prefetch=3, grid=(4, nb), in_specs=[g_spec, r_spec],
            out_specs=pl.BlockSpec((None, tr, w), lambda j, i, c_ref, x_ref, y_ref: (j, i, 0))),
        out_shape=jax.ShapeDtypeStruct((4, h, w), BF16), compiler_params=_params(("parallel", "parallel")))(*pvec, g, recv)


def _sum_lead(x, name):
    n, rows, w = x.shape
    tr = _row_tile(rows, w, (1 << 21) // n)

    def body(x_ref, o_ref):
        acc = x_ref[0].astype(F32)
        for j in range(1, n):
            acc = acc + x_ref[j].astype(F32)
        o_ref[...] = acc

    return pl.pallas_call(
        body, name=name, grid=(rows // tr,), in_specs=[pl.BlockSpec((n, tr, w), lambda i: (0, i, 0))],
        out_specs=pl.BlockSpec((tr, w), lambda i: (i, 0)), out_shape=jax.ShapeDtypeStruct((rows, w), F32),
        compiler_params=_params(("parallel",)))(x)


def _sum_into(own, landed, buf, layer, cvec, layers, name):
    n, rows, w = landed.shape
    tr = _row_tile(rows, w, 1 << 19)

    def body(c_ref, own_ref, x_ref, *refs):
        o_ref = refs[-1]
        acc = own_ref[...].astype(F32)
        for j in range(n):
            acc = acc + x_ref[j].astype(F32)
        o_ref[...] = acc

    in_specs = [pl.BlockSpec((None, tr, w), lambda i, c_ref: (0, i, 0)), pl.BlockSpec((n, tr, w), lambda i, c_ref: (0, i, 0))]
    args = [cvec, own, landed]
    if buf is not None:
        in_specs.append(ANY)
        args.append(buf)
    return pl.pallas_call(
        body, name=name,
        grid_spec=pltpu.PrefetchScalarGridSpec(
            num_scalar_prefetch=1, grid=(rows // tr,), in_specs=in_specs,
            out_specs=pl.BlockSpec((None, None, tr, w), lambda i, c_ref: (layer, c_ref[0], i, 0))),
        out_shape=jax.ShapeDtypeStruct((layers, 2, rows, w), F32), input_output_aliases={} if buf is None else {3: 0},
        compiler_params=_params(("arbitrary",)))(*args)


def _adamw(w, g, m, v, name):
    shape = w.shape
    cols = shape[-1]
    rows = w.size // cols
    tr = _row_tile(rows, cols, 1 << 19)

    def body(w_ref, g_ref, m_ref, v_ref, d_ref, mo_ref, vo_ref):
        gv = g_ref[...]
        mn = ADAM_B1 * m_ref[...] + (1.0 - ADAM_B1) * gv
        vn = ADAM_B2 * v_ref[...] + (1.0 - ADAM_B2) * jnp.square(gv)
        m_hat = mn / (1.0 - ADAM_B1 ** ADAM_STEP)
        v_hat = vn / (1.0 - ADAM_B2 ** ADAM_STEP)
        d_ref[...] = -ADAM_LR * (m_hat / (jnp.sqrt(v_hat) + ADAM_EPS) + ADAM_WD * w_ref[...])
        mo_ref[...] = mn
        vo_ref[...] = vn

    spec = pl.BlockSpec((tr, cols), lambda i: (i, 0))
    outs = pl.pallas_call(
        body, name=name, grid=(rows // tr,), in_specs=[spec] * 4, out_specs=[spec] * 3,
        out_shape=[jax.ShapeDtypeStruct((rows, cols), F32)] * 3,
        compiler_params=_params(("parallel",)))(*[a.reshape(rows, cols) for a in (w, g, m, v)])
    return [o.reshape(shape) for o in outs]


def _silu_grad(x):
    s = jax.nn.sigmoid(x)
    return s * (1.0 + x * (1.0 - s))


def _mod_fwd(cs16, w_mod, b_sh, name):
    l, d, wc = w_mod.shape
    tn = _pick(wc, (512, 384, 256, 128))

    def body(c_ref, w_ref, b_ref, o_ref):
        a = jax.nn.silu(c_ref[...]).astype(BF16)
        o_ref[...] = jnp.dot(a, w_ref[...].astype(BF16), preferred_element_type=F32) + b_ref[...]

    return pl.pallas_call(
        body, name=name, grid=(l, wc // tn),
        in_specs=[_full_spec((16, d)), pl.BlockSpec((None, d, tn), lambda i, j: (i, 0, j)), pl.BlockSpec((None, 1, tn), lambda i, j: (i, 0, j))],
        out_specs=pl.BlockSpec((None, 16, tn), lambda i, j: (i, 0, j)), out_shape=jax.ShapeDtypeStruct((l, 16, wc), F32),
        compiler_params=_params(("parallel", "parallel")))(cs16, w_mod, b_sh)


def _mod_dw(cs16, dm_sh, name):
    l, _, wc = dm_sh.shape
    d = cs16.shape[1]
    tr = _pick(d, (512, 256, 128))
    tc = _pick(wc, (512, 384, 256, 128))

    def body(c_ref, dm_ref, o_ref):
        a = jax.nn.silu(c_ref[...]).astype(BF16)
        o_ref[...] = lax.dot_general(a, dm_ref[...].astype(BF16), (((0,), (0,)), ((), ())), preferred_element_type=F32)

    return pl.pallas_call(
        body, name=name, grid=(l, d // tr, wc // tc),
        in_specs=[pl.BlockSpec((16, tr), lambda i, r, j: (0, r)), pl.BlockSpec((None, 16, tc), lambda i, r, j: (i, 0, j))],
        out_specs=pl.BlockSpec((None, tr, tc), lambda i, r, j: (i, r, j)), out_shape=jax.ShapeDtypeStruct((l, d, wc), F32),
        compiler_params=_params(("parallel", "parallel", "parallel")))(cs16, dm_sh)


def _mod_dc(dm_sh, w_mod, c_ctx, name):
    l, d, wc = w_mod.shape
    tk = _pick(wc, (512, 384, 256, 128))
    nk = wc // tk

    def body(dm_ref, w_ref, c_ref, o_ref, acc_ref):
        i, j = pl.program_id(0), pl.program_id(1)

        @pl.when(jnp.logical_and(i == 0, j == 0))
        def _():
            acc_ref[...] = jnp.zeros_like(acc_ref)

        acc_ref[...] += lax.dot_general(dm_ref[...].astype(BF16), w_ref[...].astype(BF16), (((1,), (1,)), ((), ())),
                                        preferred_element_type=F32)

        @pl.when(jnp.logical_and(i == l - 1, j == nk - 1))
        def _():
            mine = jnp.where(lax.axis_index("c") == 0, 1.0, 0.0)
            o_ref[...] = acc_ref[8:9, :] * _silu_grad(c_ref[...]) * mine

    return pl.pallas_call(
        body, name=name, grid=(l, nk),
        in_specs=[pl.BlockSpec((None, 16, tk), lambda i, j: (i, 0, j)), pl.BlockSpec((None, d, tk), lambda i, j: (i, 0, j)), _full_spec((1, d))],
        out_specs=_full_spec((1, d)), out_shape=jax.ShapeDtypeStruct((1, d), F32), scratch_shapes=[pltpu.VMEM((16, d), F32)],
        compiler_params=_params(("arbitrary", "arbitrary")))(dm_sh, w_mod, c_ctx)


def _dmod_assemble(gath, name):
    _, l, _, w = gath.shape
    gath = gath.transpose(1, 2, 0, 3)
    tc = _pick(w, (2048, 1024, 512, 256, 128))

    def body(lat_ref, ctx_ref, o_ref, b_ref):
        ctx = ctx_ref[0:1, :]
        for dev in range(1, 8):
            ctx = ctx + ctx_ref[dev:dev + 1, :]
        lat = lat_ref[...]
        o_ref[0:8, :] = lat
        o_ref[8:9, :] = ctx
        o_ref[9:16, :] = jnp.zeros((7, tc), F32)
        b_ref[...] = jnp.sum(lat, axis=0, keepdims=True) + ctx

    return pl.pallas_call(
        body, name=name, grid=(l, w // tc),
        in_specs=[pl.BlockSpec((None, None, 8, tc), lambda i, j: (i, 0, 0, j)), pl.BlockSpec((None, None, 8, tc), lambda i, j: (i, 1, 0, j))],
        out_specs=[pl.BlockSpec((None, 16, tc), lambda i, j: (i, 0, j)), pl.BlockSpec((None, 1, tc), lambda i, j: (i, 0, j))],
        out_shape=[jax.ShapeDtypeStruct((l, 16, w), F32), jax.ShapeDtypeStruct((l, 1, w), F32)],
        compiler_params=_params(("parallel", "parallel")))(gath, gath)


SMALL = ("c_ctx", "g_mix", "g_ffn", "mla_q_a_norm", "mla_kv_a_norm", "mla_q_nope_norm", "mla_q_rope_norm", "mla_k_nope_norm",
         "mla_k_rope_norm", "pool_w", "pool_scale", "swa_q_norm", "swa_k_norm", "swa_sink", "na_q_norm", "na_k_norm", "na_rpb",
         "ffn_conv_b")
PACK_W = 512
PACK_Q = 8 * PACK_W


def _pack(arrs):
    flat = []
    for a in arrs:
        f = a.reshape(-1)
        flat.append(jnp.pad(f, (0, (-f.size) % PACK_Q)))
    return jnp.concatenate(flat).reshape(-1, PACK_W)


def _unpack(packed, shapes):
    flat, out, o = packed.reshape(-1), [], 0
    for s in shapes:
        n = 1
        for dim in s:
            n *= dim
        out.append(flat[o:o + n].reshape(s))
        o += n + (-n) % PACK_Q
    return out


def _all_sum(p, name):
    pair = _pair_up([p], name + "_pair")[0]
    chip = _sum_lead(pair, name + "_sum2")
    return _sum_lead(_chip_gather([chip], ["lead"], name + "_gather")[0], name + "_sum4")


WEIGHTS = ("c_ctx", "w_mod", "b_mod", "g_mix", "g_ffn", "w_in", "w_out", "mla_q_a_norm", "mla_w_qb", "mla_kv_a_norm", "mla_w_kvb",
           "mla_q_nope_norm", "mla_q_rope_norm", "mla_k_nope_norm", "mla_k_rope_norm", "pool_w", "pool_scale", "swa_q_norm",
           "swa_k_norm", "swa_sink", "na_q_norm", "na_k_norm", "na_rpb", "ffn_w_up", "ffn_conv_w", "ffn_conv_b", "ffn_w_down")
BIG = ("w_in", "mla_w_qb", "w_out", "ffn_w_down", "ffn_w_up", "mla_w_kvb")
BIG_KINDS = ("cm", "cm", "cm", "cm", "lb", "lb")


def _step(a):
    x, c, ctx = a["x"], a["c"], a["ctx"]
    n_lat, d = x.shape[1], x.shape[2]
    n_ctx = ctx.shape[1]
    l = DEPTH
    px, py, pc = lax.axis_index("x"), lax.axis_index("y"), lax.axis_index("c")
    chip = 2 * px + py
    pvec = [p.reshape(1).astype(I32) for p in (pc, px, py)]
    cvec = pvec[0]

    bf = {k: a[k].astype(BF16) for k in BIG}
    w = {key: [None] * l for key in ("w_in", "w_qb", "w_out", "w_down", "w_up", "w_kvb")}

    def gather_start(li, after):
        shards = [bf[k][li] for k in BIG]
        shapes = [s.shape for s in shards]
        lands = [_place_own(s, kd, pvec[1:], "w_place") for s, kd in zip(shards, BIG_GATHER)]
        return _remote_start(f"w_start_{li}", shards + lands, 18, _w_gather_plan(shapes), after) + (shapes,)

    def gather_finish(li, started, after):
        ssem, rsem, bufs, _, shapes = started
        bufs = _remote_wait(f"w_wait_{li}", ssem, rsem, bufs, 18, _w_gather_plan(shapes), after)
        w_in_g, w_qb_g, w_out_g, w_down_g, w_up_g, w_kvb_g = _w_fill(bufs[6:], shapes, "w_fill")
        w["w_in"][li] = _w_in_layout(w_in_g.transpose(1, 0, 2).reshape(d, P_COLS))
        w["w_qb"][li] = _w_qb_layout(w_qb_g.transpose(1, 0, 2).reshape(512, 768))
        w["w_out"][li], w["w_down"][li] = w_out_g.reshape(-1, d), w_down_g.reshape(-1, d)
        w["w_up"][li], w["w_kvb"][li] = w_up_g, w_kvb_g

    started = gather_start(0, c)

    c_all = _chip_gather(_pair_up([c], "c_pair"), ["lead"], "c_gather")[0].reshape(8, d)
    cs16 = jnp.concatenate([c_all, a["c_ctx"][None, :], jnp.zeros((7, d), F32)], axis=0)
    wc = a["w_mod"].shape[-1]
    b_sh = lax.dynamic_slice_in_dim(a["b_mod"], chip * wc, wc, axis=1)[:, None, :]
    mod_sh = _mod_fwd(cs16, a["w_mod"], b_sh, "mod_fwd")
    mod_all, conv_w_full = _chip_gather([mod_sh, a["ffn_conv_w"]], ["lead", "lane"], "mod_gather")
    mod_all = mod_all.transpose(1, 2, 0, 3).reshape(l, 16, 4 * wc)
    mods = jnp.stack([lax.dynamic_index_in_dim(mod_all, 2 * chip + pc, axis=1, keepdims=False), mod_all[:, 8]], axis=1)
    mods = mods.reshape(l, 2, 6, d)

    full = {k: a[k] for k in SMALL if k != "c_ctx"}
    full["ffn_conv_w"] = conv_w_full
    w.update(_small_weights(full, mods))
    w["mods"] = [w["mods"][li] for li in range(l)]
    w["g_mix"] = [w["g_mix"][li] for li in range(l)]
    tab = _rope_table(n_lat, n_ctx)
    gather_finish(0, started, mods)
    xs = jnp.concatenate([x[0], ctx[0]], axis=0)
    saved = []
    for li in range(l):
        if li + 1 < l:
            started = gather_start(li + 1, w["w_kvb"][li])
            w["g_mix"][li] = w["g_mix"][li] + started[3][0:1, 0:1]
        xs, s = _layer_fwd(xs, w, li, tab, n_lat)
        saved.append(s)
        if li + 1 < l:
            gather_finish(li + 1, started, xs)
    loss, dx = _loss_kernel(xs, a["loss_target"][0], n_lat // TM, "loss")
    loss = lax.psum(loss[0, 0], ("x", "y", "c"))

    grads = [None] * l
    g_big = [None] * len(BIG)

    def scatter_finish(li, started, after):
        ssem, rsem, bufs, _ = started
        bufs = _remote_wait(f"g_wait_{li}", ssem, rsem, bufs, 18, _g_scatter_plan, after)
        return [_sum_into(own, landed, buf, li, cvec, l, "g_sum4") for own, landed, buf in zip(bufs[:6], bufs[6:], g_big)]

    pending = None
    for li in reversed(range(l)):
        dx, grads[li] = _layer_bwd(dx, saved[li], w, li, tab, n_lat)
        if pending is not None:
            g_big = scatter_finish(li + 1, pending, dx)
        g = grads[li]
        ops = [_w_in_unlayout(g["w_in"]).reshape(d, 4, -1).transpose(1, 0, 2),
               _w_qb_unlayout(g["w_qb"]).reshape(512, 4, 192).transpose(1, 0, 2),
               g["w_out"].reshape(4, -1, d), g["w_down"].reshape(4, -1, d), g["w_up"], g["w_kvb"]]
        recv = _half_swap(ops, BIG_KINDS, "g_swap")
        halves = [_half_add(o, r, kd, pvec, "g_half_add") for o, r, kd in zip(ops, recv, BIG_KINDS)]
        lands = [jnp.zeros((3,) + h.shape[1:], BF16) for h in halves]
        pending = _remote_start(f"g_start_{li}", halves + lands, 18, _g_scatter_plan, halves[0])
        if li > 0:
            w["mods"][li - 1] = w["mods"][li - 1] + pending[3][0, 0]

    dmods = jnp.stack([grads[li]["mods"] for li in range(l)], axis=0).reshape(l, 2, 6 * d)
    dm_gath = _chip_gather(_pair_up([dmods], "dmod_pair"), ["lead"], "dmod_gather")[0].reshape(8, l, 2, 6 * d)
    dmod_all, g_b_mod = _dmod_assemble(dm_gath, "dmod_assemble")
    dm_sh = lax.dynamic_slice_in_dim(dmod_all, chip * wc, wc, axis=2)
    g_w_mod = _mod_dw(cs16, dm_sh, "mod_dw")
    g_c_ctx = _mod_dc(dm_sh, a["w_mod"], a["c_ctx"][None, :], "mod_dc")
    g_out = {"w_mod": g_w_mod, "b_mod": g_b_mod.reshape(l, 6 * d)}

    rg = _reference_grads(grads, big=False)
    rg["c_ctx"] = g_c_ctx[0]
    packed = _all_sum(_pack([rg[k] for k in SMALL] + [rg["ffn_conv_w"]]), "small")
    small_g = _unpack(packed, [a[k].shape for k in SMALL] + [rg["ffn_conv_w"].shape])
    for k, g in zip(SMALL, small_g[:-1]):
        g_out[k] = g
    cw = a["ffn_conv_w"].shape[-1]
    g_out["ffn_conv_w"] = lax.dynamic_slice_in_dim(small_g[-1], chip * cw, cw, axis=2)

    upd = {}
    pk = lambda pre: _pack([a[pre + k] for k in SMALL])
    outs = _adamw(pk(""), _pack([g_out[k] for k in SMALL]), pk("m_"), pk("v_"), "adamw_small")
    for o, kind in zip(outs, ("delta", "m", "v")):
        for k, val in zip(SMALL, _unpack(o, [a[k].shape for k in SMALL])):
            upd[kind, k] = val
    def adamw_each(keys):
        for k in keys:
            outs = _adamw(a[k], g_out[k], a["m_" + k], a["v_" + k], "adamw_" + k)
            for o, kind in zip(outs, ("delta", "m", "v")):
                upd[kind, k] = o

    adamw_each(("w_mod", "b_mod", "ffn_conv_w"))
    g_big = _sib_fill(scatter_finish(0, pending, upd["delta", "w_mod"]), lambda cc: (slice(None), cc), "g_pair")
    g_out.update({k: g.reshape(a[k].shape) for k, g in zip(BIG, g_big)})
    adamw_each(BIG)
    grad_x = dx[0:n_lat].reshape(x.shape)
    return (loss, grad_x, *[g_out[k] for k in WEIGHTS], *[upd["delta", k] for k in WEIGHTS],
            *[upd["m", k] for k in WEIGHTS], *[upd["v", k] for k in WEIGHTS])


def kernel(x, c, ctx, c_ctx, w_mod, b_mod, g_mix, g_ffn, w_in, w_out, mla_q_a_norm, mla_w_qb, mla_kv_a_norm, mla_w_kvb, mla_q_nope_norm, mla_q_rope_norm, mla_k_nope_norm, mla_k_rope_norm, pool_w, pool_scale, swa_q_norm, swa_k_norm, swa_sink, na_q_norm, na_k_norm, na_rpb, ffn_w_up, ffn_conv_w, ffn_conv_b, ffn_w_down, loss_target, m_c_ctx, m_w_mod, m_b_mod, m_g_mix, m_g_ffn, m_w_in, m_w_out, m_mla_q_a_norm, m_mla_w_qb, m_mla_kv_a_norm, m_mla_w_kvb, m_mla_q_nope_norm, m_mla_q_rope_norm, m_mla_k_nope_norm, m_mla_k_rope_norm, m_pool_w, m_pool_scale, m_swa_q_norm, m_swa_k_norm, m_swa_sink, m_na_q_norm, m_na_k_norm, m_na_rpb, m_ffn_w_up, m_ffn_conv_w, m_ffn_conv_b, m_ffn_w_down, v_c_ctx, v_w_mod, v_b_mod, v_g_mix, v_g_ffn, v_w_in, v_w_out, v_mla_q_a_norm, v_mla_w_qb, v_mla_kv_a_norm, v_mla_w_kvb, v_mla_q_nope_norm, v_mla_q_rope_norm, v_mla_k_nope_norm, v_mla_k_rope_norm, v_pool_w, v_pool_scale, v_swa_q_norm, v_swa_k_norm, v_swa_sink, v_na_q_norm, v_na_k_norm, v_na_rpb, v_ffn_w_up, v_ffn_conv_w, v_ffn_conv_b, v_ffn_w_down):
    return _step(dict(locals()))
```

```python
import functools

import jax
import jax.numpy as jnp
from jax import lax
from jax.experimental import pallas as pl
from jax.experimental.pallas import tpu as pltpu

F32 = jnp.float32
BF16 = jnp.bfloat16
I32 = jnp.int32

DEPTH = 4
GRID_W = 64
ROPE_BASE = 10000.0
EPS = 1e-6
NEG = -1e30
MLA_SCALE = 192.0 ** -0.5
HD_SCALE = 64.0 ** -0.5
NA_KROWS = 12
SWA_KEYS = 512
P_COLS = 3648
PW = 3840
TM = 256
HALO = 8
ADAM_LR, ADAM_B1, ADAM_B2, ADAM_EPS, ADAM_WD, ADAM_STEP = 0.001, 0.9, 0.999, 1e-08, 0.01, 10
VMEM_LIMIT = 56 * 1024 * 1024
GRAD_WIRE = BF16
MESH = pl.DeviceIdType.MESH


def _pick(n, cands):
    for c in cands:
        if n % c == 0:
            return c
    return n


def _params(sem=None):
    return pltpu.CompilerParams(dimension_semantics=sem, vmem_limit_bytes=VMEM_LIMIT)


@jax.custom_vjp
def _bdot(a, b):
    return jnp.dot(a.astype(BF16), b.astype(BF16), preferred_element_type=F32)


def _bdot_fwd(a, b):
    return _bdot(a, b), (a.astype(BF16), b.astype(BF16))


def _bdot_bwd(res, g):
    a, b = res
    gb = g.astype(BF16)
    da = lax.dot_general(gb, b, (((1,), (1,)), ((), ())), preferred_element_type=F32)
    db = lax.dot_general(a, gb, (((0,), (0,)), ((), ())), preferred_element_type=F32)
    return da, db


_bdot.defvjp(_bdot_fwd, _bdot_bwd)


@jax.custom_vjp
def _bdot_nt(a, b):
    return lax.dot_general(a.astype(BF16), b.astype(BF16), (((1,), (1,)), ((), ())), preferred_element_type=F32)


def _bdot_nt_fwd(a, b):
    return _bdot_nt(a, b), (a.astype(BF16), b.astype(BF16))


def _bdot_nt_bwd(res, g):
    a, b = res
    gb = g.astype(BF16)
    da = jnp.dot(gb, b, preferred_element_type=F32)
    db = lax.dot_general(gb, a, (((0,), (0,)), ((), ())), preferred_element_type=F32)
    return da, db


_bdot_nt.defvjp(_bdot_nt_fwd, _bdot_nt_bwd)


def _rms(x, g):
    return x * lax.rsqrt(jnp.mean(x * x, axis=-1, keepdims=True) + EPS) * g


def _rope(x, cos, sin):
    xr = jnp.concatenate([-x[:, 16:32], x[:, 0:16], -x[:, 48:64], x[:, 32:48]], axis=-1)
    return x * cos + xr * sin


def _sel(is_ctx, mod, row):
    return jnp.where(is_ctx, mod[1, row:row + 1, :], mod[0, row:row + 1, :])


MM_VMEM_BUDGET = 40 * 1024 * 1024
HBM_BYTES_PER_STEP = 1 << 20


def _mm_tiles(m, n, k, mode, osize, n_unit, k_unit):
    lanes = (3840, 2816, 2048, 1280, 1024, 768, 512, 256)
    subl = (4352, 2176, 1088, 1024, 640, 544, 512, 256)
    tms = [c for c in (lanes if mode == "tn" else subl) if m % c == 0] or [m]
    tns = [c for c in lanes if n_unit % c == 0] or [n_unit]
    tks = [c for c in (subl if mode == "tn" else lanes) if k_unit % c == 0]
    if k_unit == k:
        tks = [k] + tks
    best = None
    for tm in tms:
        for tn in tns:
            for tk in tks:
                nk = k // tk
                vmem = 4 * (tm * tk + tk * tn) + 2 * tm * tn * osize + tm * tn * 4
                if vmem > MM_VMEM_BUDGET:
                    continue
                a_reads = 1 if nk == 1 else n // tn
                b_reads = 1 if (nk == 1 and n == tn) else m // tm
                steps = (m // tm) * (n // tn) * nk
                cost = (2 * m * k * a_reads + 2 * k * n * b_reads + m * n * osize + steps * HBM_BYTES_PER_STEP
                        + (12 * m * n * nk if nk > 1 else 0))
                if best is None or cost < best[0]:
                    best = (cost, tm, tn, tk)
    return best[1:]


def _mm(a, b, mode, out_dtype, name, a_split=False, b_split=False, o_split=False):
    def dims(x, split):
        return (x.shape[1], 2 * x.shape[2]) if split else x.shape

    ar, ac = dims(a, a_split)
    br, bc = dims(b, b_split)
    if mode == "nn":
        m, k, n = ar, ac, bc
        assert br == k
    elif mode == "nt":
        m, k, n = ar, ac, br
        assert bc == k
    else:
        k, m, n = ar, ac, bc
        assert br == k
    n_unit = n // 2 if (o_split or (b_split and mode != "nt")) else n
    k_unit = k // 2 if (mode != "tn" and (a_split or (b_split and mode == "nt"))) else k
    tm, tn, tk = _mm_tiles(m, n, k, mode, jnp.dtype(out_dtype).itemsize, n_unit, k_unit)
    nk = k // tk

    def spec(split, tr, tc, ncols, ridx, cidx):
        if not split:
            return pl.BlockSpec((tr, tc), lambda i, j, kk: (ridx(i, j, kk), cidx(i, j, kk)))
        nh = (ncols // 2) // tc
        return pl.BlockSpec((None, tr, tc), lambda i, j, kk: (cidx(i, j, kk) // nh, ridx(i, j, kk), cidx(i, j, kk) % nh))

    gi = lambda i, j, kk: i
    gj = lambda i, j, kk: j
    gk = lambda i, j, kk: kk
    if mode == "tn":
        a_spec = spec(a_split, tk, tm, ac, gk, gi)
    else:
        a_spec = spec(a_split, tm, tk, ac, gi, gk)
    if mode == "nt":
        b_spec = spec(b_split, tn, tk, bc, gj, gk)
    else:
        b_spec = spec(b_split, tk, tn, bc, gk, gj)
    o_spec = spec(o_split, tm, tn, n, gi, gj)
    dn = {"nn": (((1,), (0,)), ((), ())), "nt": (((1,), (1,)), ((), ())), "tn": (((0,), (0,)), ((), ()))}[mode]

    def body(a_ref, b_ref, o_ref, acc_ref):
        kk = pl.program_id(2)

        @pl.when(kk == 0)
        def _():
            acc_ref[...] = jnp.zeros_like(acc_ref)

        acc_ref[...] += lax.dot_general(a_ref[...], b_ref[...], dn, preferred_element_type=F32)

        @pl.when(kk == nk - 1)
        def _():
            o_ref[...] = acc_ref[...].astype(o_ref.dtype)

    def body_whole_k(a_ref, b_ref, o_ref):
        o_ref[...] = lax.dot_general(a_ref[...], b_ref[...], dn, preferred_element_type=F32).astype(o_ref.dtype)

    oshape = (2, m, n // 2) if o_split else (m, n)
    return pl.pallas_call(
        body if nk > 1 else body_whole_k, name=name, grid=(m // tm, n // tn, nk), in_specs=[a_spec, b_spec], out_specs=o_spec,
        out_shape=jax.ShapeDtypeStruct(oshape, out_dtype), scratch_shapes=[pltpu.VMEM((tm, tn), F32)] if nk > 1 else [],
        compiler_params=_params(("parallel", "parallel", "arbitrary")))(a, b)


def _mm_exact(a, b, name):
    def body(a_ref, b_ref, o_ref):
        o_ref[...] = jnp.dot(a_ref[...], b_ref[...], preferred_element_type=F32, precision=lax.Precision.HIGHEST)

    return pl.pallas_call(body, name=name, out_shape=jax.ShapeDtypeStruct((a.shape[0], b.shape[1]), F32),
                          compiler_params=_params())(a, b)


def _row_spec(width, col=0):
    return pl.BlockSpec((TM, width), lambda i: (i, col))


def _full_spec(shape):
    nd = len(shape)
    return pl.BlockSpec(shape, lambda *_: (0,) * nd)


def _normmod_fn(x, g, mod, is_ctx, row):
    return _rms(x, g) * (1.0 + _sel(is_ctx, mod, row + 1)) + _sel(is_ctx, mod, row)


def _normmod_fwd(x, g, mod, row, nbl, name):
    r, d = x.shape

    def body(x_ref, g_ref, mod_ref, h_ref):
        is_ctx = pl.program_id(0) >= nbl
        h_ref[...] = _normmod_fn(x_ref[...], g_ref[...], mod_ref[...], is_ctx, row).astype(BF16)

    return pl.pallas_call(
        body, name=name, grid=(r // TM,), in_specs=[_row_spec(d), _full_spec((1, d)), _full_spec(mod.shape)],
        out_specs=_row_spec(d), out_shape=jax.ShapeDtypeStruct((r, d), BF16), compiler_params=_params(("parallel",)))(x, g, mod)


def _normmod_bwd(x, g, mod, dh, dx_in, row, nbl, name):
    r, d = x.shape

    def body(x_ref, g_ref, mod_ref, dh_ref, dxin_ref, dx_ref, dg_ref, dmod_ref):
        i = pl.program_id(0)
        is_ctx = i >= nbl

        @pl.when(i == 0)
        def _():
            dg_ref[...] = jnp.zeros_like(dg_ref)
            dmod_ref[...] = jnp.zeros_like(dmod_ref)

        _, vjp = jax.vjp(lambda xx, gg, mm: _normmod_fn(xx, gg, mm, is_ctx, row), x_ref[...], g_ref[...], mod_ref[...])
        dx, dg, dmod = vjp(dh_ref[...])
        dx_ref[...] = dxin_ref[...] + dx
        dg_ref[...] += dg
        dmod_ref[...] += dmod

    return pl.pallas_call(
        body, name=name, grid=(r // TM,),
        in_specs=[_row_spec(d), _full_spec((1, d)), _full_spec(mod.shape), _row_spec(d), _row_spec(d)],
        out_specs=[_row_spec(d), _full_spec((1, d)), _full_spec(mod.shape)],
        out_shape=[jax.ShapeDtypeStruct((r, d), F32), jax.ShapeDtypeStruct((1, d), F32), jax.ShapeDtypeStruct(mod.shape, F32)],
        compiler_params=_params(("arbitrary",)))(x, g, mod, dh, dx_in)


def _resid_fwd(x, y, mod, row, nbl, name):
    r, d = x.shape

    def body(x_ref, y_ref, mod_ref, o_ref):
        is_ctx = pl.program_id(0) >= nbl
        o_ref[...] = x_ref[...] + _sel(is_ctx, mod_ref[...], row) * y_ref[...]

    return pl.pallas_call(
        body, name=name, grid=(r // TM,), in_specs=[_row_spec(d), _row_spec(d), _full_spec(mod.shape)],
        out_specs=_row_spec(d), out_shape=jax.ShapeDtypeStruct((r, d), F32), compiler_params=_params(("parallel",)))(x, y, mod)


def _resid_bwd(dx, y, mod, row, nbl, name):
    r, d = dx.shape

    def body(dx_ref, y_ref, mod_ref, dy_ref, dmod_ref):
        i = pl.program_id(0)
        is_ctx = i >= nbl

        @pl.when(i == 0)
        def _():
            dmod_ref[...] = jnp.zeros_like(dmod_ref)

        dxv = dx_ref[...]
        dy_ref[...] = (_sel(is_ctx, mod_ref[...], row) * dxv).astype(BF16)
        dgate = jnp.sum(dxv * y_ref[...], axis=0, keepdims=True)

        @pl.when(is_ctx)
        def _():
            dmod_ref[1, row:row + 1, :] += dgate

        @pl.when(jnp.logical_not(is_ctx))
        def _():
            dmod_ref[0, row:row + 1, :] += dgate

    return pl.pallas_call(
        body, name=name, grid=(r // TM,), in_specs=[_row_spec(d), _row_spec(d), _full_spec(mod.shape)],
        out_specs=[_row_spec(d), _full_spec(mod.shape)],
        out_shape=[jax.ShapeDtypeStruct((r, d), BF16), jax.ShapeDtypeStruct(mod.shape, F32)],
        compiler_params=_params(("arbitrary",)))(dx, y, mod)


def _loss_kernel(x, target, nbl, name):
    r, d = x.shape

    def body(x_ref, t_ref, loss_ref, dx_ref):
        i = pl.program_id(0)

        @pl.when(i == 0)
        def _():
            loss_ref[...] = jnp.zeros_like(loss_ref)

        @pl.when(i < nbl)
        def _():
            e = x_ref[...] - t_ref[...]
            dx_ref[...] = e / d
            loss_ref[...] += 0.5 * jnp.sum(jnp.mean(e * e, axis=-1, keepdims=True), axis=0, keepdims=True)

        @pl.when(i >= nbl)
        def _():
            dx_ref[...] = jnp.zeros_like(dx_ref)

    return pl.pallas_call(
        body, name=name, grid=(r // TM,),
        in_specs=[_row_spec(d), pl.BlockSpec((TM, d), lambda i: (jnp.minimum(i, nbl - 1), 0))],
        out_specs=[_full_spec((1, 1)), _row_spec(d)],
        out_shape=[jax.ShapeDtypeStruct((1, 1), F32), jax.ShapeDtypeStruct((r, d), F32)],
        compiler_params=_params(("arbitrary",)))(x, target)


SP_QA, SP_KVA, SP_QN, SP_QR, SP_KN, SP_KR, SP_SQ, SP_SK, SP_NQ, SP_NK, SP_SINK = range(11)
C_CQ, C_CKV, C_KR, C_POOL, C_SQ, C_SK, C_SV, C_NQ, C_NK, C_NV = 0, 512, 768, 896, 1408, 1920, 2048, 2176, 2688, 3200


def _prep_fn(p, tab, sp, wqb, wkvb):
    cos, sin = tab[:, 0:64], tab[:, 64:128]
    q = _bdot(_rms(p[:, C_CQ:C_CQ + 512], sp[SP_QA:SP_QA + 1, 0:512]), wqb)
    kv = _bdot(_rms(p[:, C_CKV:C_CKV + 256], sp[SP_KVA:SP_KVA + 1, 0:256]), wkvb)
    krr = _rope(_rms(p[:, C_KR:C_KR + 64], sp[SP_KR:SP_KR + 1, 0:64]), cos, sin)
    zero = jnp.zeros_like(krr)
    aq, ak, av = [], [], []
    for h in range(4):
        qn = _rms(q[:, 128 * h:128 * h + 128], sp[SP_QN:SP_QN + 1, 0:128])
        qr = _rope(_rms(q[:, 512 + 64 * h:576 + 64 * h], sp[SP_QR:SP_QR + 1, 0:64]), cos, sin)
        kn = _rms(kv[:, 256 * h:256 * h + 128], sp[SP_KN:SP_KN + 1, 0:128])
        aq += [qn, qr, zero]
        ak += [kn, krr, zero]
        av.append(kv[:, 256 * h + 128:256 * h + 256])
    cq = [_rope(_rms(p[:, C_SQ + 64 * h:C_SQ + 64 * h + 64], sp[SP_SQ:SP_SQ + 1, 0:64]), cos, sin) for h in range(8)]
    ck = [_rope(_rms(p[:, C_SK + 64 * h:C_SK + 64 * h + 64], sp[SP_SK:SP_SK + 1, 0:64]), cos, sin) for h in range(2)]
    dq = [_rms(p[:, C_NQ + 64 * h:C_NQ + 64 * h + 64], sp[SP_NQ:SP_NQ + 1, 0:64]) for h in range(8)]
    dk = [_rms(p[:, C_NK + 64 * h:C_NK + 64 * h + 64], sp[SP_NK:SP_NK + 1, 0:64]) for h in range(8)]
    cat = lambda xs: jnp.concatenate(xs, axis=-1)
    return (cat(aq), cat(ak), cat(av), cat(cq), cat(ck), p[:, C_SV:C_SV + 128], cat(dq), cat(dk), p[:, C_NV:C_NV + 512])


PREP_WIDTHS = (1024, 1024, 512, 512, 128, 128, 512, 512, 512)


def _prep_fwd(p, tab, sp, wqb, wkvb, name):
    r = p.shape[0]

    def body(p_ref, tab_ref, sp_ref, wqb_ref, wkvb_ref, *outs):
        res = _prep_fn(p_ref[...], tab_ref[...], sp_ref[...], wqb_ref[...].astype(F32), wkvb_ref[...].astype(F32))
        for o_ref, v in zip(outs, res):
            o_ref[...] = v.astype(BF16)

    return pl.pallas_call(
        body, name=name, grid=(r // TM,),
        in_specs=[_row_spec(PW), _row_spec(128), _full_spec(sp.shape), _full_spec(wqb.shape), _full_spec(wkvb.shape)],
        out_specs=[_row_spec(w) for w in PREP_WIDTHS],
        out_shape=[jax.ShapeDtypeStruct((r, w), BF16) for w in PREP_WIDTHS],
        compiler_params=_params(("parallel",)))(p, tab, sp, wqb, wkvb)


def _prep_bwd(p, tab, sp, wqb, wkvb, cots, dpool, name):
    r = p.shape[0]

    def body(p_ref, tab_ref, sp_ref, wqb_ref, wkvb_ref, *rest):
        cot_refs, dpool_ref = rest[:9], rest[9]
        dp_ref, dsp_ref, dwqb_ref, dwkvb_ref = rest[10:]
        i = pl.program_id(0)

        @pl.when(i == 0)
        def _():
            dsp_ref[...] = jnp.zeros_like(dsp_ref)
            dwqb_ref[...] = jnp.zeros_like(dwqb_ref)
            dwkvb_ref[...] = jnp.zeros_like(dwkvb_ref)

        tab = tab_ref[...]
        _, vjp = jax.vjp(lambda pp, ss, wq, wk: _prep_fn(pp, tab, ss, wq, wk),
                         p_ref[...], sp_ref[...], wqb_ref[...].astype(F32), wkvb_ref[...].astype(F32))
        dp, dsp, dwq, dwk = vjp(tuple(c[...] for c in cot_refs))
        dp_ref[...] = dp.astype(BF16)
        dp_ref[:, C_POOL:C_POOL + 512] = dpool_ref[...].astype(BF16)
        dsp_ref[...] += dsp
        dwqb_ref[...] += dwq
        dwkvb_ref[...] += dwk

    return pl.pallas_call(
        body, name=name, grid=(r // TM,),
        in_specs=[_row_spec(PW), _row_spec(128), _full_spec(sp.shape), _full_spec(wqb.shape), _full_spec(wkvb.shape)]
        + [_row_spec(w) for w in PREP_WIDTHS] + [_row_spec(512)],
        out_specs=[_row_spec(PW), _full_spec(sp.shape), _full_spec(wqb.shape), _full_spec(wkvb.shape)],
        out_shape=[jax.ShapeDtypeStruct((r, PW), BF16), jax.ShapeDtypeStruct(sp.shape, F32),
                   jax.ShapeDtypeStruct(wqb.shape, F32), jax.ShapeDtypeStruct(wkvb.shape, F32)],
        compiler_params=_params(("arbitrary",)))(p, tab, sp, wqb, wkvb, *cots, dpool)


def _mla_probs(q, k, is_ctx, n_lat):
    s = lax.dot_general(q, k, (((1,), (1,)), ((), ())), preferred_element_type=F32) * MLA_SCALE
    kid = lax.broadcasted_iota(I32, (1, s.shape[1]), 1)
    s = s + jnp.where(jnp.logical_and(is_ctx, kid < n_lat), NEG, 0.0)
    e = jnp.exp(s - jnp.max(s, axis=-1, keepdims=True))
    return e * (1.0 / jnp.sum(e, axis=-1, keepdims=True))


def _mla_fwd(aq, ak, av, n_lat, name):
    r = aq.shape[0]
    nbl = n_lat // TM

    def body(q_ref, k_ref, v_ref, o_ref):
        p = _mla_probs(q_ref[...], k_ref[...], pl.program_id(1) >= nbl, n_lat)
        o_ref[...] = jnp.dot(p.astype(BF16), v_ref[...], preferred_element_type=F32).astype(BF16)

    return pl.pallas_call(
        body, name=name, grid=(4, r // TM),
        in_specs=[pl.BlockSpec((TM, 256), lambda h, i: (i, h)), pl.BlockSpec((r, 256), lambda h, i: (0, h)),
                  pl.BlockSpec((r, 128), lambda h, i: (0, h))],
        out_specs=pl.BlockSpec((TM, 128), lambda h, i: (i, h)),
        out_shape=jax.ShapeDtypeStruct((r, 512), BF16), compiler_params=_params(("parallel", "parallel")))(aq, ak, av)


def _mla_bwd(aq, ak, av, dmix, n_lat, name):
    r = aq.shape[0]
    nbl = n_lat // TM

    def body(q_ref, k_ref, v_ref, do_ref, dq_ref, dk_ref, dv_ref):
        i = pl.program_id(1)

        @pl.when(i == 0)
        def _():
            dk_ref[...] = jnp.zeros_like(dk_ref)
            dv_ref[...] = jnp.zeros_like(dv_ref)

        q, k, v = q_ref[...], k_ref[...], v_ref[...]
        dob = do_ref[...].astype(BF16)
        p = _mla_probs(q, k, i >= nbl, n_lat)
        dv_ref[...] += lax.dot_general(p.astype(BF16), dob, (((0,), (0,)), ((), ())), preferred_element_type=F32)
        dp = lax.dot_general(dob, v, (((1,), (1,)), ((), ())), preferred_element_type=F32)
        ds = (p * (dp - jnp.sum(dp * p, axis=-1, keepdims=True)) * MLA_SCALE).astype(BF16)
        dq_ref[...] = jnp.dot(ds, k, preferred_element_type=F32)
        dk_ref[...] += lax.dot_general(ds, q, (((0,), (0,)), ((), ())), preferred_element_type=F32)

    return pl.pallas_call(
        body, name=name, grid=(4, r // TM),
        in_specs=[pl.BlockSpec((TM, 256), lambda h, i: (i, h)), pl.BlockSpec((r, 256), lambda h, i: (0, h)),
                  pl.BlockSpec((r, 128), lambda h, i: (0, h)), pl.BlockSpec((TM, 128), lambda h, i: (i, h))],
        out_specs=[pl.BlockSpec((TM, 256), lambda h, i: (i, h)), pl.BlockSpec((r, 256), lambda h, i: (0, h)),
                   pl.BlockSpec((r, 128), lambda h, i: (0, h))],
        out_shape=[jax.ShapeDtypeStruct((r, 1024), F32), jax.ShapeDtypeStruct((r, 1024), F32), jax.ShapeDtypeStruct((r, 512), F32)],
        compiler_params=_params(("parallel", "arbitrary")))(aq, ak, av, dmix)


def _pool_fn(ext, w, sc, gid0, grp, is_ctx, n_lat, r_all):
    gid = gid0 + lax.broadcasted_iota(I32, (TM + 2 * HALO, 1), 0)
    lo = jnp.where(is_ctx, n_lat, 0)
    hi = jnp.where(is_ctx, r_all, n_lat)
    z = jnp.where(jnp.logical_and(gid >= lo, gid < hi), ext, 0.0)
    w2 = jnp.roll(z, 1, axis=0) + z
    w4 = jnp.roll(w2, 1, axis=0) + jnp.roll(w2, -1, axis=0)
    w8 = jnp.roll(w4, 2, axis=0) + jnp.roll(w4, -2, axis=0)
    w16 = jnp.roll(w8, 4, axis=0) + jnp.roll(w8, -4, axis=0)
    win = jnp.where(grp == 0, w2, jnp.where(grp == 1, w4, jnp.where(grp == 2, w8, w16)))
    half = jnp.left_shift(1, grp)
    cnt = jnp.maximum(jnp.minimum(gid + half, hi) - jnp.maximum(gid - half, lo), 1).astype(F32)
    d = (win / cnt - ext)[HALO:HALO + TM]
    return _bdot(d, w) * sc


def _pool_ext(u_ref, i, r_all):
    s0 = pl.multiple_of(jnp.maximum(i * TM - HALO, 0), HALO)
    s2 = pl.multiple_of(jnp.minimum(i * TM + TM, r_all - HALO), HALO)
    ext = jnp.concatenate([u_ref[pl.ds(s0, HALO), :], u_ref[pl.ds(pl.multiple_of(i * TM, TM), TM), :], u_ref[pl.ds(s2, HALO), :]], axis=0)
    return ext, s0, s2


def _pool_specs(r):
    return [pl.BlockSpec((r, 128), lambda g, i: (0, C_POOL // 128 + g)), pl.BlockSpec((None, 128, 128), lambda g, i: (g, 0, 0)),
            pl.BlockSpec((None, 1, 128), lambda g, i: (g, 0, 0))]


def _pool_fwd(p, pool_w, pool_sc, n_lat, name):
    r = p.shape[0]
    nbl = n_lat // TM

    def body(u_ref, w_ref, sc_ref, o_ref):
        g, i = pl.program_id(0), pl.program_id(1)
        ext, _, _ = _pool_ext(u_ref, i, r)
        o_ref[...] = _pool_fn(ext, w_ref[...], sc_ref[...], i * TM - HALO, g, i >= nbl, n_lat, r).astype(BF16)

    return pl.pallas_call(
        body, name=name, grid=(4, r // TM), in_specs=_pool_specs(r), out_specs=pl.BlockSpec((TM, 128), lambda g, i: (i, g)),
        out_shape=jax.ShapeDtypeStruct((r, 512), BF16), compiler_params=_params(("parallel", "parallel")))(p, pool_w, pool_sc)


def _pool_bwd(p, pool_w, pool_sc, dmix, n_lat, name):
    r = p.shape[0]
    nbl = n_lat // TM

    def body(u_ref, w_ref, sc_ref, do_ref, du_ref, dw_ref, dsc_ref):
        g, i = pl.program_id(0), pl.program_id(1)

        @pl.when(i == 0)
        def _():
            du_ref[...] = jnp.zeros_like(du_ref)
            dw_ref[...] = jnp.zeros_like(dw_ref)
            dsc_ref[...] = jnp.zeros_like(dsc_ref)

        ext, s0, s2 = _pool_ext(u_ref, i, r)
        _, vjp = jax.vjp(lambda e, w, s: _pool_fn(e, w, s, i * TM - HALO, g, i >= nbl, n_lat, r), ext, w_ref[...], sc_ref[...])
        dext, dw, dsc = vjp(do_ref[...])
        du_ref[pl.ds(s0, HALO), :] += dext[0:HALO]
        du_ref[pl.ds(pl.multiple_of(i * TM, TM), TM), :] += dext[HALO:HALO + TM]
        du_ref[pl.ds(s2, HALO), :] += dext[HALO + TM:]
        dw_ref[...] += dw
        dsc_ref[...] += dsc

    return pl.pallas_call(
        body, name=name, grid=(4, r // TM), in_specs=_pool_specs(r) + [pl.BlockSpec((TM, 128), lambda g, i: (i, 4 + g))],
        out_specs=[pl.BlockSpec((r, 128), lambda g, i: (0, g)), pl.BlockSpec((None, 128, 128), lambda g, i: (g, 0, 0)),
                   pl.BlockSpec((None, 1, 128), lambda g, i: (g, 0, 0))],
        out_shape=[jax.ShapeDtypeStruct((r, 512), F32), jax.ShapeDtypeStruct((4, 128, 128), F32), jax.ShapeDtypeStruct((4, 1, 128), F32)],
        compiler_params=_params(("parallel", "arbitrary")))(p, pool_w, pool_sc, dmix)


def _softmax_parts(parts, extra=None):
    m = functools.reduce(jnp.maximum, [jnp.max(s, axis=-1, keepdims=True) for s in parts])
    if extra is not None:
        m = jnp.maximum(m, extra)
    m = lax.stop_gradient(m)
    es = [jnp.exp(s - m) for s in parts]
    den = functools.reduce(jnp.add, [jnp.sum(e, axis=-1, keepdims=True) for e in es])
    if extra is not None:
        den = den + jnp.exp(extra - m)
    inv = 1.0 / den
    return [e * inv for e in es]


def _swa_band(qpos0, kpos0, is_ctx):
    qpos = qpos0 + lax.broadcasted_iota(I32, (TM, SWA_KEYS), 0)
    kpos = kpos0 + lax.broadcasted_iota(I32, (TM, SWA_KEYS), 1)
    valid = jnp.logical_and(jnp.abs(kpos - qpos) <= 128, jnp.logical_not(is_ctx))
    band = jnp.where(valid, 0.0, NEG)
    return jnp.concatenate([band] * 4, axis=0)


def _swa_fn(q4, kw, vw, kc, vc, sink4, band):
    qs = jnp.concatenate([q4[:, 64 * g:64 * g + 64] for g in range(4)], axis=0)
    s_loc = _bdot_nt(qs, kw) * HD_SCALE + band
    s_ctx = _bdot_nt(qs, kc) * HD_SCALE
    sink = jnp.concatenate([jnp.broadcast_to(sink4[:, g:g + 1], (TM, 1)) for g in range(4)], axis=0)
    p_loc, p_ctx = _softmax_parts([s_loc, s_ctx], sink)
    o = _bdot(p_loc, vw) + _bdot(p_ctx, vc)
    return jnp.concatenate([o[TM * g:TM * (g + 1)] for g in range(4)], axis=1)


def _swa_window(i, n_lat):
    return pl.multiple_of(jnp.clip(i * TM - 128, 0, n_lat - SWA_KEYS), 128)


def _swa_fwd(cq, ck, cv, sp, n_lat, name):
    r = cq.shape[0]
    nbl = n_lat // TM

    def body(q_ref, k_ref, v_ref, sp_ref, o_ref):
        i = pl.program_id(0)
        k0 = _swa_window(i, n_lat)
        kw, vw = k_ref[pl.ds(k0, SWA_KEYS), :].astype(F32), v_ref[pl.ds(k0, SWA_KEYS), :].astype(F32)
        kc, vc = k_ref[pl.ds(n_lat, r - n_lat), :].astype(F32), v_ref[pl.ds(n_lat, r - n_lat), :].astype(F32)
        band = _swa_band(i * TM, k0, i >= nbl)
        for j in range(2):
            c = slice(64 * j, 64 * j + 64)
            o = _swa_fn(q_ref[:, 256 * j:256 * j + 256].astype(F32), kw[:, c], vw[:, c], kc[:, c], vc[:, c],
                        sp_ref[SP_SINK:SP_SINK + 1, 4 * j:4 * j + 4], band)
            o_ref[:, 256 * j:256 * j + 256] = o.astype(BF16)

    return pl.pallas_call(
        body, name=name, grid=(r // TM,),
        in_specs=[_row_spec(512), _full_spec((r, 128)), _full_spec((r, 128)), _full_spec(sp.shape)],
        out_specs=_row_spec(512), out_shape=jax.ShapeDtypeStruct((r, 512), BF16), compiler_params=_params(("parallel",)))(cq, ck, cv, sp)


def _swa_bwd(cq, ck, cv, sp, dmix, n_lat, name):
    r = cq.shape[0]
    nbl = n_lat // TM
    nc = r - n_lat

    def body(q_ref, k_ref, v_ref, sp_ref, do_ref, dq_ref, dk_ref, dv_ref, dsp_ref):
        i = pl.program_id(0)

        @pl.when(i == 0)
        def _():
            dk_ref[...] = jnp.zeros_like(dk_ref)
            dv_ref[...] = jnp.zeros_like(dv_ref)
            dsp_ref[...] = jnp.zeros_like(dsp_ref)

        k0 = _swa_window(i, n_lat)
        kw, vw = k_ref[pl.ds(k0, SWA_KEYS), :].astype(F32), v_ref[pl.ds(k0, SWA_KEYS), :].astype(F32)
        kc, vc = k_ref[pl.ds(n_lat, nc), :].astype(F32), v_ref[pl.ds(n_lat, nc), :].astype(F32)
        dkw, dvw, dkc, dvc, dsk = [], [], [], [], []
        band = _swa_band(i * TM, k0, i >= nbl)
        for j in range(2):
            c = slice(64 * j, 64 * j + 64)
            _, vjp = jax.vjp(lambda q4, a, b, cc, d, s: _swa_fn(q4, a, b, cc, d, s, band),
                             q_ref[:, 256 * j:256 * j + 256].astype(F32), kw[:, c], vw[:, c], kc[:, c], vc[:, c],
                             sp_ref[SP_SINK:SP_SINK + 1, 4 * j:4 * j + 4])
            dq4, a, b, cc, d, s = vjp(do_ref[:, 256 * j:256 * j + 256])
            dq_ref[:, 256 * j:256 * j + 256] = dq4
            dkw.append(a), dvw.append(b), dkc.append(cc), dvc.append(d), dsk.append(s)
        cat = lambda xs: jnp.concatenate(xs, axis=1)
        dk_ref[pl.ds(k0, SWA_KEYS), :] += cat(dkw)
        dv_ref[pl.ds(k0, SWA_KEYS), :] += cat(dvw)
        dk_ref[pl.ds(n_lat, nc), :] += cat(dkc)
        dv_ref[pl.ds(n_lat, nc), :] += cat(dvc)
        dsp_ref[SP_SINK:SP_SINK + 1, 0:8] += cat(dsk)

    return pl.pallas_call(
        body, name=name, grid=(r // TM,),
        in_specs=[_row_spec(512), _full_spec((r, 128)), _full_spec((r, 128)), _full_spec(sp.shape), _row_spec(512, 2)],
        out_specs=[_row_spec(512), _full_spec((r, 128)), _full_spec((r, 128)), _full_spec(sp.shape)],
        out_shape=[jax.ShapeDtypeStruct((r, 512), F32), jax.ShapeDtypeStruct((r, 128), F32), jax.ShapeDtypeStruct((r, 128), F32),
                   jax.ShapeDtypeStruct(sp.shape, F32)],
        compiler_params=_params(("arbitrary",)))(cq, ck, cv, sp, dmix)


def _na_fn(q, kw, vw, kc, vc, bias):
    s_loc = _bdot_nt(q, kw) * HD_SCALE + bias
    s_ctx = _bdot_nt(q, kc) * HD_SCALE
    p_loc, p_ctx = _softmax_parts([s_loc, s_ctx])
    return _bdot(p_loc, vw) + _bdot(p_ctx, vc)


NA_MASKED = 15


def _na_geometry(i, n_lat, is_ctx):
    rows = n_lat // GRID_W
    qrow0 = i * (TM // GRID_W)
    krow0 = jnp.clip(qrow0 - 4, 0, rows - NA_KROWS)
    dr = []
    for qi in range(TM // GRID_W):
        r_lo = jnp.clip(qrow0 + qi - 4, 0, rows - 8)
        row = []
        for kj in range(NA_KROWS):
            kr = krow0 + kj
            ok = jnp.logical_and(jnp.logical_and(kr >= r_lo, kr < r_lo + 8), jnp.logical_not(is_ctx))
            row.append(jnp.where(ok, kr - (qrow0 + qi) + 7, NA_MASKED))
        dr.append(row)
    return pl.multiple_of(krow0 * GRID_W, GRID_W), dr


def _na_bias(t1_ref, hh, dr):
    return jnp.concatenate([jnp.concatenate([t1_ref[hh, dr[qi][kj]] for kj in range(NA_KROWS)], axis=1)
                            for qi in range(TM // GRID_W)], axis=0)


def _na_specs(r):
    return [pl.BlockSpec((TM, 128), lambda pr, i: (i, pr)), pl.BlockSpec((r, 128), lambda pr, i: (0, pr)),
            pl.BlockSpec((r, 128), lambda pr, i: (0, pr)), pl.BlockSpec((2, 16, GRID_W, GRID_W), lambda pr, i: (pr, 0, 0, 0))]


def _na_fwd(dq, dk, dv, t1, n_lat, name):
    r = dq.shape[0]
    nbl = n_lat // TM
    nc = r - n_lat
    nk = NA_KROWS * GRID_W

    def body(q_ref, k_ref, v_ref, t1_ref, o_ref):
        i = pl.program_id(1)
        k0, dr = _na_geometry(jnp.minimum(i, nbl - 1), n_lat, i >= nbl)
        kw, vw = k_ref[pl.ds(k0, nk), :].astype(F32), v_ref[pl.ds(k0, nk), :].astype(F32)
        kc, vc = k_ref[pl.ds(n_lat, nc), :].astype(F32), v_ref[pl.ds(n_lat, nc), :].astype(F32)
        for hh in range(2):
            c = slice(64 * hh, 64 * hh + 64)
            o = _na_fn(q_ref[:, c].astype(F32), kw[:, c], vw[:, c], kc[:, c], vc[:, c], _na_bias(t1_ref, hh, dr))
            o_ref[:, c] = o.astype(BF16)

    return pl.pallas_call(
        body, name=name, grid=(4, r // TM), in_specs=_na_specs(r), out_specs=pl.BlockSpec((TM, 128), lambda pr, i: (i, pr)),
        out_shape=jax.ShapeDtypeStruct((r, 512), BF16), compiler_params=_params(("parallel", "parallel")))(dq, dk, dv, t1)


def _na_bwd(dq, dk, dv, t1, dmix, n_lat, name):
    r = dq.shape[0]
    nbl = n_lat // TM
    nc = r - n_lat
    nk = NA_KROWS * GRID_W

    def body(q_ref, k_ref, v_ref, t1_ref, do_ref, dq_ref, dk_ref, dv_ref, dt1_ref):
        i = pl.program_id(1)

        @pl.when(i == 0)
        def _():
            dk_ref[...] = jnp.zeros_like(dk_ref)
            dv_ref[...] = jnp.zeros_like(dv_ref)
            dt1_ref[...] = jnp.zeros_like(dt1_ref)

        k0, dr = _na_geometry(jnp.minimum(i, nbl - 1), n_lat, i >= nbl)
        kw, vw = k_ref[pl.ds(k0, nk), :].astype(F32), v_ref[pl.ds(k0, nk), :].astype(F32)
        kc, vc = k_ref[pl.ds(n_lat, nc), :].astype(F32), v_ref[pl.ds(n_lat, nc), :].astype(F32)
        dkw, dvw, dkc, dvc = [], [], [], []
        for hh in range(2):
            c = slice(64 * hh, 64 * hh + 64)
            _, vjp = jax.vjp(_na_fn,
                             q_ref[:, c].astype(F32), kw[:, c], vw[:, c], kc[:, c], vc[:, c], _na_bias(t1_ref, hh, dr))
            dqh, a, b, cc, d, dbias = vjp(do_ref[:, c])
            dq_ref[:, c] = dqh
            dkw.append(a), dvw.append(b), dkc.append(cc), dvc.append(d)
            for qi in range(TM // GRID_W):
                for kj in range(NA_KROWS):
                    dt1_ref[hh, dr[qi][kj]] += dbias[GRID_W * qi:GRID_W * (qi + 1), GRID_W * kj:GRID_W * (kj + 1)]
        cat = lambda xs: jnp.concatenate(xs, axis=1)
        dk_ref[pl.ds(k0, nk), :] += cat(dkw)
        dv_ref[pl.ds(k0, nk), :] += cat(dvw)
        dk_ref[pl.ds(n_lat, nc), :] += cat(dkc)
        dv_ref[pl.ds(n_lat, nc), :] += cat(dvc)

    return pl.pallas_call(
        body, name=name, grid=(4, r // TM), in_specs=_na_specs(r) + [pl.BlockSpec((TM, 128), lambda pr, i: (i, 12 + pr))],
        out_specs=[pl.BlockSpec((TM, 128), lambda pr, i: (i, pr)), pl.BlockSpec((r, 128), lambda pr, i: (0, pr)),
                   pl.BlockSpec((r, 128), lambda pr, i: (0, pr)), pl.BlockSpec((2, 16, GRID_W, GRID_W), lambda pr, i: (pr, 0, 0, 0))],
        out_shape=[jax.ShapeDtypeStruct((r, 512), F32)] * 3 + [jax.ShapeDtypeStruct((8, 16, GRID_W, GRID_W), F32)],
        compiler_params=_params(("parallel", "arbitrary")))(dq, dk, dv, t1, dmix)


def _conv_ext(main_ref, prev_ref, next_ref, edges):
    prev_ok, next_ok = edges
    return jnp.concatenate([jnp.where(prev_ok, prev_ref[...], 0.0), main_ref[...], jnp.where(next_ok, next_ref[...], 0.0)], axis=0)


def _conv_edges(i, nbl, nb):
    return jnp.logical_and(i != 0, i != nbl), jnp.logical_and(i != nbl - 1, i != nb - 1)


def _conv_apply(ext, w, b):
    up = jnp.roll(ext, 1, axis=0)
    dn = jnp.roll(ext, -1, axis=0)
    return up * w[0:1] + ext * w[1:2] + dn * w[2:3] + b, up, dn


def _conv_in_specs(tc, r):
    nb8 = TM // HALO
    last8 = r // HALO - 1

    def trio(half):
        return [pl.BlockSpec((None, TM, tc), lambda j, i: (half, i, j)),
                pl.BlockSpec((None, HALO, tc), lambda j, i: (half, jnp.maximum(i * nb8 - 1, 0), j)),
                pl.BlockSpec((None, HALO, tc), lambda j, i: (half, jnp.minimum((i + 1) * nb8, last8), j))]

    wb = [pl.BlockSpec((None, 3, tc), lambda j, i: (0, 0, j)), pl.BlockSpec((None, 3, tc), lambda j, i: (1, 0, j)),
          pl.BlockSpec((None, 1, tc), lambda j, i: (0, 0, j)), pl.BlockSpec((None, 1, tc), lambda j, i: (1, 0, j))]
    return trio(0) + trio(1) + wb


def _convgate_fwd(a3, cw, cb, n_lat, name):
    _, r, ff = a3.shape
    nbl = n_lat // TM
    tc = _pick(ff, (512, 256, 128))

    def body(g_ref, gp_ref, gn_ref, v_ref, vp_ref, vn_ref, wg_ref, wv_ref, bg_ref, bv_ref, u_ref):
        edges = _conv_edges(pl.program_id(1), nbl, r // TM)
        gg, _, _ = _conv_apply(_conv_ext(g_ref, gp_ref, gn_ref, edges), wg_ref[...], bg_ref[...])
        gv, _, _ = _conv_apply(_conv_ext(v_ref, vp_ref, vn_ref, edges), wv_ref[...], bv_ref[...])
        u_ref[...] = (jax.nn.silu(gg[HALO:HALO + TM]) * gv[HALO:HALO + TM]).astype(BF16)

    return pl.pallas_call(
        body, name=name, grid=(ff // tc, r // TM), in_specs=_conv_in_specs(tc, r),
        out_specs=pl.BlockSpec((TM, tc), lambda j, i: (i, j)), out_shape=jax.ShapeDtypeStruct((r, ff), BF16),
        compiler_params=_params(("parallel", "parallel")))(a3, a3, a3, a3, a3, a3, cw, cw, cb, cb)


def _convgate_bwd(a3, cw, cb, du, n_lat, name):
    _, r, ff = a3.shape
    nbl = n_lat // TM
    tc = _pick(ff, (512, 256, 128))
    nb8 = TM // HALO
    last8 = r // HALO - 1

    def body(g_ref, gp_ref, gn_ref, v_ref, vp_ref, vn_ref, wg_ref, wv_ref, bg_ref, bv_ref, du_ref, dup_ref, dun_ref,
             da_ref, dcw_ref, dcb_ref):
        i = pl.program_id(1)

        @pl.when(i == 0)
        def _():
            dcw_ref[...] = jnp.zeros_like(dcw_ref)
            dcb_ref[...] = jnp.zeros_like(dcb_ref)

        edges = _conv_edges(i, nbl, r // TM)
        wg, wv = wg_ref[...], wv_ref[...]
        eg, ev = _conv_ext(g_ref, gp_ref, gn_ref, edges), _conv_ext(v_ref, vp_ref, vn_ref, edges)
        gg, ug, dg_ = _conv_apply(eg, wg, bg_ref[...])
        gv, uv, dv_ = _conv_apply(ev, wv, bv_ref[...])
        due = _conv_ext(du_ref, dup_ref, dun_ref, edges)
        sg = jax.nn.sigmoid(gg)
        dgg = due * gv * (sg * (1.0 + gg * (1.0 - sg)))
        dgv = due * (gg * sg)
        main = slice(HALO, HALO + TM)
        for h, (dgx, w, ex, upx, dnx) in enumerate(((dgg, wg, eg, ug, dg_), (dgv, wv, ev, uv, dv_))):
            da = dgx * w[1:2] + jnp.roll(dgx, -1, axis=0) * w[0:1] + jnp.roll(dgx, 1, axis=0) * w[2:3]
            da_ref[h] = da[main].astype(BF16)
            dm = dgx[main]
            dcw_ref[h, 0:1, :] += jnp.sum(dm * upx[main], axis=0, keepdims=True)
            dcw_ref[h, 1:2, :] += jnp.sum(dm * ex[main], axis=0, keepdims=True)
            dcw_ref[h, 2:3, :] += jnp.sum(dm * dnx[main], axis=0, keepdims=True)
            dcb_ref[h] += jnp.sum(dm, axis=0, keepdims=True)

    du_specs = [pl.BlockSpec((TM, tc), lambda j, i: (i, j)),
                pl.BlockSpec((HALO, tc), lambda j, i: (jnp.maximum(i * nb8 - 1, 0), j)),
                pl.BlockSpec((HALO, tc), lambda j, i: (jnp.minimum((i + 1) * nb8, last8), j))]
    return pl.pallas_call(
        body, name=name, grid=(ff // tc, r // TM), in_specs=_conv_in_specs(tc, r) + du_specs,
        out_specs=[pl.BlockSpec((2, TM, tc), lambda j, i: (0, i, j)), pl.BlockSpec((2, 3, tc), lambda j, i: (0, 0, j)),
                   pl.BlockSpec((2, 1, tc), lambda j, i: (0, 0, j))],
        out_shape=[jax.ShapeDtypeStruct((2, r, ff), BF16), jax.ShapeDtypeStruct((2, 3, ff), F32), jax.ShapeDtypeStruct((2, 1, ff), F32)],
        compiler_params=_params(("parallel", "arbitrary")))(a3, a3, a3, a3, a3, a3, cw, cw, cb, cb, du, du, du)


def _layer_fwd(x, w, l, tab, n_lat, before_out=None):
    nbl = n_lat // TM
    mod = w["mods"][l]
    h1 = _normmod_fwd(x, w["g_mix"][l], mod, 0, nbl, "normmod_fwd")
    p = _mm(h1, w["w_in"][l], "nn", F32, "mm_in")
    qkv = _prep_fwd(p, tab, w["sp"][l], w["w_qb"][l], w["w_kvb"][l], "prep_fwd")
    oa = _mla_fwd(qkv[0], qkv[1], qkv[2], n_lat, "mla_fwd")
    ob = _pool_fwd(p, w["pool_w"][l], w["pool_sc"][l], n_lat, "pool_fwd")
    oc = _swa_fwd(qkv[3], qkv[4], qkv[5], w["sp"][l], n_lat, "swa_fwd")
    od = _na_fwd(qkv[6], qkv[7], qkv[8], w["t1"][l], n_lat, "na_fwd")
    mix = jnp.concatenate([oa, ob, oc, od], axis=1)
    if before_out is not None:
        before_out(mix)
    y = _mm(mix, w["w_out"][l], "nn", F32, "mm_out")
    x1 = _resid_fwd(x, y, mod, 2, nbl, "resid_fwd")
    h2 = _normmod_fwd(x1, w["g_ffn"][l], mod, 3, nbl, "normmod_fwd")
    a3 = _mm(h2, w["w_up"][l], "nn", F32, "mm_up", o_split=True)
    u = _convgate_fwd(a3, w["conv_w"][l], w["conv_b"][l], n_lat, "convgate_fwd")
    y2 = _mm(u, w["w_down"][l], "nn", F32, "mm_down")
    x2 = _resid_fwd(x1, y2, mod, 5, nbl, "resid_fwd")
    return x2, dict(x=x, h1=h1, p=p, qkv=qkv, mix=mix, y=y, x1=x1, h2=h2, a3=a3, u=u, y2=y2)


def _layer_bwd(dx, s, w, l, tab, n_lat):
    nbl = n_lat // TM
    mod = w["mods"][l]
    g = {}
    dy2, dmod_a = _resid_bwd(dx, s["y2"], mod, 5, nbl, "resid_bwd")
    g["w_down"] = _mm(s["u"], dy2, "tn", GRAD_WIRE, "mm_dwdown")
    du = _mm(dy2, w["w_down"][l], "nt", F32, "mm_du")
    da3, g["conv_w"], g["conv_b"] = _convgate_bwd(s["a3"], w["conv_w"][l], w["conv_b"][l], du, n_lat, "convgate_bwd")
    g["w_up"] = _mm(s["h2"], da3, "tn", GRAD_WIRE, "mm_dwup", b_split=True)
    dh2 = _mm(da3, w["w_up"][l], "nt", F32, "mm_dh2", a_split=True)
    dx1, g["g_ffn"], dmod_b = _normmod_bwd(s["x1"], w["g_ffn"][l], mod, dh2, dx, 3, nbl, "normmod_bwd")
    dy, dmod_c = _resid_bwd(dx1, s["y"], mod, 2, nbl, "resid_bwd")
    g["w_out"] = _mm(s["mix"], dy, "tn", GRAD_WIRE, "mm_dwout")
    dmix = _mm(dy, w["w_out"][l], "nt", F32, "mm_dmix")
    qkv = s["qkv"]
    daq, dak, dav = _mla_bwd(qkv[0], qkv[1], qkv[2], dmix, n_lat, "mla_bwd")
    dpool, g["pool_w"], g["pool_sc"] = _pool_bwd(s["p"], w["pool_w"][l], w["pool_sc"][l], dmix, n_lat, "pool_bwd")
    dcq, dck, dcv, dsp_c = _swa_bwd(qkv[3], qkv[4], qkv[5], w["sp"][l], dmix, n_lat, "swa_bwd")
    ddq, ddk, ddv, g["t1"] = _na_bwd(qkv[6], qkv[7], qkv[8], w["t1"][l], dmix, n_lat, "na_bwd")
    dp, dsp_p, g["w_qb"], g["w_kvb"] = _prep_bwd(s["p"], tab, w["sp"][l], w["w_qb"][l], w["w_kvb"][l],
                                                (daq, dak, dav, dcq, dck, dcv, ddq, ddk, ddv), dpool, "prep_bwd")
    g["sp"] = dsp_c + dsp_p
    g["w_in"] = _mm(s["h1"], dp, "tn", GRAD_WIRE, "mm_dwin")
    dh1 = _mm(dp, w["w_in"][l], "nt", F32, "mm_dh1")
    dx0, g["g_mix"], dmod_d = _normmod_bwd(s["x"], w["g_mix"][l], mod, dh1, dx1, 0, nbl, "normmod_bwd")
    g["mods"] = dmod_a + dmod_b + dmod_c + dmod_d
    return dx0, g


def _local_step(x_all, target, w, tab, n_lat):
    saved = []
    x = x_all
    for l in range(DEPTH):
        x, s = _layer_fwd(x, w, l, tab, n_lat)
        saved.append(s)
    loss, dx = _loss_kernel(x, target, n_lat // TM, "loss")
    grads = [None] * DEPTH
    for l in reversed(range(DEPTH)):
        dx, grads[l] = _layer_bwd(dx, saved[l], w, l, tab, n_lat)
    return loss[0, 0], dx, grads


def _pad_cols(a, widths):
    parts, o = [], 0
    for take, pad in widths:
        parts.append(a[..., o:o + take])
        if pad:
            parts.append(jnp.zeros(a.shape[:-1] + (pad,), a.dtype))
        o += take
    return jnp.concatenate(parts, axis=-1)


def _w_in_layout(w_in):
    return _pad_cols(w_in, [(832, 64), (P_COLS - 832, PW - P_COLS - 64)])


def _w_in_unlayout(g):
    return jnp.concatenate([g[..., 0:832], g[..., 896:896 + P_COLS - 832]], axis=-1)


def _w_qb_layout(w):
    s = w.reshape(w.shape[:-1] + (4, 192))
    return jnp.concatenate([s[..., 0:128].reshape(w.shape[:-1] + (512,)), s[..., 128:192].reshape(w.shape[:-1] + (256,))], axis=-1)


def _w_qb_unlayout(g):
    n = g[..., 0:512].reshape(g.shape[:-1] + (4, 128))
    r = g[..., 512:768].reshape(g.shape[:-1] + (4, 64))
    return jnp.concatenate([n, r], axis=-1).reshape(g.shape[:-1] + (768,))


SP_SLOTS = (("mla_q_a_norm", 512), ("mla_kv_a_norm", 256), ("mla_q_nope_norm", 128), ("mla_q_rope_norm", 64),
            ("mla_k_nope_norm", 128), ("mla_k_rope_norm", 64), ("swa_q_norm", 64), ("swa_k_norm", 64),
            ("na_q_norm", 64), ("na_k_norm", 64), ("swa_sink", 8))


def _sp_pack(small):
    rows = [jnp.pad(small[k], ((0, 0), (0, 512 - n))) for k, n in SP_SLOTS]
    rows += [jnp.zeros_like(rows[0])] * (16 - len(rows))
    return jnp.stack(rows, axis=1)


def _sp_unpack(sp):
    return {k: sp[:, i, 0:n] for i, (k, n) in enumerate(SP_SLOTS)}


def _rpb_onehot():
    qc = lax.broadcasted_iota(I32, (GRID_W, GRID_W), 0)
    kc = lax.broadcasted_iota(I32, (GRID_W, GRID_W), 1)
    dc = (jnp.clip(kc - qc, -15, 15) + 15).reshape(1, GRID_W * GRID_W)
    return (lax.broadcasted_iota(I32, (32, GRID_W * GRID_W), 0) == dc).astype(F32)


def _rpb_expand(rpb):
    l = rpb.shape[0]
    flat = jnp.pad(rpb, ((0, 0), (0, 0), (0, 1), (0, 1))).reshape(l * 128, 32)
    t1 = _mm_exact(flat, _rpb_onehot(), "rpb_expand").reshape(l, 8, 16, GRID_W, GRID_W)
    qc = lax.broadcasted_iota(I32, (GRID_W, GRID_W), 0)
    kc = lax.broadcasted_iota(I32, (GRID_W, GRID_W), 1)
    c_lo = jnp.clip(qc - 8, 0, GRID_W - 16)
    col_ok = jnp.logical_and(kc >= c_lo, kc < c_lo + 16)
    row_ok = lax.broadcasted_iota(I32, (16, 1, 1), 0) != NA_MASKED
    return jnp.where(jnp.logical_and(col_ok[None], row_ok), t1, NEG)


def _rpb_fold(dt1):
    l = dt1.shape[0]
    g = _mm_exact(dt1.reshape(l * 128, GRID_W * GRID_W), _rpb_onehot().T, "rpb_fold")
    return g.reshape(l, 8, 16, 32)[:, :, 0:15, 0:31]


def _rope_table(n_lat, n_ctx):
    t = jnp.arange(n_lat)
    inv = ROPE_BASE ** (-jnp.arange(0, 32, 2, dtype=F32) / 32)
    ar = (t // GRID_W).astype(F32)[:, None] * inv
    ac = (t % GRID_W).astype(F32)[:, None] * inv
    cos = jnp.concatenate([jnp.cos(ar), jnp.cos(ar), jnp.cos(ac), jnp.cos(ac)], axis=-1)
    sin = jnp.concatenate([jnp.sin(ar), jnp.sin(ar), jnp.sin(ac), jnp.sin(ac)], axis=-1)
    tab = jnp.concatenate([cos, sin], axis=-1)
    ident = jnp.concatenate([jnp.ones((n_ctx, 64), F32), jnp.zeros((n_ctx, 64), F32)], axis=-1)
    return jnp.concatenate([tab, ident], axis=0)


def _small_weights(full, mods):
    l = full["g_mix"].shape[0]
    ff = full["ffn_conv_b"].shape[1] // 2
    return dict(
        mods=mods, g_mix=full["g_mix"][:, None, :], g_ffn=full["g_ffn"][:, None, :],
        sp=_sp_pack(full), pool_w=full["pool_w"], pool_sc=full["pool_scale"].reshape(l, 4, 1, 128),
        t1=_rpb_expand(full["na_rpb"]),
        conv_w=full["ffn_conv_w"].reshape(l, 3, 2, ff).transpose(0, 2, 1, 3),
        conv_b=full["ffn_conv_b"].reshape(l, 2, 1, ff))


def _kernel_weights(full, mods):
    w = _small_weights(full, mods)
    w.update(w_in=_w_in_layout(full["w_in"]).astype(BF16), w_out=full["w_out"].astype(BF16),
             w_up=full["ffn_w_up"].astype(BF16), w_down=full["ffn_w_down"].astype(BF16),
             w_qb=_w_qb_layout(full["mla_w_qb"]).astype(BF16), w_kvb=full["mla_w_kvb"].astype(BF16))
    return w


def _reference_grads(grads, big=True):
    st = lambda k: jnp.stack([g[k] for g in grads], axis=0)
    l = len(grads)
    out = dict(
        g_mix=st("g_mix")[:, 0], g_ffn=st("g_ffn")[:, 0],
        pool_w=st("pool_w"), pool_scale=st("pool_sc").reshape(l, 512), na_rpb=_rpb_fold(st("t1")),
        ffn_conv_w=st("conv_w").transpose(0, 2, 1, 3).reshape(l, 3, -1), ffn_conv_b=st("conv_b").reshape(l, -1),
        mods=st("mods"))
    if big:
        out.update(w_in=_w_in_unlayout(st("w_in")), w_out=st("w_out"), ffn_w_up=st("w_up"), ffn_w_down=st("w_down"),
                   mla_w_qb=_w_qb_unlayout(st("w_qb")), mla_w_kvb=st("w_kvb"))
    out.update(_sp_unpack(st("sp")))
    return out


ANY = pl.BlockSpec(memory_space=pl.ANY)


def _flip(x, y, j):
    return (1 - x if j >> 1 else x), (1 - y if j & 1 else y)


def _comm_call(name, ins, out_shapes, n_copies, plan, aliases=None):
    n_in, n_out = len(ins), len(out_shapes)

    def body(*refs):
        in_refs, out_refs = refs[:n_in], refs[n_in:n_in + n_out]
        ssem, rsem = refs[n_in + n_out:]
        pos = (lax.axis_index("x"), lax.axis_index("y"), lax.axis_index("c"))
        copies = plan(in_refs, out_refs, pos)
        assert len(copies) == n_copies
        descs = []
        for i, (src, dst, peer) in enumerate(copies):
            if peer is None:
                d = pltpu.make_async_copy(src, dst, ssem.at[i])
            else:
                d = pltpu.make_async_remote_copy(src_ref=src, dst_ref=dst, send_sem=ssem.at[i], recv_sem=rsem.at[i],
                                                 device_id=peer, device_id_type=MESH)
            d.start()
            descs.append(d)
        for d in descs:
            d.wait()

    return pl.pallas_call(
        body, name=name, in_specs=[ANY] * n_in, out_specs=[ANY] * n_out, out_shape=list(out_shapes),
        input_output_aliases=aliases or {},
        scratch_shapes=[pltpu.SemaphoreType.DMA((n_copies,)), pltpu.SemaphoreType.DMA((n_copies,))])(*ins)


def _sib_fill(bufs, part, name):
    def plan(ins, outs, pos):
        x, y, c = pos
        return [(o_ref.at[part(c)], o_ref.at[part(c)], (x, y, 1 - c)) for o_ref in outs]

    shapes = [jax.ShapeDtypeStruct(b.shape, b.dtype) for b in bufs]
    return _comm_call(name, bufs, shapes, len(bufs), plan, aliases={i: i for i in range(len(bufs))})


HBM = pl.BlockSpec(memory_space=pltpu.HBM)
SEM = pl.BlockSpec(memory_space=pltpu.SEMAPHORE)
DATAFLOW = pltpu.SideEffectType.DATAFLOW_SIDE_EFFECTING


def _remote_start(name, bufs, n_copies, plan, after):
    nb = len(bufs)

    def body(*refs):
        ssem, rsem, token = refs[nb + 1], refs[nb + 2], refs[-1]
        pos = (lax.axis_index("x"), lax.axis_index("y"), lax.axis_index("c"))
        copies = plan(refs[:nb], pos)
        assert len(copies) == n_copies
        for i, (src, dst, peer) in enumerate(copies):
            pltpu.make_async_remote_copy(src_ref=src, dst_ref=dst, send_sem=ssem.at[i], recv_sem=rsem.at[i],
                                         device_id=peer, device_id_type=MESH).start()
        token[...] = jnp.zeros_like(token)

    outs = pl.pallas_call(
        body, name=name,
        out_shape=(pltpu.SemaphoreType.DMA((n_copies,)), pltpu.SemaphoreType.DMA((n_copies,)),
                   *[pltpu.HBM(b.shape, b.dtype) for b in bufs], jax.ShapeDtypeStruct((8, 128), F32)),
        in_specs=[HBM] * nb + [ANY], out_specs=(SEM, SEM, *[HBM] * nb, pl.BlockSpec(memory_space=pltpu.VMEM)),
        input_output_aliases={i: 2 + i for i in range(nb)},
        compiler_params=pltpu.CompilerParams(has_side_effects=DATAFLOW),
    )(*[pltpu.with_memory_space_constraint(b, pltpu.HBM) for b in bufs], after)
    return outs[0], outs[1], list(outs[2:2 + nb]), outs[-1]


def _remote_wait(name, ssem, rsem, bufs, n_copies, plan, after):
    nb = len(bufs)

    def body(*refs):
        ssem_ref, rsem_ref = refs[nb], refs[nb + 1]
        pos = (lax.axis_index("x"), lax.axis_index("y"), lax.axis_index("c"))
        copies = plan(refs[:nb], pos)
        assert len(copies) == n_copies
        for i, (src, dst, peer) in enumerate(copies):
            cp = pltpu.make_async_remote_copy(src_ref=src, dst_ref=dst, send_sem=ssem_ref.at[i], recv_sem=rsem_ref.at[i],
                                              device_id=peer, device_id_type=MESH)
            cp.wait_send()
            cp.wait_recv()

    outs = pl.pallas_call(
        body, name=name, out_shape=tuple(pltpu.HBM(b.shape, b.dtype) for b in bufs),
        in_specs=[HBM] * nb + [SEM, SEM, ANY], out_specs=tuple([HBM] * nb), input_output_aliases={i: i for i in range(nb)},
        compiler_params=pltpu.CompilerParams(has_side_effects=DATAFLOW),
    )(*bufs, ssem, rsem, after)
    return list(outs)


BIG_GATHER = ("axis1", "axis1", "axis1", "axis1", "lane", "lane")


def _place_own(shard, kind, xyvec, name):
    rows, cols = shard.shape
    tr = _row_tile(rows, cols)
    if kind == "axis1":
        shape = (4, rows, cols)
        o_spec = pl.BlockSpec((None, tr, cols), lambda i, x_ref, y_ref: (2 * x_ref[0] + y_ref[0], i, 0))
    else:
        shape = (rows, 4 * cols)
        o_spec = pl.BlockSpec((tr, cols), lambda i, x_ref, y_ref: (i, 2 * x_ref[0] + y_ref[0]))

    def body(x_ref, y_ref, s_ref, o_ref):
        o_ref[...] = s_ref[...]

    return pl.pallas_call(
        body, name=name,
        grid_spec=pltpu.PrefetchScalarGridSpec(num_scalar_prefetch=2, grid=(rows // tr,),
                                               in_specs=[pl.BlockSpec((tr, cols), lambda i, x_ref, y_ref: (i, 0))], out_specs=o_spec),
        out_shape=jax.ShapeDtypeStruct(shape, shard.dtype), compiler_params=_params(("parallel",)))(*xyvec, shard)


def _w_gather_plan(shapes, kinds):
    n = len(kinds)

    def plan(refs, pos):
        x, y, c = pos
        k = 2 * x + y
        cps = []
        for s_ref, l_ref, kind, shp in zip(refs[:n], refs[n:], kinds, shapes):
            h, w = shp[0] // 2, shp[1]
            rows = pl.ds(pl.multiple_of(c * h, 16), h)
            dst = l_ref.at[k, rows, :] if kind == "axis1" else l_ref.at[rows, pl.ds(pl.multiple_of(k * w, 128), w)]
            for j in (1, 2, 3):
                tx, ty = _flip(x, y, j)
                cps.append((s_ref.at[rows, :], dst, (tx, ty, c)))
        return cps

    return plan


def _w_fill(lands, shapes, kinds, name):
    def plan(ins, outs, pos):
        x, y, c = pos
        cps = []
        for o_ref, kind, shp in zip(outs, kinds, shapes):
            h = shp[0] // 2
            rows = pl.ds(pl.multiple_of(c * h, 16), h)
            part = o_ref.at[:, rows, :] if kind == "axis1" else o_ref.at[rows, :]
            cps.append((part, part, (x, y, 1 - c)))
        return cps

    return _comm_call(name, lands, [jax.ShapeDtypeStruct(b.shape, b.dtype) for b in lands], len(lands), plan,
                      aliases={i: i for i in range(len(lands))})


def _g_scatter_plan(kinds, widths):
    n = len(kinds)

    def plan(refs, pos):
        x, y, c = pos
        cps = []
        for s_ref, l_ref, kind, w in zip(refs[:n], refs[n:], kinds, widths):
            for j in (1, 2, 3):
                tx, ty = _flip(x, y, j)
                kj = 2 * tx + ty
                src = s_ref.at[kj] if kind == "cm" else s_ref.at[:, pl.ds(pl.multiple_of(kj * w, 128), w)]
                cps.append((src, l_ref.at[j - 1], (tx, ty, c)))
        return cps

    return plan


def _pair_up(xs, name):
    def plan(ins, outs, pos):
        x, y, c = pos
        cps = []
        for i_ref, o_ref in zip(ins, outs):
            cps.append((i_ref, o_ref.at[c], None))
            cps.append((i_ref, o_ref.at[c], (x, y, 1 - c)))
        return cps

    return _comm_call(name, xs, [jax.ShapeDtypeStruct((2,) + a.shape, a.dtype) for a in xs], 2 * len(xs), plan)


def _chip_gather(xs, kinds, name):
    def dst(o_ref, kind, k, x_shape):
        if kind == "lead":
            return o_ref.at[k]
        w = x_shape[-1]
        return o_ref.at[(slice(None),) * (len(x_shape) - 1) + (pl.ds(pl.multiple_of(k * w, 128), w),)]

    def plan(ins, outs, pos):
        x, y, c = pos
        k = 2 * x + y
        cps = []
        for i_ref, o_ref, kind, a in zip(ins, outs, kinds, xs):
            cps.append((i_ref, dst(o_ref, kind, k, a.shape), None))
            for j in (1, 2, 3):
                tx, ty = _flip(x, y, j)
                cps.append((i_ref, dst(o_ref, kind, k, a.shape), (tx, ty, c)))
        return cps

    def oshape(a, kind):
        return (4,) + a.shape if kind == "lead" else a.shape[:-1] + (4 * a.shape[-1],)

    return _comm_call(name, xs, [jax.ShapeDtypeStruct(oshape(a, kd), a.dtype) for a, kd in zip(xs, kinds)], 4 * len(xs), plan)


def _row_tile(rows, cols, budget=1 << 20):
    for t in (2048, 1024, 512, 256, 128, 64, 32, 16, 8):
        if rows % t == 0 and t * cols * 4 <= budget:
            return t
    return rows


def _sum_lead(x, name):
    n, rows, w = x.shape
    tr = _row_tile(rows, w, (1 << 21) // n)

    def body(x_ref, o_ref):
        acc = x_ref[0].astype(F32)
        for j in range(1, n):
            acc = acc + x_ref[j].astype(F32)
        o_ref[...] = acc

    return pl.pallas_call(
        body, name=name, grid=(rows // tr,), in_specs=[pl.BlockSpec((n, tr, w), lambda i: (0, i, 0))],
        out_specs=pl.BlockSpec((tr, w), lambda i: (i, 0)), out_shape=jax.ShapeDtypeStruct((rows, w), F32),
        compiler_params=_params(("parallel",)))(x)


def _sum_into(g, landed, buf, layer, pvec, kind, layers, name):
    n, rows, w = landed.shape
    tr = _row_tile(rows, w, 1 << 19)

    def body(c_ref, x_ref, y_ref, own_ref, l_ref, *refs):
        o_ref = refs[-1]
        acc = own_ref[...].astype(F32)
        for j in range(n):
            acc = acc + l_ref[j].astype(F32)
        o_ref[...] = acc

    if kind == "cm":
        own_spec = pl.BlockSpec((None, tr, w), lambda i, c_ref, x_ref, y_ref: (2 * x_ref[0] + y_ref[0], i, 0))
    else:
        own_spec = pl.BlockSpec((tr, w), lambda i, c_ref, x_ref, y_ref: (i, 2 * x_ref[0] + y_ref[0]))
    in_specs = [own_spec, pl.BlockSpec((n, tr, w), lambda i, c_ref, x_ref, y_ref: (0, i, 0))]
    args = [*pvec, g, landed]
    if buf is not None:
        in_specs.append(ANY)
        args.append(buf)
    return pl.pallas_call(
        body, name=name,
        grid_spec=pltpu.PrefetchScalarGridSpec(
            num_scalar_prefetch=3, grid=(rows // tr,), in_specs=in_specs,
            out_specs=pl.BlockSpec((None, None, tr, w), lambda i, c_ref, x_ref, y_ref: (layer, c_ref[0], i, 0))),
        out_shape=jax.ShapeDtypeStruct((layers, 2, rows, w), F32), input_output_aliases={} if buf is None else {5: 0},
        compiler_params=_params(("arbitrary",)))(*args)


def _adamw_math(w, g, m, v):
    mn = ADAM_B1 * m + (1.0 - ADAM_B1) * g
    vn = ADAM_B2 * v + (1.0 - ADAM_B2) * jnp.square(g)
    m_hat = mn / (1.0 - ADAM_B1 ** ADAM_STEP)
    v_hat = vn / (1.0 - ADAM_B2 ** ADAM_STEP)
    return -ADAM_LR * (m_hat / (jnp.sqrt(v_hat) + ADAM_EPS) + ADAM_WD * w), mn, vn


def _adamw_pair(w, g2, m, v, name):
    l, rows, cols = w.shape
    tr = _row_tile(rows, cols, 1 << 19)

    def body(w_ref, g0_ref, g1_ref, m_ref, v_ref, g_ref, d_ref, mo_ref, vo_ref):
        gv = g0_ref[...] + g1_ref[...]
        g_ref[...] = gv
        d_ref[...], mo_ref[...], vo_ref[...] = _adamw_math(w_ref[...], gv, m_ref[...], v_ref[...])

    spec = pl.BlockSpec((None, tr, cols), lambda li, i: (li, i, 0))
    half = lambda cc: pl.BlockSpec((None, None, tr, cols), lambda li, i: (li, cc, i, 0))
    return pl.pallas_call(
        body, name=name, grid=(l, rows // tr), in_specs=[spec, half(0), half(1), spec, spec], out_specs=[spec] * 4,
        out_shape=[jax.ShapeDtypeStruct(w.shape, F32)] * 4, compiler_params=_params(("parallel", "parallel")))(w, g2, g2, m, v)


def _adamw(w, g, m, v, name):
    shape = w.shape
    cols = shape[-1]
    rows = w.size // cols
    tr = _row_tile(rows, cols, 1 << 19)

    def body(w_ref, g_ref, m_ref, v_ref, d_ref, mo_ref, vo_ref):
        d_ref[...], mo_ref[...], vo_ref[...] = _adamw_math(w_ref[...], g_ref[...], m_ref[...], v_ref[...])

    spec = pl.BlockSpec((tr, cols), lambda i: (i, 0))
    outs = pl.pallas_call(
        body, name=name, grid=(rows // tr,), in_specs=[spec] * 4, out_specs=[spec] * 3,
        out_shape=[jax.ShapeDtypeStruct((rows, cols), F32)] * 3,
        compiler_params=_params(("parallel",)))(*[a.reshape(rows, cols) for a in (w, g, m, v)])
    return [o.reshape(shape) for o in outs]


def _silu_grad(x):
    s = jax.nn.sigmoid(x)
    return s * (1.0 + x * (1.0 - s))


def _mod_fwd(cs16, w_mod, b_sh, name):
    l, d, wc = w_mod.shape
    tn = _pick(wc, (512, 384, 256, 128))

    def body(c_ref, w_ref, b_ref, o_ref):
        a = jax.nn.silu(c_ref[...]).astype(BF16)
        o_ref[...] = jnp.dot(a, w_ref[...].astype(BF16), preferred_element_type=F32) + b_ref[...]

    return pl.pallas_call(
        body, name=name, grid=(l, wc // tn),
        in_specs=[_full_spec((16, d)), pl.BlockSpec((None, d, tn), lambda i, j: (i, 0, j)), pl.BlockSpec((None, 1, tn), lambda i, j: (i, 0, j))],
        out_specs=pl.BlockSpec((None, 16, tn), lambda i, j: (i, 0, j)), out_shape=jax.ShapeDtypeStruct((l, 16, wc), F32),
        compiler_params=_params(("parallel", "parallel")))(cs16, w_mod, b_sh)


def _mod_dw(cs16, dm_sh, name):
    l, _, wc = dm_sh.shape
    d = cs16.shape[1]
    tr = _pick(d, (512, 256, 128))
    tc = _pick(wc, (512, 384, 256, 128))

    def body(c_ref, dm_ref, o_ref):
        a = jax.nn.silu(c_ref[...]).astype(BF16)
        o_ref[...] = lax.dot_general(a, dm_ref[...].astype(BF16), (((0,), (0,)), ((), ())), preferred_element_type=F32)

    return pl.pallas_call(
        body, name=name, grid=(l, d // tr, wc // tc),
        in_specs=[pl.BlockSpec((16, tr), lambda i, r, j: (0, r)), pl.BlockSpec((None, 16, tc), lambda i, r, j: (i, 0, j))],
        out_specs=pl.BlockSpec((None, tr, tc), lambda i, r, j: (i, r, j)), out_shape=jax.ShapeDtypeStruct((l, d, wc), F32),
        compiler_params=_params(("parallel", "parallel", "parallel")))(cs16, dm_sh)


def _mod_dc(dm_sh, w_mod, c_ctx, name):
    l, d, wc = w_mod.shape
    tk = _pick(wc, (512, 384, 256, 128))
    nk = wc // tk

    def body(dm_ref, w_ref, c_ref, o_ref, acc_ref):
        i, j = pl.program_id(0), pl.program_id(1)

        @pl.when(jnp.logical_and(i == 0, j == 0))
        def _():
            acc_ref[...] = jnp.zeros_like(acc_ref)

        acc_ref[...] += lax.dot_general(dm_ref[...].astype(BF16), w_ref[...].astype(BF16), (((1,), (1,)), ((), ())),
                                        preferred_element_type=F32)

        @pl.when(jnp.logical_and(i == l - 1, j == nk - 1))
        def _():
            mine = jnp.where(lax.axis_index("c") == 0, 1.0, 0.0)
            o_ref[...] = acc_ref[8:9, :] * _silu_grad(c_ref[...]) * mine

    return pl.pallas_call(
        body, name=name, grid=(l, nk),
        in_specs=[pl.BlockSpec((None, 16, tk), lambda i, j: (i, 0, j)), pl.BlockSpec((None, d, tk), lambda i, j: (i, 0, j)), _full_spec((1, d))],
        out_specs=_full_spec((1, d)), out_shape=jax.ShapeDtypeStruct((1, d), F32), scratch_shapes=[pltpu.VMEM((16, d), F32)],
        compiler_params=_params(("arbitrary", "arbitrary")))(dm_sh, w_mod, c_ctx)


def _dmod_assemble(gath, name):
    _, l, _, w = gath.shape
    gath = gath.transpose(1, 2, 0, 3)
    tc = _pick(w, (2048, 1024, 512, 256, 128))

    def body(lat_ref, ctx_ref, o_ref, b_ref):
        ctx = ctx_ref[0:1, :]
        for dev in range(1, 8):
            ctx = ctx + ctx_ref[dev:dev + 1, :]
        lat = lat_ref[...]
        o_ref[0:8, :] = lat
        o_ref[8:9, :] = ctx
        o_ref[9:16, :] = jnp.zeros((7, tc), F32)
        b_ref[...] = jnp.sum(lat, axis=0, keepdims=True) + ctx

    return pl.pallas_call(
        body, name=name, grid=(l, w // tc),
        in_specs=[pl.BlockSpec((None, None, 8, tc), lambda i, j: (i, 0, 0, j)), pl.BlockSpec((None, None, 8, tc), lambda i, j: (i, 1, 0, j))],
        out_specs=[pl.BlockSpec((None, 16, tc), lambda i, j: (i, 0, j)), pl.BlockSpec((None, 1, tc), lambda i, j: (i, 0, j))],
        out_shape=[jax.ShapeDtypeStruct((l, 16, w), F32), jax.ShapeDtypeStruct((l, 1, w), F32)],
        compiler_params=_params(("parallel", "parallel")))(gath, gath)


SMALL = ("c_ctx", "g_mix", "g_ffn", "mla_q_a_norm", "mla_kv_a_norm", "mla_q_nope_norm", "mla_q_rope_norm", "mla_k_nope_norm",
         "mla_k_rope_norm", "pool_w", "pool_scale", "swa_q_norm", "swa_k_norm", "swa_sink", "na_q_norm", "na_k_norm", "na_rpb",
         "ffn_conv_b")
PACK_W = 512
PACK_Q = 8 * PACK_W


def _pack(arrs):
    flat = []
    for a in arrs:
        f = a.reshape(-1)
        flat.append(jnp.pad(f, (0, (-f.size) % PACK_Q)))
    return jnp.concatenate(flat).reshape(-1, PACK_W)


def _unpack(packed, shapes):
    flat, out, o = packed.reshape(-1), [], 0
    for s in shapes:
        n = 1
        for dim in s:
            n *= dim
        out.append(flat[o:o + n].reshape(s))
        o += n + (-n) % PACK_Q
    return out


def _all_sum(p, name):
    pair = _pair_up([p], name + "_pair")[0]
    chip = _sum_lead(pair, name + "_sum2")
    return _sum_lead(_chip_gather([chip], ["lead"], name + "_gather")[0], name + "_sum4")


WEIGHTS = ("c_ctx", "w_mod", "b_mod", "g_mix", "g_ffn", "w_in", "w_out", "mla_q_a_norm", "mla_w_qb", "mla_kv_a_norm", "mla_w_kvb",
           "mla_q_nope_norm", "mla_q_rope_norm", "mla_k_nope_norm", "mla_k_rope_norm", "pool_w", "pool_scale", "swa_q_norm",
           "swa_k_norm", "swa_sink", "na_q_norm", "na_k_norm", "na_rpb", "ffn_w_up", "ffn_conv_w", "ffn_conv_b", "ffn_w_down")
BIG = ("w_in", "mla_w_qb", "w_out", "ffn_w_down", "ffn_w_up", "mla_w_kvb")
BIG_KINDS = ("cm", "cm", "cm", "cm", "lb", "lb")


def _step(a):
    x, c, ctx = a["x"], a["c"], a["ctx"]
    n_lat, d = x.shape[1], x.shape[2]
    n_ctx = ctx.shape[1]
    l = DEPTH
    px, py, pc = lax.axis_index("x"), lax.axis_index("y"), lax.axis_index("c")
    chip = 2 * px + py
    pvec = [p.reshape(1).astype(I32) for p in (pc, px, py)]
    cvec = pvec[0]

    bf = {k: a[k].astype(BF16) for k in BIG}
    w = {key: [None] * l for key in ("w_in", "w_qb", "w_out", "w_down", "w_up", "w_kvb")}

    finish = {"w_in": lambda g: _w_in_layout(g.transpose(1, 0, 2).reshape(d, P_COLS)),
              "mla_w_qb": lambda g: _w_qb_layout(g.transpose(1, 0, 2).reshape(512, 768)),
              "w_out": lambda g: g.reshape(-1, d), "ffn_w_down": lambda g: g.reshape(-1, d),
              "ffn_w_up": lambda g: g, "mla_w_kvb": lambda g: g}
    slot = dict(zip(BIG, ("w_in", "w_qb", "w_out", "w_down", "w_up", "w_kvb")))
    kind_of = dict(zip(BIG, BIG_GATHER))

    def gather_start(li, keys, tag, after):
        shards = [bf[k][li] for k in keys]
        shapes = [s.shape for s in shards]
        kinds = [kind_of[k] for k in keys]
        lands = [_place_own(s, kd, pvec[1:], "w_place") for s, kd in zip(shards, kinds)]
        plan = _w_gather_plan(shapes, kinds)
        return _remote_start(f"w_start_{li}{tag}", shards + lands, 3 * len(keys), plan, after) + (shapes, kinds, keys, plan)

    def gather_finish(li, started, tag, after):
        ssem, rsem, bufs, _, shapes, kinds, keys, plan = started
        bufs = _remote_wait(f"w_wait_{li}{tag}", ssem, rsem, bufs, 3 * len(keys), plan, after)
        for k, g in zip(keys, _w_fill(bufs[len(keys):], shapes, kinds, "w_fill")):
            w[slot[k]][li] = finish[k](g)

    first, rest = ("w_in", "mla_w_qb", "mla_w_kvb"), ("w_out", "ffn_w_down", "ffn_w_up")
    started = gather_start(0, first, "a", c)
    started_rest = gather_start(0, rest, "b", started[3])
    c = c + started_rest[3][0:1, 0:1]

    c_all = _chip_gather(_pair_up([c], "c_pair"), ["lead"], "c_gather")[0].reshape(8, d)
    cs16 = jnp.concatenate([c_all, a["c_ctx"][None, :], jnp.zeros((7, d), F32)], axis=0)
    wc = a["w_mod"].shape[-1]
    b_sh = lax.dynamic_slice_in_dim(a["b_mod"], chip * wc, wc, axis=1)[:, None, :]
    mod_sh = _mod_fwd(cs16, a["w_mod"], b_sh, "mod_fwd")
    mod_all, conv_w_full = _chip_gather([mod_sh, a["ffn_conv_w"]], ["lead", "lane"], "mod_gather")
    mod_all = mod_all.transpose(1, 2, 0, 3).reshape(l, 16, 4 * wc)
    mods = jnp.stack([lax.dynamic_index_in_dim(mod_all, 2 * chip + pc, axis=1, keepdims=False), mod_all[:, 8]], axis=1)
    mods = mods.reshape(l, 2, 6, d)

    full = {k: a[k] for k in SMALL if k != "c_ctx"}
    full["ffn_conv_w"] = conv_w_full
    w.update(_small_weights(full, mods))
    w["mods"] = [w["mods"][li] for li in range(l)]
    w["g_mix"] = [w["g_mix"][li] for li in range(l)]
    w["g_ffn"] = [w["g_ffn"][li] for li in range(l)]
    tab = _rope_table(n_lat, n_ctx)
    gather_finish(0, started, "a", mods)
    xs = jnp.concatenate([x[0], ctx[0]], axis=0)
    saved = []
    nxt = {}

    def rest_of_layer0(mix):
        gather_finish(0, started_rest, "b", mix)
        nxt[1] = gather_start(1, BIG, "", w["w_up"][0])
        w["g_ffn"][0] = w["g_ffn"][0] + nxt[1][3][0:1, 0:1]

    for li in range(l):
        if 1 <= li < l - 1:
            nxt[li + 1] = gather_start(li + 1, BIG, "", w["w_kvb"][li])
            w["g_mix"][li] = w["g_mix"][li] + nxt[li + 1][3][0:1, 0:1]
        xs, s = _layer_fwd(xs, w, li, tab, n_lat, before_out=rest_of_layer0 if li == 0 else None)
        saved.append(s)
        if li + 1 < l:
            gather_finish(li + 1, nxt[li + 1], "", xs)
    loss, dx = _loss_kernel(xs, a["loss_target"][0], n_lat // TM, "loss")
    loss = lax.psum(loss[0, 0], ("x", "y", "c"))

    grads = [None] * l
    g_big = [None] * len(BIG)

    widths = [a[k].shape[-1] for k in BIG]
    scatter_plan = _g_scatter_plan(BIG_KINDS, widths)

    def scatter_finish(li, started, after):
        ssem, rsem, bufs, _ = started
        bufs = _remote_wait(f"g_wait_{li}", ssem, rsem, bufs, 18, scatter_plan, after)
        return [_sum_into(own, landed, buf, li, pvec, kd, l, "g_sum4")
                for own, landed, buf, kd in zip(bufs[:6], bufs[6:], g_big, BIG_KINDS)]

    pending = None
    for li in reversed(range(l)):
        dx, grads[li] = _layer_bwd(dx, saved[li], w, li, tab, n_lat)
        if pending is not None:
            g_big = scatter_finish(li + 1, pending, dx)
        g = grads[li]
        ops = [_w_in_unlayout(g["w_in"]).reshape(d, 4, -1).transpose(1, 0, 2),
               _w_qb_unlayout(g["w_qb"].astype(GRAD_WIRE)).reshape(512, 4, 192).transpose(1, 0, 2),
               g["w_out"].reshape(4, -1, d), g["w_down"].reshape(4, -1, d), g["w_up"], g["w_kvb"].astype(GRAD_WIRE)]
        lands = [jnp.zeros((3, a[k].shape[1], wd), GRAD_WIRE) for k, wd in zip(BIG, widths)]
        pending = _remote_start(f"g_start_{li}", ops + lands, 18, scatter_plan, ops[0])
        if li > 0:
            w["mods"][li - 1] = w["mods"][li - 1] + pending[3][0, 0]

    dmods = jnp.stack([grads[li]["mods"] for li in range(l)], axis=0).reshape(l, 2, 6 * d)
    dmods = dmods + pending[3][0, 0]
    dm_gath = _chip_gather(_pair_up([dmods], "dmod_pair"), ["lead"], "dmod_gather")[0].reshape(8, l, 2, 6 * d)
    dmod_all, g_b_mod = _dmod_assemble(dm_gath, "dmod_assemble")
    dm_sh = lax.dynamic_slice_in_dim(dmod_all, chip * wc, wc, axis=2)
    g_w_mod = _mod_dw(cs16, dm_sh, "mod_dw")
    g_c_ctx = _mod_dc(dm_sh, a["w_mod"], a["c_ctx"][None, :], "mod_dc")
    g_out = {"w_mod": g_w_mod, "b_mod": g_b_mod.reshape(l, 6 * d)}

    rg = _reference_grads(grads, big=False)
    rg["c_ctx"] = g_c_ctx[0]
    packed = _all_sum(_pack([rg[k] for k in SMALL] + [rg["ffn_conv_w"]]) + pending[3][0, 0], "small")
    small_g = _unpack(packed, [a[k].shape for k in SMALL] + [rg["ffn_conv_w"].shape])
    for k, g in zip(SMALL, small_g[:-1]):
        g_out[k] = g
    cw = a["ffn_conv_w"].shape[-1]
    g_out["ffn_conv_w"] = lax.dynamic_slice_in_dim(small_g[-1], chip * cw, cw, axis=2)

    upd = {}
    pk = lambda pre: _pack([a[pre + k] for k in SMALL])
    outs = _adamw(pk(""), _pack([g_out[k] for k in SMALL]), pk("m_"), pk("v_"), "adamw_small")
    for o, kind in zip(outs, ("delta", "m", "v")):
        for k, val in zip(SMALL, _unpack(o, [a[k].shape for k in SMALL])):
            upd[kind, k] = val
    def adamw_each(keys):
        for k in keys:
            outs = _adamw(a[k], g_out[k], a["m_" + k], a["v_" + k], "adamw_" + k)
            for o, kind in zip(outs, ("delta", "m", "v")):
                upd[kind, k] = o

    adamw_each(("w_mod", "b_mod", "ffn_conv_w"))
    g_big = _sib_fill(scatter_finish(0, pending, upd["delta", "w_mod"]), lambda cc: (slice(None), cc), "g_pair")
    for k, g2 in zip(BIG, g_big):
        g_out[k], upd["delta", k], upd["m", k], upd["v", k] = _adamw_pair(a[k], g2, a["m_" + k], a["v_" + k], "adamw_" + k)
    grad_x = dx[0:n_lat].reshape(x.shape)
    return (loss, grad_x, *[g_out[k] for k in WEIGHTS], *[upd["delta", k] for k in WEIGHTS],
            *[upd["m", k] for k in WEIGHTS], *[upd["v", k] for k in WEIGHTS])


def kernel(x, c, ctx, c_ctx, w_mod, b_mod, g_mix, g_ffn, w_in, w_out, mla_q_a_norm, mla_w_qb, mla_kv_a_norm, mla_w_kvb, mla_q_nope_norm, mla_q_rope_norm, mla_k_nope_norm, mla_k_rope_norm, pool_w, pool_scale, swa_q_norm, swa_k_norm, swa_sink, na_q_norm, na_k_norm, na_rpb, ffn_w_up, ffn_conv_w, ffn_conv_b, ffn_w_down, loss_target, m_c_ctx, m_w_mod, m_b_mod, m_g_mix, m_g_ffn, m_w_in, m_w_out, m_mla_q_a_norm, m_mla_w_qb, m_mla_kv_a_norm, m_mla_w_kvb, m_mla_q_nope_norm, m_mla_q_rope_norm, m_mla_k_nope_norm, m_mla_k_rope_norm, m_pool_w, m_pool_scale, m_swa_q_norm, m_swa_k_norm, m_swa_sink, m_na_q_norm, m_na_k_norm, m_na_rpb, m_ffn_w_up, m_ffn_conv_w, m_ffn_conv_b, m_ffn_w_down, v_c_ctx, v_w_mod, v_b_mod, v_g_mix, v_g_ffn, v_w_in, v_w_out, v_mla_q_a_norm, v_mla_w_qb, v_mla_kv_a_norm, v_mla_w_kvb, v_mla_q_nope_norm, v_mla_q_rope_norm, v_mla_k_nope_norm, v_mla_k_rope_norm, v_pool_w, v_pool_scale, v_swa_q_norm, v_swa_k_norm, v_swa_sink, v_na_q_norm, v_na_k_norm, v_na_rpb, v_ffn_w_up, v_ffn_conv_w, v_ffn_conv_b, v_ffn_w_down):
    return _step(dict(locals()))
```

```python
import functools

import jax
import jax.numpy as jnp
from jax import lax
from jax.experimental import pallas as pl
from jax.experimental.pallas import tpu as pltpu

F32 = jnp.float32
BF16 = jnp.bfloat16
I32 = jnp.int32

DEPTH = 4
GRID_W = 64
ROPE_BASE = 10000.0
EPS = 1e-6
NEG = -1e30
MLA_SCALE = 192.0 ** -0.5
HD_SCALE = 64.0 ** -0.5
NA_KROWS = 12
SWA_KEYS = 512
P_COLS = 3648
PW = 3840
TM = 256
HALO = 8
ADAM_LR, ADAM_B1, ADAM_B2, ADAM_EPS, ADAM_WD, ADAM_STEP = 0.001, 0.9, 0.999, 1e-08, 0.01, 10
VMEM_LIMIT = 56 * 1024 * 1024
GRAD_WIRE = BF16
MESH = pl.DeviceIdType.MESH


def _pick(n, cands):
    for c in cands:
        if n % c == 0:
            return c
    return n


def _params(sem=None):
    return pltpu.CompilerParams(dimension_semantics=sem, vmem_limit_bytes=VMEM_LIMIT)


@jax.custom_vjp
def _bdot(a, b):
    return jnp.dot(a.astype(BF16), b.astype(BF16), preferred_element_type=F32)


def _bdot_fwd(a, b):
    return _bdot(a, b), (a.astype(BF16), b.astype(BF16))


def _bdot_bwd(res, g):
    a, b = res
    gb = g.astype(BF16)
    da = lax.dot_general(gb, b, (((1,), (1,)), ((), ())), preferred_element_type=F32)
    db = lax.dot_general(a, gb, (((0,), (0,)), ((), ())), preferred_element_type=F32)
    return da, db


_bdot.defvjp(_bdot_fwd, _bdot_bwd)


@jax.custom_vjp
def _bdot_nt(a, b):
    return lax.dot_general(a.astype(BF16), b.astype(BF16), (((1,), (1,)), ((), ())), preferred_element_type=F32)


def _bdot_nt_fwd(a, b):
    return _bdot_nt(a, b), (a.astype(BF16), b.astype(BF16))


def _bdot_nt_bwd(res, g):
    a, b = res
    gb = g.astype(BF16)
    da = jnp.dot(gb, b, preferred_element_type=F32)
    db = lax.dot_general(gb, a, (((0,), (0,)), ((), ())), preferred_element_type=F32)
    return da, db


_bdot_nt.defvjp(_bdot_nt_fwd, _bdot_nt_bwd)


def _rms(x, g):
    return x * lax.rsqrt(jnp.mean(x * x, axis=-1, keepdims=True) + EPS) * g


def _rope(x, cos, sin):
    xr = jnp.concatenate([-x[:, 16:32], x[:, 0:16], -x[:, 48:64], x[:, 32:48]], axis=-1)
    return x * cos + xr * sin


def _sel(is_ctx, mod, row):
    return jnp.where(is_ctx, mod[1, row:row + 1, :], mod[0, row:row + 1, :])


MM_VMEM_BUDGET = 40 * 1024 * 1024
HBM_BYTES_PER_STEP = 1 << 20


def _mm_tiles(m, n, k, mode, osize, n_unit, k_unit):
    lanes = (3840, 2816, 2048, 1280, 1024, 768, 512, 256)
    subl = (4352, 2176, 1088, 1024, 640, 544, 512, 256)
    tms = [c for c in (lanes if mode == "tn" else subl) if m % c == 0] or [m]
    tns = [c for c in lanes if n_unit % c == 0] or [n_unit]
    tks = [c for c in (subl if mode == "tn" else lanes) if k_unit % c == 0]
    if k_unit == k:
        tks = [k] + tks
    best = None
    for tm in tms:
        for tn in tns:
            for tk in tks:
                nk = k // tk
                vmem = 4 * (tm * tk + tk * tn) + 2 * tm * tn * osize + tm * tn * 4
                if vmem > MM_VMEM_BUDGET:
                    continue
                a_reads = 1 if nk == 1 else n // tn
                b_reads = 1 if (nk == 1 and n == tn) else m // tm
                steps = (m // tm) * (n // tn) * nk
                cost = (2 * m * k * a_reads + 2 * k * n * b_reads + m * n * osize + steps * HBM_BYTES_PER_STEP
                        + (12 * m * n * nk if nk > 1 else 0))
                if best is None or cost < best[0]:
                    best = (cost, tm, tn, tk)
    return best[1:]


def _mm(a, b, mode, out_dtype, name, a_split=False, b_split=False, o_split=False):
    def dims(x, split):
        return (x.shape[1], 2 * x.shape[2]) if split else x.shape

    ar, ac = dims(a, a_split)
    br, bc = dims(b, b_split)
    if mode == "nn":
        m, k, n = ar, ac, bc
        assert br == k
    elif mode == "nt":
        m, k, n = ar, ac, br
        assert bc == k
    else:
        k, m, n = ar, ac, bc
        assert br == k
    n_unit = n // 2 if (o_split or (b_split and mode != "nt")) else n
    k_unit = k // 2 if (mode != "tn" and (a_split or (b_split and mode == "nt"))) else k
    tm, tn, tk = _mm_tiles(m, n, k, mode, jnp.dtype(out_dtype).itemsize, n_unit, k_unit)
    nk = k // tk

    def spec(split, tr, tc, ncols, ridx, cidx):
        if not split:
            return pl.BlockSpec((tr, tc), lambda i, j, kk: (ridx(i, j, kk), cidx(i, j, kk)))
        nh = (ncols // 2) // tc
        return pl.BlockSpec((None, tr, tc), lambda i, j, kk: (cidx(i, j, kk) // nh, ridx(i, j, kk), cidx(i, j, kk) % nh))

    gi = lambda i, j, kk: i
    gj = lambda i, j, kk: j
    gk = lambda i, j, kk: kk
    if mode == "tn":
        a_spec = spec(a_split, tk, tm, ac, gk, gi)
    else:
        a_spec = spec(a_split, tm, tk, ac, gi, gk)
    if mode == "nt":
        b_spec = spec(b_split, tn, tk, bc, gj, gk)
    else:
        b_spec = spec(b_split, tk, tn, bc, gk, gj)
    o_spec = spec(o_split, tm, tn, n, gi, gj)
    dn = {"nn": (((1,), (0,)), ((), ())), "nt": (((1,), (1,)), ((), ())), "tn": (((0,), (0,)), ((), ()))}[mode]

    def body(a_ref, b_ref, o_ref, acc_ref):
        kk = pl.program_id(2)

        @pl.when(kk == 0)
        def _():
            acc_ref[...] = jnp.zeros_like(acc_ref)

        acc_ref[...] += lax.dot_general(a_ref[...], b_ref[...], dn, preferred_element_type=F32)

        @pl.when(kk == nk - 1)
        def _():
            o_ref[...] = acc_ref[...].astype(o_ref.dtype)

    def body_whole_k(a_ref, b_ref, o_ref):
        o_ref[...] = lax.dot_general(a_ref[...], b_ref[...], dn, preferred_element_type=F32).astype(o_ref.dtype)

    oshape = (2, m, n // 2) if o_split else (m, n)
    return pl.pallas_call(
        body if nk > 1 else body_whole_k, name=name, grid=(m // tm, n // tn, nk), in_specs=[a_spec, b_spec], out_specs=o_spec,
        out_shape=jax.ShapeDtypeStruct(oshape, out_dtype), scratch_shapes=[pltpu.VMEM((tm, tn), F32)] if nk > 1 else [],
        compiler_params=_params(("parallel", "parallel", "arbitrary")))(a, b)


def _mm_exact(a, b, name):
    def body(a_ref, b_ref, o_ref):
        o_ref[...] = jnp.dot(a_ref[...], b_ref[...], preferred_element_type=F32, precision=lax.Precision.HIGHEST)

    return pl.pallas_call(body, name=name, out_shape=jax.ShapeDtypeStruct((a.shape[0], b.shape[1]), F32),
                          compiler_params=_params())(a, b)


def _row_spec(width, col=0):
    return pl.BlockSpec((TM, width), lambda i: (i, col))


def _full_spec(shape):
    nd = len(shape)
    return pl.BlockSpec(shape, lambda *_: (0,) * nd)


def _normmod_fn(x, g, mod, is_ctx, row):
    return _rms(x, g) * (1.0 + _sel(is_ctx, mod, row + 1)) + _sel(is_ctx, mod, row)


def _normmod_fwd(x, g, mod, row, nbl, name):
    r, d = x.shape

    def body(x_ref, g_ref, mod_ref, h_ref):
        is_ctx = pl.program_id(0) >= nbl
        h_ref[...] = _normmod_fn(x_ref[...], g_ref[...], mod_ref[...], is_ctx, row).astype(BF16)

    return pl.pallas_call(
        body, name=name, grid=(r // TM,), in_specs=[_row_spec(d), _full_spec((1, d)), _full_spec(mod.shape)],
        out_specs=_row_spec(d), out_shape=jax.ShapeDtypeStruct((r, d), BF16), compiler_params=_params(("parallel",)))(x, g, mod)


def _normmod_bwd(x, g, mod, dh, dx_in, row, nbl, name):
    r, d = x.shape

    def body(x_ref, g_ref, mod_ref, dh_ref, dxin_ref, dx_ref, dg_ref, dmod_ref):
        i = pl.program_id(0)
        is_ctx = i >= nbl

        @pl.when(i == 0)
        def _():
            dg_ref[...] = jnp.zeros_like(dg_ref)
            dmod_ref[...] = jnp.zeros_like(dmod_ref)

        _, vjp = jax.vjp(lambda xx, gg, mm: _normmod_fn(xx, gg, mm, is_ctx, row), x_ref[...], g_ref[...], mod_ref[...])
        dx, dg, dmod = vjp(dh_ref[...])
        dx_ref[...] = dxin_ref[...] + dx
        dg_ref[...] += dg
        dmod_ref[...] += dmod

    return pl.pallas_call(
        body, name=name, grid=(r // TM,),
        in_specs=[_row_spec(d), _full_spec((1, d)), _full_spec(mod.shape), _row_spec(d), _row_spec(d)],
        out_specs=[_row_spec(d), _full_spec((1, d)), _full_spec(mod.shape)],
        out_shape=[jax.ShapeDtypeStruct((r, d), F32), jax.ShapeDtypeStruct((1, d), F32), jax.ShapeDtypeStruct(mod.shape, F32)],
        compiler_params=_params(("arbitrary",)))(x, g, mod, dh, dx_in)


def _resid_fwd(x, y, mod, row, nbl, name):
    r, d = x.shape

    def body(x_ref, y_ref, mod_ref, o_ref):
        is_ctx = pl.program_id(0) >= nbl
        o_ref[...] = x_ref[...] + _sel(is_ctx, mod_ref[...], row) * y_ref[...]

    return pl.pallas_call(
        body, name=name, grid=(r // TM,), in_specs=[_row_spec(d), _row_spec(d), _full_spec(mod.shape)],
        out_specs=_row_spec(d), out_shape=jax.ShapeDtypeStruct((r, d), F32), compiler_params=_params(("parallel",)))(x, y, mod)


def _resid_bwd(dx, y, mod, row, nbl, name):
    r, d = dx.shape

    def body(dx_ref, y_ref, mod_ref, dy_ref, dmod_ref):
        i = pl.program_id(0)
        is_ctx = i >= nbl

        @pl.when(i == 0)
        def _():
            dmod_ref[...] = jnp.zeros_like(dmod_ref)

        dxv = dx_ref[...]
        dy_ref[...] = (_sel(is_ctx, mod_ref[...], row) * dxv).astype(BF16)
        dgate = jnp.sum(dxv * y_ref[...], axis=0, keepdims=True)

        @pl.when(is_ctx)
        def _():
            dmod_ref[1, row:row + 1, :] += dgate

        @pl.when(jnp.logical_not(is_ctx))
        def _():
            dmod_ref[0, row:row + 1, :] += dgate

    return pl.pallas_call(
        body, name=name, grid=(r // TM,), in_specs=[_row_spec(d), _row_spec(d), _full_spec(mod.shape)],
        out_specs=[_row_spec(d), _full_spec(mod.shape)],
        out_shape=[jax.ShapeDtypeStruct((r, d), BF16), jax.ShapeDtypeStruct(mod.shape, F32)],
        compiler_params=_params(("arbitrary",)))(dx, y, mod)


def _loss_kernel(x, target, nbl, name):
    r, d = x.shape

    def body(x_ref, t_ref, loss_ref, dx_ref):
        i = pl.program_id(0)

        @pl.when(i == 0)
        def _():
            loss_ref[...] = jnp.zeros_like(loss_ref)

        @pl.when(i < nbl)
        def _():
            e = x_ref[...] - t_ref[...]
            dx_ref[...] = e / d
            loss_ref[...] += 0.5 * jnp.sum(jnp.mean(e * e, axis=-1, keepdims=True), axis=0, keepdims=True)

        @pl.when(i >= nbl)
        def _():
            dx_ref[...] = jnp.zeros_like(dx_ref)

    return pl.pallas_call(
        body, name=name, grid=(r // TM,),
        in_specs=[_row_spec(d), pl.BlockSpec((TM, d), lambda i: (jnp.minimum(i, nbl - 1), 0))],
        out_specs=[_full_spec((1, 1)), _row_spec(d)],
        out_shape=[jax.ShapeDtypeStruct((1, 1), F32), jax.ShapeDtypeStruct((r, d), F32)],
        compiler_params=_params(("arbitrary",)))(x, target)


SP_QA, SP_KVA, SP_QN, SP_QR, SP_KN, SP_KR, SP_SQ, SP_SK, SP_NQ, SP_NK, SP_SINK = range(11)
C_CQ, C_CKV, C_KR, C_POOL, C_SQ, C_SK, C_SV, C_NQ, C_NK, C_NV = 0, 512, 768, 896, 1408, 1920, 2048, 2176, 2688, 3200


def _prep_fn(p, tab, sp, wqb, wkvb):
    cos, sin = tab[:, 0:64], tab[:, 64:128]
    q = _bdot(_rms(p[:, C_CQ:C_CQ + 512], sp[SP_QA:SP_QA + 1, 0:512]), wqb)
    kv = _bdot(_rms(p[:, C_CKV:C_CKV + 256], sp[SP_KVA:SP_KVA + 1, 0:256]), wkvb)
    krr = _rope(_rms(p[:, C_KR:C_KR + 64], sp[SP_KR:SP_KR + 1, 0:64]), cos, sin)
    zero = jnp.zeros_like(krr)
    aq, ak, av = [], [], []
    for h in range(4):
        qn = _rms(q[:, 128 * h:128 * h + 128], sp[SP_QN:SP_QN + 1, 0:128])
        qr = _rope(_rms(q[:, 512 + 64 * h:576 + 64 * h], sp[SP_QR:SP_QR + 1, 0:64]), cos, sin)
        kn = _rms(kv[:, 256 * h:256 * h + 128], sp[SP_KN:SP_KN + 1, 0:128])
        aq += [qn, qr, zero]
        ak += [kn, krr, zero]
        av.append(kv[:, 256 * h + 128:256 * h + 256])
    cq = [_rope(_rms(p[:, C_SQ + 64 * h:C_SQ + 64 * h + 64], sp[SP_SQ:SP_SQ + 1, 0:64]), cos, sin) for h in range(8)]
    ck = [_rope(_rms(p[:, C_SK + 64 * h:C_SK + 64 * h + 64], sp[SP_SK:SP_SK + 1, 0:64]), cos, sin) for h in range(2)]
    dq = [_rms(p[:, C_NQ + 64 * h:C_NQ + 64 * h + 64], sp[SP_NQ:SP_NQ + 1, 0:64]) for h in range(8)]
    dk = [_rms(p[:, C_NK + 64 * h:C_NK + 64 * h + 64], sp[SP_NK:SP_NK + 1, 0:64]) for h in range(8)]
    cat = lambda xs: jnp.concatenate(xs, axis=-1)
    return (cat(aq), cat(ak), cat(av), cat(cq), cat(ck), p[:, C_SV:C_SV + 128], cat(dq), cat(dk), p[:, C_NV:C_NV + 512])


PREP_WIDTHS = (1024, 1024, 512, 512, 128, 128, 512, 512, 512)


def _prep_fwd(p, tab, sp, wqb, wkvb, name):
    r = p.shape[0]

    def body(p_ref, tab_ref, sp_ref, wqb_ref, wkvb_ref, *outs):
        res = _prep_fn(p_ref[...], tab_ref[...], sp_ref[...], wqb_ref[...].astype(F32), wkvb_ref[...].astype(F32))
        for o_ref, v in zip(outs, res):
            o_ref[...] = v.astype(BF16)

    return pl.pallas_call(
        body, name=name, grid=(r // TM,),
        in_specs=[_row_spec(PW), _row_spec(128), _full_spec(sp.shape), _full_spec(wqb.shape), _full_spec(wkvb.shape)],
        out_specs=[_row_spec(w) for w in PREP_WIDTHS],
        out_shape=[jax.ShapeDtypeStruct((r, w), BF16) for w in PREP_WIDTHS],
        compiler_params=_params(("parallel",)))(p, tab, sp, wqb, wkvb)


def _prep_bwd(p, tab, sp, wqb, wkvb, cots, dpool, name):
    r = p.shape[0]

    def body(p_ref, tab_ref, sp_ref, wqb_ref, wkvb_ref, *rest):
        cot_refs, dpool_ref = rest[:9], rest[9]
        dp_ref, dsp_ref, dwqb_ref, dwkvb_ref = rest[10:]
        i = pl.program_id(0)

        @pl.when(i == 0)
        def _():
            dsp_ref[...] = jnp.zeros_like(dsp_ref)
            dwqb_ref[...] = jnp.zeros_like(dwqb_ref)
            dwkvb_ref[...] = jnp.zeros_like(dwkvb_ref)

        tab = tab_ref[...]
        _, vjp = jax.vjp(lambda pp, ss, wq, wk: _prep_fn(pp, tab, ss, wq, wk),
                         p_ref[...], sp_ref[...], wqb_ref[...].astype(F32), wkvb_ref[...].astype(F32))
        dp, dsp, dwq, dwk = vjp(tuple(c[...] for c in cot_refs))
        dp_ref[...] = dp.astype(BF16)
        dp_ref[:, C_POOL:C_POOL + 512] = dpool_ref[...].astype(BF16)
        dsp_ref[...] += dsp
        dwqb_ref[...] += dwq
        dwkvb_ref[...] += dwk

    return pl.pallas_call(
        body, name=name, grid=(r // TM,),
        in_specs=[_row_spec(PW), _row_spec(128), _full_spec(sp.shape), _full_spec(wqb.shape), _full_spec(wkvb.shape)]
        + [_row_spec(w) for w in PREP_WIDTHS] + [_row_spec(512)],
        out_specs=[_row_spec(PW), _full_spec(sp.shape), _full_spec(wqb.shape), _full_spec(wkvb.shape)],
        out_shape=[jax.ShapeDtypeStruct((r, PW), BF16), jax.ShapeDtypeStruct(sp.shape, F32),
                   jax.ShapeDtypeStruct(wqb.shape, F32), jax.ShapeDtypeStruct(wkvb.shape, F32)],
        compiler_params=_params(("arbitrary",)))(p, tab, sp, wqb, wkvb, *cots, dpool)


def _mla_probs(q, k, is_ctx, n_lat):
    s = lax.dot_general(q, k, (((1,), (1,)), ((), ())), preferred_element_type=F32) * MLA_SCALE
    kid = lax.broadcasted_iota(I32, (1, s.shape[1]), 1)
    s = s + jnp.where(jnp.logical_and(is_ctx, kid < n_lat), NEG, 0.0)
    e = jnp.exp(s - jnp.max(s, axis=-1, keepdims=True))
    return e * (1.0 / jnp.sum(e, axis=-1, keepdims=True))


def _mla_fwd(aq, ak, av, n_lat, name):
    r = aq.shape[0]
    nbl = n_lat // TM

    def body(q_ref, k_ref, v_ref, o_ref):
        p = _mla_probs(q_ref[...], k_ref[...], pl.program_id(1) >= nbl, n_lat)
        o_ref[...] = jnp.dot(p.astype(BF16), v_ref[...], preferred_element_type=F32).astype(BF16)

    return pl.pallas_call(
        body, name=name, grid=(4, r // TM),
        in_specs=[pl.BlockSpec((TM, 256), lambda h, i: (i, h)), pl.BlockSpec((r, 256), lambda h, i: (0, h)),
                  pl.BlockSpec((r, 128), lambda h, i: (0, h))],
        out_specs=pl.BlockSpec((TM, 128), lambda h, i: (i, h)),
        out_shape=jax.ShapeDtypeStruct((r, 512), BF16), compiler_params=_params(("parallel", "parallel")))(aq, ak, av)


def _mla_bwd(aq, ak, av, dmix, n_lat, name):
    r = aq.shape[0]
    nbl = n_lat // TM

    def body(q_ref, k_ref, v_ref, do_ref, dq_ref, dk_ref, dv_ref):
        i = pl.program_id(1)

        @pl.when(i == 0)
        def _():
            dk_ref[...] = jnp.zeros_like(dk_ref)
            dv_ref[...] = jnp.zeros_like(dv_ref)

        q, k, v = q_ref[...], k_ref[...], v_ref[...]
        dob = do_ref[...].astype(BF16)
        p = _mla_probs(q, k, i >= nbl, n_lat)
        dv_ref[...] += lax.dot_general(p.astype(BF16), dob, (((0,), (0,)), ((), ())), preferred_element_type=F32)
        dp = lax.dot_general(dob, v, (((1,), (1,)), ((), ())), preferred_element_type=F32)
        ds = (p * (dp - jnp.sum(dp * p, axis=-1, keepdims=True)) * MLA_SCALE).astype(BF16)
        dq_ref[...] = jnp.dot(ds, k, preferred_element_type=F32)
        dk_ref[...] += lax.dot_general(ds, q, (((0,), (0,)), ((), ())), preferred_element_type=F32)

    return pl.pallas_call(
        body, name=name, grid=(4, r // TM),
        in_specs=[pl.BlockSpec((TM, 256), lambda h, i: (i, h)), pl.BlockSpec((r, 256), lambda h, i: (0, h)),
                  pl.BlockSpec((r, 128), lambda h, i: (0, h)), pl.BlockSpec((TM, 128), lambda h, i: (i, h))],
        out_specs=[pl.BlockSpec((TM, 256), lambda h, i: (i, h)), pl.BlockSpec((r, 256), lambda h, i: (0, h)),
                   pl.BlockSpec((r, 128), lambda h, i: (0, h))],
        out_shape=[jax.ShapeDtypeStruct((r, 1024), F32), jax.ShapeDtypeStruct((r, 1024), F32), jax.ShapeDtypeStruct((r, 512), F32)],
        compiler_params=_params(("parallel", "arbitrary")))(aq, ak, av, dmix)


def _pool_fn(ext, w, sc, gid0, grp, is_ctx, n_lat, r_all):
    gid = gid0 + lax.broadcasted_iota(I32, (TM + 2 * HALO, 1), 0)
    lo = jnp.where(is_ctx, n_lat, 0)
    hi = jnp.where(is_ctx, r_all, n_lat)
    z = jnp.where(jnp.logical_and(gid >= lo, gid < hi), ext, 0.0)
    w2 = jnp.roll(z, 1, axis=0) + z
    w4 = jnp.roll(w2, 1, axis=0) + jnp.roll(w2, -1, axis=0)
    w8 = jnp.roll(w4, 2, axis=0) + jnp.roll(w4, -2, axis=0)
    w16 = jnp.roll(w8, 4, axis=0) + jnp.roll(w8, -4, axis=0)
    win = jnp.where(grp == 0, w2, jnp.where(grp == 1, w4, jnp.where(grp == 2, w8, w16)))
    half = jnp.left_shift(1, grp)
    cnt = jnp.maximum(jnp.minimum(gid + half, hi) - jnp.maximum(gid - half, lo), 1).astype(F32)
    d = (win / cnt - ext)[HALO:HALO + TM]
    return _bdot(d, w) * sc


def _pool_ext(u_ref, i, r_all):
    s0 = pl.multiple_of(jnp.maximum(i * TM - HALO, 0), HALO)
    s2 = pl.multiple_of(jnp.minimum(i * TM + TM, r_all - HALO), HALO)
    ext = jnp.concatenate([u_ref[pl.ds(s0, HALO), :], u_ref[pl.ds(pl.multiple_of(i * TM, TM), TM), :], u_ref[pl.ds(s2, HALO), :]], axis=0)
    return ext, s0, s2


def _pool_specs(r):
    return [pl.BlockSpec((r, 128), lambda g, i: (0, C_POOL // 128 + g)), pl.BlockSpec((None, 128, 128), lambda g, i: (g, 0, 0)),
            pl.BlockSpec((None, 1, 128), lambda g, i: (g, 0, 0))]


def _pool_fwd(p, pool_w, pool_sc, n_lat, name):
    r = p.shape[0]
    nbl = n_lat // TM

    def body(u_ref, w_ref, sc_ref, o_ref):
        g, i = pl.program_id(0), pl.program_id(1)
        ext, _, _ = _pool_ext(u_ref, i, r)
        o_ref[...] = _pool_fn(ext, w_ref[...], sc_ref[...], i * TM - HALO, g, i >= nbl, n_lat, r).astype(BF16)

    return pl.pallas_call(
        body, name=name, grid=(4, r // TM), in_specs=_pool_specs(r), out_specs=pl.BlockSpec((TM, 128), lambda g, i: (i, g)),
        out_shape=jax.ShapeDtypeStruct((r, 512), BF16), compiler_params=_params(("parallel", "parallel")))(p, pool_w, pool_sc)


def _pool_bwd(p, pool_w, pool_sc, dmix, n_lat, name):
    r = p.shape[0]
    nbl = n_lat // TM

    def body(u_ref, w_ref, sc_ref, do_ref, du_ref, dw_ref, dsc_ref):
        g, i = pl.program_id(0), pl.program_id(1)

        @pl.when(i == 0)
        def _():
            du_ref[...] = jnp.zeros_like(du_ref)
            dw_ref[...] = jnp.zeros_like(dw_ref)
            dsc_ref[...] = jnp.zeros_like(dsc_ref)

        ext, s0, s2 = _pool_ext(u_ref, i, r)
        _, vjp = jax.vjp(lambda e, w, s: _pool_fn(e, w, s, i * TM - HALO, g, i >= nbl, n_lat, r), ext, w_ref[...], sc_ref[...])
        dext, dw, dsc = vjp(do_ref[...])
        du_ref[pl.ds(s0, HALO), :] += dext[0:HALO]
        du_ref[pl.ds(pl.multiple_of(i * TM, TM), TM), :] += dext[HALO:HALO + TM]
        du_ref[pl.ds(s2, HALO), :] += dext[HALO + TM:]
        dw_ref[...] += dw
        dsc_ref[...] += dsc

    return pl.pallas_call(
        body, name=name, grid=(4, r // TM), in_specs=_pool_specs(r) + [pl.BlockSpec((TM, 128), lambda g, i: (i, 4 + g))],
        out_specs=[pl.BlockSpec((r, 128), lambda g, i: (0, g)), pl.BlockSpec((None, 128, 128), lambda g, i: (g, 0, 0)),
                   pl.BlockSpec((None, 1, 128), lambda g, i: (g, 0, 0))],
        out_shape=[jax.ShapeDtypeStruct((r, 512), F32), jax.ShapeDtypeStruct((4, 128, 128), F32), jax.ShapeDtypeStruct((4, 1, 128), F32)],
        compiler_params=_params(("parallel", "arbitrary")))(p, pool_w, pool_sc, dmix)


def _softmax_parts(parts, extra=None):
    m = functools.reduce(jnp.maximum, [jnp.max(s, axis=-1, keepdims=True) for s in parts])
    if extra is not None:
        m = jnp.maximum(m, extra)
    m = lax.stop_gradient(m)
    es = [jnp.exp(s - m) for s in parts]
    den = functools.reduce(jnp.add, [jnp.sum(e, axis=-1, keepdims=True) for e in es])
    if extra is not None:
        den = den + jnp.exp(extra - m)
    inv = 1.0 / den
    return [e * inv for e in es]


def _swa_band(qpos0, kpos0, is_ctx):
    qpos = qpos0 + lax.broadcasted_iota(I32, (TM, SWA_KEYS), 0)
    kpos = kpos0 + lax.broadcasted_iota(I32, (TM, SWA_KEYS), 1)
    valid = jnp.logical_and(jnp.abs(kpos - qpos) <= 128, jnp.logical_not(is_ctx))
    band = jnp.where(valid, 0.0, NEG)
    return jnp.concatenate([band] * 4, axis=0)


def _swa_fn(q4, kw, vw, kc, vc, sink4, band):
    qs = jnp.concatenate([q4[:, 64 * g:64 * g + 64] for g in range(4)], axis=0)
    s_loc = _bdot_nt(qs, kw) * HD_SCALE + band
    s_ctx = _bdot_nt(qs, kc) * HD_SCALE
    sink = jnp.concatenate([jnp.broadcast_to(sink4[:, g:g + 1], (TM, 1)) for g in range(4)], axis=0)
    p_loc, p_ctx = _softmax_parts([s_loc, s_ctx], sink)
    o = _bdot(p_loc, vw) + _bdot(p_ctx, vc)
    return jnp.concatenate([o[TM * g:TM * (g + 1)] for g in range(4)], axis=1)


def _swa_window(i, n_lat):
    return pl.multiple_of(jnp.clip(i * TM - 128, 0, n_lat - SWA_KEYS), 128)


def _swa_fwd(cq, ck, cv, sp, n_lat, name):
    r = cq.shape[0]
    nbl = n_lat // TM

    def body(q_ref, k_ref, v_ref, sp_ref, o_ref):
        i = pl.program_id(0)
        k0 = _swa_window(i, n_lat)
        kw, vw = k_ref[pl.ds(k0, SWA_KEYS), :].astype(F32), v_ref[pl.ds(k0, SWA_KEYS), :].astype(F32)
        kc, vc = k_ref[pl.ds(n_lat, r - n_lat), :].astype(F32), v_ref[pl.ds(n_lat, r - n_lat), :].astype(F32)
        band = _swa_band(i * TM, k0, i >= nbl)
        for j in range(2):
            c = slice(64 * j, 64 * j + 64)
            o = _swa_fn(q_ref[:, 256 * j:256 * j + 256].astype(F32), kw[:, c], vw[:, c], kc[:, c], vc[:, c],
                        sp_ref[SP_SINK:SP_SINK + 1, 4 * j:4 * j + 4], band)
            o_ref[:, 256 * j:256 * j + 256] = o.astype(BF16)

    return pl.pallas_call(
        body, name=name, grid=(r // TM,),
        in_specs=[_row_spec(512), _full_spec((r, 128)), _full_spec((r, 128)), _full_spec(sp.shape)],
        out_specs=_row_spec(512), out_shape=jax.ShapeDtypeStruct((r, 512), BF16), compiler_params=_params(("parallel",)))(cq, ck, cv, sp)


def _swa_bwd(cq, ck, cv, sp, dmix, n_lat, name):
    r = cq.shape[0]
    nbl = n_lat // TM
    nc = r - n_lat

    def body(q_ref, k_ref, v_ref, sp_ref, do_ref, dq_ref, dk_ref, dv_ref, dsp_ref):
        i = pl.program_id(0)

        @pl.when(i == 0)
        def _():
            dk_ref[...] = jnp.zeros_like(dk_ref)
            dv_ref[...] = jnp.zeros_like(dv_ref)
            dsp_ref[...] = jnp.zeros_like(dsp_ref)

        k0 = _swa_window(i, n_lat)
        kw, vw = k_ref[pl.ds(k0, SWA_KEYS), :].astype(F32), v_ref[pl.ds(k0, SWA_KEYS), :].astype(F32)
        kc, vc = k_ref[pl.ds(n_lat, nc), :].astype(F32), v_ref[pl.ds(n_lat, nc), :].astype(F32)
        dkw, dvw, dkc, dvc, dsk = [], [], [], [], []
        band = _swa_band(i * TM, k0, i >= nbl)
        for j in range(2):
            c = slice(64 * j, 64 * j + 64)
            _, vjp = jax.vjp(lambda q4, a, b, cc, d, s: _swa_fn(q4, a, b, cc, d, s, band),
                             q_ref[:, 256 * j:256 * j + 256].astype(F32), kw[:, c], vw[:, c], kc[:, c], vc[:, c],
                             sp_ref[SP_SINK:SP_SINK + 1, 4 * j:4 * j + 4])
            dq4, a, b, cc, d, s = vjp(do_ref[:, 256 * j:256 * j + 256])
            dq_ref[:, 256 * j:256 * j + 256] = dq4
            dkw.append(a), dvw.append(b), dkc.append(cc), dvc.append(d), dsk.append(s)
        cat = lambda xs: jnp.concatenate(xs, axis=1)
        dk_ref[pl.ds(k0, SWA_KEYS), :] += cat(dkw)
        dv_ref[pl.ds(k0, SWA_KEYS), :] += cat(dvw)
        dk_ref[pl.ds(n_lat, nc), :] += cat(dkc)
        dv_ref[pl.ds(n_lat, nc), :] += cat(dvc)
        dsp_ref[SP_SINK:SP_SINK + 1, 0:8] += cat(dsk)

    return pl.pallas_call(
        body, name=name, grid=(r // TM,),
        in_specs=[_row_spec(512), _full_spec((r, 128)), _full_spec((r, 128)), _full_spec(sp.shape), _row_spec(512, 2)],
        out_specs=[_row_spec(512), _full_spec((r, 128)), _full_spec((r, 128)), _full_spec(sp.shape)],
        out_shape=[jax.ShapeDtypeStruct((r, 512), F32), jax.ShapeDtypeStruct((r, 128), F32), jax.ShapeDtypeStruct((r, 128), F32),
                   jax.ShapeDtypeStruct(sp.shape, F32)],
        compiler_params=_params(("arbitrary",)))(cq, ck, cv, sp, dmix)


def _na_fn(q, kw, vw, kc, vc, bias):
    s_loc = _bdot_nt(q, kw) * HD_SCALE + bias
    s_ctx = _bdot_nt(q, kc) * HD_SCALE
    p_loc, p_ctx = _softmax_parts([s_loc, s_ctx])
    return _bdot(p_loc, vw) + _bdot(p_ctx, vc)


NA_MASKED = 15


def _na_geometry(i, n_lat, is_ctx):
    rows = n_lat // GRID_W
    qrow0 = i * (TM // GRID_W)
    krow0 = jnp.clip(qrow0 - 4, 0, rows - NA_KROWS)
    dr = []
    for qi in range(TM // GRID_W):
        r_lo = jnp.clip(qrow0 + qi - 4, 0, rows - 8)
        row = []
        for kj in range(NA_KROWS):
            kr = krow0 + kj
            ok = jnp.logical_and(jnp.logical_and(kr >= r_lo, kr < r_lo + 8), jnp.logical_not(is_ctx))
            row.append(jnp.where(ok, kr - (qrow0 + qi) + 7, NA_MASKED))
        dr.append(row)
    return pl.multiple_of(krow0 * GRID_W, GRID_W), dr


def _na_bias(t1_ref, hh, dr):
    return jnp.concatenate([jnp.concatenate([t1_ref[hh, dr[qi][kj]] for kj in range(NA_KROWS)], axis=1)
                            for qi in range(TM // GRID_W)], axis=0)


def _na_specs(r):
    return [pl.BlockSpec((TM, 128), lambda pr, i: (i, pr)), pl.BlockSpec((r, 128), lambda pr, i: (0, pr)),
            pl.BlockSpec((r, 128), lambda pr, i: (0, pr)), pl.BlockSpec((2, 16, GRID_W, GRID_W), lambda pr, i: (pr, 0, 0, 0))]


def _na_fwd(dq, dk, dv, t1, n_lat, name):
    r = dq.shape[0]
    nbl = n_lat // TM
    nc = r - n_lat
    nk = NA_KROWS * GRID_W

    def body(q_ref, k_ref, v_ref, t1_ref, o_ref):
        i = pl.program_id(1)
        k0, dr = _na_geometry(jnp.minimum(i, nbl - 1), n_lat, i >= nbl)
        kw, vw = k_ref[pl.ds(k0, nk), :].astype(F32), v_ref[pl.ds(k0, nk), :].astype(F32)
        kc, vc = k_ref[pl.ds(n_lat, nc), :].astype(F32), v_ref[pl.ds(n_lat, nc), :].astype(F32)
        for hh in range(2):
            c = slice(64 * hh, 64 * hh + 64)
            o = _na_fn(q_ref[:, c].astype(F32), kw[:, c], vw[:, c], kc[:, c], vc[:, c], _na_bias(t1_ref, hh, dr))
            o_ref[:, c] = o.astype(BF16)

    return pl.pallas_call(
        body, name=name, grid=(4, r // TM), in_specs=_na_specs(r), out_specs=pl.BlockSpec((TM, 128), lambda pr, i: (i, pr)),
        out_shape=jax.ShapeDtypeStruct((r, 512), BF16), compiler_params=_params(("parallel", "parallel")))(dq, dk, dv, t1)


def _na_bwd(dq, dk, dv, t1, dmix, n_lat, name):
    r = dq.shape[0]
    nbl = n_lat // TM
    nc = r - n_lat
    nk = NA_KROWS * GRID_W

    def body(q_ref, k_ref, v_ref, t1_ref, do_ref, dq_ref, dk_ref, dv_ref, dt1_ref):
        i = pl.program_id(1)

        @pl.when(i == 0)
        def _():
            dk_ref[...] = jnp.zeros_like(dk_ref)
            dv_ref[...] = jnp.zeros_like(dv_ref)
            dt1_ref[...] = jnp.zeros_like(dt1_ref)

        k0, dr = _na_geometry(jnp.minimum(i, nbl - 1), n_lat, i >= nbl)
        kw, vw = k_ref[pl.ds(k0, nk), :].astype(F32), v_ref[pl.ds(k0, nk), :].astype(F32)
        kc, vc = k_ref[pl.ds(n_lat, nc), :].astype(F32), v_ref[pl.ds(n_lat, nc), :].astype(F32)
        dkw, dvw, dkc, dvc = [], [], [], []
        for hh in range(2):
            c = slice(64 * hh, 64 * hh + 64)
            _, vjp = jax.vjp(_na_fn,
                             q_ref[:, c].astype(F32), kw[:, c], vw[:, c], kc[:, c], vc[:, c], _na_bias(t1_ref, hh, dr))
            dqh, a, b, cc, d, dbias = vjp(do_ref[:, c])
            dq_ref[:, c] = dqh
            dkw.append(a), dvw.append(b), dkc.append(cc), dvc.append(d)
            for qi in range(TM // GRID_W):
                for kj in range(NA_KROWS):
                    dt1_ref[hh, dr[qi][kj]] += dbias[GRID_W * qi:GRID_W * (qi + 1), GRID_W * kj:GRID_W * (kj + 1)]
        cat = lambda xs: jnp.concatenate(xs, axis=1)
        dk_ref[pl.ds(k0, nk), :] += cat(dkw)
        dv_ref[pl.ds(k0, nk), :] += cat(dvw)
        dk_ref[pl.ds(n_lat, nc), :] += cat(dkc)
        dv_ref[pl.ds(n_lat, nc), :] += cat(dvc)

    return pl.pallas_call(
        body, name=name, grid=(4, r // TM), in_specs=_na_specs(r) + [pl.BlockSpec((TM, 128), lambda pr, i: (i, 12 + pr))],
        out_specs=[pl.BlockSpec((TM, 128), lambda pr, i: (i, pr)), pl.BlockSpec((r, 128), lambda pr, i: (0, pr)),
                   pl.BlockSpec((r, 128), lambda pr, i: (0, pr)), pl.BlockSpec((2, 16, GRID_W, GRID_W), lambda pr, i: (pr, 0, 0, 0))],
        out_shape=[jax.ShapeDtypeStruct((r, 512), F32)] * 3 + [jax.ShapeDtypeStruct((8, 16, GRID_W, GRID_W), F32)],
        compiler_params=_params(("parallel", "arbitrary")))(dq, dk, dv, t1, dmix)


def _conv_ext(main_ref, prev_ref, next_ref, edges):
    prev_ok, next_ok = edges
    return jnp.concatenate([jnp.where(prev_ok, prev_ref[...], 0.0), main_ref[...], jnp.where(next_ok, next_ref[...], 0.0)], axis=0)


def _conv_edges(i, nbl, nb):
    return jnp.logical_and(i != 0, i != nbl), jnp.logical_and(i != nbl - 1, i != nb - 1)


def _conv_apply(ext, w, b):
    up = jnp.roll(ext, 1, axis=0)
    dn = jnp.roll(ext, -1, axis=0)
    return up * w[0:1] + ext * w[1:2] + dn * w[2:3] + b, up, dn


def _conv_in_specs(tc, r):
    nb8 = TM // HALO
    last8 = r // HALO - 1

    def trio(half):
        return [pl.BlockSpec((None, TM, tc), lambda j, i: (half, i, j)),
                pl.BlockSpec((None, HALO, tc), lambda j, i: (half, jnp.maximum(i * nb8 - 1, 0), j)),
                pl.BlockSpec((None, HALO, tc), lambda j, i: (half, jnp.minimum((i + 1) * nb8, last8), j))]

    wb = [pl.BlockSpec((None, 3, tc), lambda j, i: (0, 0, j)), pl.BlockSpec((None, 3, tc), lambda j, i: (1, 0, j)),
          pl.BlockSpec((None, 1, tc), lambda j, i: (0, 0, j)), pl.BlockSpec((None, 1, tc), lambda j, i: (1, 0, j))]
    return trio(0) + trio(1) + wb


def _convgate_fwd(a3, cw, cb, n_lat, name):
    _, r, ff = a3.shape
    nbl = n_lat // TM
    tc = _pick(ff, (512, 256, 128))

    def body(g_ref, gp_ref, gn_ref, v_ref, vp_ref, vn_ref, wg_ref, wv_ref, bg_ref, bv_ref, u_ref):
        edges = _conv_edges(pl.program_id(1), nbl, r // TM)
        gg, _, _ = _conv_apply(_conv_ext(g_ref, gp_ref, gn_ref, edges), wg_ref[...], bg_ref[...])
        gv, _, _ = _conv_apply(_conv_ext(v_ref, vp_ref, vn_ref, edges), wv_ref[...], bv_ref[...])
        u_ref[...] = (jax.nn.silu(gg[HALO:HALO + TM]) * gv[HALO:HALO + TM]).astype(BF16)

    return pl.pallas_call(
        body, name=name, grid=(ff // tc, r // TM), in_specs=_conv_in_specs(tc, r),
        out_specs=pl.BlockSpec((TM, tc), lambda j, i: (i, j)), out_shape=jax.ShapeDtypeStruct((r, ff), BF16),
        compiler_params=_params(("parallel", "parallel")))(a3, a3, a3, a3, a3, a3, cw, cw, cb, cb)


def _convgate_bwd(a3, cw, cb, du, n_lat, name):
    _, r, ff = a3.shape
    nbl = n_lat // TM
    tc = _pick(ff, (512, 256, 128))
    nb8 = TM // HALO
    last8 = r // HALO - 1

    def body(g_ref, gp_ref, gn_ref, v_ref, vp_ref, vn_ref, wg_ref, wv_ref, bg_ref, bv_ref, du_ref, dup_ref, dun_ref,
             da_ref, dcw_ref, dcb_ref):
        i = pl.program_id(1)

        @pl.when(i == 0)
        def _():
            dcw_ref[...] = jnp.zeros_like(dcw_ref)
            dcb_ref[...] = jnp.zeros_like(dcb_ref)

        edges = _conv_edges(i, nbl, r // TM)
        wg, wv = wg_ref[...], wv_ref[...]
        eg, ev = _conv_ext(g_ref, gp_ref, gn_ref, edges), _conv_ext(v_ref, vp_ref, vn_ref, edges)
        gg, ug, dg_ = _conv_apply(eg, wg, bg_ref[...])
        gv, uv, dv_ = _conv_apply(ev, wv, bv_ref[...])
        due = _conv_ext(du_ref, dup_ref, dun_ref, edges)
        sg = jax.nn.sigmoid(gg)
        dgg = due * gv * (sg * (1.0 + gg * (1.0 - sg)))
        dgv = due * (gg * sg)
        main = slice(HALO, HALO + TM)
        for h, (dgx, w, ex, upx, dnx) in enumerate(((dgg, wg, eg, ug, dg_), (dgv, wv, ev, uv, dv_))):
            da = dgx * w[1:2] + jnp.roll(dgx, -1, axis=0) * w[0:1] + jnp.roll(dgx, 1, axis=0) * w[2:3]
            da_ref[h] = da[main].astype(BF16)
            dm = dgx[main]
            dcw_ref[h, 0:1, :] += jnp.sum(dm * upx[main], axis=0, keepdims=True)
            dcw_ref[h, 1:2, :] += jnp.sum(dm * ex[main], axis=0, keepdims=True)
            dcw_ref[h, 2:3, :] += jnp.sum(dm * dnx[main], axis=0, keepdims=True)
            dcb_ref[h] += jnp.sum(dm, axis=0, keepdims=True)

    du_specs = [pl.BlockSpec((TM, tc), lambda j, i: (i, j)),
                pl.BlockSpec((HALO, tc), lambda j, i: (jnp.maximum(i * nb8 - 1, 0), j)),
                pl.BlockSpec((HALO, tc), lambda j, i: (jnp.minimum((i + 1) * nb8, last8), j))]
    return pl.pallas_call(
        body, name=name, grid=(ff // tc, r // TM), in_specs=_conv_in_specs(tc, r) + du_specs,
        out_specs=[pl.BlockSpec((2, TM, tc), lambda j, i: (0, i, j)), pl.BlockSpec((2, 3, tc), lambda j, i: (0, 0, j)),
                   pl.BlockSpec((2, 1, tc), lambda j, i: (0, 0, j))],
        out_shape=[jax.ShapeDtypeStruct((2, r, ff), BF16), jax.ShapeDtypeStruct((2, 3, ff), F32), jax.ShapeDtypeStruct((2, 1, ff), F32)],
        compiler_params=_params(("parallel", "arbitrary")))(a3, a3, a3, a3, a3, a3, cw, cw, cb, cb, du, du, du)


def _layer_fwd(x, w, l, tab, n_lat, before_out=None):
    nbl = n_lat // TM
    mod = w["mods"][l]
    h1 = _normmod_fwd(x, w["g_mix"][l], mod, 0, nbl, "normmod_fwd")
    p = _mm(h1, w["w_in"][l], "nn", F32, "mm_in")
    qkv = _prep_fwd(p, tab, w["sp"][l], w["w_qb"][l], w["w_kvb"][l], "prep_fwd")
    oa = _mla_fwd(qkv[0], qkv[1], qkv[2], n_lat, "mla_fwd")
    ob = _pool_fwd(p, w["pool_w"][l], w["pool_sc"][l], n_lat, "pool_fwd")
    oc = _swa_fwd(qkv[3], qkv[4], qkv[5], w["sp"][l], n_lat, "swa_fwd")
    od = _na_fwd(qkv[6], qkv[7], qkv[8], w["t1"][l], n_lat, "na_fwd")
    mix = jnp.concatenate([oa, ob, oc, od], axis=1)
    if before_out is not None:
        before_out(mix)
    y = _mm(mix, w["w_out"][l], "nn", F32, "mm_out")
    x1 = _resid_fwd(x, y, mod, 2, nbl, "resid_fwd")
    h2 = _normmod_fwd(x1, w["g_ffn"][l], mod, 3, nbl, "normmod_fwd")
    a3 = _mm(h2, w["w_up"][l], "nn", F32, "mm_up", o_split=True)
    u = _convgate_fwd(a3, w["conv_w"][l], w["conv_b"][l], n_lat, "convgate_fwd")
    y2 = _mm(u, w["w_down"][l], "nn", F32, "mm_down")
    x2 = _resid_fwd(x1, y2, mod, 5, nbl, "resid_fwd")
    return x2, dict(x=x, h1=h1, p=p, qkv=qkv, mix=mix, y=y, x1=x1, h2=h2, a3=a3, u=u, y2=y2)


def _layer_bwd(dx, s, w, l, tab, n_lat, after_ffn=None):
    nbl = n_lat // TM
    mod = w["mods"][l]
    g = {}
    dy2, dmod_a = _resid_bwd(dx, s["y2"], mod, 5, nbl, "resid_bwd")
    g["w_down"] = _mm(s["u"], dy2, "tn", GRAD_WIRE, "mm_dwdown")
    du = _mm(dy2, w["w_down"][l], "nt", F32, "mm_du")
    da3, g["conv_w"], g["conv_b"] = _convgate_bwd(s["a3"], w["conv_w"][l], w["conv_b"][l], du, n_lat, "convgate_bwd")
    g["w_up"] = _mm(s["h2"], da3, "tn", GRAD_WIRE, "mm_dwup", b_split=True)
    dh2 = _mm(da3, w["w_up"][l], "nt", F32, "mm_dh2", a_split=True)
    g_ffn = w["g_ffn"][l] if after_ffn is None else w["g_ffn"][l] + after_ffn(g)[0:1, 0:1]
    dx1, g["g_ffn"], dmod_b = _normmod_bwd(s["x1"], g_ffn, mod, dh2, dx, 3, nbl, "normmod_bwd")
    dy, dmod_c = _resid_bwd(dx1, s["y"], mod, 2, nbl, "resid_bwd")
    g["w_out"] = _mm(s["mix"], dy, "tn", GRAD_WIRE, "mm_dwout")
    dmix = _mm(dy, w["w_out"][l], "nt", F32, "mm_dmix")
    qkv = s["qkv"]
    daq, dak, dav = _mla_bwd(qkv[0], qkv[1], qkv[2], dmix, n_lat, "mla_bwd")
    dpool, g["pool_w"], g["pool_sc"] = _pool_bwd(s["p"], w["pool_w"][l], w["pool_sc"][l], dmix, n_lat, "pool_bwd")
    dcq, dck, dcv, dsp_c = _swa_bwd(qkv[3], qkv[4], qkv[5], w["sp"][l], dmix, n_lat, "swa_bwd")
    ddq, ddk, ddv, g["t1"] = _na_bwd(qkv[6], qkv[7], qkv[8], w["t1"][l], dmix, n_lat, "na_bwd")
    dp, dsp_p, g["w_qb"], g["w_kvb"] = _prep_bwd(s["p"], tab, w["sp"][l], w["w_qb"][l], w["w_kvb"][l],
                                                (daq, dak, dav, dcq, dck, dcv, ddq, ddk, ddv), dpool, "prep_bwd")
    g["sp"] = dsp_c + dsp_p
    g["w_in"] = _mm(s["h1"], dp, "tn", GRAD_WIRE, "mm_dwin")
    dh1 = _mm(dp, w["w_in"][l], "nt", F32, "mm_dh1")
    dx0, g["g_mix"], dmod_d = _normmod_bwd(s["x"], w["g_mix"][l], mod, dh1, dx1, 0, nbl, "normmod_bwd")
    g["mods"] = dmod_a + dmod_b + dmod_c + dmod_d
    return dx0, g


def _local_step(x_all, target, w, tab, n_lat):
    saved = []
    x = x_all
    for l in range(DEPTH):
        x, s = _layer_fwd(x, w, l, tab, n_lat)
        saved.append(s)
    loss, dx = _loss_kernel(x, target, n_lat // TM, "loss")
    grads = [None] * DEPTH
    for l in reversed(range(DEPTH)):
        dx, grads[l] = _layer_bwd(dx, saved[l], w, l, tab, n_lat)
    return loss[0, 0], dx, grads


def _pad_cols(a, widths):
    parts, o = [], 0
    for take, pad in widths:
        parts.append(a[..., o:o + take])
        if pad:
            parts.append(jnp.zeros(a.shape[:-1] + (pad,), a.dtype))
        o += take
    return jnp.concatenate(parts, axis=-1)


def _w_in_layout(w_in):
    return _pad_cols(w_in, [(832, 64), (P_COLS - 832, PW - P_COLS - 64)])


def _w_in_unlayout(g):
    return jnp.concatenate([g[..., 0:832], g[..., 896:896 + P_COLS - 832]], axis=-1)


def _w_qb_layout(w):
    s = w.reshape(w.shape[:-1] + (4, 192))
    return jnp.concatenate([s[..., 0:128].reshape(w.shape[:-1] + (512,)), s[..., 128:192].reshape(w.shape[:-1] + (256,))], axis=-1)


def _w_qb_unlayout(g):
    n = g[..., 0:512].reshape(g.shape[:-1] + (4, 128))
    r = g[..., 512:768].reshape(g.shape[:-1] + (4, 64))
    return jnp.concatenate([n, r], axis=-1).reshape(g.shape[:-1] + (768,))


SP_SLOTS = (("mla_q_a_norm", 512), ("mla_kv_a_norm", 256), ("mla_q_nope_norm", 128), ("mla_q_rope_norm", 64),
            ("mla_k_nope_norm", 128), ("mla_k_rope_norm", 64), ("swa_q_norm", 64), ("swa_k_norm", 64),
            ("na_q_norm", 64), ("na_k_norm", 64), ("swa_sink", 8))


def _sp_pack(small):
    rows = [jnp.pad(small[k], ((0, 0), (0, 512 - n))) for k, n in SP_SLOTS]
    rows += [jnp.zeros_like(rows[0])] * (16 - len(rows))
    return jnp.stack(rows, axis=1)


def _sp_unpack(sp):
    return {k: sp[:, i, 0:n] for i, (k, n) in enumerate(SP_SLOTS)}


def _rpb_onehot():
    qc = lax.broadcasted_iota(I32, (GRID_W, GRID_W), 0)
    kc = lax.broadcasted_iota(I32, (GRID_W, GRID_W), 1)
    dc = (jnp.clip(kc - qc, -15, 15) + 15).reshape(1, GRID_W * GRID_W)
    return (lax.broadcasted_iota(I32, (32, GRID_W * GRID_W), 0) == dc).astype(F32)


def _rpb_expand(rpb):
    l = rpb.shape[0]
    flat = jnp.pad(rpb, ((0, 0), (0, 0), (0, 1), (0, 1))).reshape(l * 128, 32)
    t1 = _mm_exact(flat, _rpb_onehot(), "rpb_expand").reshape(l, 8, 16, GRID_W, GRID_W)
    qc = lax.broadcasted_iota(I32, (GRID_W, GRID_W), 0)
    kc = lax.broadcasted_iota(I32, (GRID_W, GRID_W), 1)
    c_lo = jnp.clip(qc - 8, 0, GRID_W - 16)
    col_ok = jnp.logical_and(kc >= c_lo, kc < c_lo + 16)
    row_ok = lax.broadcasted_iota(I32, (16, 1, 1), 0) != NA_MASKED
    return jnp.where(jnp.logical_and(col_ok[None], row_ok), t1, NEG)


def _rpb_fold(dt1):
    l = dt1.shape[0]
    g = _mm_exact(dt1.reshape(l * 128, GRID_W * GRID_W), _rpb_onehot().T, "rpb_fold")
    return g.reshape(l, 8, 16, 32)[:, :, 0:15, 0:31]


def _rope_table(n_lat, n_ctx):
    t = jnp.arange(n_lat)
    inv = ROPE_BASE ** (-jnp.arange(0, 32, 2, dtype=F32) / 32)
    ar = (t // GRID_W).astype(F32)[:, None] * inv
    ac = (t % GRID_W).astype(F32)[:, None] * inv
    cos = jnp.concatenate([jnp.cos(ar), jnp.cos(ar), jnp.cos(ac), jnp.cos(ac)], axis=-1)
    sin = jnp.concatenate([jnp.sin(ar), jnp.sin(ar), jnp.sin(ac), jnp.sin(ac)], axis=-1)
    tab = jnp.concatenate([cos, sin], axis=-1)
    ident = jnp.concatenate([jnp.ones((n_ctx, 64), F32), jnp.zeros((n_ctx, 64), F32)], axis=-1)
    return jnp.concatenate([tab, ident], axis=0)


def _small_weights(full, mods):
    l = full["g_mix"].shape[0]
    ff = full["ffn_conv_b"].shape[1] // 2
    return dict(
        mods=mods, g_mix=full["g_mix"][:, None, :], g_ffn=full["g_ffn"][:, None, :],
        sp=_sp_pack(full), pool_w=full["pool_w"], pool_sc=full["pool_scale"].reshape(l, 4, 1, 128),
        t1=_rpb_expand(full["na_rpb"]),
        conv_w=full["ffn_conv_w"].reshape(l, 3, 2, ff).transpose(0, 2, 1, 3),
        conv_b=full["ffn_conv_b"].reshape(l, 2, 1, ff))


def _kernel_weights(full, mods):
    w = _small_weights(full, mods)
    w.update(w_in=_w_in_layout(full["w_in"]).astype(BF16), w_out=full["w_out"].astype(BF16),
             w_up=full["ffn_w_up"].astype(BF16), w_down=full["ffn_w_down"].astype(BF16),
             w_qb=_w_qb_layout(full["mla_w_qb"]).astype(BF16), w_kvb=full["mla_w_kvb"].astype(BF16))
    return w


def _reference_grads(grads, big=True):
    st = lambda k: jnp.stack([g[k] for g in grads], axis=0)
    l = len(grads)
    out = dict(
        g_mix=st("g_mix")[:, 0], g_ffn=st("g_ffn")[:, 0],
        pool_w=st("pool_w"), pool_scale=st("pool_sc").reshape(l, 512), na_rpb=_rpb_fold(st("t1")),
        ffn_conv_w=st("conv_w").transpose(0, 2, 1, 3).reshape(l, 3, -1), ffn_conv_b=st("conv_b").reshape(l, -1),
        mods=st("mods"))
    if big:
        out.update(w_in=_w_in_unlayout(st("w_in")), w_out=st("w_out"), ffn_w_up=st("w_up"), ffn_w_down=st("w_down"),
                   mla_w_qb=_w_qb_unlayout(st("w_qb")), mla_w_kvb=st("w_kvb"))
    out.update(_sp_unpack(st("sp")))
    return out


ANY = pl.BlockSpec(memory_space=pl.ANY)


def _flip(x, y, j):
    return (1 - x if j >> 1 else x), (1 - y if j & 1 else y)


def _comm_call(name, ins, out_shapes, n_copies, plan, aliases=None):
    n_in, n_out = len(ins), len(out_shapes)

    def body(*refs):
        in_refs, out_refs = refs[:n_in], refs[n_in:n_in + n_out]
        ssem, rsem = refs[n_in + n_out:]
        pos = (lax.axis_index("x"), lax.axis_index("y"), lax.axis_index("c"))
        copies = plan(in_refs, out_refs, pos)
        assert len(copies) == n_copies
        descs = []
        for i, (src, dst, peer) in enumerate(copies):
            if peer is None:
                d = pltpu.make_async_copy(src, dst, ssem.at[i])
            else:
                d = pltpu.make_async_remote_copy(src_ref=src, dst_ref=dst, send_sem=ssem.at[i], recv_sem=rsem.at[i],
                                                 device_id=peer, device_id_type=MESH)
            d.start()
            descs.append(d)
        for d in descs:
            d.wait()

    return pl.pallas_call(
        body, name=name, in_specs=[ANY] * n_in, out_specs=[ANY] * n_out, out_shape=list(out_shapes),
        input_output_aliases=aliases or {},
        scratch_shapes=[pltpu.SemaphoreType.DMA((n_copies,)), pltpu.SemaphoreType.DMA((n_copies,))])(*ins)


def _sib_fill(bufs, part, name):
    def plan(ins, outs, pos):
        x, y, c = pos
        return [(o_ref.at[part(c)], o_ref.at[part(c)], (x, y, 1 - c)) for o_ref in outs]

    shapes = [jax.ShapeDtypeStruct(b.shape, b.dtype) for b in bufs]
    return _comm_call(name, bufs, shapes, len(bufs), plan, aliases={i: i for i in range(len(bufs))})


HBM = pl.BlockSpec(memory_space=pltpu.HBM)
SEM = pl.BlockSpec(memory_space=pltpu.SEMAPHORE)
DATAFLOW = pltpu.SideEffectType.DATAFLOW_SIDE_EFFECTING


def _remote_start(name, bufs, n_copies, plan, after):
    nb, na = len(bufs), len(after)

    def body(*refs):
        ssem, rsem, token = refs[nb + na], refs[nb + na + 1], refs[-1]
        pos = (lax.axis_index("x"), lax.axis_index("y"), lax.axis_index("c"))
        copies = plan(refs[:nb], pos)
        assert len(copies) == n_copies
        for i, (src, dst, peer) in enumerate(copies):
            pltpu.make_async_remote_copy(src_ref=src, dst_ref=dst, send_sem=ssem.at[i], recv_sem=rsem.at[i],
                                         device_id=peer, device_id_type=MESH).start()
        token[...] = jnp.zeros_like(token)

    outs = pl.pallas_call(
        body, name=name,
        out_shape=(pltpu.SemaphoreType.DMA((n_copies,)), pltpu.SemaphoreType.DMA((n_copies,)),
                   *[pltpu.HBM(b.shape, b.dtype) for b in bufs], jax.ShapeDtypeStruct((8, 128), F32)),
        in_specs=[HBM] * nb + [ANY] * na, out_specs=(SEM, SEM, *[HBM] * nb, pl.BlockSpec(memory_space=pltpu.VMEM)),
        input_output_aliases={i: 2 + i for i in range(nb)},
        compiler_params=pltpu.CompilerParams(has_side_effects=DATAFLOW),
    )(*[pltpu.with_memory_space_constraint(b, pltpu.HBM) for b in bufs], *after)
    return outs[0], outs[1], list(outs[2:2 + nb]), outs[-1]


def _remote_wait(name, ssem, rsem, bufs, n_copies, plan, after):
    nb = len(bufs)

    def body(*refs):
        ssem_ref, rsem_ref = refs[nb], refs[nb + 1]
        pos = (lax.axis_index("x"), lax.axis_index("y"), lax.axis_index("c"))
        copies = plan(refs[:nb], pos)
        assert len(copies) == n_copies
        for i, (src, dst, peer) in enumerate(copies):
            cp = pltpu.make_async_remote_copy(src_ref=src, dst_ref=dst, send_sem=ssem_ref.at[i], recv_sem=rsem_ref.at[i],
                                              device_id=peer, device_id_type=MESH)
            cp.wait_send()
            cp.wait_recv()

    outs = pl.pallas_call(
        body, name=name, out_shape=tuple(pltpu.HBM(b.shape, b.dtype) for b in bufs),
        in_specs=[HBM] * nb + [SEM, SEM, ANY], out_specs=tuple([HBM] * nb), input_output_aliases={i: i for i in range(nb)},
        compiler_params=pltpu.CompilerParams(has_side_effects=DATAFLOW),
    )(*bufs, ssem, rsem, after)
    return list(outs)


BIG_GATHER = ("axis1", "axis1", "axis1", "axis1", "lane", "lane")


def _place_own(shard, kind, xyvec, name):
    rows, cols = shard.shape
    tr = _row_tile(rows, cols)
    if kind == "axis1":
        shape = (4, rows, cols)
        o_spec = pl.BlockSpec((None, tr, cols), lambda i, x_ref, y_ref: (2 * x_ref[0] + y_ref[0], i, 0))
    else:
        shape = (rows, 4 * cols)
        o_spec = pl.BlockSpec((tr, cols), lambda i, x_ref, y_ref: (i, 2 * x_ref[0] + y_ref[0]))

    def body(x_ref, y_ref, s_ref, o_ref):
        o_ref[...] = s_ref[...]

    return pl.pallas_call(
        body, name=name,
        grid_spec=pltpu.PrefetchScalarGridSpec(num_scalar_prefetch=2, grid=(rows // tr,),
                                               in_specs=[pl.BlockSpec((tr, cols), lambda i, x_ref, y_ref: (i, 0))], out_specs=o_spec),
        out_shape=jax.ShapeDtypeStruct(shape, shard.dtype), compiler_params=_params(("parallel",)))(*xyvec, shard)


def _w_gather_plan(shapes, kinds):
    n = len(kinds)

    def plan(refs, pos):
        x, y, c = pos
        k = 2 * x + y
        cps = []
        for s_ref, l_ref, kind, shp in zip(refs[:n], refs[n:], kinds, shapes):
            h, w = shp[0] // 2, shp[1]
            rows = pl.ds(pl.multiple_of(c * h, 16), h)
            dst = l_ref.at[k, rows, :] if kind == "axis1" else l_ref.at[rows, pl.ds(pl.multiple_of(k * w, 128), w)]
            for j in (1, 2, 3):
                tx, ty = _flip(x, y, j)
                cps.append((s_ref.at[rows, :], dst, (tx, ty, c)))
        return cps

    return plan


def _w_fill(lands, shapes, kinds, name):
    def plan(ins, outs, pos):
        x, y, c = pos
        cps = []
        for o_ref, kind, shp in zip(outs, kinds, shapes):
            h = shp[0] // 2
            rows = pl.ds(pl.multiple_of(c * h, 16), h)
            part = o_ref.at[:, rows, :] if kind == "axis1" else o_ref.at[rows, :]
            cps.append((part, part, (x, y, 1 - c)))
        return cps

    return _comm_call(name, lands, [jax.ShapeDtypeStruct(b.shape, b.dtype) for b in lands], len(lands), plan,
                      aliases={i: i for i in range(len(lands))})


def _g_scatter_plan(kinds, widths):
    n = len(kinds)

    def plan(refs, pos):
        x, y, c = pos
        cps = []
        for s_ref, l_ref, kind, w in zip(refs[:n], refs[n:], kinds, widths):
            for j in (1, 2, 3):
                tx, ty = _flip(x, y, j)
                kj = 2 * tx + ty
                src = s_ref.at[kj] if kind == "cm" else s_ref.at[:, pl.ds(pl.multiple_of(kj * w, 128), w)]
                cps.append((src, l_ref.at[j - 1], (tx, ty, c)))
        return cps

    return plan


def _pair_up(xs, name):
    def plan(ins, outs, pos):
        x, y, c = pos
        cps = []
        for i_ref, o_ref in zip(ins, outs):
            cps.append((i_ref, o_ref.at[c], None))
            cps.append((i_ref, o_ref.at[c], (x, y, 1 - c)))
        return cps

    return _comm_call(name, xs, [jax.ShapeDtypeStruct((2,) + a.shape, a.dtype) for a in xs], 2 * len(xs), plan)


def _chip_gather(xs, kinds, name):
    def dst(o_ref, kind, k, x_shape):
        if kind == "lead":
            return o_ref.at[k]
        w = x_shape[-1]
        return o_ref.at[(slice(None),) * (len(x_shape) - 1) + (pl.ds(pl.multiple_of(k * w, 128), w),)]

    def plan(ins, outs, pos):
        x, y, c = pos
        k = 2 * x + y
        cps = []
        for i_ref, o_ref, kind, a in zip(ins, outs, kinds, xs):
            cps.append((i_ref, dst(o_ref, kind, k, a.shape), None))
            for j in (1, 2, 3):
                tx, ty = _flip(x, y, j)
                cps.append((i_ref, dst(o_ref, kind, k, a.shape), (tx, ty, c)))
        return cps

    def oshape(a, kind):
        return (4,) + a.shape if kind == "lead" else a.shape[:-1] + (4 * a.shape[-1],)

    return _comm_call(name, xs, [jax.ShapeDtypeStruct(oshape(a, kd), a.dtype) for a, kd in zip(xs, kinds)], 4 * len(xs), plan)


def _row_tile(rows, cols, budget=1 << 20):
    for t in (2048, 1024, 512, 256, 128, 64, 32, 16, 8):
        if rows % t == 0 and t * cols * 4 <= budget:
            return t
    return rows


def _sum_lead(x, name):
    n, rows, w = x.shape
    tr = _row_tile(rows, w, (1 << 21) // n)

    def body(x_ref, o_ref):
        acc = x_ref[0].astype(F32)
        for j in range(1, n):
            acc = acc + x_ref[j].astype(F32)
        o_ref[...] = acc

    return pl.pallas_call(
        body, name=name, grid=(rows // tr,), in_specs=[pl.BlockSpec((n, tr, w), lambda i: (0, i, 0))],
        out_specs=pl.BlockSpec((tr, w), lambda i: (i, 0)), out_shape=jax.ShapeDtypeStruct((rows, w), F32),
        compiler_params=_params(("parallel",)))(x)


def _sum_into(g, landed, buf, layer, pvec, kind, layers, name):
    n, rows, w = landed.shape
    tr = _row_tile(rows, w, 1 << 19)

    def body(c_ref, x_ref, y_ref, own_ref, l_ref, *refs):
        o_ref = refs[-1]
        acc = own_ref[...].astype(F32)
        for j in range(n):
            acc = acc + l_ref[j].astype(F32)
        o_ref[...] = acc

    if kind == "cm":
        own_spec = pl.BlockSpec((None, tr, w), lambda i, c_ref, x_ref, y_ref: (2 * x_ref[0] + y_ref[0], i, 0))
    else:
        own_spec = pl.BlockSpec((tr, w), lambda i, c_ref, x_ref, y_ref: (i, 2 * x_ref[0] + y_ref[0]))
    in_specs = [own_spec, pl.BlockSpec((n, tr, w), lambda i, c_ref, x_ref, y_ref: (0, i, 0))]
    args = [*pvec, g, landed]
    if buf is not None:
        in_specs.append(ANY)
        args.append(buf)
    return pl.pallas_call(
        body, name=name,
        grid_spec=pltpu.PrefetchScalarGridSpec(
            num_scalar_prefetch=3, grid=(rows // tr,), in_specs=in_specs,
            out_specs=pl.BlockSpec((None, None, tr, w), lambda i, c_ref, x_ref, y_ref: (layer, c_ref[0], i, 0))),
        out_shape=jax.ShapeDtypeStruct((layers, 2, rows, w), F32), input_output_aliases={} if buf is None else {5: 0},
        compiler_params=_params(("arbitrary",)))(*args)


def _adamw_math(w, g, m, v):
    mn = ADAM_B1 * m + (1.0 - ADAM_B1) * g
    vn = ADAM_B2 * v + (1.0 - ADAM_B2) * jnp.square(g)
    m_hat = mn / (1.0 - ADAM_B1 ** ADAM_STEP)
    v_hat = vn / (1.0 - ADAM_B2 ** ADAM_STEP)
    return -ADAM_LR * (m_hat / (jnp.sqrt(v_hat) + ADAM_EPS) + ADAM_WD * w), mn, vn


def _adamw_pair(w, g2, m, v, name):
    l, rows, cols = w.shape
    tr = _row_tile(rows, cols, 1 << 19)

    def body(w_ref, g0_ref, g1_ref, m_ref, v_ref, g_ref, d_ref, mo_ref, vo_ref):
        gv = g0_ref[...] + g1_ref[...]
        g_ref[...] = gv
        d_ref[...], mo_ref[...], vo_ref[...] = _adamw_math(w_ref[...], gv, m_ref[...], v_ref[...])

    spec = pl.BlockSpec((None, tr, cols), lambda li, i: (li, i, 0))
    half = lambda cc: pl.BlockSpec((None, None, tr, cols), lambda li, i: (li, cc, i, 0))
    return pl.pallas_call(
        body, name=name, grid=(l, rows // tr), in_specs=[spec, half(0), half(1), spec, spec], out_specs=[spec] * 4,
        out_shape=[jax.ShapeDtypeStruct(w.shape, F32)] * 4, compiler_params=_params(("parallel", "parallel")))(w, g2, g2, m, v)


def _adamw(w, g, m, v, name):
    shape = w.shape
    cols = shape[-1]
    rows = w.size // cols
    tr = _row_tile(rows, cols, 1 << 19)

    def body(w_ref, g_ref, m_ref, v_ref, d_ref, mo_ref, vo_ref):
        d_ref[...], mo_ref[...], vo_ref[...] = _adamw_math(w_ref[...], g_ref[...], m_ref[...], v_ref[...])

    spec = pl.BlockSpec((tr, cols), lambda i: (i, 0))
    outs = pl.pallas_call(
        body, name=name, grid=(rows // tr,), in_specs=[spec] * 4, out_specs=[spec] * 3,
        out_shape=[jax.ShapeDtypeStruct((rows, cols), F32)] * 3,
        compiler_params=_params(("parallel",)))(*[a.reshape(rows, cols) for a in (w, g, m, v)])
    return [o.reshape(shape) for o in outs]


def _silu_grad(x):
    s = jax.nn.sigmoid(x)
    return s * (1.0 + x * (1.0 - s))


def _mod_fwd(cs16, w_mod, b_sh, name):
    l, d, wc = w_mod.shape
    tn = _pick(wc, (512, 384, 256, 128))

    def body(c_ref, w_ref, b_ref, o_ref):
        a = jax.nn.silu(c_ref[...]).astype(BF16)
        o_ref[...] = jnp.dot(a, w_ref[...].astype(BF16), preferred_element_type=F32) + b_ref[...]

    return pl.pallas_call(
        body, name=name, grid=(l, wc // tn),
        in_specs=[_full_spec((16, d)), pl.BlockSpec((None, d, tn), lambda i, j: (i, 0, j)), pl.BlockSpec((None, 1, tn), lambda i, j: (i, 0, j))],
        out_specs=pl.BlockSpec((None, 16, tn), lambda i, j: (i, 0, j)), out_shape=jax.ShapeDtypeStruct((l, 16, wc), F32),
        compiler_params=_params(("parallel", "parallel")))(cs16, w_mod, b_sh)


def _mod_dw(cs16, dm_sh, name):
    l, _, wc = dm_sh.shape
    d = cs16.shape[1]
    tr = _pick(d, (512, 256, 128))
    tc = _pick(wc, (512, 384, 256, 128))

    def body(c_ref, dm_ref, o_ref):
        a = jax.nn.silu(c_ref[...]).astype(BF16)
        o_ref[...] = lax.dot_general(a, dm_ref[...].astype(BF16), (((0,), (0,)), ((), ())), preferred_element_type=F32)

    return pl.pallas_call(
        body, name=name, grid=(l, d // tr, wc // tc),
        in_specs=[pl.BlockSpec((16, tr), lambda i, r, j: (0, r)), pl.BlockSpec((None, 16, tc), lambda i, r, j: (i, 0, j))],
        out_specs=pl.BlockSpec((None, tr, tc), lambda i, r, j: (i, r, j)), out_shape=jax.ShapeDtypeStruct((l, d, wc), F32),
        compiler_params=_params(("parallel", "parallel", "parallel")))(cs16, dm_sh)


def _mod_dc(dm_sh, w_mod, c_ctx, name):
    l, d, wc = w_mod.shape
    tk = _pick(wc, (512, 384, 256, 128))
    nk = wc // tk

    def body(dm_ref, w_ref, c_ref, o_ref, acc_ref):
        i, j = pl.program_id(0), pl.program_id(1)

        @pl.when(jnp.logical_and(i == 0, j == 0))
        def _():
            acc_ref[...] = jnp.zeros_like(acc_ref)

        acc_ref[...] += lax.dot_general(dm_ref[...].astype(BF16), w_ref[...].astype(BF16), (((1,), (1,)), ((), ())),
                                        preferred_element_type=F32)

        @pl.when(jnp.logical_and(i == l - 1, j == nk - 1))
        def _():
            mine = jnp.where(lax.axis_index("c") == 0, 1.0, 0.0)
            o_ref[...] = acc_ref[8:9, :] * _silu_grad(c_ref[...]) * mine

    return pl.pallas_call(
        body, name=name, grid=(l, nk),
        in_specs=[pl.BlockSpec((None, 16, tk), lambda i, j: (i, 0, j)), pl.BlockSpec((None, d, tk), lambda i, j: (i, 0, j)), _full_spec((1, d))],
        out_specs=_full_spec((1, d)), out_shape=jax.ShapeDtypeStruct((1, d), F32), scratch_shapes=[pltpu.VMEM((16, d), F32)],
        compiler_params=_params(("arbitrary", "arbitrary")))(dm_sh, w_mod, c_ctx)


def _dmod_assemble(gath, name):
    _, l, _, w = gath.shape
    gath = gath.transpose(1, 2, 0, 3)
    tc = _pick(w, (2048, 1024, 512, 256, 128))

    def body(lat_ref, ctx_ref, o_ref, b_ref):
        ctx = ctx_ref[0:1, :]
        for dev in range(1, 8):
            ctx = ctx + ctx_ref[dev:dev + 1, :]
        lat = lat_ref[...]
        o_ref[0:8, :] = lat
        o_ref[8:9, :] = ctx
        o_ref[9:16, :] = jnp.zeros((7, tc), F32)
        b_ref[...] = jnp.sum(lat, axis=0, keepdims=True) + ctx

    return pl.pallas_call(
        body, name=name, grid=(l, w // tc),
        in_specs=[pl.BlockSpec((None, None, 8, tc), lambda i, j: (i, 0, 0, j)), pl.BlockSpec((None, None, 8, tc), lambda i, j: (i, 1, 0, j))],
        out_specs=[pl.BlockSpec((None, 16, tc), lambda i, j: (i, 0, j)), pl.BlockSpec((None, 1, tc), lambda i, j: (i, 0, j))],
        out_shape=[jax.ShapeDtypeStruct((l, 16, w), F32), jax.ShapeDtypeStruct((l, 1, w), F32)],
        compiler_params=_params(("parallel", "parallel")))(gath, gath)


SMALL = ("c_ctx", "g_mix", "g_ffn", "mla_q_a_norm", "mla_kv_a_norm", "mla_q_nope_norm", "mla_q_rope_norm", "mla_k_nope_norm",
         "mla_k_rope_norm", "pool_w", "pool_scale", "swa_q_norm", "swa_k_norm", "swa_sink", "na_q_norm", "na_k_norm", "na_rpb",
         "ffn_conv_b")
PACK_W = 512
PACK_Q = 8 * PACK_W


def _pack(arrs):
    flat = []
    for a in arrs:
        f = a.reshape(-1)
        flat.append(jnp.pad(f, (0, (-f.size) % PACK_Q)))
    return jnp.concatenate(flat).reshape(-1, PACK_W)


def _unpack(packed, shapes):
    flat, out, o = packed.reshape(-1), [], 0
    for s in shapes:
        n = 1
        for dim in s:
            n *= dim
        out.append(flat[o:o + n].reshape(s))
        o += n + (-n) % PACK_Q
    return out


def _all_sum(p, name):
    pair = _pair_up([p], name + "_pair")[0]
    chip = _sum_lead(pair, name + "_sum2")
    return _sum_lead(_chip_gather([chip], ["lead"], name + "_gather")[0], name + "_sum4")


WEIGHTS = ("c_ctx", "w_mod", "b_mod", "g_mix", "g_ffn", "w_in", "w_out", "mla_q_a_norm", "mla_w_qb", "mla_kv_a_norm", "mla_w_kvb",
           "mla_q_nope_norm", "mla_q_rope_norm", "mla_k_nope_norm", "mla_k_rope_norm", "pool_w", "pool_scale", "swa_q_norm",
           "swa_k_norm", "swa_sink", "na_q_norm", "na_k_norm", "na_rpb", "ffn_w_up", "ffn_conv_w", "ffn_conv_b", "ffn_w_down")
BIG = ("w_in", "mla_w_qb", "w_out", "ffn_w_down", "ffn_w_up", "mla_w_kvb")
BIG_KINDS = ("cm", "cm", "cm", "cm", "lb", "lb")


def _step(a):
    x, c, ctx = a["x"], a["c"], a["ctx"]
    n_lat, d = x.shape[1], x.shape[2]
    n_ctx = ctx.shape[1]
    l = DEPTH
    px, py, pc = lax.axis_index("x"), lax.axis_index("y"), lax.axis_index("c")
    chip = 2 * px + py
    pvec = [p.reshape(1).astype(I32) for p in (pc, px, py)]
    cvec = pvec[0]

    bf = {k: a[k].astype(BF16) for k in BIG}
    w = {key: [None] * l for key in ("w_in", "w_qb", "w_out", "w_down", "w_up", "w_kvb")}

    finish = {"w_in": lambda g: _w_in_layout(g.transpose(1, 0, 2).reshape(d, P_COLS)),
              "mla_w_qb": lambda g: _w_qb_layout(g.transpose(1, 0, 2).reshape(512, 768)),
              "w_out": lambda g: g.reshape(-1, d), "ffn_w_down": lambda g: g.reshape(-1, d),
              "ffn_w_up": lambda g: g, "mla_w_kvb": lambda g: g}
    slot = dict(zip(BIG, ("w_in", "w_qb", "w_out", "w_down", "w_up", "w_kvb")))
    kind_of = dict(zip(BIG, BIG_GATHER))

    def gather_start(li, keys, tag, after):
        shards = [bf[k][li] for k in keys]
        shapes = [s.shape for s in shards]
        kinds = [kind_of[k] for k in keys]
        lands = [_place_own(s, kd, pvec[1:], "w_place") for s, kd in zip(shards, kinds)]
        plan = _w_gather_plan(shapes, kinds)
        return _remote_start(f"w_start_{li}{tag}", shards + lands, 3 * len(keys), plan, after) + (shapes, kinds, keys, plan)

    def gather_finish(li, started, tag, after):
        ssem, rsem, bufs, _, shapes, kinds, keys, plan = started
        bufs = _remote_wait(f"w_wait_{li}{tag}", ssem, rsem, bufs, 3 * len(keys), plan, after)
        for k, g in zip(keys, _w_fill(bufs[len(keys):], shapes, kinds, "w_fill")):
            w[slot[k]][li] = finish[k](g)

    first, rest = ("w_in", "mla_w_qb", "mla_w_kvb"), ("w_out", "ffn_w_down", "ffn_w_up")
    started = gather_start(0, first, "a", [c])
    c = c + started[3][0:1, 0:1]

    c_all = _chip_gather(_pair_up([c], "c_pair"), ["lead"], "c_gather")[0].reshape(8, d)
    cs16 = jnp.concatenate([c_all, a["c_ctx"][None, :], jnp.zeros((7, d), F32)], axis=0)
    wc = a["w_mod"].shape[-1]
    b_sh = lax.dynamic_slice_in_dim(a["b_mod"], chip * wc, wc, axis=1)[:, None, :]
    mod_sh = _mod_fwd(cs16, a["w_mod"], b_sh, "mod_fwd")
    mod_all, conv_w_full = _chip_gather([mod_sh, a["ffn_conv_w"]], ["lead", "lane"], "mod_gather")
    started_rest = gather_start(0, rest, "b", [mod_all])
    mod_all = mod_all + started_rest[3][0, 0]
    mod_all = mod_all.transpose(1, 2, 0, 3).reshape(l, 16, 4 * wc)
    mods = jnp.stack([lax.dynamic_index_in_dim(mod_all, 2 * chip + pc, axis=1, keepdims=False), mod_all[:, 8]], axis=1)
    mods = mods.reshape(l, 2, 6, d)

    full = {k: a[k] for k in SMALL if k != "c_ctx"}
    full["ffn_conv_w"] = conv_w_full
    w.update(_small_weights(full, mods))
    w["mods"] = [w["mods"][li] for li in range(l)]
    w["g_mix"] = [w["g_mix"][li] for li in range(l)]
    w["g_ffn"] = [w["g_ffn"][li] for li in range(l)]
    tab = _rope_table(n_lat, n_ctx)
    gather_finish(0, started, "a", mods)
    xs = jnp.concatenate([x[0], ctx[0]], axis=0)
    saved = []
    nxt = {}

    def rest_of_layer0(mix):
        gather_finish(0, started_rest, "b", mix)
        nxt[1] = gather_start(1, BIG, "", [w["w_up"][0]])
        w["g_ffn"][0] = w["g_ffn"][0] + nxt[1][3][0:1, 0:1]

    for li in range(l):
        if 1 <= li < l - 1:
            nxt[li + 1] = gather_start(li + 1, BIG, "", [w["w_kvb"][li]])
            w["g_mix"][li] = w["g_mix"][li] + nxt[li + 1][3][0:1, 0:1]
        xs, s = _layer_fwd(xs, w, li, tab, n_lat, before_out=rest_of_layer0 if li == 0 else None)
        saved.append(s)
        if li + 1 < l:
            gather_finish(li + 1, nxt[li + 1], "", xs)
    loss, dx = _loss_kernel(xs, a["loss_target"][0], n_lat // TM, "loss")
    loss = lax.psum(loss[0, 0], ("x", "y", "c"))

    grads = [None] * l
    g_buf = {k: None for k in BIG}
    kind_g = dict(zip(BIG, BIG_KINDS))
    ffn_keys = ("ffn_w_down", "ffn_w_up")
    att_keys = ("w_in", "mla_w_qb", "w_out", "mla_w_kvb")

    def grad_pieces(g, keys):
        ops = {"w_in": lambda: _w_in_unlayout(g["w_in"]).reshape(d, 4, -1).transpose(1, 0, 2),
               "mla_w_qb": lambda: _w_qb_unlayout(g["w_qb"].astype(GRAD_WIRE)).reshape(512, 4, 192).transpose(1, 0, 2),
               "w_out": lambda: g["w_out"].reshape(4, -1, d), "ffn_w_down": lambda: g["w_down"].reshape(4, -1, d),
               "ffn_w_up": lambda: g["w_up"], "mla_w_kvb": lambda: g["w_kvb"].astype(GRAD_WIRE)}
        return [ops[k]() for k in keys]

    def scatter_start(li, g, keys, tag, after):
        ops = grad_pieces(g, keys)
        lands = [jnp.zeros((3, a[k].shape[1], a[k].shape[2]), GRAD_WIRE) for k in keys]
        plan = _g_scatter_plan([kind_g[k] for k in keys], [a[k].shape[2] for k in keys])
        return _remote_start(f"g_start_{li}{tag}", ops + lands, 3 * len(keys), plan, after) + (keys, plan)

    def scatter_finish(li, started, tag, after):
        ssem, rsem, bufs, _, keys, plan = started
        bufs = _remote_wait(f"g_wait_{li}{tag}", ssem, rsem, bufs, 3 * len(keys), plan, after)
        for k, own, landed in zip(keys, bufs[:len(keys)], bufs[len(keys):]):
            g_buf[k] = _sum_into(own, landed, g_buf[k], li, pvec, kind_g[k], l, "g_sum4")

    pending, early = None, []

    def ffn_grads_of_layer0(g):
        early.append(scatter_start(0, g, ffn_keys, "a", [g["w_up"]]))
        return early[0][3]

    for li in reversed(range(l)):
        dx, grads[li] = _layer_bwd(dx, saved[li], w, li, tab, n_lat, after_ffn=ffn_grads_of_layer0 if li == 0 else None)
        if pending is not None:
            scatter_finish(li + 1, pending, "", dx)
        if li > 0:
            pending = scatter_start(li, grads[li], BIG, "", [grads[li]["w_in"]])
            w["mods"][li - 1] = w["mods"][li - 1] + pending[3][0, 0]

    dmods = jnp.stack([grads[li]["mods"] for li in range(l)], axis=0).reshape(l, 2, 6 * d)
    dm_gath = _chip_gather(_pair_up([dmods], "dmod_pair"), ["lead"], "dmod_gather")[0].reshape(8, l, 2, 6 * d)
    dmod_all, g_b_mod = _dmod_assemble(dm_gath, "dmod_assemble")
    dm_sh = lax.dynamic_slice_in_dim(dmod_all, chip * wc, wc, axis=2)
    g_c_ctx = _mod_dc(dm_sh, a["w_mod"], a["c_ctx"][None, :], "mod_dc")

    rg = _reference_grads(grads, big=False)
    rg["c_ctx"] = g_c_ctx[0]
    packed = _all_sum(_pack([rg[k] for k in SMALL] + [rg["ffn_conv_w"]]), "small")
    late = scatter_start(0, grads[0], att_keys, "b", [packed, dmod_all])
    g_w_mod = _mod_dw(cs16, dm_sh + late[3][0, 0], "mod_dw")
    g_out = {"w_mod": g_w_mod, "b_mod": g_b_mod.reshape(l, 6 * d)}
    small_g = _unpack(packed, [a[k].shape for k in SMALL] + [rg["ffn_conv_w"].shape])
    for k, g in zip(SMALL, small_g[:-1]):
        g_out[k] = g
    cw = a["ffn_conv_w"].shape[-1]
    g_out["ffn_conv_w"] = lax.dynamic_slice_in_dim(small_g[-1], chip * cw, cw, axis=2)

    upd = {}
    pk = lambda pre: _pack([a[pre + k] for k in SMALL])
    outs = _adamw(pk(""), _pack([g_out[k] for k in SMALL]), pk("m_"), pk("v_"), "adamw_small")
    for o, kind in zip(outs, ("delta", "m", "v")):
        for k, val in zip(SMALL, _unpack(o, [a[k].shape for k in SMALL])):
            upd[kind, k] = val
    def adamw_each(keys):
        for k in keys:
            outs = _adamw(a[k], g_out[k], a["m_" + k], a["v_" + k], "adamw_" + k)
            for o, kind in zip(outs, ("delta", "m", "v")):
                upd[kind, k] = o

    adamw_each(("w_mod", "b_mod", "ffn_conv_w"))
    scatter_finish(0, early[0], "a", upd["delta", "w_mod"])
    scatter_finish(0, late, "b", upd["delta", "w_mod"])
    g_big = _sib_fill([g_buf[k] for k in BIG], lambda cc: (slice(None), cc), "g_pair")
    for k, g2 in zip(BIG, g_big):
        g_out[k], upd["delta", k], upd["m", k], upd["v", k] = _adamw_pair(a[k], g2, a["m_" + k], a["v_" + k], "adamw_" + k)
    grad_x = dx[0:n_lat].reshape(x.shape)
    return (loss, grad_x, *[g_out[k] for k in WEIGHTS], *[upd["delta", k] for k in WEIGHTS],
            *[upd["m", k] for k in WEIGHTS], *[upd["v", k] for k in WEIGHTS])


def kernel(x, c, ctx, c_ctx, w_mod, b_mod, g_mix, g_ffn, w_in, w_out, mla_q_a_norm, mla_w_qb, mla_kv_a_norm, mla_w_kvb, mla_q_nope_norm, mla_q_rope_norm, mla_k_nope_norm, mla_k_rope_norm, pool_w, pool_scale, swa_q_norm, swa_k_norm, swa_sink, na_q_norm, na_k_norm, na_rpb, ffn_w_up, ffn_conv_w, ffn_conv_b, ffn_w_down, loss_target, m_c_ctx, m_w_mod, m_b_mod, m_g_mix, m_g_ffn, m_w_in, m_w_out, m_mla_q_a_norm, m_mla_w_qb, m_mla_kv_a_norm, m_mla_w_kvb, m_mla_q_nope_norm, m_mla_q_rope_norm, m_mla_k_nope_norm, m_mla_k_rope_norm, m_pool_w, m_pool_scale, m_swa_q_norm, m_swa_k_norm, m_swa_sink, m_na_q_norm, m_na_k_norm, m_na_rpb, m_ffn_w_up, m_ffn_conv_w, m_ffn_conv_b, m_ffn_w_down, v_c_ctx, v_w_mod, v_b_mod, v_g_mix, v_g_ffn, v_w_in, v_w_out, v_mla_q_a_norm, v_mla_w_qb, v_mla_kv_a_norm, v_mla_w_kvb, v_mla_q_nope_norm, v_mla_q_rope_norm, v_mla_k_nope_norm, v_mla_k_rope_norm, v_pool_w, v_pool_scale, v_swa_q_norm, v_swa_k_norm, v_swa_sink, v_na_q_norm, v_na_k_norm, v_na_rpb, v_ffn_w_up, v_ffn_conv_w, v_ffn_conv_b, v_ffn_w_down):
    return _step(dict(locals()))
```

```python
import functools

import jax
import jax.numpy as jnp
from jax import lax
from jax.experimental import pallas as pl
from jax.experimental.pallas import tpu as pltpu

F32 = jnp.float32
BF16 = jnp.bfloat16
I32 = jnp.int32

DEPTH = 4
GRID_W = 64
ROPE_BASE = 10000.0
EPS = 1e-6
NEG = -1e30
MLA_SCALE = 192.0 ** -0.5
HD_SCALE = 64.0 ** -0.5
NA_KROWS = 12
SWA_KEYS = 512
P_COLS = 3648
PW = 3840
TM = 256
HALO = 8
ADAM_LR, ADAM_B1, ADAM_B2, ADAM_EPS, ADAM_WD, ADAM_STEP = 0.001, 0.9, 0.999, 1e-08, 0.01, 10
VMEM_LIMIT = 56 * 1024 * 1024
GRAD_WIRE = BF16
MESH = pl.DeviceIdType.MESH


def _pick(n, cands):
    for c in cands:
        if n % c == 0:
            return c
    return n


def _params(sem=None):
    return pltpu.CompilerParams(dimension_semantics=sem, vmem_limit_bytes=VMEM_LIMIT)


@jax.custom_vjp
def _bdot(a, b):
    return jnp.dot(a.astype(BF16), b.astype(BF16), preferred_element_type=F32)


def _bdot_fwd(a, b):
    return _bdot(a, b), (a.astype(BF16), b.astype(BF16))


def _bdot_bwd(res, g):
    a, b = res
    gb = g.astype(BF16)
    da = lax.dot_general(gb, b, (((1,), (1,)), ((), ())), preferred_element_type=F32)
    db = lax.dot_general(a, gb, (((0,), (0,)), ((), ())), preferred_element_type=F32)
    return da, db


_bdot.defvjp(_bdot_fwd, _bdot_bwd)


@jax.custom_vjp
def _bdot_nt(a, b):
    return lax.dot_general(a.astype(BF16), b.astype(BF16), (((1,), (1,)), ((), ())), preferred_element_type=F32)


def _bdot_nt_fwd(a, b):
    return _bdot_nt(a, b), (a.astype(BF16), b.astype(BF16))


def _bdot_nt_bwd(res, g):
    a, b = res
    gb = g.astype(BF16)
    da = jnp.dot(gb, b, preferred_element_type=F32)
    db = lax.dot_general(gb, a, (((0,), (0,)), ((), ())), preferred_element_type=F32)
    return da, db


_bdot_nt.defvjp(_bdot_nt_fwd, _bdot_nt_bwd)


def _rms(x, g):
    return x * lax.rsqrt(jnp.mean(x * x, axis=-1, keepdims=True) + EPS) * g


def _rope(x, cos, sin):
    xr = jnp.concatenate([-x[:, 16:32], x[:, 0:16], -x[:, 48:64], x[:, 32:48]], axis=-1)
    return x * cos + xr * sin


def _sel(is_ctx, mod, row):
    return jnp.where(is_ctx, mod[1, row:row + 1, :], mod[0, row:row + 1, :])


MM_VMEM_BUDGET = 40 * 1024 * 1024
HBM_BYTES_PER_STEP = 1 << 20


def _mm_tiles(m, n, k, mode, osize, n_unit, k_unit):
    lanes = (3840, 2816, 2048, 1280, 1024, 768, 512, 256)
    subl = (4352, 2176, 1088, 1024, 640, 544, 512, 256)
    tms = [c for c in (lanes if mode == "tn" else subl) if m % c == 0] or [m]
    tns = [c for c in lanes if n_unit % c == 0] or [n_unit]
    tks = [c for c in (subl if mode == "tn" else lanes) if k_unit % c == 0]
    if k_unit == k:
        tks = [k] + tks
    best = None
    for tm in tms:
        for tn in tns:
            for tk in tks:
                nk = k // tk
                vmem = 4 * (tm * tk + tk * tn) + 2 * tm * tn * osize + tm * tn * 4
                if vmem > MM_VMEM_BUDGET:
                    continue
                a_reads = 1 if nk == 1 else n // tn
                b_reads = 1 if (nk == 1 and n == tn) else m // tm
                steps = (m // tm) * (n // tn) * nk
                cost = (2 * m * k * a_reads + 2 * k * n * b_reads + m * n * osize + steps * HBM_BYTES_PER_STEP
                        + (12 * m * n * nk if nk > 1 else 0))
                if best is None or cost < best[0]:
                    best = (cost, tm, tn, tk)
    return best[1:]


def _mm(a, b, mode, out_dtype, name, a_split=False, b_split=False, o_split=False):
    def dims(x, split):
        return (x.shape[1], 2 * x.shape[2]) if split else x.shape

    ar, ac = dims(a, a_split)
    br, bc = dims(b, b_split)
    if mode == "nn":
        m, k, n = ar, ac, bc
        assert br == k
    elif mode == "nt":
        m, k, n = ar, ac, br
        assert bc == k
    else:
        k, m, n = ar, ac, bc
        assert br == k
    n_unit = n // 2 if (o_split or (b_split and mode != "nt")) else n
    k_unit = k // 2 if (mode != "tn" and (a_split or (b_split and mode == "nt"))) else k
    tm, tn, tk = _mm_tiles(m, n, k, mode, jnp.dtype(out_dtype).itemsize, n_unit, k_unit)
    nk = k // tk

    def spec(split, tr, tc, ncols, ridx, cidx):
        if not split:
            return pl.BlockSpec((tr, tc), lambda i, j, kk: (ridx(i, j, kk), cidx(i, j, kk)))
        nh = (ncols // 2) // tc
        return pl.BlockSpec((None, tr, tc), lambda i, j, kk: (cidx(i, j, kk) // nh, ridx(i, j, kk), cidx(i, j, kk) % nh))

    gi = lambda i, j, kk: i
    gj = lambda i, j, kk: j
    gk = lambda i, j, kk: kk
    if mode == "tn":
        a_spec = spec(a_split, tk, tm, ac, gk, gi)
    else:
        a_spec = spec(a_split, tm, tk, ac, gi, gk)
    if mode == "nt":
        b_spec = spec(b_split, tn, tk, bc, gj, gk)
    else:
        b_spec = spec(b_split, tk, tn, bc, gk, gj)
    o_spec = spec(o_split, tm, tn, n, gi, gj)
    dn = {"nn": (((1,), (0,)), ((), ())), "nt": (((1,), (1,)), ((), ())), "tn": (((0,), (0,)), ((), ()))}[mode]

    def body(a_ref, b_ref, o_ref, acc_ref):
        kk = pl.program_id(2)

        @pl.when(kk == 0)
        def _():
            acc_ref[...] = jnp.zeros_like(acc_ref)

        acc_ref[...] += lax.dot_general(a_ref[...], b_ref[...], dn, preferred_element_type=F32)

        @pl.when(kk == nk - 1)
        def _():
            o_ref[...] = acc_ref[...].astype(o_ref.dtype)

    def body_whole_k(a_ref, b_ref, o_ref):
        o_ref[...] = lax.dot_general(a_ref[...], b_ref[...], dn, preferred_element_type=F32).astype(o_ref.dtype)

    oshape = (2, m, n // 2) if o_split else (m, n)
    return pl.pallas_call(
        body if nk > 1 else body_whole_k, name=name, grid=(m // tm, n // tn, nk), in_specs=[a_spec, b_spec], out_specs=o_spec,
        out_shape=jax.ShapeDtypeStruct(oshape, out_dtype), scratch_shapes=[pltpu.VMEM((tm, tn), F32)] if nk > 1 else [],
        compiler_params=_params(("parallel", "parallel", "arbitrary")))(a, b)


def _mm_exact(a, b, name):
    def body(a_ref, b_ref, o_ref):
        o_ref[...] = jnp.dot(a_ref[...], b_ref[...], preferred_element_type=F32, precision=lax.Precision.HIGHEST)

    return pl.pallas_call(body, name=name, out_shape=jax.ShapeDtypeStruct((a.shape[0], b.shape[1]), F32),
                          compiler_params=_params())(a, b)


def _row_spec(width, col=0):
    return pl.BlockSpec((TM, width), lambda i: (i, col))


def _full_spec(shape):
    nd = len(shape)
    return pl.BlockSpec(shape, lambda *_: (0,) * nd)


def _normmod_fn(x, g, mod, is_ctx, row):
    return _rms(x, g) * (1.0 + _sel(is_ctx, mod, row + 1)) + _sel(is_ctx, mod, row)


def _normmod_fwd(x, g, mod, row, nbl, name):
    r, d = x.shape

    def body(x_ref, g_ref, mod_ref, h_ref):
        is_ctx = pl.program_id(0) >= nbl
        h_ref[...] = _normmod_fn(x_ref[...], g_ref[...], mod_ref[...], is_ctx, row).astype(BF16)

    return pl.pallas_call(
        body, name=name, grid=(r // TM,), in_specs=[_row_spec(d), _full_spec((1, d)), _full_spec(mod.shape)],
        out_specs=_row_spec(d), out_shape=jax.ShapeDtypeStruct((r, d), BF16), compiler_params=_params(("parallel",)))(x, g, mod)


def _normmod_bwd(x, g, mod, dh, dx_in, row, nbl, name):
    r, d = x.shape

    def body(x_ref, g_ref, mod_ref, dh_ref, dxin_ref, dx_ref, dg_ref, dmod_ref):
        i = pl.program_id(0)
        is_ctx = i >= nbl

        @pl.when(i == 0)
        def _():
            dg_ref[...] = jnp.zeros_like(dg_ref)
            dmod_ref[...] = jnp.zeros_like(dmod_ref)

        _, vjp = jax.vjp(lambda xx, gg, mm: _normmod_fn(xx, gg, mm, is_ctx, row), x_ref[...], g_ref[...], mod_ref[...])
        dx, dg, dmod = vjp(dh_ref[...])
        dx_ref[...] = dxin_ref[...] + dx
        dg_ref[...] += dg
        dmod_ref[...] += dmod

    return pl.pallas_call(
        body, name=name, grid=(r // TM,),
        in_specs=[_row_spec(d), _full_spec((1, d)), _full_spec(mod.shape), _row_spec(d), _row_spec(d)],
        out_specs=[_row_spec(d), _full_spec((1, d)), _full_spec(mod.shape)],
        out_shape=[jax.ShapeDtypeStruct((r, d), F32), jax.ShapeDtypeStruct((1, d), F32), jax.ShapeDtypeStruct(mod.shape, F32)],
        compiler_params=_params(("arbitrary",)))(x, g, mod, dh, dx_in)


def _resid_fwd(x, y, mod, row, nbl, name):
    r, d = x.shape

    def body(x_ref, y_ref, mod_ref, o_ref):
        is_ctx = pl.program_id(0) >= nbl
        o_ref[...] = x_ref[...] + _sel(is_ctx, mod_ref[...], row) * y_ref[...]

    return pl.pallas_call(
        body, name=name, grid=(r // TM,), in_specs=[_row_spec(d), _row_spec(d), _full_spec(mod.shape)],
        out_specs=_row_spec(d), out_shape=jax.ShapeDtypeStruct((r, d), F32), compiler_params=_params(("parallel",)))(x, y, mod)


def _resid_bwd(dx, y, mod, row, nbl, name):
    r, d = dx.shape

    def body(dx_ref, y_ref, mod_ref, dy_ref, dmod_ref):
        i = pl.program_id(0)
        is_ctx = i >= nbl

        @pl.when(i == 0)
        def _():
            dmod_ref[...] = jnp.zeros_like(dmod_ref)

        dxv = dx_ref[...]
        dy_ref[...] = (_sel(is_ctx, mod_ref[...], row) * dxv).astype(BF16)
        dgate = jnp.sum(dxv * y_ref[...], axis=0, keepdims=True)

        @pl.when(is_ctx)
        def _():
            dmod_ref[1, row:row + 1, :] += dgate

        @pl.when(jnp.logical_not(is_ctx))
        def _():
            dmod_ref[0, row:row + 1, :] += dgate

    return pl.pallas_call(
        body, name=name, grid=(r // TM,), in_specs=[_row_spec(d), _row_spec(d), _full_spec(mod.shape)],
        out_specs=[_row_spec(d), _full_spec(mod.shape)],
        out_shape=[jax.ShapeDtypeStruct((r, d), BF16), jax.ShapeDtypeStruct(mod.shape, F32)],
        compiler_params=_params(("arbitrary",)))(dx, y, mod)


def _loss_kernel(x, target, nbl, name):
    r, d = x.shape

    def body(x_ref, t_ref, loss_ref, dx_ref):
        i = pl.program_id(0)

        @pl.when(i == 0)
        def _():
            loss_ref[...] = jnp.zeros_like(loss_ref)

        @pl.when(i < nbl)
        def _():
            e = x_ref[...] - t_ref[...]
            dx_ref[...] = e / d
            loss_ref[...] += 0.5 * jnp.sum(jnp.mean(e * e, axis=-1, keepdims=True), axis=0, keepdims=True)

        @pl.when(i >= nbl)
        def _():
            dx_ref[...] = jnp.zeros_like(dx_ref)

    return pl.pallas_call(
        body, name=name, grid=(r // TM,),
        in_specs=[_row_spec(d), pl.BlockSpec((TM, d), lambda i: (jnp.minimum(i, nbl - 1), 0))],
        out_specs=[_full_spec((1, 1)), _row_spec(d)],
        out_shape=[jax.ShapeDtypeStruct((1, 1), F32), jax.ShapeDtypeStruct((r, d), F32)],
        compiler_params=_params(("arbitrary",)))(x, target)


SP_QA, SP_KVA, SP_QN, SP_QR, SP_KN, SP_KR, SP_SQ, SP_SK, SP_NQ, SP_NK, SP_SINK = range(11)
C_CQ, C_CKV, C_KR, C_POOL, C_SQ, C_SK, C_SV, C_NQ, C_NK, C_NV = 0, 512, 768, 896, 1408, 1920, 2048, 2176, 2688, 3200


def _prep_fn(p, tab, sp, wqb, wkvb):
    cos, sin = tab[:, 0:64], tab[:, 64:128]
    q = _bdot(_rms(p[:, C_CQ:C_CQ + 512], sp[SP_QA:SP_QA + 1, 0:512]), wqb)
    kv = _bdot(_rms(p[:, C_CKV:C_CKV + 256], sp[SP_KVA:SP_KVA + 1, 0:256]), wkvb)
    krr = _rope(_rms(p[:, C_KR:C_KR + 64], sp[SP_KR:SP_KR + 1, 0:64]), cos, sin)
    zero = jnp.zeros_like(krr)
    aq, ak, av = [], [], []
    for h in range(4):
        qn = _rms(q[:, 128 * h:128 * h + 128], sp[SP_QN:SP_QN + 1, 0:128])
        qr = _rope(_rms(q[:, 512 + 64 * h:576 + 64 * h], sp[SP_QR:SP_QR + 1, 0:64]), cos, sin)
        kn = _rms(kv[:, 256 * h:256 * h + 128], sp[SP_KN:SP_KN + 1, 0:128])
        aq += [qn, qr, zero]
        ak += [kn, krr, zero]
        av.append(kv[:, 256 * h + 128:256 * h + 256])
    cq = [_rope(_rms(p[:, C_SQ + 64 * h:C_SQ + 64 * h + 64], sp[SP_SQ:SP_SQ + 1, 0:64]), cos, sin) for h in range(8)]
    ck = [_rope(_rms(p[:, C_SK + 64 * h:C_SK + 64 * h + 64], sp[SP_SK:SP_SK + 1, 0:64]), cos, sin) for h in range(2)]
    dq = [_rms(p[:, C_NQ + 64 * h:C_NQ + 64 * h + 64], sp[SP_NQ:SP_NQ + 1, 0:64]) for h in range(8)]
    dk = [_rms(p[:, C_NK + 64 * h:C_NK + 64 * h + 64], sp[SP_NK:SP_NK + 1, 0:64]) for h in range(8)]
    cat = lambda xs: jnp.concatenate(xs, axis=-1)
    return (cat(aq), cat(ak), cat(av), cat(cq), cat(ck), p[:, C_SV:C_SV + 128], cat(dq), cat(dk), p[:, C_NV:C_NV + 512])


PREP_WIDTHS = (1024, 1024, 512, 512, 128, 128, 512, 512, 512)


def _prep_fwd(p, tab, sp, wqb, wkvb, name):
    r = p.shape[0]

    def body(p_ref, tab_ref, sp_ref, wqb_ref, wkvb_ref, *outs):
        res = _prep_fn(p_ref[...], tab_ref[...], sp_ref[...], wqb_ref[...].astype(F32), wkvb_ref[...].astype(F32))
        for o_ref, v in zip(outs, res):
            o_ref[...] = v.astype(BF16)

    return pl.pallas_call(
        body, name=name, grid=(r // TM,),
        in_specs=[_row_spec(PW), _row_spec(128), _full_spec(sp.shape), _full_spec(wqb.shape), _full_spec(wkvb.shape)],
        out_specs=[_row_spec(w) for w in PREP_WIDTHS],
        out_shape=[jax.ShapeDtypeStruct((r, w), BF16) for w in PREP_WIDTHS],
        compiler_params=_params(("parallel",)))(p, tab, sp, wqb, wkvb)


def _prep_bwd(p, tab, sp, wqb, wkvb, cots, dpool, name):
    r = p.shape[0]

    def body(p_ref, tab_ref, sp_ref, wqb_ref, wkvb_ref, *rest):
        cot_refs, dpool_ref = rest[:9], rest[9]
        dp_ref, dsp_ref, dwqb_ref, dwkvb_ref = rest[10:]
        i = pl.program_id(0)

        @pl.when(i == 0)
        def _():
            dsp_ref[...] = jnp.zeros_like(dsp_ref)
            dwqb_ref[...] = jnp.zeros_like(dwqb_ref)
            dwkvb_ref[...] = jnp.zeros_like(dwkvb_ref)

        tab = tab_ref[...]
        _, vjp = jax.vjp(lambda pp, ss, wq, wk: _prep_fn(pp, tab, ss, wq, wk),
                         p_ref[...], sp_ref[...], wqb_ref[...].astype(F32), wkvb_ref[...].astype(F32))
        dp, dsp, dwq, dwk = vjp(tuple(c[...] for c in cot_refs))
        dp_ref[...] = dp.astype(BF16)
        dp_ref[:, C_POOL:C_POOL + 512] = dpool_ref[...].astype(BF16)
        dsp_ref[...] += dsp
        dwqb_ref[...] += dwq
        dwkvb_ref[...] += dwk

    return pl.pallas_call(
        body, name=name, grid=(r // TM,),
        in_specs=[_row_spec(PW), _row_spec(128), _full_spec(sp.shape), _full_spec(wqb.shape), _full_spec(wkvb.shape)]
        + [_row_spec(w) for w in PREP_WIDTHS] + [_row_spec(512)],
        out_specs=[_row_spec(PW), _full_spec(sp.shape), _full_spec(wqb.shape), _full_spec(wkvb.shape)],
        out_shape=[jax.ShapeDtypeStruct((r, PW), BF16), jax.ShapeDtypeStruct(sp.shape, F32),
                   jax.ShapeDtypeStruct(wqb.shape, F32), jax.ShapeDtypeStruct(wkvb.shape, F32)],
        compiler_params=_params(("arbitrary",)))(p, tab, sp, wqb, wkvb, *cots, dpool)


def _mla_probs(q, k, is_ctx, n_lat):
    s = lax.dot_general(q, k, (((1,), (1,)), ((), ())), preferred_element_type=F32) * MLA_SCALE
    kid = lax.broadcasted_iota(I32, (1, s.shape[1]), 1)
    s = s + jnp.where(jnp.logical_and(is_ctx, kid < n_lat), NEG, 0.0)
    e = jnp.exp(s - jnp.max(s, axis=-1, keepdims=True))
    return e * (1.0 / jnp.sum(e, axis=-1, keepdims=True))


def _mla_fwd(aq, ak, av, n_lat, name):
    r = aq.shape[0]
    nbl = n_lat // TM

    def body(q_ref, k_ref, v_ref, o_ref):
        p = _mla_probs(q_ref[...], k_ref[...], pl.program_id(1) >= nbl, n_lat)
        o_ref[...] = jnp.dot(p.astype(BF16), v_ref[...], preferred_element_type=F32).astype(BF16)

    return pl.pallas_call(
        body, name=name, grid=(4, r // TM),
        in_specs=[pl.BlockSpec((TM, 256), lambda h, i: (i, h)), pl.BlockSpec((r, 256), lambda h, i: (0, h)),
                  pl.BlockSpec((r, 128), lambda h, i: (0, h))],
        out_specs=pl.BlockSpec((TM, 128), lambda h, i: (i, h)),
        out_shape=jax.ShapeDtypeStruct((r, 512), BF16), compiler_params=_params(("parallel", "parallel")))(aq, ak, av)


def _mla_bwd(aq, ak, av, dmix, n_lat, name):
    r = aq.shape[0]
    nbl = n_lat // TM

    def body(q_ref, k_ref, v_ref, do_ref, dq_ref, dk_ref, dv_ref):
        i = pl.program_id(1)

        @pl.when(i == 0)
        def _():
            dk_ref[...] = jnp.zeros_like(dk_ref)
            dv_ref[...] = jnp.zeros_like(dv_ref)

        q, k, v = q_ref[...], k_ref[...], v_ref[...]
        dob = do_ref[...].astype(BF16)
        p = _mla_probs(q, k, i >= nbl, n_lat)
        dv_ref[...] += lax.dot_general(p.astype(BF16), dob, (((0,), (0,)), ((), ())), preferred_element_type=F32)
        dp = lax.dot_general(dob, v, (((1,), (1,)), ((), ())), preferred_element_type=F32)
        ds = (p * (dp - jnp.sum(dp * p, axis=-1, keepdims=True)) * MLA_SCALE).astype(BF16)
        dq_ref[...] = jnp.dot(ds, k, preferred_element_type=F32)
        dk_ref[...] += lax.dot_general(ds, q, (((0,), (0,)), ((), ())), preferred_element_type=F32)

    return pl.pallas_call(
        body, name=name, grid=(4, r // TM),
        in_specs=[pl.BlockSpec((TM, 256), lambda h, i: (i, h)), pl.BlockSpec((r, 256), lambda h, i: (0, h)),
                  pl.BlockSpec((r, 128), lambda h, i: (0, h)), pl.BlockSpec((TM, 128), lambda h, i: (i, h))],
        out_specs=[pl.BlockSpec((TM, 256), lambda h, i: (i, h)), pl.BlockSpec((r, 256), lambda h, i: (0, h)),
                   pl.BlockSpec((r, 128), lambda h, i: (0, h))],
        out_shape=[jax.ShapeDtypeStruct((r, 1024), F32), jax.ShapeDtypeStruct((r, 1024), F32), jax.ShapeDtypeStruct((r, 512), F32)],
        compiler_params=_params(("parallel", "arbitrary")))(aq, ak, av, dmix)


def _pool_fn(ext, w, sc, gid0, grp, is_ctx, n_lat, r_all):
    gid = gid0 + lax.broadcasted_iota(I32, (TM + 2 * HALO, 1), 0)
    lo = jnp.where(is_ctx, n_lat, 0)
    hi = jnp.where(is_ctx, r_all, n_lat)
    z = jnp.where(jnp.logical_and(gid >= lo, gid < hi), ext, 0.0)
    w2 = jnp.roll(z, 1, axis=0) + z
    w4 = jnp.roll(w2, 1, axis=0) + jnp.roll(w2, -1, axis=0)
    w8 = jnp.roll(w4, 2, axis=0) + jnp.roll(w4, -2, axis=0)
    w16 = jnp.roll(w8, 4, axis=0) + jnp.roll(w8, -4, axis=0)
    win = jnp.where(grp == 0, w2, jnp.where(grp == 1, w4, jnp.where(grp == 2, w8, w16)))
    half = jnp.left_shift(1, grp)
    cnt = jnp.maximum(jnp.minimum(gid + half, hi) - jnp.maximum(gid - half, lo), 1).astype(F32)
    d = (win / cnt - ext)[HALO:HALO + TM]
    return _bdot(d, w) * sc


def _pool_ext(u_ref, i, r_all):
    s0 = pl.multiple_of(jnp.maximum(i * TM - HALO, 0), HALO)
    s2 = pl.multiple_of(jnp.minimum(i * TM + TM, r_all - HALO), HALO)
    ext = jnp.concatenate([u_ref[pl.ds(s0, HALO), :], u_ref[pl.ds(pl.multiple_of(i * TM, TM), TM), :], u_ref[pl.ds(s2, HALO), :]], axis=0)
    return ext, s0, s2


def _pool_specs(r):
    return [pl.BlockSpec((r, 128), lambda g, i: (0, C_POOL // 128 + g)), pl.BlockSpec((None, 128, 128), lambda g, i: (g, 0, 0)),
            pl.BlockSpec((None, 1, 128), lambda g, i: (g, 0, 0))]


def _pool_fwd(p, pool_w, pool_sc, n_lat, name):
    r = p.shape[0]
    nbl = n_lat // TM

    def body(u_ref, w_ref, sc_ref, o_ref):
        g, i = pl.program_id(0), pl.program_id(1)
        ext, _, _ = _pool_ext(u_ref, i, r)
        o_ref[...] = _pool_fn(ext, w_ref[...], sc_ref[...], i * TM - HALO, g, i >= nbl, n_lat, r).astype(BF16)

    return pl.pallas_call(
        body, name=name, grid=(4, r // TM), in_specs=_pool_specs(r), out_specs=pl.BlockSpec((TM, 128), lambda g, i: (i, g)),
        out_shape=jax.ShapeDtypeStruct((r, 512), BF16), compiler_params=_params(("parallel", "parallel")))(p, pool_w, pool_sc)


def _pool_bwd(p, pool_w, pool_sc, dmix, n_lat, name):
    r = p.shape[0]
    nbl = n_lat // TM

    def body(u_ref, w_ref, sc_ref, do_ref, du_ref, dw_ref, dsc_ref):
        g, i = pl.program_id(0), pl.program_id(1)

        @pl.when(i == 0)
        def _():
            du_ref[...] = jnp.zeros_like(du_ref)
            dw_ref[...] = jnp.zeros_like(dw_ref)
            dsc_ref[...] = jnp.zeros_like(dsc_ref)

        ext, s0, s2 = _pool_ext(u_ref, i, r)
        _, vjp = jax.vjp(lambda e, w, s: _pool_fn(e, w, s, i * TM - HALO, g, i >= nbl, n_lat, r), ext, w_ref[...], sc_ref[...])
        dext, dw, dsc = vjp(do_ref[...])
        du_ref[pl.ds(s0, HALO), :] += dext[0:HALO]
        du_ref[pl.ds(pl.multiple_of(i * TM, TM), TM), :] += dext[HALO:HALO + TM]
        du_ref[pl.ds(s2, HALO), :] += dext[HALO + TM:]
        dw_ref[...] += dw
        dsc_ref[...] += dsc

    return pl.pallas_call(
        body, name=name, grid=(4, r // TM), in_specs=_pool_specs(r) + [pl.BlockSpec((TM, 128), lambda g, i: (i, 4 + g))],
        out_specs=[pl.BlockSpec((r, 128), lambda g, i: (0, g)), pl.BlockSpec((None, 128, 128), lambda g, i: (g, 0, 0)),
                   pl.BlockSpec((None, 1, 128), lambda g, i: (g, 0, 0))],
        out_shape=[jax.ShapeDtypeStruct((r, 512), F32), jax.ShapeDtypeStruct((4, 128, 128), F32), jax.ShapeDtypeStruct((4, 1, 128), F32)],
        compiler_params=_params(("parallel", "arbitrary")))(p, pool_w, pool_sc, dmix)


def _softmax_parts(parts, extra=None):
    m = functools.reduce(jnp.maximum, [jnp.max(s, axis=-1, keepdims=True) for s in parts])
    if extra is not None:
        m = jnp.maximum(m, extra)
    m = lax.stop_gradient(m)
    es = [jnp.exp(s - m) for s in parts]
    den = functools.reduce(jnp.add, [jnp.sum(e, axis=-1, keepdims=True) for e in es])
    if extra is not None:
        den = den + jnp.exp(extra - m)
    inv = 1.0 / den
    return [e * inv for e in es]


def _swa_band(qpos0, kpos0, is_ctx):
    qpos = qpos0 + lax.broadcasted_iota(I32, (TM, SWA_KEYS), 0)
    kpos = kpos0 + lax.broadcasted_iota(I32, (TM, SWA_KEYS), 1)
    valid = jnp.logical_and(jnp.abs(kpos - qpos) <= 128, jnp.logical_not(is_ctx))
    band = jnp.where(valid, 0.0, NEG)
    return jnp.concatenate([band] * 4, axis=0)


def _swa_fn(q4, kw, vw, kc, vc, sink4, band):
    qs = jnp.concatenate([q4[:, 64 * g:64 * g + 64] for g in range(4)], axis=0)
    s_loc = _bdot_nt(qs, kw) * HD_SCALE + band
    s_ctx = _bdot_nt(qs, kc) * HD_SCALE
    sink = jnp.concatenate([jnp.broadcast_to(sink4[:, g:g + 1], (TM, 1)) for g in range(4)], axis=0)
    p_loc, p_ctx = _softmax_parts([s_loc, s_ctx], sink)
    o = _bdot(p_loc, vw) + _bdot(p_ctx, vc)
    return jnp.concatenate([o[TM * g:TM * (g + 1)] for g in range(4)], axis=1)


def _swa_window(i, n_lat):
    return pl.multiple_of(jnp.clip(i * TM - 128, 0, n_lat - SWA_KEYS), 128)


def _swa_fwd(cq, ck, cv, sp, n_lat, name):
    r = cq.shape[0]
    nbl = n_lat // TM

    def body(q_ref, k_ref, v_ref, sp_ref, o_ref):
        i = pl.program_id(0)
        k0 = _swa_window(i, n_lat)
        kw, vw = k_ref[pl.ds(k0, SWA_KEYS), :].astype(F32), v_ref[pl.ds(k0, SWA_KEYS), :].astype(F32)
        kc, vc = k_ref[pl.ds(n_lat, r - n_lat), :].astype(F32), v_ref[pl.ds(n_lat, r - n_lat), :].astype(F32)
        band = _swa_band(i * TM, k0, i >= nbl)
        for j in range(2):
            c = slice(64 * j, 64 * j + 64)
            o = _swa_fn(q_ref[:, 256 * j:256 * j + 256].astype(F32), kw[:, c], vw[:, c], kc[:, c], vc[:, c],
                        sp_ref[SP_SINK:SP_SINK + 1, 4 * j:4 * j + 4], band)
            o_ref[:, 256 * j:256 * j + 256] = o.astype(BF16)

    return pl.pallas_call(
        body, name=name, grid=(r // TM,),
        in_specs=[_row_spec(512), _full_spec((r, 128)), _full_spec((r, 128)), _full_spec(sp.shape)],
        out_specs=_row_spec(512), out_shape=jax.ShapeDtypeStruct((r, 512), BF16), compiler_params=_params(("parallel",)))(cq, ck, cv, sp)


def _swa_bwd(cq, ck, cv, sp, dmix, n_lat, name):
    r = cq.shape[0]
    nbl = n_lat // TM
    nc = r - n_lat

    def body(q_ref, k_ref, v_ref, sp_ref, do_ref, dq_ref, dk_ref, dv_ref, dsp_ref):
        i = pl.program_id(0)

        @pl.when(i == 0)
        def _():
            dk_ref[...] = jnp.zeros_like(dk_ref)
            dv_ref[...] = jnp.zeros_like(dv_ref)
            dsp_ref[...] = jnp.zeros_like(dsp_ref)

        k0 = _swa_window(i, n_lat)
        kw, vw = k_ref[pl.ds(k0, SWA_KEYS), :].astype(F32), v_ref[pl.ds(k0, SWA_KEYS), :].astype(F32)
        kc, vc = k_ref[pl.ds(n_lat, nc), :].astype(F32), v_ref[pl.ds(n_lat, nc), :].astype(F32)
        dkw, dvw, dkc, dvc, dsk = [], [], [], [], []
        band = _swa_band(i * TM, k0, i >= nbl)
        for j in range(2):
            c = slice(64 * j, 64 * j + 64)
            _, vjp = jax.vjp(lambda q4, a, b, cc, d, s: _swa_fn(q4, a, b, cc, d, s, band),
                             q_ref[:, 256 * j:256 * j + 256].astype(F32), kw[:, c], vw[:, c], kc[:, c], vc[:, c],
                             sp_ref[SP_SINK:SP_SINK + 1, 4 * j:4 * j + 4])
            dq4, a, b, cc, d, s = vjp(do_ref[:, 256 * j:256 * j + 256])
            dq_ref[:, 256 * j:256 * j + 256] = dq4
            dkw.append(a), dvw.append(b), dkc.append(cc), dvc.append(d), dsk.append(s)
        cat = lambda xs: jnp.concatenate(xs, axis=1)
        dk_ref[pl.ds(k0, SWA_KEYS), :] += cat(dkw)
        dv_ref[pl.ds(k0, SWA_KEYS), :] += cat(dvw)
        dk_ref[pl.ds(n_lat, nc), :] += cat(dkc)
        dv_ref[pl.ds(n_lat, nc), :] += cat(dvc)
        dsp_ref[SP_SINK:SP_SINK + 1, 0:8] += cat(dsk)

    return pl.pallas_call(
        body, name=name, grid=(r // TM,),
        in_specs=[_row_spec(512), _full_spec((r, 128)), _full_spec((r, 128)), _full_spec(sp.shape), _row_spec(512, 2)],
        out_specs=[_row_spec(512), _full_spec((r, 128)), _full_spec((r, 128)), _full_spec(sp.shape)],
        out_shape=[jax.ShapeDtypeStruct((r, 512), F32), jax.ShapeDtypeStruct((r, 128), F32), jax.ShapeDtypeStruct((r, 128), F32),
                   jax.ShapeDtypeStruct(sp.shape, F32)],
        compiler_params=_params(("arbitrary",)))(cq, ck, cv, sp, dmix)


def _na_fn(q, kw, vw, kc, vc, bias):
    s_loc = _bdot_nt(q, kw) * HD_SCALE + bias
    s_ctx = _bdot_nt(q, kc) * HD_SCALE
    p_loc, p_ctx = _softmax_parts([s_loc, s_ctx])
    return _bdot(p_loc, vw) + _bdot(p_ctx, vc)


NA_MASKED = 15


def _na_geometry(i, n_lat, is_ctx):
    rows = n_lat // GRID_W
    qrow0 = i * (TM // GRID_W)
    krow0 = jnp.clip(qrow0 - 4, 0, rows - NA_KROWS)
    dr = []
    for qi in range(TM // GRID_W):
        r_lo = jnp.clip(qrow0 + qi - 4, 0, rows - 8)
        row = []
        for kj in range(NA_KROWS):
            kr = krow0 + kj
            ok = jnp.logical_and(jnp.logical_and(kr >= r_lo, kr < r_lo + 8), jnp.logical_not(is_ctx))
            row.append(jnp.where(ok, kr - (qrow0 + qi) + 7, NA_MASKED))
        dr.append(row)
    return pl.multiple_of(krow0 * GRID_W, GRID_W), dr


def _na_bias(t1_ref, hh, dr):
    return jnp.concatenate([jnp.concatenate([t1_ref[hh, dr[qi][kj]] for kj in range(NA_KROWS)], axis=1)
                            for qi in range(TM // GRID_W)], axis=0)


def _na_specs(r):
    return [pl.BlockSpec((TM, 128), lambda pr, i: (i, pr)), pl.BlockSpec((r, 128), lambda pr, i: (0, pr)),
            pl.BlockSpec((r, 128), lambda pr, i: (0, pr)), pl.BlockSpec((2, 16, GRID_W, GRID_W), lambda pr, i: (pr, 0, 0, 0))]


def _na_fwd(dq, dk, dv, t1, n_lat, name):
    r = dq.shape[0]
    nbl = n_lat // TM
    nc = r - n_lat
    nk = NA_KROWS * GRID_W

    def body(q_ref, k_ref, v_ref, t1_ref, o_ref):
        i = pl.program_id(1)
        k0, dr = _na_geometry(jnp.minimum(i, nbl - 1), n_lat, i >= nbl)
        kw, vw = k_ref[pl.ds(k0, nk), :].astype(F32), v_ref[pl.ds(k0, nk), :].astype(F32)
        kc, vc = k_ref[pl.ds(n_lat, nc), :].astype(F32), v_ref[pl.ds(n_lat, nc), :].astype(F32)
        for hh in range(2):
            c = slice(64 * hh, 64 * hh + 64)
            o = _na_fn(q_ref[:, c].astype(F32), kw[:, c], vw[:, c], kc[:, c], vc[:, c], _na_bias(t1_ref, hh, dr))
            o_ref[:, c] = o.astype(BF16)

    return pl.pallas_call(
        body, name=name, grid=(4, r // TM), in_specs=_na_specs(r), out_specs=pl.BlockSpec((TM, 128), lambda pr, i: (i, pr)),
        out_shape=jax.ShapeDtypeStruct((r, 512), BF16), compiler_params=_params(("parallel", "parallel")))(dq, dk, dv, t1)


def _na_bwd(dq, dk, dv, t1, dmix, n_lat, name):
    r = dq.shape[0]
    nbl = n_lat // TM
    nc = r - n_lat
    nk = NA_KROWS * GRID_W

    def body(q_ref, k_ref, v_ref, t1_ref, do_ref, dq_ref, dk_ref, dv_ref, dt1_ref):
        i = pl.program_id(1)

        @pl.when(i == 0)
        def _():
            dk_ref[...] = jnp.zeros_like(dk_ref)
            dv_ref[...] = jnp.zeros_like(dv_ref)
            dt1_ref[...] = jnp.zeros_like(dt1_ref)

        k0, dr = _na_geometry(jnp.minimum(i, nbl - 1), n_lat, i >= nbl)
        kw, vw = k_ref[pl.ds(k0, nk), :].astype(F32), v_ref[pl.ds(k0, nk), :].astype(F32)
        kc, vc = k_ref[pl.ds(n_lat, nc), :].astype(F32), v_ref[pl.ds(n_lat, nc), :].astype(F32)
        dkw, dvw, dkc, dvc = [], [], [], []
        for hh in range(2):
            c = slice(64 * hh, 64 * hh + 64)
            _, vjp = jax.vjp(_na_fn,
                             q_ref[:, c].astype(F32), kw[:, c], vw[:, c], kc[:, c], vc[:, c], _na_bias(t1_ref, hh, dr))
            dqh, a, b, cc, d, dbias = vjp(do_ref[:, c])
            dq_ref[:, c] = dqh
            dkw.append(a), dvw.append(b), dkc.append(cc), dvc.append(d)
            for qi in range(TM // GRID_W):
                for kj in range(NA_KROWS):
                    dt1_ref[hh, dr[qi][kj]] += dbias[GRID_W * qi:GRID_W * (qi + 1), GRID_W * kj:GRID_W * (kj + 1)]
        cat = lambda xs: jnp.concatenate(xs, axis=1)
        dk_ref[pl.ds(k0, nk), :] += cat(dkw)
        dv_ref[pl.ds(k0, nk), :] += cat(dvw)
        dk_ref[pl.ds(n_lat, nc), :] += cat(dkc)
        dv_ref[pl.ds(n_lat, nc), :] += cat(dvc)

    return pl.pallas_call(
        body, name=name, grid=(4, r // TM), in_specs=_na_specs(r) + [pl.BlockSpec((TM, 128), lambda pr, i: (i, 12 + pr))],
        out_specs=[pl.BlockSpec((TM, 128), lambda pr, i: (i, pr)), pl.BlockSpec((r, 128), lambda pr, i: (0, pr)),
                   pl.BlockSpec((r, 128), lambda pr, i: (0, pr)), pl.BlockSpec((2, 16, GRID_W, GRID_W), lambda pr, i: (pr, 0, 0, 0))],
        out_shape=[jax.ShapeDtypeStruct((r, 512), F32)] * 3 + [jax.ShapeDtypeStruct((8, 16, GRID_W, GRID_W), F32)],
        compiler_params=_params(("parallel", "arbitrary")))(dq, dk, dv, t1, dmix)


def _conv_ext(main_ref, prev_ref, next_ref, edges):
    prev_ok, next_ok = edges
    return jnp.concatenate([jnp.where(prev_ok, prev_ref[...], 0.0), main_ref[...], jnp.where(next_ok, next_ref[...], 0.0)], axis=0)


def _conv_edges(i, nbl, nb):
    return jnp.logical_and(i != 0, i != nbl), jnp.logical_and(i != nbl - 1, i != nb - 1)


def _conv_apply(ext, w, b):
    up = jnp.roll(ext, 1, axis=0)
    dn = jnp.roll(ext, -1, axis=0)
    return up * w[0:1] + ext * w[1:2] + dn * w[2:3] + b, up, dn


def _conv_in_specs(tc, r):
    nb8 = TM // HALO
    last8 = r // HALO - 1

    def trio(half):
        return [pl.BlockSpec((None, TM, tc), lambda j, i: (half, i, j)),
                pl.BlockSpec((None, HALO, tc), lambda j, i: (half, jnp.maximum(i * nb8 - 1, 0), j)),
                pl.BlockSpec((None, HALO, tc), lambda j, i: (half, jnp.minimum((i + 1) * nb8, last8), j))]

    wb = [pl.BlockSpec((None, 3, tc), lambda j, i: (0, 0, j)), pl.BlockSpec((None, 3, tc), lambda j, i: (1, 0, j)),
          pl.BlockSpec((None, 1, tc), lambda j, i: (0, 0, j)), pl.BlockSpec((None, 1, tc), lambda j, i: (1, 0, j))]
    return trio(0) + trio(1) + wb


def _convgate_fwd(a3, cw, cb, n_lat, name):
    _, r, ff = a3.shape
    nbl = n_lat // TM
    tc = _pick(ff, (512, 256, 128))

    def body(g_ref, gp_ref, gn_ref, v_ref, vp_ref, vn_ref, wg_ref, wv_ref, bg_ref, bv_ref, u_ref):
        edges = _conv_edges(pl.program_id(1), nbl, r // TM)
        gg, _, _ = _conv_apply(_conv_ext(g_ref, gp_ref, gn_ref, edges), wg_ref[...], bg_ref[...])
        gv, _, _ = _conv_apply(_conv_ext(v_ref, vp_ref, vn_ref, edges), wv_ref[...], bv_ref[...])
        u_ref[...] = (jax.nn.silu(gg[HALO:HALO + TM]) * gv[HALO:HALO + TM]).astype(BF16)

    return pl.pallas_call(
        body, name=name, grid=(ff // tc, r // TM), in_specs=_conv_in_specs(tc, r),
        out_specs=pl.BlockSpec((TM, tc), lambda j, i: (i, j)), out_shape=jax.ShapeDtypeStruct((r, ff), BF16),
        compiler_params=_params(("parallel", "parallel")))(a3, a3, a3, a3, a3, a3, cw, cw, cb, cb)


def _convgate_bwd(a3, cw, cb, du, n_lat, name):
    _, r, ff = a3.shape
    nbl = n_lat // TM
    tc = _pick(ff, (512, 256, 128))
    nb8 = TM // HALO
    last8 = r // HALO - 1

    def body(g_ref, gp_ref, gn_ref, v_ref, vp_ref, vn_ref, wg_ref, wv_ref, bg_ref, bv_ref, du_ref, dup_ref, dun_ref,
             da_ref, dcw_ref, dcb_ref):
        i = pl.program_id(1)

        @pl.when(i == 0)
        def _():
            dcw_ref[...] = jnp.zeros_like(dcw_ref)
            dcb_ref[...] = jnp.zeros_like(dcb_ref)

        edges = _conv_edges(i, nbl, r // TM)
        wg, wv = wg_ref[...], wv_ref[...]
        eg, ev = _conv_ext(g_ref, gp_ref, gn_ref, edges), _conv_ext(v_ref, vp_ref, vn_ref, edges)
        gg, ug, dg_ = _conv_apply(eg, wg, bg_ref[...])
        gv, uv, dv_ = _conv_apply(ev, wv, bv_ref[...])
        due = _conv_ext(du_ref, dup_ref, dun_ref, edges)
        sg = jax.nn.sigmoid(gg)
        dgg = due * gv * (sg * (1.0 + gg * (1.0 - sg)))
        dgv = due * (gg * sg)
        main = slice(HALO, HALO + TM)
        for h, (dgx, w, ex, upx, dnx) in enumerate(((dgg, wg, eg, ug, dg_), (dgv, wv, ev, uv, dv_))):
            da = dgx * w[1:2] + jnp.roll(dgx, -1, axis=0) * w[0:1] + jnp.roll(dgx, 1, axis=0) * w[2:3]
            da_ref[h] = da[main].astype(BF16)
            dm = dgx[main]
            dcw_ref[h, 0:1, :] += jnp.sum(dm * upx[main], axis=0, keepdims=True)
            dcw_ref[h, 1:2, :] += jnp.sum(dm * ex[main], axis=0, keepdims=True)
            dcw_ref[h, 2:3, :] += jnp.sum(dm * dnx[main], axis=0, keepdims=True)
            dcb_ref[h] += jnp.sum(dm, axis=0, keepdims=True)

    du_specs = [pl.BlockSpec((TM, tc), lambda j, i: (i, j)),
                pl.BlockSpec((HALO, tc), lambda j, i: (jnp.maximum(i * nb8 - 1, 0), j)),
                pl.BlockSpec((HALO, tc), lambda j, i: (jnp.minimum((i + 1) * nb8, last8), j))]
    return pl.pallas_call(
        body, name=name, grid=(ff // tc, r // TM), in_specs=_conv_in_specs(tc, r) + du_specs,
        out_specs=[pl.BlockSpec((2, TM, tc), lambda j, i: (0, i, j)), pl.BlockSpec((2, 3, tc), lambda j, i: (0, 0, j)),
                   pl.BlockSpec((2, 1, tc), lambda j, i: (0, 0, j))],
        out_shape=[jax.ShapeDtypeStruct((2, r, ff), BF16), jax.ShapeDtypeStruct((2, 3, ff), F32), jax.ShapeDtypeStruct((2, 1, ff), F32)],
        compiler_params=_params(("parallel", "arbitrary")))(a3, a3, a3, a3, a3, a3, cw, cw, cb, cb, du, du, du)


def _layer_fwd(x, w, l, tab, n_lat, before_out=None):
    nbl = n_lat // TM
    mod = w["mods"][l]
    h1 = _normmod_fwd(x, w["g_mix"][l], mod, 0, nbl, "normmod_fwd")
    p = _mm(h1, w["w_in"][l], "nn", F32, "mm_in")
    qkv = _prep_fwd(p, tab, w["sp"][l], w["w_qb"][l], w["w_kvb"][l], "prep_fwd")
    oa = _mla_fwd(qkv[0], qkv[1], qkv[2], n_lat, "mla_fwd")
    ob = _pool_fwd(p, w["pool_w"][l], w["pool_sc"][l], n_lat, "pool_fwd")
    oc = _swa_fwd(qkv[3], qkv[4], qkv[5], w["sp"][l], n_lat, "swa_fwd")
    od = _na_fwd(qkv[6], qkv[7], qkv[8], w["t1"][l], n_lat, "na_fwd")
    mix = jnp.concatenate([oa, ob, oc, od], axis=1)
    if before_out is not None:
        before_out(mix)
    y = _mm(mix, w["w_out"][l], "nn", F32, "mm_out")
    x1 = _resid_fwd(x, y, mod, 2, nbl, "resid_fwd")
    h2 = _normmod_fwd(x1, w["g_ffn"][l], mod, 3, nbl, "normmod_fwd")
    a3 = _mm(h2, w["w_up"][l], "nn", F32, "mm_up", o_split=True)
    u = _convgate_fwd(a3, w["conv_w"][l], w["conv_b"][l], n_lat, "convgate_fwd")
    y2 = _mm(u, w["w_down"][l], "nn", F32, "mm_down")
    x2 = _resid_fwd(x1, y2, mod, 5, nbl, "resid_fwd")
    return x2, dict(x=x, h1=h1, p=p, qkv=qkv, mix=mix, y=y, x1=x1, h2=h2, a3=a3, u=u, y2=y2)


def _layer_bwd(dx, s, w, l, tab, n_lat, after_ffn=None):
    nbl = n_lat // TM
    mod = w["mods"][l]
    g = {}
    dy2, dmod_a = _resid_bwd(dx, s["y2"], mod, 5, nbl, "resid_bwd")
    g["w_down"] = _mm(s["u"], dy2, "tn", GRAD_WIRE, "mm_dwdown")
    du = _mm(dy2, w["w_down"][l], "nt", F32, "mm_du")
    da3, g["conv_w"], g["conv_b"] = _convgate_bwd(s["a3"], w["conv_w"][l], w["conv_b"][l], du, n_lat, "convgate_bwd")
    g["w_up"] = _mm(s["h2"], da3, "tn", GRAD_WIRE, "mm_dwup", b_split=True)
    dh2 = _mm(da3, w["w_up"][l], "nt", F32, "mm_dh2", a_split=True)
    g_ffn = w["g_ffn"][l] if after_ffn is None else w["g_ffn"][l] + after_ffn(g)[0:1, 0:1]
    dx1, g["g_ffn"], dmod_b = _normmod_bwd(s["x1"], g_ffn, mod, dh2, dx, 3, nbl, "normmod_bwd")
    dy, dmod_c = _resid_bwd(dx1, s["y"], mod, 2, nbl, "resid_bwd")
    g["w_out"] = _mm(s["mix"], dy, "tn", GRAD_WIRE, "mm_dwout")
    dmix = _mm(dy, w["w_out"][l], "nt", F32, "mm_dmix")
    qkv = s["qkv"]
    daq, dak, dav = _mla_bwd(qkv[0], qkv[1], qkv[2], dmix, n_lat, "mla_bwd")
    dpool, g["pool_w"], g["pool_sc"] = _pool_bwd(s["p"], w["pool_w"][l], w["pool_sc"][l], dmix, n_lat, "pool_bwd")
    dcq, dck, dcv, dsp_c = _swa_bwd(qkv[3], qkv[4], qkv[5], w["sp"][l], dmix, n_lat, "swa_bwd")
    ddq, ddk, ddv, g["t1"] = _na_bwd(qkv[6], qkv[7], qkv[8], w["t1"][l], dmix, n_lat, "na_bwd")
    dp, dsp_p, g["w_qb"], g["w_kvb"] = _prep_bwd(s["p"], tab, w["sp"][l], w["w_qb"][l], w["w_kvb"][l],
                                                (daq, dak, dav, dcq, dck, dcv, ddq, ddk, ddv), dpool, "prep_bwd")
    g["sp"] = dsp_c + dsp_p
    g["w_in"] = _mm(s["h1"], dp, "tn", GRAD_WIRE, "mm_dwin")
    dh1 = _mm(dp, w["w_in"][l], "nt", F32, "mm_dh1")
    dx0, g["g_mix"], dmod_d = _normmod_bwd(s["x"], w["g_mix"][l], mod, dh1, dx1, 0, nbl, "normmod_bwd")
    g["mods"] = dmod_a + dmod_b + dmod_c + dmod_d
    return dx0, g


def _local_step(x_all, target, w, tab, n_lat):
    saved = []
    x = x_all
    for l in range(DEPTH):
        x, s = _layer_fwd(x, w, l, tab, n_lat)
        saved.append(s)
    loss, dx = _loss_kernel(x, target, n_lat // TM, "loss")
    grads = [None] * DEPTH
    for l in reversed(range(DEPTH)):
        dx, grads[l] = _layer_bwd(dx, saved[l], w, l, tab, n_lat)
    return loss[0, 0], dx, grads


def _pad_cols(a, widths):
    parts, o = [], 0
    for take, pad in widths:
        parts.append(a[..., o:o + take])
        if pad:
            parts.append(jnp.zeros(a.shape[:-1] + (pad,), a.dtype))
        o += take
    return jnp.concatenate(parts, axis=-1)


def _w_in_layout(w_in):
    return _pad_cols(w_in, [(832, 64), (P_COLS - 832, PW - P_COLS - 64)])


def _w_in_unlayout(g):
    return jnp.concatenate([g[..., 0:832], g[..., 896:896 + P_COLS - 832]], axis=-1)


def _w_qb_layout(w):
    s = w.reshape(w.shape[:-1] + (4, 192))
    return jnp.concatenate([s[..., 0:128].reshape(w.shape[:-1] + (512,)), s[..., 128:192].reshape(w.shape[:-1] + (256,))], axis=-1)


def _w_qb_unlayout(g):
    n = g[..., 0:512].reshape(g.shape[:-1] + (4, 128))
    r = g[..., 512:768].reshape(g.shape[:-1] + (4, 64))
    return jnp.concatenate([n, r], axis=-1).reshape(g.shape[:-1] + (768,))


SP_SLOTS = (("mla_q_a_norm", 512), ("mla_kv_a_norm", 256), ("mla_q_nope_norm", 128), ("mla_q_rope_norm", 64),
            ("mla_k_nope_norm", 128), ("mla_k_rope_norm", 64), ("swa_q_norm", 64), ("swa_k_norm", 64),
            ("na_q_norm", 64), ("na_k_norm", 64), ("swa_sink", 8))


def _sp_pack(small):
    rows = [jnp.pad(small[k], ((0, 0), (0, 512 - n))) for k, n in SP_SLOTS]
    rows += [jnp.zeros_like(rows[0])] * (16 - len(rows))
    return jnp.stack(rows, axis=1)


def _sp_unpack(sp):
    return {k: sp[:, i, 0:n] for i, (k, n) in enumerate(SP_SLOTS)}


def _rpb_onehot():
    qc = lax.broadcasted_iota(I32, (GRID_W, GRID_W), 0)
    kc = lax.broadcasted_iota(I32, (GRID_W, GRID_W), 1)
    dc = (jnp.clip(kc - qc, -15, 15) + 15).reshape(1, GRID_W * GRID_W)
    return (lax.broadcasted_iota(I32, (32, GRID_W * GRID_W), 0) == dc).astype(F32)


def _rpb_expand(rpb):
    l = rpb.shape[0]
    flat = jnp.pad(rpb, ((0, 0), (0, 0), (0, 1), (0, 1))).reshape(l * 128, 32)
    t1 = _mm_exact(flat, _rpb_onehot(), "rpb_expand").reshape(l, 8, 16, GRID_W, GRID_W)
    qc = lax.broadcasted_iota(I32, (GRID_W, GRID_W), 0)
    kc = lax.broadcasted_iota(I32, (GRID_W, GRID_W), 1)
    c_lo = jnp.clip(qc - 8, 0, GRID_W - 16)
    col_ok = jnp.logical_and(kc >= c_lo, kc < c_lo + 16)
    row_ok = lax.broadcasted_iota(I32, (16, 1, 1), 0) != NA_MASKED
    return jnp.where(jnp.logical_and(col_ok[None], row_ok), t1, NEG)


def _rpb_fold(dt1):
    l = dt1.shape[0]
    g = _mm_exact(dt1.reshape(l * 128, GRID_W * GRID_W), _rpb_onehot().T, "rpb_fold")
    return g.reshape(l, 8, 16, 32)[:, :, 0:15, 0:31]


def _rope_table(n_lat, n_ctx):
    t = jnp.arange(n_lat)
    inv = ROPE_BASE ** (-jnp.arange(0, 32, 2, dtype=F32) / 32)
    ar = (t // GRID_W).astype(F32)[:, None] * inv
    ac = (t % GRID_W).astype(F32)[:, None] * inv
    cos = jnp.concatenate([jnp.cos(ar), jnp.cos(ar), jnp.cos(ac), jnp.cos(ac)], axis=-1)
    sin = jnp.concatenate([jnp.sin(ar), jnp.sin(ar), jnp.sin(ac), jnp.sin(ac)], axis=-1)
    tab = jnp.concatenate([cos, sin], axis=-1)
    ident = jnp.concatenate([jnp.ones((n_ctx, 64), F32), jnp.zeros((n_ctx, 64), F32)], axis=-1)
    return jnp.concatenate([tab, ident], axis=0)


def _small_weights(full, mods):
    l = full["g_mix"].shape[0]
    ff = full["ffn_conv_b"].shape[1] // 2
    return dict(
        mods=mods, g_mix=full["g_mix"][:, None, :], g_ffn=full["g_ffn"][:, None, :],
        sp=_sp_pack(full), pool_w=full["pool_w"], pool_sc=full["pool_scale"].reshape(l, 4, 1, 128),
        t1=_rpb_expand(full["na_rpb"]),
        conv_w=full["ffn_conv_w"].reshape(l, 3, 2, ff).transpose(0, 2, 1, 3),
        conv_b=full["ffn_conv_b"].reshape(l, 2, 1, ff))


def _kernel_weights(full, mods):
    w = _small_weights(full, mods)
    w.update(w_in=_w_in_layout(full["w_in"]).astype(BF16), w_out=full["w_out"].astype(BF16),
             w_up=full["ffn_w_up"].astype(BF16), w_down=full["ffn_w_down"].astype(BF16),
             w_qb=_w_qb_layout(full["mla_w_qb"]).astype(BF16), w_kvb=full["mla_w_kvb"].astype(BF16))
    return w


def _reference_grads(grads, big=True):
    st = lambda k: jnp.stack([g[k] for g in grads], axis=0)
    l = len(grads)
    out = dict(
        g_mix=st("g_mix")[:, 0], g_ffn=st("g_ffn")[:, 0],
        pool_w=st("pool_w"), pool_scale=st("pool_sc").reshape(l, 512), na_rpb=_rpb_fold(st("t1")),
        ffn_conv_w=st("conv_w").transpose(0, 2, 1, 3).reshape(l, 3, -1), ffn_conv_b=st("conv_b").reshape(l, -1),
        mods=st("mods"))
    if big:
        out.update(w_in=_w_in_unlayout(st("w_in")), w_out=st("w_out"), ffn_w_up=st("w_up"), ffn_w_down=st("w_down"),
                   mla_w_qb=_w_qb_unlayout(st("w_qb")), mla_w_kvb=st("w_kvb"))
    out.update(_sp_unpack(st("sp")))
    return out


ANY = pl.BlockSpec(memory_space=pl.ANY)


def _flip(x, y, j):
    return (1 - x if j >> 1 else x), (1 - y if j & 1 else y)


def _comm_call(name, ins, out_shapes, n_copies, plan, aliases=None):
    n_in, n_out = len(ins), len(out_shapes)

    def body(*refs):
        in_refs, out_refs = refs[:n_in], refs[n_in:n_in + n_out]
        ssem, rsem = refs[n_in + n_out:]
        pos = (lax.axis_index("x"), lax.axis_index("y"), lax.axis_index("c"))
        copies = plan(in_refs, out_refs, pos)
        assert len(copies) == n_copies
        descs = []
        for i, (src, dst, peer) in enumerate(copies):
            if peer is None:
                d = pltpu.make_async_copy(src, dst, ssem.at[i])
            else:
                d = pltpu.make_async_remote_copy(src_ref=src, dst_ref=dst, send_sem=ssem.at[i], recv_sem=rsem.at[i],
                                                 device_id=peer, device_id_type=MESH)
            d.start()
            descs.append(d)
        for d in descs:
            d.wait()

    return pl.pallas_call(
        body, name=name, in_specs=[ANY] * n_in, out_specs=[ANY] * n_out, out_shape=list(out_shapes),
        input_output_aliases=aliases or {},
        scratch_shapes=[pltpu.SemaphoreType.DMA((n_copies,)), pltpu.SemaphoreType.DMA((n_copies,))])(*ins)


def _sib_fill(bufs, part, name):
    def plan(ins, outs, pos):
        x, y, c = pos
        return [(o_ref.at[part(c)], o_ref.at[part(c)], (x, y, 1 - c)) for o_ref in outs]

    shapes = [jax.ShapeDtypeStruct(b.shape, b.dtype) for b in bufs]
    return _comm_call(name, bufs, shapes, len(bufs), plan, aliases={i: i for i in range(len(bufs))})


HBM = pl.BlockSpec(memory_space=pltpu.HBM)
SEM = pl.BlockSpec(memory_space=pltpu.SEMAPHORE)
DATAFLOW = pltpu.SideEffectType.DATAFLOW_SIDE_EFFECTING


def _remote_start(name, bufs, n_copies, plan, after):
    nb, na = len(bufs), len(after)

    def body(*refs):
        ssem, rsem, token = refs[nb + na], refs[nb + na + 1], refs[-1]
        pos = (lax.axis_index("x"), lax.axis_index("y"), lax.axis_index("c"))
        copies = plan(refs[:nb], pos)
        assert len(copies) == n_copies
        for i, (src, dst, peer) in enumerate(copies):
            pltpu.make_async_remote_copy(src_ref=src, dst_ref=dst, send_sem=ssem.at[i], recv_sem=rsem.at[i],
                                         device_id=peer, device_id_type=MESH).start()
        token[...] = jnp.zeros_like(token)

    outs = pl.pallas_call(
        body, name=name,
        out_shape=(pltpu.SemaphoreType.DMA((n_copies,)), pltpu.SemaphoreType.DMA((n_copies,)),
                   *[pltpu.HBM(b.shape, b.dtype) for b in bufs], jax.ShapeDtypeStruct((8, 128), F32)),
        in_specs=[HBM] * nb + [ANY] * na, out_specs=(SEM, SEM, *[HBM] * nb, pl.BlockSpec(memory_space=pltpu.VMEM)),
        input_output_aliases={i: 2 + i for i in range(nb)},
        compiler_params=pltpu.CompilerParams(has_side_effects=DATAFLOW),
    )(*[pltpu.with_memory_space_constraint(b, pltpu.HBM) for b in bufs], *after)
    return outs[0], outs[1], list(outs[2:2 + nb]), outs[-1]


def _remote_wait(name, ssem, rsem, bufs, n_copies, plan, after):
    nb = len(bufs)

    def body(*refs):
        ssem_ref, rsem_ref = refs[nb], refs[nb + 1]
        pos = (lax.axis_index("x"), lax.axis_index("y"), lax.axis_index("c"))
        copies = plan(refs[:nb], pos)
        assert len(copies) == n_copies
        for i, (src, dst, peer) in enumerate(copies):
            cp = pltpu.make_async_remote_copy(src_ref=src, dst_ref=dst, send_sem=ssem_ref.at[i], recv_sem=rsem_ref.at[i],
                                              device_id=peer, device_id_type=MESH)
            cp.wait_send()
            cp.wait_recv()

    outs = pl.pallas_call(
        body, name=name, out_shape=tuple(pltpu.HBM(b.shape, b.dtype) for b in bufs),
        in_specs=[HBM] * nb + [SEM, SEM, ANY], out_specs=tuple([HBM] * nb), input_output_aliases={i: i for i in range(nb)},
        compiler_params=pltpu.CompilerParams(has_side_effects=DATAFLOW),
    )(*bufs, ssem, rsem, after)
    return list(outs)


BIG_GATHER = ("axis1", "axis1", "axis1", "axis1", "lane", "lane")


def _place_own(shard, kind, xyvec, name):
    rows, cols = shard.shape
    tr = _row_tile(rows, cols)
    if kind == "axis1":
        shape = (4, rows, cols)
        o_spec = pl.BlockSpec((None, tr, cols), lambda i, x_ref, y_ref: (2 * x_ref[0] + y_ref[0], i, 0))
    else:
        shape = (rows, 4 * cols)
        o_spec = pl.BlockSpec((tr, cols), lambda i, x_ref, y_ref: (i, 2 * x_ref[0] + y_ref[0]))

    def body(x_ref, y_ref, s_ref, o_ref):
        o_ref[...] = s_ref[...]

    return pl.pallas_call(
        body, name=name,
        grid_spec=pltpu.PrefetchScalarGridSpec(num_scalar_prefetch=2, grid=(rows // tr,),
                                               in_specs=[pl.BlockSpec((tr, cols), lambda i, x_ref, y_ref: (i, 0))], out_specs=o_spec),
        out_shape=jax.ShapeDtypeStruct(shape, shard.dtype), compiler_params=_params(("parallel",)))(*xyvec, shard)


def _w_gather_plan(shapes, kinds):
    n = len(kinds)

    def plan(refs, pos):
        x, y, c = pos
        k = 2 * x + y
        cps = []
        for s_ref, l_ref, kind, shp in zip(refs[:n], refs[n:], kinds, shapes):
            h, w = shp[0] // 2, shp[1]
            rows = pl.ds(pl.multiple_of(c * h, 16), h)
            dst = l_ref.at[k, rows, :] if kind == "axis1" else l_ref.at[rows, pl.ds(pl.multiple_of(k * w, 128), w)]
            for j in (1, 2, 3):
                tx, ty = _flip(x, y, j)
                cps.append((s_ref.at[rows, :], dst, (tx, ty, c)))
        return cps

    return plan


def _w_fill(lands, shapes, kinds, name):
    def plan(ins, outs, pos):
        x, y, c = pos
        cps = []
        for o_ref, kind, shp in zip(outs, kinds, shapes):
            h = shp[0] // 2
            rows = pl.ds(pl.multiple_of(c * h, 16), h)
            part = o_ref.at[:, rows, :] if kind == "axis1" else o_ref.at[rows, :]
            cps.append((part, part, (x, y, 1 - c)))
        return cps

    return _comm_call(name, lands, [jax.ShapeDtypeStruct(b.shape, b.dtype) for b in lands], len(lands), plan,
                      aliases={i: i for i in range(len(lands))})


def _g_scatter_plan(kinds, widths):
    n = len(kinds)

    def plan(refs, pos):
        x, y, c = pos
        cps = []
        for s_ref, l_ref, kind, w in zip(refs[:n], refs[n:], kinds, widths):
            for j in (1, 2, 3):
                tx, ty = _flip(x, y, j)
                kj = 2 * tx + ty
                src = s_ref.at[kj] if kind == "cm" else s_ref.at[:, pl.ds(pl.multiple_of(kj * w, 128), w)]
                cps.append((src, l_ref.at[j - 1], (tx, ty, c)))
        return cps

    return plan


def _pair_up(xs, name):
    def plan(ins, outs, pos):
        x, y, c = pos
        cps = []
        for i_ref, o_ref in zip(ins, outs):
            cps.append((i_ref, o_ref.at[c], None))
            cps.append((i_ref, o_ref.at[c], (x, y, 1 - c)))
        return cps

    return _comm_call(name, xs, [jax.ShapeDtypeStruct((2,) + a.shape, a.dtype) for a in xs], 2 * len(xs), plan)


def _chip_gather(xs, kinds, name):
    def dst(o_ref, kind, k, x_shape):
        if kind == "lead":
            return o_ref.at[k]
        w = x_shape[-1]
        return o_ref.at[(slice(None),) * (len(x_shape) - 1) + (pl.ds(pl.multiple_of(k * w, 128), w),)]

    def plan(ins, outs, pos):
        x, y, c = pos
        k = 2 * x + y
        cps = []
        for i_ref, o_ref, kind, a in zip(ins, outs, kinds, xs):
            cps.append((i_ref, dst(o_ref, kind, k, a.shape), None))
            for j in (1, 2, 3):
                tx, ty = _flip(x, y, j)
                cps.append((i_ref, dst(o_ref, kind, k, a.shape), (tx, ty, c)))
        return cps

    def oshape(a, kind):
        return (4,) + a.shape if kind == "lead" else a.shape[:-1] + (4 * a.shape[-1],)

    return _comm_call(name, xs, [jax.ShapeDtypeStruct(oshape(a, kd), a.dtype) for a, kd in zip(xs, kinds)], 4 * len(xs), plan)


def _row_tile(rows, cols, budget=1 << 20):
    for t in (2048, 1024, 512, 256, 128, 64, 32, 16, 8):
        if rows % t == 0 and t * cols * 4 <= budget:
            return t
    return rows


def _sum_lead(x, name):
    n, rows, w = x.shape
    tr = _row_tile(rows, w, (1 << 21) // n)

    def body(x_ref, o_ref):
        acc = x_ref[0].astype(F32)
        for j in range(1, n):
            acc = acc + x_ref[j].astype(F32)
        o_ref[...] = acc

    return pl.pallas_call(
        body, name=name, grid=(rows // tr,), in_specs=[pl.BlockSpec((n, tr, w), lambda i: (0, i, 0))],
        out_specs=pl.BlockSpec((tr, w), lambda i: (i, 0)), out_shape=jax.ShapeDtypeStruct((rows, w), F32),
        compiler_params=_params(("parallel",)))(x)


def _sum_into(g, landed, buf, layer, pvec, kind, layers, name):
    n, rows, w = landed.shape
    tr = _row_tile(rows, w)

    def body(c_ref, x_ref, y_ref, own_ref, l_ref, *refs):
        o_ref = refs[-1]
        acc = own_ref[...].astype(F32)
        for j in range(n):
            acc = acc + l_ref[j].astype(F32)
        o_ref[...] = acc

    if kind == "cm":
        own_spec = pl.BlockSpec((None, tr, w), lambda i, c_ref, x_ref, y_ref: (2 * x_ref[0] + y_ref[0], i, 0))
    else:
        own_spec = pl.BlockSpec((tr, w), lambda i, c_ref, x_ref, y_ref: (i, 2 * x_ref[0] + y_ref[0]))
    in_specs = [own_spec, pl.BlockSpec((n, tr, w), lambda i, c_ref, x_ref, y_ref: (0, i, 0))]
    args = [*pvec, g, landed]
    if buf is not None:
        in_specs.append(ANY)
        args.append(buf)
    return pl.pallas_call(
        body, name=name,
        grid_spec=pltpu.PrefetchScalarGridSpec(
            num_scalar_prefetch=3, grid=(rows // tr,), in_specs=in_specs,
            out_specs=pl.BlockSpec((None, None, tr, w), lambda i, c_ref, x_ref, y_ref: (layer, c_ref[0], i, 0))),
        out_shape=jax.ShapeDtypeStruct((layers, 2, rows, w), F32), input_output_aliases={} if buf is None else {5: 0},
        compiler_params=_params(("arbitrary",)))(*args)


def _adamw_math(w, g, m, v):
    mn = ADAM_B1 * m + (1.0 - ADAM_B1) * g
    vn = ADAM_B2 * v + (1.0 - ADAM_B2) * jnp.square(g)
    m_hat = mn / (1.0 - ADAM_B1 ** ADAM_STEP)
    v_hat = vn / (1.0 - ADAM_B2 ** ADAM_STEP)
    return -ADAM_LR * (m_hat / (jnp.sqrt(v_hat) + ADAM_EPS) + ADAM_WD * w), mn, vn


def _adamw_pair(w, g2, m, v, name):
    l, rows, cols = w.shape
    tr = _row_tile(rows, cols)

    def body(w_ref, g0_ref, g1_ref, m_ref, v_ref, g_ref, d_ref, mo_ref, vo_ref):
        gv = g0_ref[...] + g1_ref[...]
        g_ref[...] = gv
        d_ref[...], mo_ref[...], vo_ref[...] = _adamw_math(w_ref[...], gv, m_ref[...], v_ref[...])

    spec = pl.BlockSpec((None, tr, cols), lambda li, i: (li, i, 0))
    half = lambda cc: pl.BlockSpec((None, None, tr, cols), lambda li, i: (li, cc, i, 0))
    return pl.pallas_call(
        body, name=name, grid=(l, rows // tr), in_specs=[spec, half(0), half(1), spec, spec], out_specs=[spec] * 4,
        out_shape=[jax.ShapeDtypeStruct(w.shape, F32)] * 4, compiler_params=_params(("parallel", "parallel")))(w, g2, g2, m, v)


def _adamw(w, g, m, v, name):
    shape = w.shape
    cols = shape[-1]
    rows = w.size // cols
    tr = _row_tile(rows, cols)

    def body(w_ref, g_ref, m_ref, v_ref, d_ref, mo_ref, vo_ref):
        d_ref[...], mo_ref[...], vo_ref[...] = _adamw_math(w_ref[...], g_ref[...], m_ref[...], v_ref[...])

    spec = pl.BlockSpec((tr, cols), lambda i: (i, 0))
    outs = pl.pallas_call(
        body, name=name, grid=(rows // tr,), in_specs=[spec] * 4, out_specs=[spec] * 3,
        out_shape=[jax.ShapeDtypeStruct((rows, cols), F32)] * 3,
        compiler_params=_params(("parallel",)))(*[a.reshape(rows, cols) for a in (w, g, m, v)])
    return [o.reshape(shape) for o in outs]


def _silu_grad(x):
    s = jax.nn.sigmoid(x)
    return s * (1.0 + x * (1.0 - s))


def _mod_fwd(cs16, w_mod, b_sh, name):
    l, d, wc = w_mod.shape
    tn = _pick(wc, (512, 384, 256, 128))

    def body(c_ref, w_ref, b_ref, o_ref):
        a = jax.nn.silu(c_ref[...]).astype(BF16)
        o_ref[...] = jnp.dot(a, w_ref[...].astype(BF16), preferred_element_type=F32) + b_ref[...]

    return pl.pallas_call(
        body, name=name, grid=(l, wc // tn),
        in_specs=[_full_spec((16, d)), pl.BlockSpec((None, d, tn), lambda i, j: (i, 0, j)), pl.BlockSpec((None, 1, tn), lambda i, j: (i, 0, j))],
        out_specs=pl.BlockSpec((None, 16, tn), lambda i, j: (i, 0, j)), out_shape=jax.ShapeDtypeStruct((l, 16, wc), F32),
        compiler_params=_params(("parallel", "parallel")))(cs16, w_mod, b_sh)


def _mod_dw(cs16, dm_sh, name):
    l, _, wc = dm_sh.shape
    d = cs16.shape[1]
    tr = _pick(d, (512, 256, 128))
    tc = _pick(wc, (512, 384, 256, 128))

    def body(c_ref, dm_ref, o_ref):
        a = jax.nn.silu(c_ref[...]).astype(BF16)
        o_ref[...] = lax.dot_general(a, dm_ref[...].astype(BF16), (((0,), (0,)), ((), ())), preferred_element_type=F32)

    return pl.pallas_call(
        body, name=name, grid=(l, d // tr, wc // tc),
        in_specs=[pl.BlockSpec((16, tr), lambda i, r, j: (0, r)), pl.BlockSpec((None, 16, tc), lambda i, r, j: (i, 0, j))],
        out_specs=pl.BlockSpec((None, tr, tc), lambda i, r, j: (i, r, j)), out_shape=jax.ShapeDtypeStruct((l, d, wc), F32),
        compiler_params=_params(("parallel", "parallel", "parallel")))(cs16, dm_sh)


def _mod_dc(dm_sh, w_mod, c_ctx, name):
    l, d, wc = w_mod.shape
    tk = _pick(wc, (512, 384, 256, 128))
    nk = wc // tk

    def body(dm_ref, w_ref, c_ref, o_ref, acc_ref):
        i, j = pl.program_id(0), pl.program_id(1)

        @pl.when(jnp.logical_and(i == 0, j == 0))
        def _():
            acc_ref[...] = jnp.zeros_like(acc_ref)

        acc_ref[...] += lax.dot_general(dm_ref[...].astype(BF16), w_ref[...].astype(BF16), (((1,), (1,)), ((), ())),
                                        preferred_element_type=F32)

        @pl.when(jnp.logical_and(i == l - 1, j == nk - 1))
        def _():
            mine = jnp.where(lax.axis_index("c") == 0, 1.0, 0.0)
            o_ref[...] = acc_ref[8:9, :] * _silu_grad(c_ref[...]) * mine

    return pl.pallas_call(
        body, name=name, grid=(l, nk),
        in_specs=[pl.BlockSpec((None, 16, tk), lambda i, j: (i, 0, j)), pl.BlockSpec((None, d, tk), lambda i, j: (i, 0, j)), _full_spec((1, d))],
        out_specs=_full_spec((1, d)), out_shape=jax.ShapeDtypeStruct((1, d), F32), scratch_shapes=[pltpu.VMEM((16, d), F32)],
        compiler_params=_params(("arbitrary", "arbitrary")))(dm_sh, w_mod, c_ctx)


def _dmod_assemble(gath, name):
    _, l, _, w = gath.shape
    gath = gath.transpose(1, 2, 0, 3)
    tc = _pick(w, (2048, 1024, 512, 256, 128))

    def body(lat_ref, ctx_ref, o_ref, b_ref):
        ctx = ctx_ref[0:1, :]
        for dev in range(1, 8):
            ctx = ctx + ctx_ref[dev:dev + 1, :]
        lat = lat_ref[...]
        o_ref[0:8, :] = lat
        o_ref[8:9, :] = ctx
        o_ref[9:16, :] = jnp.zeros((7, tc), F32)
        b_ref[...] = jnp.sum(lat, axis=0, keepdims=True) + ctx

    return pl.pallas_call(
        body, name=name, grid=(l, w // tc),
        in_specs=[pl.BlockSpec((None, None, 8, tc), lambda i, j: (i, 0, 0, j)), pl.BlockSpec((None, None, 8, tc), lambda i, j: (i, 1, 0, j))],
        out_specs=[pl.BlockSpec((None, 16, tc), lambda i, j: (i, 0, j)), pl.BlockSpec((None, 1, tc), lambda i, j: (i, 0, j))],
        out_shape=[jax.ShapeDtypeStruct((l, 16, w), F32), jax.ShapeDtypeStruct((l, 1, w), F32)],
        compiler_params=_params(("parallel", "parallel")))(gath, gath)


SMALL = ("c_ctx", "g_mix", "g_ffn", "mla_q_a_norm", "mla_kv_a_norm", "mla_q_nope_norm", "mla_q_rope_norm", "mla_k_nope_norm",
         "mla_k_rope_norm", "pool_w", "pool_scale", "swa_q_norm", "swa_k_norm", "swa_sink", "na_q_norm", "na_k_norm", "na_rpb",
         "ffn_conv_b")
PACK_W = 512
PACK_Q = 8 * PACK_W


def _pack(arrs):
    flat = []
    for a in arrs:
        f = a.reshape(-1)
        flat.append(jnp.pad(f, (0, (-f.size) % PACK_Q)))
    return jnp.concatenate(flat).reshape(-1, PACK_W)


def _unpack(packed, shapes):
    flat, out, o = packed.reshape(-1), [], 0
    for s in shapes:
        n = 1
        for dim in s:
            n *= dim
        out.append(flat[o:o + n].reshape(s))
        o += n + (-n) % PACK_Q
    return out


def _all_sum(p, name):
    pair = _pair_up([p], name + "_pair")[0]
    chip = _sum_lead(pair, name + "_sum2")
    return _sum_lead(_chip_gather([chip], ["lead"], name + "_gather")[0], name + "_sum4")


WEIGHTS = ("c_ctx", "w_mod", "b_mod", "g_mix", "g_ffn", "w_in", "w_out", "mla_q_a_norm", "mla_w_qb", "mla_kv_a_norm", "mla_w_kvb",
           "mla_q_nope_norm", "mla_q_rope_norm", "mla_k_nope_norm", "mla_k_rope_norm", "pool_w", "pool_scale", "swa_q_norm",
           "swa_k_norm", "swa_sink", "na_q_norm", "na_k_norm", "na_rpb", "ffn_w_up", "ffn_conv_w", "ffn_conv_b", "ffn_w_down")
BIG = ("w_in", "mla_w_qb", "w_out", "ffn_w_down", "ffn_w_up", "mla_w_kvb")
BIG_KINDS = ("cm", "cm", "cm", "cm", "lb", "lb")


def _step(a):
    x, c, ctx = a["x"], a["c"], a["ctx"]
    n_lat, d = x.shape[1], x.shape[2]
    n_ctx = ctx.shape[1]
    l = DEPTH
    px, py, pc = lax.axis_index("x"), lax.axis_index("y"), lax.axis_index("c")
    chip = 2 * px + py
    pvec = [p.reshape(1).astype(I32) for p in (pc, px, py)]
    cvec = pvec[0]

    bf = {k: a[k].astype(BF16) for k in BIG}
    w = {key: [None] * l for key in ("w_in", "w_qb", "w_out", "w_down", "w_up", "w_kvb")}

    finish = {"w_in": lambda g: _w_in_layout(g.transpose(1, 0, 2).reshape(d, P_COLS)),
              "mla_w_qb": lambda g: _w_qb_layout(g.transpose(1, 0, 2).reshape(512, 768)),
              "w_out": lambda g: g.reshape(-1, d), "ffn_w_down": lambda g: g.reshape(-1, d),
              "ffn_w_up": lambda g: g, "mla_w_kvb": lambda g: g}
    slot = dict(zip(BIG, ("w_in", "w_qb", "w_out", "w_down", "w_up", "w_kvb")))
    kind_of = dict(zip(BIG, BIG_GATHER))

    def gather_start(li, keys, tag, after):
        shards = [bf[k][li] for k in keys]
        shapes = [s.shape for s in shards]
        kinds = [kind_of[k] for k in keys]
        lands = [_place_own(s, kd, pvec[1:], "w_place") for s, kd in zip(shards, kinds)]
        plan = _w_gather_plan(shapes, kinds)
        return _remote_start(f"w_start_{li}{tag}", shards + lands, 3 * len(keys), plan, after) + (shapes, kinds, keys, plan)

    def gather_finish(li, started, tag, after):
        ssem, rsem, bufs, _, shapes, kinds, keys, plan = started
        bufs = _remote_wait(f"w_wait_{li}{tag}", ssem, rsem, bufs, 3 * len(keys), plan, after)
        for k, g in zip(keys, _w_fill(bufs[len(keys):], shapes, kinds, "w_fill")):
            w[slot[k]][li] = finish[k](g)

    first, rest = ("w_in", "mla_w_qb", "mla_w_kvb"), ("w_out", "ffn_w_down", "ffn_w_up")
    started = gather_start(0, first, "a", [c])
    c = c + started[3][0:1, 0:1]

    c_all = _chip_gather(_pair_up([c], "c_pair"), ["lead"], "c_gather")[0].reshape(8, d)
    cs16 = jnp.concatenate([c_all, a["c_ctx"][None, :], jnp.zeros((7, d), F32)], axis=0)
    wc = a["w_mod"].shape[-1]
    b_sh = lax.dynamic_slice_in_dim(a["b_mod"], chip * wc, wc, axis=1)[:, None, :]
    mod_sh = _mod_fwd(cs16, a["w_mod"], b_sh, "mod_fwd")
    mod_all, conv_w_full = _chip_gather([mod_sh, a["ffn_conv_w"]], ["lead", "lane"], "mod_gather")
    started_rest = gather_start(0, rest, "b", [mod_all])
    mod_all = mod_all + started_rest[3][0, 0]
    mod_all = mod_all.transpose(1, 2, 0, 3).reshape(l, 16, 4 * wc)
    mods = jnp.stack([lax.dynamic_index_in_dim(mod_all, 2 * chip + pc, axis=1, keepdims=False), mod_all[:, 8]], axis=1)
    mods = mods.reshape(l, 2, 6, d)

    full = {k: a[k] for k in SMALL if k != "c_ctx"}
    full["ffn_conv_w"] = conv_w_full
    w.update(_small_weights(full, mods))
    w["mods"] = [w["mods"][li] for li in range(l)]
    w["g_mix"] = [w["g_mix"][li] for li in range(l)]
    w["g_ffn"] = [w["g_ffn"][li] for li in range(l)]
    tab = _rope_table(n_lat, n_ctx)
    gather_finish(0, started, "a", mods)
    xs = jnp.concatenate([x[0], ctx[0]], axis=0)
    saved = []
    nxt = {}

    def rest_of_layer0(mix):
        gather_finish(0, started_rest, "b", mix)
        nxt[1] = gather_start(1, BIG, "", [w["w_up"][0]])
        w["g_ffn"][0] = w["g_ffn"][0] + nxt[1][3][0:1, 0:1]

    for li in range(l):
        if 1 <= li < l - 1:
            nxt[li + 1] = gather_start(li + 1, BIG, "", [w["w_kvb"][li]])
            w["g_mix"][li] = w["g_mix"][li] + nxt[li + 1][3][0:1, 0:1]
        xs, s = _layer_fwd(xs, w, li, tab, n_lat, before_out=rest_of_layer0 if li == 0 else None)
        saved.append(s)
        if li + 1 < l:
            gather_finish(li + 1, nxt[li + 1], "", xs)
    loss, dx = _loss_kernel(xs, a["loss_target"][0], n_lat // TM, "loss")
    loss = lax.psum(loss[0, 0], ("x", "y", "c"))

    grads = [None] * l
    g_buf = {k: None for k in BIG}
    kind_g = dict(zip(BIG, BIG_KINDS))
    ffn_keys = ("ffn_w_down", "ffn_w_up")
    att_keys = ("w_in", "mla_w_qb", "w_out", "mla_w_kvb")

    def grad_pieces(g, keys):
        ops = {"w_in": lambda: _w_in_unlayout(g["w_in"]).reshape(d, 4, -1).transpose(1, 0, 2),
               "mla_w_qb": lambda: _w_qb_unlayout(g["w_qb"].astype(GRAD_WIRE)).reshape(512, 4, 192).transpose(1, 0, 2),
               "w_out": lambda: g["w_out"].reshape(4, -1, d), "ffn_w_down": lambda: g["w_down"].reshape(4, -1, d),
               "ffn_w_up": lambda: g["w_up"], "mla_w_kvb": lambda: g["w_kvb"].astype(GRAD_WIRE)}
        return [ops[k]() for k in keys]

    def scatter_start(li, g, keys, tag, after):
        ops = grad_pieces(g, keys)
        lands = [jnp.zeros((3, a[k].shape[1], a[k].shape[2]), GRAD_WIRE) for k in keys]
        plan = _g_scatter_plan([kind_g[k] for k in keys], [a[k].shape[2] for k in keys])
        return _remote_start(f"g_start_{li}{tag}", ops + lands, 3 * len(keys), plan, after) + (keys, plan)

    def scatter_finish(li, started, tag, after):
        ssem, rsem, bufs, _, keys, plan = started
        bufs = _remote_wait(f"g_wait_{li}{tag}", ssem, rsem, bufs, 3 * len(keys), plan, after)
        for k, own, landed in zip(keys, bufs[:len(keys)], bufs[len(keys):]):
            g_buf[k] = _sum_into(own, landed, g_buf[k], li, pvec, kind_g[k], l, "g_sum4")

    pending, early = None, []

    def ffn_grads_of_layer0(g):
        early.append(scatter_start(0, g, ffn_keys, "a", [g["w_up"]]))
        return early[0][3]

    for li in reversed(range(l)):
        dx, grads[li] = _layer_bwd(dx, saved[li], w, li, tab, n_lat, after_ffn=ffn_grads_of_layer0 if li == 0 else None)
        if pending is not None:
            scatter_finish(li + 1, pending, "", dx)
        if li > 0:
            pending = scatter_start(li, grads[li], BIG, "", [grads[li]["w_in"]])
            w["mods"][li - 1] = w["mods"][li - 1] + pending[3][0, 0]

    dmods = jnp.stack([grads[li]["mods"] for li in range(l)], axis=0).reshape(l, 2, 6 * d)
    dm_gath = _chip_gather(_pair_up([dmods], "dmod_pair"), ["lead"], "dmod_gather")[0].reshape(8, l, 2, 6 * d)
    dmod_all, g_b_mod = _dmod_assemble(dm_gath, "dmod_assemble")
    dm_sh = lax.dynamic_slice_in_dim(dmod_all, chip * wc, wc, axis=2)
    g_c_ctx = _mod_dc(dm_sh, a["w_mod"], a["c_ctx"][None, :], "mod_dc")

    rg = _reference_grads(grads, big=False)
    rg["c_ctx"] = g_c_ctx[0]
    packed = _all_sum(_pack([rg[k] for k in SMALL] + [rg["ffn_conv_w"]]), "small")
    late = scatter_start(0, grads[0], att_keys, "b", [packed, dmod_all])
    g_w_mod = _mod_dw(cs16, dm_sh + late[3][0, 0], "mod_dw")
    g_out = {"w_mod": g_w_mod, "b_mod": g_b_mod.reshape(l, 6 * d)}
    small_g = _unpack(packed, [a[k].shape for k in SMALL] + [rg["ffn_conv_w"].shape])
    for k, g in zip(SMALL, small_g[:-1]):
        g_out[k] = g
    cw = a["ffn_conv_w"].shape[-1]
    g_out["ffn_conv_w"] = lax.dynamic_slice_in_dim(small_g[-1], chip * cw, cw, axis=2)

    upd = {}
    pk = lambda pre: _pack([a[pre + k] for k in SMALL])
    outs = _adamw(pk(""), _pack([g_out[k] for k in SMALL]), pk("m_"), pk("v_"), "adamw_small")
    for o, kind in zip(outs, ("delta", "m", "v")):
        for k, val in zip(SMALL, _unpack(o, [a[k].shape for k in SMALL])):
            upd[kind, k] = val
    def adamw_each(keys):
        for k in keys:
            outs = _adamw(a[k], g_out[k], a["m_" + k], a["v_" + k], "adamw_" + k)
            for o, kind in zip(outs, ("delta", "m", "v")):
                upd[kind, k] = o

    adamw_each(("w_mod", "b_mod", "ffn_conv_w"))
    scatter_finish(0, early[0], "a", upd["delta", "w_mod"])
    scatter_finish(0, late, "b", upd["delta", "w_mod"])
    g_big = _sib_fill([g_buf[k] for k in BIG], lambda cc: (slice(None), cc), "g_pair")
    for k, g2 in zip(BIG, g_big):
        g_out[k], upd["delta", k], upd["m", k], upd["v", k] = _adamw_pair(a[k], g2, a["m_" + k], a["v_" + k], "adamw_" + k)
    grad_x = dx[0:n_lat].reshape(x.shape)
    return (loss, grad_x, *[g_out[k] for k in WEIGHTS], *[upd["delta", k] for k in WEIGHTS],
            *[upd["m", k] for k in WEIGHTS], *[upd["v", k] for k in WEIGHTS])


def kernel(x, c, ctx, c_ctx, w_mod, b_mod, g_mix, g_ffn, w_in, w_out, mla_q_a_norm, mla_w_qb, mla_kv_a_norm, mla_w_kvb, mla_q_nope_norm, mla_q_rope_norm, mla_k_nope_norm, mla_k_rope_norm, pool_w, pool_scale, swa_q_norm, swa_k_norm, swa_sink, na_q_norm, na_k_norm, na_rpb, ffn_w_up, ffn_conv_w, ffn_conv_b, ffn_w_down, loss_target, m_c_ctx, m_w_mod, m_b_mod, m_g_mix, m_g_ffn, m_w_in, m_w_out, m_mla_q_a_norm, m_mla_w_qb, m_mla_kv_a_norm, m_mla_w_kvb, m_mla_q_nope_norm, m_mla_q_rope_norm, m_mla_k_nope_norm, m_mla_k_rope_norm, m_pool_w, m_pool_scale, m_swa_q_norm, m_swa_k_norm, m_swa_sink, m_na_q_norm, m_na_k_norm, m_na_rpb, m_ffn_w_up, m_ffn_conv_w, m_ffn_conv_b, m_ffn_w_down, v_c_ctx, v_w_mod, v_b_mod, v_g_mix, v_g_ffn, v_w_in, v_w_out, v_mla_q_a_norm, v_mla_w_qb, v_mla_kv_a_norm, v_mla_w_kvb, v_mla_q_nope_norm, v_mla_q_rope_norm, v_mla_k_nope_norm, v_mla_k_rope_norm, v_pool_w, v_pool_scale, v_swa_q_norm, v_swa_k_norm, v_swa_sink, v_na_q_norm, v_na_k_norm, v_na_rpb, v_ffn_w_up, v_ffn_conv_w, v_ffn_conv_b, v_ffn_w_down):
    return _step(dict(locals()))
```

```python
import functools

import jax
import jax.numpy as jnp
from jax import lax
from jax.experimental import pallas as pl
from jax.experimental.pallas import tpu as pltpu

F32 = jnp.float32
BF16 = jnp.bfloat16
I32 = jnp.int32

DEPTH = 4
GRID_W = 64
ROPE_BASE = 10000.0
EPS = 1e-6
NEG = -1e30
MLA_SCALE = 192.0 ** -0.5
HD_SCALE = 64.0 ** -0.5
NA_KROWS = 12
SWA_KEYS = 512
P_COLS = 3648
PW = 3840
TM = 256
HALO = 8
ADAM_LR, ADAM_B1, ADAM_B2, ADAM_EPS, ADAM_WD, ADAM_STEP = 0.001, 0.9, 0.999, 1e-08, 0.01, 10
VMEM_LIMIT = 56 * 1024 * 1024
GRAD_WIRE = BF16
MESH = pl.DeviceIdType.MESH


def _pick(n, cands):
    for c in cands:
        if n % c == 0:
            return c
    return n


def _params(sem=None):
    return pltpu.CompilerParams(dimension_semantics=sem, vmem_limit_bytes=VMEM_LIMIT)


@jax.custom_vjp
def _bdot(a, b):
    return jnp.dot(a.astype(BF16), b.astype(BF16), preferred_element_type=F32)


def _bdot_fwd(a, b):
    return _bdot(a, b), (a.astype(BF16), b.astype(BF16))


def _bdot_bwd(res, g):
    a, b = res
    gb = g.astype(BF16)
    da = lax.dot_general(gb, b, (((1,), (1,)), ((), ())), preferred_element_type=F32)
    db = lax.dot_general(a, gb, (((0,), (0,)), ((), ())), preferred_element_type=F32)
    return da, db


_bdot.defvjp(_bdot_fwd, _bdot_bwd)


@jax.custom_vjp
def _bdot_nt(a, b):
    return lax.dot_general(a.astype(BF16), b.astype(BF16), (((1,), (1,)), ((), ())), preferred_element_type=F32)


def _bdot_nt_fwd(a, b):
    return _bdot_nt(a, b), (a.astype(BF16), b.astype(BF16))


def _bdot_nt_bwd(res, g):
    a, b = res
    gb = g.astype(BF16)
    da = jnp.dot(gb, b, preferred_element_type=F32)
    db = lax.dot_general(gb, a, (((0,), (0,)), ((), ())), preferred_element_type=F32)
    return da, db


_bdot_nt.defvjp(_bdot_nt_fwd, _bdot_nt_bwd)


def _rms(x, g):
    return x * lax.rsqrt(jnp.mean(x * x, axis=-1, keepdims=True) + EPS) * g


def _rope(x, cos, sin):
    xr = jnp.concatenate([-x[:, 16:32], x[:, 0:16], -x[:, 48:64], x[:, 32:48]], axis=-1)
    return x * cos + xr * sin


def _sel(is_ctx, mod, row):
    return jnp.where(is_ctx, mod[1, row:row + 1, :], mod[0, row:row + 1, :])


MM_VMEM_BUDGET = 40 * 1024 * 1024
HBM_BYTES_PER_STEP = 1 << 20


def _mm_tiles(m, n, k, mode, osize, n_unit, k_unit):
    lanes = (3840, 2816, 2048, 1280, 1024, 768, 512, 256)
    subl = (4352, 2176, 1088, 1024, 640, 544, 512, 256)
    tms = [c for c in (lanes if mode == "tn" else subl) if m % c == 0] or [m]
    tns = [c for c in lanes if n_unit % c == 0] or [n_unit]
    tks = [c for c in (subl if mode == "tn" else lanes) if k_unit % c == 0]
    if k_unit == k:
        tks = [k] + tks
    best = None
    for tm in tms:
        for tn in tns:
            for tk in tks:
                nk = k // tk
                vmem = 4 * (tm * tk + tk * tn) + 2 * tm * tn * osize + tm * tn * 4
                if vmem > MM_VMEM_BUDGET:
                    continue
                a_reads = 1 if nk == 1 else n // tn
                b_reads = 1 if (nk == 1 and n == tn) else m // tm
                steps = (m // tm) * (n // tn) * nk
                cost = (2 * m * k * a_reads + 2 * k * n * b_reads + m * n * osize + steps * HBM_BYTES_PER_STEP
                        + (12 * m * n * nk if nk > 1 else 0))
                if best is None or cost < best[0]:
                    best = (cost, tm, tn, tk)
    return best[1:]


def _mm(a, b, mode, out_dtype, name, a_split=False, b_split=False, o_split=False):
    def dims(x, split):
        return (x.shape[1], 2 * x.shape[2]) if split else x.shape

    ar, ac = dims(a, a_split)
    br, bc = dims(b, b_split)
    if mode == "nn":
        m, k, n = ar, ac, bc
        assert br == k
    elif mode == "nt":
        m, k, n = ar, ac, br
        assert bc == k
    else:
        k, m, n = ar, ac, bc
        assert br == k
    n_unit = n // 2 if (o_split or (b_split and mode != "nt")) else n
    k_unit = k // 2 if (mode != "tn" and (a_split or (b_split and mode == "nt"))) else k
    tm, tn, tk = _mm_tiles(m, n, k, mode, jnp.dtype(out_dtype).itemsize, n_unit, k_unit)
    nk = k // tk

    def spec(split, tr, tc, ncols, ridx, cidx):
        if not split:
            return pl.BlockSpec((tr, tc), lambda i, j, kk: (ridx(i, j, kk), cidx(i, j, kk)))
        nh = (ncols // 2) // tc
        return pl.BlockSpec((None, tr, tc), lambda i, j, kk: (cidx(i, j, kk) // nh, ridx(i, j, kk), cidx(i, j, kk) % nh))

    gi = lambda i, j, kk: i
    gj = lambda i, j, kk: j
    gk = lambda i, j, kk: kk
    if mode == "tn":
        a_spec = spec(a_split, tk, tm, ac, gk, gi)
    else:
        a_spec = spec(a_split, tm, tk, ac, gi, gk)
    if mode == "nt":
        b_spec = spec(b_split, tn, tk, bc, gj, gk)
    else:
        b_spec = spec(b_split, tk, tn, bc, gk, gj)
    o_spec = spec(o_split, tm, tn, n, gi, gj)
    dn = {"nn": (((1,), (0,)), ((), ())), "nt": (((1,), (1,)), ((), ())), "tn": (((0,), (0,)), ((), ()))}[mode]

    def body(a_ref, b_ref, o_ref, acc_ref):
        kk = pl.program_id(2)

        @pl.when(kk == 0)
        def _():
            acc_ref[...] = jnp.zeros_like(acc_ref)

        acc_ref[...] += lax.dot_general(a_ref[...], b_ref[...], dn, preferred_element_type=F32)

        @pl.when(kk == nk - 1)
        def _():
            o_ref[...] = acc_ref[...].astype(o_ref.dtype)

    def body_whole_k(a_ref, b_ref, o_ref):
        o_ref[...] = lax.dot_general(a_ref[...], b_ref[...], dn, preferred_element_type=F32).astype(o_ref.dtype)

    oshape = (2, m, n // 2) if o_split else (m, n)
    return pl.pallas_call(
        body if nk > 1 else body_whole_k, name=name, grid=(m // tm, n // tn, nk), in_specs=[a_spec, b_spec], out_specs=o_spec,
        out_shape=jax.ShapeDtypeStruct(oshape, out_dtype), scratch_shapes=[pltpu.VMEM((tm, tn), F32)] if nk > 1 else [],
        compiler_params=_params(("parallel", "parallel", "arbitrary")))(a, b)


def _mm_exact(a, b, name):
    def body(a_ref, b_ref, o_ref):
        o_ref[...] = jnp.dot(a_ref[...], b_ref[...], preferred_element_type=F32, precision=lax.Precision.HIGHEST)

    return pl.pallas_call(body, name=name, out_shape=jax.ShapeDtypeStruct((a.shape[0], b.shape[1]), F32),
                          compiler_params=_params())(a, b)


def _row_spec(width, col=0):
    return pl.BlockSpec((TM, width), lambda i: (i, col))


def _full_spec(shape):
    nd = len(shape)
    return pl.BlockSpec(shape, lambda *_: (0,) * nd)


def _normmod_fn(x, g, mod, is_ctx, row):
    return _rms(x, g) * (1.0 + _sel(is_ctx, mod, row + 1)) + _sel(is_ctx, mod, row)


def _normmod_fwd(x, g, mod, row, nbl, name):
    r, d = x.shape

    def body(x_ref, g_ref, mod_ref, h_ref):
        is_ctx = pl.program_id(0) >= nbl
        h_ref[...] = _normmod_fn(x_ref[...], g_ref[...], mod_ref[...], is_ctx, row).astype(BF16)

    return pl.pallas_call(
        body, name=name, grid=(r // TM,), in_specs=[_row_spec(d), _full_spec((1, d)), _full_spec(mod.shape)],
        out_specs=_row_spec(d), out_shape=jax.ShapeDtypeStruct((r, d), BF16), compiler_params=_params(("parallel",)))(x, g, mod)


def _normmod_bwd(x, g, mod, dh, dx_in, row, nbl, name):
    r, d = x.shape

    def body(x_ref, g_ref, mod_ref, dh_ref, dxin_ref, dx_ref, dg_ref, dmod_ref):
        i = pl.program_id(0)
        is_ctx = i >= nbl

        @pl.when(i == 0)
        def _():
            dg_ref[...] = jnp.zeros_like(dg_ref)
            dmod_ref[...] = jnp.zeros_like(dmod_ref)

        _, vjp = jax.vjp(lambda xx, gg, mm: _normmod_fn(xx, gg, mm, is_ctx, row), x_ref[...], g_ref[...], mod_ref[...])
        dx, dg, dmod = vjp(dh_ref[...])
        dx_ref[...] = dxin_ref[...] + dx
        dg_ref[...] += dg
        dmod_ref[...] += dmod

    return pl.pallas_call(
        body, name=name, grid=(r // TM,),
        in_specs=[_row_spec(d), _full_spec((1, d)), _full_spec(mod.shape), _row_spec(d), _row_spec(d)],
        out_specs=[_row_spec(d), _full_spec((1, d)), _full_spec(mod.shape)],
        out_shape=[jax.ShapeDtypeStruct((r, d), F32), jax.ShapeDtypeStruct((1, d), F32), jax.ShapeDtypeStruct(mod.shape, F32)],
        compiler_params=_params(("arbitrary",)))(x, g, mod, dh, dx_in)


def _resid_fwd(x, y, mod, row, nbl, name):
    r, d = x.shape

    def body(x_ref, y_ref, mod_ref, o_ref):
        is_ctx = pl.program_id(0) >= nbl
        o_ref[...] = x_ref[...] + _sel(is_ctx, mod_ref[...], row) * y_ref[...]

    return pl.pallas_call(
        body, name=name, grid=(r // TM,), in_specs=[_row_spec(d), _row_spec(d), _full_spec(mod.shape)],
        out_specs=_row_spec(d), out_shape=jax.ShapeDtypeStruct((r, d), F32), compiler_params=_params(("parallel",)))(x, y, mod)


def _resid_bwd(dx, y, mod, row, nbl, name):
    r, d = dx.shape

    def body(dx_ref, y_ref, mod_ref, dy_ref, dmod_ref):
        i = pl.program_id(0)
        is_ctx = i >= nbl

        @pl.when(i == 0)
        def _():
            dmod_ref[...] = jnp.zeros_like(dmod_ref)

        dxv = dx_ref[...]
        dy_ref[...] = (_sel(is_ctx, mod_ref[...], row) * dxv).astype(BF16)
        dgate = jnp.sum(dxv * y_ref[...], axis=0, keepdims=True)

        @pl.when(is_ctx)
        def _():
            dmod_ref[1, row:row + 1, :] += dgate

        @pl.when(jnp.logical_not(is_ctx))
        def _():
            dmod_ref[0, row:row + 1, :] += dgate

    return pl.pallas_call(
        body, name=name, grid=(r // TM,), in_specs=[_row_spec(d), _row_spec(d), _full_spec(mod.shape)],
        out_specs=[_row_spec(d), _full_spec(mod.shape)],
        out_shape=[jax.ShapeDtypeStruct((r, d), BF16), jax.ShapeDtypeStruct(mod.shape, F32)],
        compiler_params=_params(("arbitrary",)))(dx, y, mod)


def _loss_kernel(x, target, nbl, name):
    r, d = x.shape

    def body(x_ref, t_ref, loss_ref, dx_ref):
        i = pl.program_id(0)

        @pl.when(i == 0)
        def _():
            loss_ref[...] = jnp.zeros_like(loss_ref)

        @pl.when(i < nbl)
        def _():
            e = x_ref[...] - t_ref[...]
            dx_ref[...] = e / d
            loss_ref[...] += 0.5 * jnp.sum(jnp.mean(e * e, axis=-1, keepdims=True), axis=0, keepdims=True)

        @pl.when(i >= nbl)
        def _():
            dx_ref[...] = jnp.zeros_like(dx_ref)

    return pl.pallas_call(
        body, name=name, grid=(r // TM,),
        in_specs=[_row_spec(d), pl.BlockSpec((TM, d), lambda i: (jnp.minimum(i, nbl - 1), 0))],
        out_specs=[_full_spec((1, 1)), _row_spec(d)],
        out_shape=[jax.ShapeDtypeStruct((1, 1), F32), jax.ShapeDtypeStruct((r, d), F32)],
        compiler_params=_params(("arbitrary",)))(x, target)


SP_QA, SP_KVA, SP_QN, SP_QR, SP_KN, SP_KR, SP_SQ, SP_SK, SP_NQ, SP_NK, SP_SINK = range(11)
C_CQ, C_CKV, C_KR, C_POOL, C_SQ, C_SK, C_SV, C_NQ, C_NK, C_NV = 0, 512, 768, 896, 1408, 1920, 2048, 2176, 2688, 3200


def _prep_fn(p, tab, sp, wqb, wkvb):
    cos, sin = tab[:, 0:64], tab[:, 64:128]
    q = _bdot(_rms(p[:, C_CQ:C_CQ + 512], sp[SP_QA:SP_QA + 1, 0:512]), wqb)
    kv = _bdot(_rms(p[:, C_CKV:C_CKV + 256], sp[SP_KVA:SP_KVA + 1, 0:256]), wkvb)
    krr = _rope(_rms(p[:, C_KR:C_KR + 64], sp[SP_KR:SP_KR + 1, 0:64]), cos, sin)
    zero = jnp.zeros_like(krr)
    aq, ak, av = [], [], []
    for h in range(4):
        qn = _rms(q[:, 128 * h:128 * h + 128], sp[SP_QN:SP_QN + 1, 0:128])
        qr = _rope(_rms(q[:, 512 + 64 * h:576 + 64 * h], sp[SP_QR:SP_QR + 1, 0:64]), cos, sin)
        kn = _rms(kv[:, 256 * h:256 * h + 128], sp[SP_KN:SP_KN + 1, 0:128])
        aq += [qn, qr, zero]
        ak += [kn, krr, zero]
        av.append(kv[:, 256 * h + 128:256 * h + 256])
    cq = [_rope(_rms(p[:, C_SQ + 64 * h:C_SQ + 64 * h + 64], sp[SP_SQ:SP_SQ + 1, 0:64]), cos, sin) for h in range(8)]
    ck = [_rope(_rms(p[:, C_SK + 64 * h:C_SK + 64 * h + 64], sp[SP_SK:SP_SK + 1, 0:64]), cos, sin) for h in range(2)]
    dq = [_rms(p[:, C_NQ + 64 * h:C_NQ + 64 * h + 64], sp[SP_NQ:SP_NQ + 1, 0:64]) for h in range(8)]
    dk = [_rms(p[:, C_NK + 64 * h:C_NK + 64 * h + 64], sp[SP_NK:SP_NK + 1, 0:64]) for h in range(8)]
    cat = lambda xs: jnp.concatenate(xs, axis=-1)
    return (cat(aq), cat(ak), cat(av), cat(cq), cat(ck), p[:, C_SV:C_SV + 128], cat(dq), cat(dk), p[:, C_NV:C_NV + 512])


PREP_WIDTHS = (1024, 1024, 512, 512, 128, 128, 512, 512, 512)


def _prep_fwd(p, tab, sp, wqb, wkvb, name):
    r = p.shape[0]

    def body(p_ref, tab_ref, sp_ref, wqb_ref, wkvb_ref, *outs):
        res = _prep_fn(p_ref[...], tab_ref[...], sp_ref[...], wqb_ref[...].astype(F32), wkvb_ref[...].astype(F32))
        for o_ref, v in zip(outs, res):
            o_ref[...] = v.astype(BF16)

    return pl.pallas_call(
        body, name=name, grid=(r // TM,),
        in_specs=[_row_spec(PW), _row_spec(128), _full_spec(sp.shape), _full_spec(wqb.shape), _full_spec(wkvb.shape)],
        out_specs=[_row_spec(w) for w in PREP_WIDTHS],
        out_shape=[jax.ShapeDtypeStruct((r, w), BF16) for w in PREP_WIDTHS],
        compiler_params=_params(("parallel",)))(p, tab, sp, wqb, wkvb)


def _prep_bwd(p, tab, sp, wqb, wkvb, cots, dpool, name):
    r = p.shape[0]

    def body(p_ref, tab_ref, sp_ref, wqb_ref, wkvb_ref, *rest):
        cot_refs, dpool_ref = rest[:9], rest[9]
        dp_ref, dsp_ref, dwqb_ref, dwkvb_ref = rest[10:]
        i = pl.program_id(0)

        @pl.when(i == 0)
        def _():
            dsp_ref[...] = jnp.zeros_like(dsp_ref)
            dwqb_ref[...] = jnp.zeros_like(dwqb_ref)
            dwkvb_ref[...] = jnp.zeros_like(dwkvb_ref)

        tab = tab_ref[...]
        _, vjp = jax.vjp(lambda pp, ss, wq, wk: _prep_fn(pp, tab, ss, wq, wk),
                         p_ref[...], sp_ref[...], wqb_ref[...].astype(F32), wkvb_ref[...].astype(F32))
        dp, dsp, dwq, dwk = vjp(tuple(c[...] for c in cot_refs))
        dp_ref[...] = dp.astype(BF16)
        dp_ref[:, C_POOL:C_POOL + 512] = dpool_ref[...].astype(BF16)
        dsp_ref[...] += dsp
        dwqb_ref[...] += dwq
        dwkvb_ref[...] += dwk

    return pl.pallas_call(
        body, name=name, grid=(r // TM,),
        in_specs=[_row_spec(PW), _row_spec(128), _full_spec(sp.shape), _full_spec(wqb.shape), _full_spec(wkvb.shape)]
        + [_row_spec(w) for w in PREP_WIDTHS] + [_row_spec(512)],
        out_specs=[_row_spec(PW), _full_spec(sp.shape), _full_spec(wqb.shape), _full_spec(wkvb.shape)],
        out_shape=[jax.ShapeDtypeStruct((r, PW), BF16), jax.ShapeDtypeStruct(sp.shape, F32),
                   jax.ShapeDtypeStruct(wqb.shape, F32), jax.ShapeDtypeStruct(wkvb.shape, F32)],
        compiler_params=_params(("arbitrary",)))(p, tab, sp, wqb, wkvb, *cots, dpool)


def _mla_scores(q, k, is_ctx, n_lat):
    s = lax.dot_general(q, k, (((1,), (1,)), ((), ())), preferred_element_type=F32) * MLA_SCALE
    kid = lax.broadcasted_iota(I32, (1, s.shape[1]), 1)
    return s + jnp.where(jnp.logical_and(is_ctx, kid < n_lat), NEG, 0.0)


def _mla_fwd(aq, ak, av, n_lat, name):
    r = aq.shape[0]
    nbl = n_lat // TM

    def body(q_ref, k_ref, v_ref, o_ref, lse_ref):
        s = _mla_scores(q_ref[...], k_ref[...], pl.program_id(1) >= nbl, n_lat)
        m = jnp.max(s, axis=-1, keepdims=True)
        e = jnp.exp(s - m)
        l = jnp.sum(e, axis=-1, keepdims=True)
        lse_ref[...] = m + jnp.log(l)
        o_ref[...] = jnp.dot((e * (1.0 / l)).astype(BF16), v_ref[...], preferred_element_type=F32).astype(BF16)

    return pl.pallas_call(
        body, name=name, grid=(4, r // TM),
        in_specs=[pl.BlockSpec((TM, 256), lambda h, i: (i, h)), pl.BlockSpec((r, 256), lambda h, i: (0, h)),
                  pl.BlockSpec((r, 128), lambda h, i: (0, h))],
        out_specs=[pl.BlockSpec((TM, 128), lambda h, i: (i, h)), pl.BlockSpec((None, TM, 1), lambda h, i: (h, i, 0))],
        out_shape=[jax.ShapeDtypeStruct((r, 512), BF16), jax.ShapeDtypeStruct((4, r, 1), F32)],
        compiler_params=_params(("parallel", "parallel")))(aq, ak, av)


def _mla_bwd(aq, ak, av, lse, dmix, n_lat, name):
    r = aq.shape[0]
    nbl = n_lat // TM

    def body(q_ref, k_ref, v_ref, lse_ref, do_ref, dq_ref, dk_ref, dv_ref):
        i = pl.program_id(1)

        @pl.when(i == 0)
        def _():
            dk_ref[...] = jnp.zeros_like(dk_ref)
            dv_ref[...] = jnp.zeros_like(dv_ref)

        q, k, v = q_ref[...], k_ref[...], v_ref[...]
        dob = do_ref[...].astype(BF16)
        p = jnp.exp(_mla_scores(q, k, i >= nbl, n_lat) - lse_ref[...])
        dv_ref[...] += lax.dot_general(p.astype(BF16), dob, (((0,), (0,)), ((), ())), preferred_element_type=F32)
        dp = lax.dot_general(dob, v, (((1,), (1,)), ((), ())), preferred_element_type=F32)
        ds = (p * (dp - jnp.sum(dp * p, axis=-1, keepdims=True)) * MLA_SCALE).astype(BF16)
        dq_ref[...] = jnp.dot(ds, k, preferred_element_type=F32)
        dk_ref[...] += lax.dot_general(ds, q, (((0,), (0,)), ((), ())), preferred_element_type=F32)

    return pl.pallas_call(
        body, name=name, grid=(4, r // TM),
        in_specs=[pl.BlockSpec((TM, 256), lambda h, i: (i, h)), pl.BlockSpec((r, 256), lambda h, i: (0, h)),
                  pl.BlockSpec((r, 128), lambda h, i: (0, h)), pl.BlockSpec((None, TM, 1), lambda h, i: (h, i, 0)),
                  pl.BlockSpec((TM, 128), lambda h, i: (i, h))],
        out_specs=[pl.BlockSpec((TM, 256), lambda h, i: (i, h)), pl.BlockSpec((r, 256), lambda h, i: (0, h)),
                   pl.BlockSpec((r, 128), lambda h, i: (0, h))],
        out_shape=[jax.ShapeDtypeStruct((r, 1024), F32), jax.ShapeDtypeStruct((r, 1024), F32), jax.ShapeDtypeStruct((r, 512), F32)],
        compiler_params=_params(("parallel", "arbitrary")))(aq, ak, av, lse, dmix)


def _pool_fn(ext, w, sc, gid0, grp, is_ctx, n_lat, r_all):
    gid = gid0 + lax.broadcasted_iota(I32, (TM + 2 * HALO, 1), 0)
    lo = jnp.where(is_ctx, n_lat, 0)
    hi = jnp.where(is_ctx, r_all, n_lat)
    z = jnp.where(jnp.logical_and(gid >= lo, gid < hi), ext, 0.0)
    w2 = jnp.roll(z, 1, axis=0) + z
    w4 = jnp.roll(w2, 1, axis=0) + jnp.roll(w2, -1, axis=0)
    w8 = jnp.roll(w4, 2, axis=0) + jnp.roll(w4, -2, axis=0)
    w16 = jnp.roll(w8, 4, axis=0) + jnp.roll(w8, -4, axis=0)
    win = jnp.where(grp == 0, w2, jnp.where(grp == 1, w4, jnp.where(grp == 2, w8, w16)))
    half = jnp.left_shift(1, grp)
    cnt = jnp.maximum(jnp.minimum(gid + half, hi) - jnp.maximum(gid - half, lo), 1).astype(F32)
    d = (win / cnt - ext)[HALO:HALO + TM]
    return _bdot(d, w) * sc


def _pool_ext(u_ref, i, r_all):
    s0 = pl.multiple_of(jnp.maximum(i * TM - HALO, 0), HALO)
    s2 = pl.multiple_of(jnp.minimum(i * TM + TM, r_all - HALO), HALO)
    ext = jnp.concatenate([u_ref[pl.ds(s0, HALO), :], u_ref[pl.ds(pl.multiple_of(i * TM, TM), TM), :], u_ref[pl.ds(s2, HALO), :]], axis=0)
    return ext, s0, s2


def _pool_specs(r):
    return [pl.BlockSpec((r, 128), lambda g, i: (0, C_POOL // 128 + g)), pl.BlockSpec((None, 128, 128), lambda g, i: (g, 0, 0)),
            pl.BlockSpec((None, 1, 128), lambda g, i: (g, 0, 0))]


def _pool_fwd(p, pool_w, pool_sc, n_lat, name):
    r = p.shape[0]
    nbl = n_lat // TM

    def body(u_ref, w_ref, sc_ref, o_ref):
        g, i = pl.program_id(0), pl.program_id(1)
        ext, _, _ = _pool_ext(u_ref, i, r)
        o_ref[...] = _pool_fn(ext, w_ref[...], sc_ref[...], i * TM - HALO, g, i >= nbl, n_lat, r).astype(BF16)

    return pl.pallas_call(
        body, name=name, grid=(4, r // TM), in_specs=_pool_specs(r), out_specs=pl.BlockSpec((TM, 128), lambda g, i: (i, g)),
        out_shape=jax.ShapeDtypeStruct((r, 512), BF16), compiler_params=_params(("parallel", "parallel")))(p, pool_w, pool_sc)


def _pool_bwd(p, pool_w, pool_sc, dmix, n_lat, name):
    r = p.shape[0]
    nbl = n_lat // TM

    def body(u_ref, w_ref, sc_ref, do_ref, du_ref, dw_ref, dsc_ref):
        g, i = pl.program_id(0), pl.program_id(1)

        @pl.when(i == 0)
        def _():
            du_ref[...] = jnp.zeros_like(du_ref)
            dw_ref[...] = jnp.zeros_like(dw_ref)
            dsc_ref[...] = jnp.zeros_like(dsc_ref)

        ext, s0, s2 = _pool_ext(u_ref, i, r)
        _, vjp = jax.vjp(lambda e, w, s: _pool_fn(e, w, s, i * TM - HALO, g, i >= nbl, n_lat, r), ext, w_ref[...], sc_ref[...])
        dext, dw, dsc = vjp(do_ref[...])
        du_ref[pl.ds(s0, HALO), :] += dext[0:HALO]
        du_ref[pl.ds(pl.multiple_of(i * TM, TM), TM), :] += dext[HALO:HALO + TM]
        du_ref[pl.ds(s2, HALO), :] += dext[HALO + TM:]
        dw_ref[...] += dw
        dsc_ref[...] += dsc

    return pl.pallas_call(
        body, name=name, grid=(4, r // TM), in_specs=_pool_specs(r) + [pl.BlockSpec((TM, 128), lambda g, i: (i, 4 + g))],
        out_specs=[pl.BlockSpec((r, 128), lambda g, i: (0, g)), pl.BlockSpec((None, 128, 128), lambda g, i: (g, 0, 0)),
                   pl.BlockSpec((None, 1, 128), lambda g, i: (g, 0, 0))],
        out_shape=[jax.ShapeDtypeStruct((r, 512), F32), jax.ShapeDtypeStruct((4, 128, 128), F32), jax.ShapeDtypeStruct((4, 1, 128), F32)],
        compiler_params=_params(("parallel", "arbitrary")))(p, pool_w, pool_sc, dmix)


def _softmax_parts(parts, extra=None):
    m = functools.reduce(jnp.maximum, [jnp.max(s, axis=-1, keepdims=True) for s in parts])
    if extra is not None:
        m = jnp.maximum(m, extra)
    m = lax.stop_gradient(m)
    es = [jnp.exp(s - m) for s in parts]
    den = functools.reduce(jnp.add, [jnp.sum(e, axis=-1, keepdims=True) for e in es])
    if extra is not None:
        den = den + jnp.exp(extra - m)
    inv = 1.0 / den
    return [e * inv for e in es]


def _swa_band(qpos0, kpos0, is_ctx):
    qpos = qpos0 + lax.broadcasted_iota(I32, (TM, SWA_KEYS), 0)
    kpos = kpos0 + lax.broadcasted_iota(I32, (TM, SWA_KEYS), 1)
    valid = jnp.logical_and(jnp.abs(kpos - qpos) <= 128, jnp.logical_not(is_ctx))
    band = jnp.where(valid, 0.0, NEG)
    return jnp.concatenate([band] * 4, axis=0)


def _swa_fn(q4, kw, vw, kc, vc, sink4, band):
    qs = jnp.concatenate([q4[:, 64 * g:64 * g + 64] for g in range(4)], axis=0)
    s_loc = _bdot_nt(qs, kw) * HD_SCALE + band
    s_ctx = _bdot_nt(qs, kc) * HD_SCALE
    sink = jnp.concatenate([jnp.broadcast_to(sink4[:, g:g + 1], (TM, 1)) for g in range(4)], axis=0)
    p_loc, p_ctx = _softmax_parts([s_loc, s_ctx], sink)
    o = _bdot(p_loc, vw) + _bdot(p_ctx, vc)
    return jnp.concatenate([o[TM * g:TM * (g + 1)] for g in range(4)], axis=1)


def _swa_window(i, n_lat):
    return pl.multiple_of(jnp.clip(i * TM - 128, 0, n_lat - SWA_KEYS), 128)


def _swa_fwd(cq, ck, cv, sp, n_lat, name):
    r = cq.shape[0]
    nbl = n_lat // TM

    def body(q_ref, k_ref, v_ref, sp_ref, o_ref):
        i = pl.program_id(0)
        k0 = _swa_window(i, n_lat)
        kw, vw = k_ref[pl.ds(k0, SWA_KEYS), :].astype(F32), v_ref[pl.ds(k0, SWA_KEYS), :].astype(F32)
        kc, vc = k_ref[pl.ds(n_lat, r - n_lat), :].astype(F32), v_ref[pl.ds(n_lat, r - n_lat), :].astype(F32)
        band = _swa_band(i * TM, k0, i >= nbl)
        for j in range(2):
            c = slice(64 * j, 64 * j + 64)
            o = _swa_fn(q_ref[:, 256 * j:256 * j + 256].astype(F32), kw[:, c], vw[:, c], kc[:, c], vc[:, c],
                        sp_ref[SP_SINK:SP_SINK + 1, 4 * j:4 * j + 4], band)
            o_ref[:, 256 * j:256 * j + 256] = o.astype(BF16)

    return pl.pallas_call(
        body, name=name, grid=(r // TM,),
        in_specs=[_row_spec(512), _full_spec((r, 128)), _full_spec((r, 128)), _full_spec(sp.shape)],
        out_specs=_row_spec(512), out_shape=jax.ShapeDtypeStruct((r, 512), BF16), compiler_params=_params(("parallel",)))(cq, ck, cv, sp)


def _swa_bwd(cq, ck, cv, sp, dmix, n_lat, name):
    r = cq.shape[0]
    nbl = n_lat // TM
    nc = r - n_lat

    def body(q_ref, k_ref, v_ref, sp_ref, do_ref, dq_ref, dk_ref, dv_ref, dsp_ref):
        i = pl.program_id(0)

        @pl.when(i == 0)
        def _():
            dk_ref[...] = jnp.zeros_like(dk_ref)
            dv_ref[...] = jnp.zeros_like(dv_ref)
            dsp_ref[...] = jnp.zeros_like(dsp_ref)

        k0 = _swa_window(i, n_lat)
        kw, vw = k_ref[pl.ds(k0, SWA_KEYS), :].astype(F32), v_ref[pl.ds(k0, SWA_KEYS), :].astype(F32)
        kc, vc = k_ref[pl.ds(n_lat, nc), :].astype(F32), v_ref[pl.ds(n_lat, nc), :].astype(F32)
        dkw, dvw, dkc, dvc, dsk = [], [], [], [], []
        band = _swa_band(i * TM, k0, i >= nbl)
        for j in range(2):
            c = slice(64 * j, 64 * j + 64)
            _, vjp = jax.vjp(lambda q4, a, b, cc, d, s: _swa_fn(q4, a, b, cc, d, s, band),
                             q_ref[:, 256 * j:256 * j + 256].astype(F32), kw[:, c], vw[:, c], kc[:, c], vc[:, c],
                             sp_ref[SP_SINK:SP_SINK + 1, 4 * j:4 * j + 4])
            dq4, a, b, cc, d, s = vjp(do_ref[:, 256 * j:256 * j + 256])
            dq_ref[:, 256 * j:256 * j + 256] = dq4
            dkw.append(a), dvw.append(b), dkc.append(cc), dvc.append(d), dsk.append(s)
        cat = lambda xs: jnp.concatenate(xs, axis=1)
        dk_ref[pl.ds(k0, SWA_KEYS), :] += cat(dkw)
        dv_ref[pl.ds(k0, SWA_KEYS), :] += cat(dvw)
        dk_ref[pl.ds(n_lat, nc), :] += cat(dkc)
        dv_ref[pl.ds(n_lat, nc), :] += cat(dvc)
        dsp_ref[SP_SINK:SP_SINK + 1, 0:8] += cat(dsk)

    return pl.pallas_call(
        body, name=name, grid=(r // TM,),
        in_specs=[_row_spec(512), _full_spec((r, 128)), _full_spec((r, 128)), _full_spec(sp.shape), _row_spec(512, 2)],
        out_specs=[_row_spec(512), _full_spec((r, 128)), _full_spec((r, 128)), _full_spec(sp.shape)],
        out_shape=[jax.ShapeDtypeStruct((r, 512), F32), jax.ShapeDtypeStruct((r, 128), F32), jax.ShapeDtypeStruct((r, 128), F32),
                   jax.ShapeDtypeStruct(sp.shape, F32)],
        compiler_params=_params(("arbitrary",)))(cq, ck, cv, sp, dmix)


def _na_fn(q, kw, vw, kc, vc, bias):
    s_loc = _bdot_nt(q, kw) * HD_SCALE + bias
    s_ctx = _bdot_nt(q, kc) * HD_SCALE
    p_loc, p_ctx = _softmax_parts([s_loc, s_ctx])
    return _bdot(p_loc, vw) + _bdot(p_ctx, vc)


NA_MASKED = 15


def _na_geometry(i, n_lat, is_ctx):
    rows = n_lat // GRID_W
    qrow0 = i * (TM // GRID_W)
    krow0 = jnp.clip(qrow0 - 4, 0, rows - NA_KROWS)
    dr = []
    for qi in range(TM // GRID_W):
        r_lo = jnp.clip(qrow0 + qi - 4, 0, rows - 8)
        row = []
        for kj in range(NA_KROWS):
            kr = krow0 + kj
            ok = jnp.logical_and(jnp.logical_and(kr >= r_lo, kr < r_lo + 8), jnp.logical_not(is_ctx))
            row.append(jnp.where(ok, kr - (qrow0 + qi) + 7, NA_MASKED))
        dr.append(row)
    return pl.multiple_of(krow0 * GRID_W, GRID_W), dr


def _na_bias(t1_ref, hh, dr):
    return jnp.concatenate([jnp.concatenate([t1_ref[hh, dr[qi][kj]] for kj in range(NA_KROWS)], axis=1)
                            for qi in range(TM // GRID_W)], axis=0)


def _na_specs(r):
    return [pl.BlockSpec((TM, 128), lambda pr, i: (i, pr)), pl.BlockSpec((r, 128), lambda pr, i: (0, pr)),
            pl.BlockSpec((r, 128), lambda pr, i: (0, pr)), pl.BlockSpec((2, 16, GRID_W, GRID_W), lambda pr, i: (pr, 0, 0, 0))]


def _na_fwd(dq, dk, dv, t1, n_lat, name):
    r = dq.shape[0]
    nbl = n_lat // TM
    nc = r - n_lat
    nk = NA_KROWS * GRID_W

    def body(q_ref, k_ref, v_ref, t1_ref, o_ref):
        i = pl.program_id(1)
        k0, dr = _na_geometry(jnp.minimum(i, nbl - 1), n_lat, i >= nbl)
        kw, vw = k_ref[pl.ds(k0, nk), :].astype(F32), v_ref[pl.ds(k0, nk), :].astype(F32)
        kc, vc = k_ref[pl.ds(n_lat, nc), :].astype(F32), v_ref[pl.ds(n_lat, nc), :].astype(F32)
        for hh in range(2):
            c = slice(64 * hh, 64 * hh + 64)
            o = _na_fn(q_ref[:, c].astype(F32), kw[:, c], vw[:, c], kc[:, c], vc[:, c], _na_bias(t1_ref, hh, dr))
            o_ref[:, c] = o.astype(BF16)

    return pl.pallas_call(
        body, name=name, grid=(4, r // TM), in_specs=_na_specs(r), out_specs=pl.BlockSpec((TM, 128), lambda pr, i: (i, pr)),
        out_shape=jax.ShapeDtypeStruct((r, 512), BF16), compiler_params=_params(("parallel", "parallel")))(dq, dk, dv, t1)


def _na_bwd(dq, dk, dv, t1, dmix, n_lat, name):
    r = dq.shape[0]
    nbl = n_lat // TM
    nc = r - n_lat
    nk = NA_KROWS * GRID_W

    def body(q_ref, k_ref, v_ref, t1_ref, do_ref, dq_ref, dk_ref, dv_ref, dt1_ref):
        i = pl.program_id(1)

        @pl.when(i == 0)
        def _():
            dk_ref[...] = jnp.zeros_like(dk_ref)
            dv_ref[...] = jnp.zeros_like(dv_ref)
            dt1_ref[...] = jnp.zeros_like(dt1_ref)

        k0, dr = _na_geometry(jnp.minimum(i, nbl - 1), n_lat, i >= nbl)
        kw, vw = k_ref[pl.ds(k0, nk), :].astype(F32), v_ref[pl.ds(k0, nk), :].astype(F32)
        kc, vc = k_ref[pl.ds(n_lat, nc), :].astype(F32), v_ref[pl.ds(n_lat, nc), :].astype(F32)
        dkw, dvw, dkc, dvc = [], [], [], []
        for hh in range(2):
            c = slice(64 * hh, 64 * hh + 64)
            _, vjp = jax.vjp(_na_fn,
                             q_ref[:, c].astype(F32), kw[:, c], vw[:, c], kc[:, c], vc[:, c], _na_bias(t1_ref, hh, dr))
            dqh, a, b, cc, d, dbias = vjp(do_ref[:, c])
            dq_ref[:, c] = dqh
            dkw.append(a), dvw.append(b), dkc.append(cc), dvc.append(d)
            for qi in range(TM // GRID_W):
                for kj in range(NA_KROWS):
                    dt1_ref[hh, dr[qi][kj]] += dbias[GRID_W * qi:GRID_W * (qi + 1), GRID_W * kj:GRID_W * (kj + 1)]
        cat = lambda xs: jnp.concatenate(xs, axis=1)
        dk_ref[pl.ds(k0, nk), :] += cat(dkw)
        dv_ref[pl.ds(k0, nk), :] += cat(dvw)
        dk_ref[pl.ds(n_lat, nc), :] += cat(dkc)
        dv_ref[pl.ds(n_lat, nc), :] += cat(dvc)

    return pl.pallas_call(
        body, name=name, grid=(4, r // TM), in_specs=_na_specs(r) + [pl.BlockSpec((TM, 128), lambda pr, i: (i, 12 + pr))],
        out_specs=[pl.BlockSpec((TM, 128), lambda pr, i: (i, pr)), pl.BlockSpec((r, 128), lambda pr, i: (0, pr)),
                   pl.BlockSpec((r, 128), lambda pr, i: (0, pr)), pl.BlockSpec((2, 16, GRID_W, GRID_W), lambda pr, i: (pr, 0, 0, 0))],
        out_shape=[jax.ShapeDtypeStruct((r, 512), F32)] * 3 + [jax.ShapeDtypeStruct((8, 16, GRID_W, GRID_W), F32)],
        compiler_params=_params(("parallel", "arbitrary")))(dq, dk, dv, t1, dmix)


def _conv_ext(main_ref, prev_ref, next_ref, edges):
    prev_ok, next_ok = edges
    return jnp.concatenate([jnp.where(prev_ok, prev_ref[...], 0.0), main_ref[...], jnp.where(next_ok, next_ref[...], 0.0)], axis=0)


def _conv_edges(i, nbl, nb):
    return jnp.logical_and(i != 0, i != nbl), jnp.logical_and(i != nbl - 1, i != nb - 1)


def _conv_apply(ext, w, b):
    up = jnp.roll(ext, 1, axis=0)
    dn = jnp.roll(ext, -1, axis=0)
    return up * w[0:1] + ext * w[1:2] + dn * w[2:3] + b, up, dn


def _conv_in_specs(tc, r):
    nb8 = TM // HALO
    last8 = r // HALO - 1

    def trio(half):
        return [pl.BlockSpec((None, TM, tc), lambda j, i: (half, i, j)),
                pl.BlockSpec((None, HALO, tc), lambda j, i: (half, jnp.maximum(i * nb8 - 1, 0), j)),
                pl.BlockSpec((None, HALO, tc), lambda j, i: (half, jnp.minimum((i + 1) * nb8, last8), j))]

    wb = [pl.BlockSpec((None, 3, tc), lambda j, i: (0, 0, j)), pl.BlockSpec((None, 3, tc), lambda j, i: (1, 0, j)),
          pl.BlockSpec((None, 1, tc), lambda j, i: (0, 0, j)), pl.BlockSpec((None, 1, tc), lambda j, i: (1, 0, j))]
    return trio(0) + trio(1) + wb


def _convgate_fwd(a3, cw, cb, n_lat, name):
    _, r, ff = a3.shape
    nbl = n_lat // TM
    tc = _pick(ff, (512, 256, 128))

    def body(g_ref, gp_ref, gn_ref, v_ref, vp_ref, vn_ref, wg_ref, wv_ref, bg_ref, bv_ref, u_ref):
        edges = _conv_edges(pl.program_id(1), nbl, r // TM)
        gg, _, _ = _conv_apply(_conv_ext(g_ref, gp_ref, gn_ref, edges), wg_ref[...], bg_ref[...])
        gv, _, _ = _conv_apply(_conv_ext(v_ref, vp_ref, vn_ref, edges), wv_ref[...], bv_ref[...])
        u_ref[...] = (jax.nn.silu(gg[HALO:HALO + TM]) * gv[HALO:HALO + TM]).astype(BF16)

    return pl.pallas_call(
        body, name=name, grid=(ff // tc, r // TM), in_specs=_conv_in_specs(tc, r),
        out_specs=pl.BlockSpec((TM, tc), lambda j, i: (i, j)), out_shape=jax.ShapeDtypeStruct((r, ff), BF16),
        compiler_params=_params(("parallel", "parallel")))(a3, a3, a3, a3, a3, a3, cw, cw, cb, cb)


def _convgate_bwd(a3, cw, cb, du, n_lat, name):
    _, r, ff = a3.shape
    nbl = n_lat // TM
    tc = _pick(ff, (512, 256, 128))
    nb8 = TM // HALO
    last8 = r // HALO - 1

    def body(g_ref, gp_ref, gn_ref, v_ref, vp_ref, vn_ref, wg_ref, wv_ref, bg_ref, bv_ref, du_ref, dup_ref, dun_ref,
             da_ref, dcw_ref, dcb_ref):
        i = pl.program_id(1)

        @pl.when(i == 0)
        def _():
            dcw_ref[...] = jnp.zeros_like(dcw_ref)
            dcb_ref[...] = jnp.zeros_like(dcb_ref)

        edges = _conv_edges(i, nbl, r // TM)
        wg, wv = wg_ref[...], wv_ref[...]
        eg, ev = _conv_ext(g_ref, gp_ref, gn_ref, edges), _conv_ext(v_ref, vp_ref, vn_ref, edges)
        gg, ug, dg_ = _conv_apply(eg, wg, bg_ref[...])
        gv, uv, dv_ = _conv_apply(ev, wv, bv_ref[...])
        due = _conv_ext(du_ref, dup_ref, dun_ref, edges)
        sg = jax.nn.sigmoid(gg)
        dgg = due * gv * (sg * (1.0 + gg * (1.0 - sg)))
        dgv = due * (gg * sg)
        main = slice(HALO, HALO + TM)
        for h, (dgx, w, ex, upx, dnx) in enumerate(((dgg, wg, eg, ug, dg_), (dgv, wv, ev, uv, dv_))):
            da = dgx * w[1:2] + jnp.roll(dgx, -1, axis=0) * w[0:1] + jnp.roll(dgx, 1, axis=0) * w[2:3]
            da_ref[h] = da[main].astype(BF16)
            dm = dgx[main]
            dcw_ref[h, 0:1, :] += jnp.sum(dm * upx[main], axis=0, keepdims=True)
            dcw_ref[h, 1:2, :] += jnp.sum(dm * ex[main], axis=0, keepdims=True)
            dcw_ref[h, 2:3, :] += jnp.sum(dm * dnx[main], axis=0, keepdims=True)
            dcb_ref[h] += jnp.sum(dm, axis=0, keepdims=True)

    du_specs = [pl.BlockSpec((TM, tc), lambda j, i: (i, j)),
                pl.BlockSpec((HALO, tc), lambda j, i: (jnp.maximum(i * nb8 - 1, 0), j)),
                pl.BlockSpec((HALO, tc), lambda j, i: (jnp.minimum((i + 1) * nb8, last8), j))]
    return pl.pallas_call(
        body, name=name, grid=(ff // tc, r // TM), in_specs=_conv_in_specs(tc, r) + du_specs,
        out_specs=[pl.BlockSpec((2, TM, tc), lambda j, i: (0, i, j)), pl.BlockSpec((2, 3, tc), lambda j, i: (0, 0, j)),
                   pl.BlockSpec((2, 1, tc), lambda j, i: (0, 0, j))],
        out_shape=[jax.ShapeDtypeStruct((2, r, ff), BF16), jax.ShapeDtypeStruct((2, 3, ff), F32), jax.ShapeDtypeStruct((2, 1, ff), F32)],
        compiler_params=_params(("parallel", "arbitrary")))(a3, a3, a3, a3, a3, a3, cw, cw, cb, cb, du, du, du)


def _layer_fwd(x, w, l, tab, n_lat, before_out=None):
    nbl = n_lat // TM
    mod = w["mods"][l]
    h1 = _normmod_fwd(x, w["g_mix"][l], mod, 0, nbl, "normmod_fwd")
    p = _mm(h1, w["w_in"][l], "nn", F32, "mm_in")
    qkv = _prep_fwd(p, tab, w["sp"][l], w["w_qb"][l], w["w_kvb"][l], "prep_fwd")
    oa, lse = _mla_fwd(qkv[0], qkv[1], qkv[2], n_lat, "mla_fwd")
    ob = _pool_fwd(p, w["pool_w"][l], w["pool_sc"][l], n_lat, "pool_fwd")
    oc = _swa_fwd(qkv[3], qkv[4], qkv[5], w["sp"][l], n_lat, "swa_fwd")
    od = _na_fwd(qkv[6], qkv[7], qkv[8], w["t1"][l], n_lat, "na_fwd")
    mix = jnp.concatenate([oa, ob, oc, od], axis=1)
    if before_out is not None:
        before_out(mix)
    y = _mm(mix, w["w_out"][l], "nn", F32, "mm_out")
    x1 = _resid_fwd(x, y, mod, 2, nbl, "resid_fwd")
    h2 = _normmod_fwd(x1, w["g_ffn"][l], mod, 3, nbl, "normmod_fwd")
    a3 = _mm(h2, w["w_up"][l], "nn", F32, "mm_up", o_split=True)
    u = _convgate_fwd(a3, w["conv_w"][l], w["conv_b"][l], n_lat, "convgate_fwd")
    y2 = _mm(u, w["w_down"][l], "nn", F32, "mm_down")
    x2 = _resid_fwd(x1, y2, mod, 5, nbl, "resid_fwd")
    return x2, dict(x=x, h1=h1, p=p, qkv=qkv, lse=lse, mix=mix, y=y, x1=x1, h2=h2, a3=a3, u=u, y2=y2)


def _layer_bwd(dx, s, w, l, tab, n_lat, after_ffn=None):
    nbl = n_lat // TM
    mod = w["mods"][l]
    g = {}
    dy2, dmod_a = _resid_bwd(dx, s["y2"], mod, 5, nbl, "resid_bwd")
    g["w_down"] = _mm(s["u"], dy2, "tn", GRAD_WIRE, "mm_dwdown")
    du = _mm(dy2, w["w_down"][l], "nt", F32, "mm_du")
    da3, g["conv_w"], g["conv_b"] = _convgate_bwd(s["a3"], w["conv_w"][l], w["conv_b"][l], du, n_lat, "convgate_bwd")
    g["w_up"] = _mm(s["h2"], da3, "tn", GRAD_WIRE, "mm_dwup", b_split=True)
    dh2 = _mm(da3, w["w_up"][l], "nt", F32, "mm_dh2", a_split=True)
    g_ffn = w["g_ffn"][l] if after_ffn is None else w["g_ffn"][l] + after_ffn(g)[0:1, 0:1]
    dx1, g["g_ffn"], dmod_b = _normmod_bwd(s["x1"], g_ffn, mod, dh2, dx, 3, nbl, "normmod_bwd")
    dy, dmod_c = _resid_bwd(dx1, s["y"], mod, 2, nbl, "resid_bwd")
    g["w_out"] = _mm(s["mix"], dy, "tn", GRAD_WIRE, "mm_dwout")
    dmix = _mm(dy, w["w_out"][l], "nt", F32, "mm_dmix")
    qkv = s["qkv"]
    daq, dak, dav = _mla_bwd(qkv[0], qkv[1], qkv[2], s["lse"], dmix, n_lat, "mla_bwd")
    dpool, g["pool_w"], g["pool_sc"] = _pool_bwd(s["p"], w["pool_w"][l], w["pool_sc"][l], dmix, n_lat, "pool_bwd")
    dcq, dck, dcv, dsp_c = _swa_bwd(qkv[3], qkv[4], qkv[5], w["sp"][l], dmix, n_lat, "swa_bwd")
    ddq, ddk, ddv, g["t1"] = _na_bwd(qkv[6], qkv[7], qkv[8], w["t1"][l], dmix, n_lat, "na_bwd")
    dp, dsp_p, g["w_qb"], g["w_kvb"] = _prep_bwd(s["p"], tab, w["sp"][l], w["w_qb"][l], w["w_kvb"][l],
                                                (daq, dak, dav, dcq, dck, dcv, ddq, ddk, ddv), dpool, "prep_bwd")
    g["sp"] = dsp_c + dsp_p
    g["w_in"] = _mm(s["h1"], dp, "tn", GRAD_WIRE, "mm_dwin")
    dh1 = _mm(dp, w["w_in"][l], "nt", F32, "mm_dh1")
    dx0, g["g_mix"], dmod_d = _normmod_bwd(s["x"], w["g_mix"][l], mod, dh1, dx1, 0, nbl, "normmod_bwd")
    g["mods"] = dmod_a + dmod_b + dmod_c + dmod_d
    return dx0, g


def _local_step(x_all, target, w, tab, n_lat):
    saved = []
    x = x_all
    for l in range(DEPTH):
        x, s = _layer_fwd(x, w, l, tab, n_lat)
        saved.append(s)
    loss, dx = _loss_kernel(x, target, n_lat // TM, "loss")
    grads = [None] * DEPTH
    for l in reversed(range(DEPTH)):
        dx, grads[l] = _layer_bwd(dx, saved[l], w, l, tab, n_lat)
    return loss[0, 0], dx, grads


def _pad_cols(a, widths):
    parts, o = [], 0
    for take, pad in widths:
        parts.append(a[..., o:o + take])
        if pad:
            parts.append(jnp.zeros(a.shape[:-1] + (pad,), a.dtype))
        o += take
    return jnp.concatenate(parts, axis=-1)


def _w_in_layout(w_in):
    return _pad_cols(w_in, [(832, 64), (P_COLS - 832, PW - P_COLS - 64)])


def _w_in_unlayout(g):
    return jnp.concatenate([g[..., 0:832], g[..., 896:896 + P_COLS - 832]], axis=-1)


def _w_qb_layout(w):
    s = w.reshape(w.shape[:-1] + (4, 192))
    return jnp.concatenate([s[..., 0:128].reshape(w.shape[:-1] + (512,)), s[..., 128:192].reshape(w.shape[:-1] + (256,))], axis=-1)


def _w_qb_unlayout(g):
    n = g[..., 0:512].reshape(g.shape[:-1] + (4, 128))
    r = g[..., 512:768].reshape(g.shape[:-1] + (4, 64))
    return jnp.concatenate([n, r], axis=-1).reshape(g.shape[:-1] + (768,))


SP_SLOTS = (("mla_q_a_norm", 512), ("mla_kv_a_norm", 256), ("mla_q_nope_norm", 128), ("mla_q_rope_norm", 64),
            ("mla_k_nope_norm", 128), ("mla_k_rope_norm", 64), ("swa_q_norm", 64), ("swa_k_norm", 64),
            ("na_q_norm", 64), ("na_k_norm", 64), ("swa_sink", 8))


def _sp_pack(small):
    rows = [jnp.pad(small[k], ((0, 0), (0, 512 - n))) for k, n in SP_SLOTS]
    rows += [jnp.zeros_like(rows[0])] * (16 - len(rows))
    return jnp.stack(rows, axis=1)


def _sp_unpack(sp):
    return {k: sp[:, i, 0:n] for i, (k, n) in enumerate(SP_SLOTS)}


def _rpb_onehot():
    qc = lax.broadcasted_iota(I32, (GRID_W, GRID_W), 0)
    kc = lax.broadcasted_iota(I32, (GRID_W, GRID_W), 1)
    dc = (jnp.clip(kc - qc, -15, 15) + 15).reshape(1, GRID_W * GRID_W)
    return (lax.broadcasted_iota(I32, (32, GRID_W * GRID_W), 0) == dc).astype(F32)


def _rpb_expand(rpb):
    l = rpb.shape[0]
    flat = jnp.pad(rpb, ((0, 0), (0, 0), (0, 1), (0, 1))).reshape(l * 128, 32)
    t1 = _mm_exact(flat, _rpb_onehot(), "rpb_expand").reshape(l, 8, 16, GRID_W, GRID_W)
    qc = lax.broadcasted_iota(I32, (GRID_W, GRID_W), 0)
    kc = lax.broadcasted_iota(I32, (GRID_W, GRID_W), 1)
    c_lo = jnp.clip(qc - 8, 0, GRID_W - 16)
    col_ok = jnp.logical_and(kc >= c_lo, kc < c_lo + 16)
    row_ok = lax.broadcasted_iota(I32, (16, 1, 1), 0) != NA_MASKED
    return jnp.where(jnp.logical_and(col_ok[None], row_ok), t1, NEG)


def _rpb_fold(dt1):
    l = dt1.shape[0]
    g = _mm_exact(dt1.reshape(l * 128, GRID_W * GRID_W), _rpb_onehot().T, "rpb_fold")
    return g.reshape(l, 8, 16, 32)[:, :, 0:15, 0:31]


def _rope_table(n_lat, n_ctx):
    t = jnp.arange(n_lat)
    inv = ROPE_BASE ** (-jnp.arange(0, 32, 2, dtype=F32) / 32)
    ar = (t // GRID_W).astype(F32)[:, None] * inv
    ac = (t % GRID_W).astype(F32)[:, None] * inv
    cos = jnp.concatenate([jnp.cos(ar), jnp.cos(ar), jnp.cos(ac), jnp.cos(ac)], axis=-1)
    sin = jnp.concatenate([jnp.sin(ar), jnp.sin(ar), jnp.sin(ac), jnp.sin(ac)], axis=-1)
    tab = jnp.concatenate([cos, sin], axis=-1)
    ident = jnp.concatenate([jnp.ones((n_ctx, 64), F32), jnp.zeros((n_ctx, 64), F32)], axis=-1)
    return jnp.concatenate([tab, ident], axis=0)


def _small_weights(full, mods):
    l = full["g_mix"].shape[0]
    ff = full["ffn_conv_b"].shape[1] // 2
    return dict(
        mods=mods, g_mix=full["g_mix"][:, None, :], g_ffn=full["g_ffn"][:, None, :],
        sp=_sp_pack(full), pool_w=full["pool_w"], pool_sc=full["pool_scale"].reshape(l, 4, 1, 128),
        t1=_rpb_expand(full["na_rpb"]),
        conv_w=full["ffn_conv_w"].reshape(l, 3, 2, ff).transpose(0, 2, 1, 3),
        conv_b=full["ffn_conv_b"].reshape(l, 2, 1, ff))


def _kernel_weights(full, mods):
    w = _small_weights(full, mods)
    w.update(w_in=_w_in_layout(full["w_in"]).astype(BF16), w_out=full["w_out"].astype(BF16),
             w_up=full["ffn_w_up"].astype(BF16), w_down=full["ffn_w_down"].astype(BF16),
             w_qb=_w_qb_layout(full["mla_w_qb"]).astype(BF16), w_kvb=full["mla_w_kvb"].astype(BF16))
    return w


def _reference_grads(grads, big=True):
    st = lambda k: jnp.stack([g[k] for g in grads], axis=0)
    l = len(grads)
    out = dict(
        g_mix=st("g_mix")[:, 0], g_ffn=st("g_ffn")[:, 0],
        pool_w=st("pool_w"), pool_scale=st("pool_sc").reshape(l, 512), na_rpb=_rpb_fold(st("t1")),
        ffn_conv_w=st("conv_w").transpose(0, 2, 1, 3).reshape(l, 3, -1), ffn_conv_b=st("conv_b").reshape(l, -1),
        mods=st("mods"))
    if big:
        out.update(w_in=_w_in_unlayout(st("w_in")), w_out=st("w_out"), ffn_w_up=st("w_up"), ffn_w_down=st("w_down"),
                   mla_w_qb=_w_qb_unlayout(st("w_qb")), mla_w_kvb=st("w_kvb"))
    out.update(_sp_unpack(st("sp")))
    return out


ANY = pl.BlockSpec(memory_space=pl.ANY)


def _flip(x, y, j):
    return (1 - x if j >> 1 else x), (1 - y if j & 1 else y)


def _comm_call(name, ins, out_shapes, n_copies, plan, aliases=None):
    n_in, n_out = len(ins), len(out_shapes)

    def body(*refs):
        in_refs, out_refs = refs[:n_in], refs[n_in:n_in + n_out]
        ssem, rsem = refs[n_in + n_out:]
        pos = (lax.axis_index("x"), lax.axis_index("y"), lax.axis_index("c"))
        copies = plan(in_refs, out_refs, pos)
        assert len(copies) == n_copies
        descs = []
        for i, (src, dst, peer) in enumerate(copies):
            if peer is None:
                d = pltpu.make_async_copy(src, dst, ssem.at[i])
            else:
                d = pltpu.make_async_remote_copy(src_ref=src, dst_ref=dst, send_sem=ssem.at[i], recv_sem=rsem.at[i],
                                                 device_id=peer, device_id_type=MESH)
            d.start()
            descs.append(d)
        for d in descs:
            d.wait()

    return pl.pallas_call(
        body, name=name, in_specs=[ANY] * n_in, out_specs=[ANY] * n_out, out_shape=list(out_shapes),
        input_output_aliases=aliases or {},
        scratch_shapes=[pltpu.SemaphoreType.DMA((n_copies,)), pltpu.SemaphoreType.DMA((n_copies,))])(*ins)


def _sib_fill(bufs, part, name):
    def plan(ins, outs, pos):
        x, y, c = pos
        return [(o_ref.at[part(c)], o_ref.at[part(c)], (x, y, 1 - c)) for o_ref in outs]

    shapes = [jax.ShapeDtypeStruct(b.shape, b.dtype) for b in bufs]
    return _comm_call(name, bufs, shapes, len(bufs), plan, aliases={i: i for i in range(len(bufs))})


HBM = pl.BlockSpec(memory_space=pltpu.HBM)
SEM = pl.BlockSpec(memory_space=pltpu.SEMAPHORE)
DATAFLOW = pltpu.SideEffectType.DATAFLOW_SIDE_EFFECTING


def _remote_start(name, bufs, n_copies, plan, after):
    nb, na = len(bufs), len(after)

    def body(*refs):
        ssem, rsem, token = refs[nb + na], refs[nb + na + 1], refs[-1]
        pos = (lax.axis_index("x"), lax.axis_index("y"), lax.axis_index("c"))
        copies = plan(refs[:nb], pos)
        assert len(copies) == n_copies
        for i, (src, dst, peer) in enumerate(copies):
            pltpu.make_async_remote_copy(src_ref=src, dst_ref=dst, send_sem=ssem.at[i], recv_sem=rsem.at[i],
                                         device_id=peer, device_id_type=MESH).start()
        token[...] = jnp.zeros_like(token)

    outs = pl.pallas_call(
        body, name=name,
        out_shape=(pltpu.SemaphoreType.DMA((n_copies,)), pltpu.SemaphoreType.DMA((n_copies,)),
                   *[pltpu.HBM(b.shape, b.dtype) for b in bufs], jax.ShapeDtypeStruct((8, 128), F32)),
        in_specs=[HBM] * nb + [ANY] * na, out_specs=(SEM, SEM, *[HBM] * nb, pl.BlockSpec(memory_space=pltpu.VMEM)),
        input_output_aliases={i: 2 + i for i in range(nb)},
        compiler_params=pltpu.CompilerParams(has_side_effects=DATAFLOW),
    )(*[pltpu.with_memory_space_constraint(b, pltpu.HBM) for b in bufs], *after)
    return outs[0], outs[1], list(outs[2:2 + nb]), outs[-1]


def _remote_wait(name, ssem, rsem, bufs, n_copies, plan, after):
    nb = len(bufs)

    def body(*refs):
        ssem_ref, rsem_ref = refs[nb], refs[nb + 1]
        pos = (lax.axis_index("x"), lax.axis_index("y"), lax.axis_index("c"))
        copies = plan(refs[:nb], pos)
        assert len(copies) == n_copies
        for i, (src, dst, peer) in enumerate(copies):
            cp = pltpu.make_async_remote_copy(src_ref=src, dst_ref=dst, send_sem=ssem_ref.at[i], recv_sem=rsem_ref.at[i],
                                              device_id=peer, device_id_type=MESH)
            cp.wait_send()
            cp.wait_recv()

    outs = pl.pallas_call(
        body, name=name, out_shape=tuple(pltpu.HBM(b.shape, b.dtype) for b in bufs),
        in_specs=[HBM] * nb + [SEM, SEM, ANY], out_specs=tuple([HBM] * nb), input_output_aliases={i: i for i in range(nb)},
        compiler_params=pltpu.CompilerParams(has_side_effects=DATAFLOW),
    )(*bufs, ssem, rsem, after)
    return list(outs)


BIG_GATHER = ("axis1", "axis1", "axis1", "axis1", "lane", "lane")


def _place_own(shard, kind, xyvec, name):
    rows, cols = shard.shape
    tr = _row_tile(rows, cols)
    if kind == "axis1":
        shape = (4, rows, cols)
        o_spec = pl.BlockSpec((None, tr, cols), lambda i, x_ref, y_ref: (2 * x_ref[0] + y_ref[0], i, 0))
    else:
        shape = (rows, 4 * cols)
        o_spec = pl.BlockSpec((tr, cols), lambda i, x_ref, y_ref: (i, 2 * x_ref[0] + y_ref[0]))

    def body(x_ref, y_ref, s_ref, o_ref):
        o_ref[...] = s_ref[...]

    return pl.pallas_call(
        body, name=name,
        grid_spec=pltpu.PrefetchScalarGridSpec(num_scalar_prefetch=2, grid=(rows // tr,),
                                               in_specs=[pl.BlockSpec((tr, cols), lambda i, x_ref, y_ref: (i, 0))], out_specs=o_spec),
        out_shape=jax.ShapeDtypeStruct(shape, shard.dtype), compiler_params=_params(("parallel",)))(*xyvec, shard)


def _w_gather_plan(shapes, kinds):
    n = len(kinds)

    def plan(refs, pos):
        x, y, c = pos
        k = 2 * x + y
        cps = []
        for s_ref, l_ref, kind, shp in zip(refs[:n], refs[n:], kinds, shapes):
            h, w = shp[0] // 2, shp[1]
            rows = pl.ds(pl.multiple_of(c * h, 16), h)
            dst = l_ref.at[k, rows, :] if kind == "axis1" else l_ref.at[rows, pl.ds(pl.multiple_of(k * w, 128), w)]
            for j in (1, 2, 3):
                tx, ty = _flip(x, y, j)
                cps.append((s_ref.at[rows, :], dst, (tx, ty, c)))
        return cps

    return plan


def _w_fill(lands, shapes, kinds, name):
    def plan(ins, outs, pos):
        x, y, c = pos
        cps = []
        for o_ref, kind, shp in zip(outs, kinds, shapes):
            h = shp[0] // 2
            rows = pl.ds(pl.multiple_of(c * h, 16), h)
            part = o_ref.at[:, rows, :] if kind == "axis1" else o_ref.at[rows, :]
            cps.append((part, part, (x, y, 1 - c)))
        return cps

    return _comm_call(name, lands, [jax.ShapeDtypeStruct(b.shape, b.dtype) for b in lands], len(lands), plan,
                      aliases={i: i for i in range(len(lands))})


def _g_scatter_plan(kinds, widths):
    n = len(kinds)

    def plan(refs, pos):
        x, y, c = pos
        cps = []
        for s_ref, l_ref, kind, w in zip(refs[:n], refs[n:], kinds, widths):
            for j in (1, 2, 3):
                tx, ty = _flip(x, y, j)
                kj = 2 * tx + ty
                src = s_ref.at[kj] if kind == "cm" else s_ref.at[:, pl.ds(pl.multiple_of(kj * w, 128), w)]
                cps.append((src, l_ref.at[j - 1], (tx, ty, c)))
        return cps

    return plan


def _pair_up(xs, name):
    def plan(ins, outs, pos):
        x, y, c = pos
        cps = []
        for i_ref, o_ref in zip(ins, outs):
            cps.append((i_ref, o_ref.at[c], None))
            cps.append((i_ref, o_ref.at[c], (x, y, 1 - c)))
        return cps

    return _comm_call(name, xs, [jax.ShapeDtypeStruct((2,) + a.shape, a.dtype) for a in xs], 2 * len(xs), plan)


def _chip_gather(xs, kinds, name):
    def dst(o_ref, kind, k, x_shape):
        if kind == "lead":
            return o_ref.at[k]
        w = x_shape[-1]
        return o_ref.at[(slice(None),) * (len(x_shape) - 1) + (pl.ds(pl.multiple_of(k * w, 128), w),)]

    def plan(ins, outs, pos):
        x, y, c = pos
        k = 2 * x + y
        cps = []
        for i_ref, o_ref, kind, a in zip(ins, outs, kinds, xs):
            cps.append((i_ref, dst(o_ref, kind, k, a.shape), None))
            for j in (1, 2, 3):
                tx, ty = _flip(x, y, j)
                cps.append((i_ref, dst(o_ref, kind, k, a.shape), (tx, ty, c)))
        return cps

    def oshape(a, kind):
        return (4,) + a.shape if kind == "lead" else a.shape[:-1] + (4 * a.shape[-1],)

    return _comm_call(name, xs, [jax.ShapeDtypeStruct(oshape(a, kd), a.dtype) for a, kd in zip(xs, kinds)], 4 * len(xs), plan)


def _row_tile(rows, cols, budget=1 << 20):
    for t in (2048, 1024, 512, 256, 128, 64, 32, 16, 8):
        if rows % t == 0 and t * cols * 4 <= budget:
            return t
    return rows


def _sum_lead(x, name):
    n, rows, w = x.shape
    tr = _row_tile(rows, w, (1 << 21) // n)

    def body(x_ref, o_ref):
        acc = x_ref[0].astype(F32)
        for j in range(1, n):
            acc = acc + x_ref[j].astype(F32)
        o_ref[...] = acc

    return pl.pallas_call(
        body, name=name, grid=(rows // tr,), in_specs=[pl.BlockSpec((n, tr, w), lambda i: (0, i, 0))],
        out_specs=pl.BlockSpec((tr, w), lambda i: (i, 0)), out_shape=jax.ShapeDtypeStruct((rows, w), F32),
        compiler_params=_params(("parallel",)))(x)


def _sum_into(g, landed, buf, layer, pvec, kind, layers, name):
    n, rows, w = landed.shape
    tr = _row_tile(rows, w)

    def body(c_ref, x_ref, y_ref, own_ref, l_ref, *refs):
        o_ref = refs[-1]
        acc = own_ref[...].astype(F32)
        for j in range(n):
            acc = acc + l_ref[j].astype(F32)
        o_ref[...] = acc

    if kind == "cm":
        own_spec = pl.BlockSpec((None, tr, w), lambda i, c_ref, x_ref, y_ref: (2 * x_ref[0] + y_ref[0], i, 0))
    else:
        own_spec = pl.BlockSpec((tr, w), lambda i, c_ref, x_ref, y_ref: (i, 2 * x_ref[0] + y_ref[0]))
    in_specs = [own_spec, pl.BlockSpec((n, tr, w), lambda i, c_ref, x_ref, y_ref: (0, i, 0))]
    args = [*pvec, g, landed]
    if buf is not None:
        in_specs.append(ANY)
        args.append(buf)
    return pl.pallas_call(
        body, name=name,
        grid_spec=pltpu.PrefetchScalarGridSpec(
            num_scalar_prefetch=3, grid=(rows // tr,), in_specs=in_specs,
            out_specs=pl.BlockSpec((None, None, tr, w), lambda i, c_ref, x_ref, y_ref: (layer, c_ref[0], i, 0))),
        out_shape=jax.ShapeDtypeStruct((layers, 2, rows, w), F32), input_output_aliases={} if buf is None else {5: 0},
        compiler_params=_params(("arbitrary",)))(*args)


def _adamw_math(w, g, m, v):
    mn = ADAM_B1 * m + (1.0 - ADAM_B1) * g
    vn = ADAM_B2 * v + (1.0 - ADAM_B2) * jnp.square(g)
    m_hat = mn / (1.0 - ADAM_B1 ** ADAM_STEP)
    v_hat = vn / (1.0 - ADAM_B2 ** ADAM_STEP)
    return -ADAM_LR * (m_hat / (jnp.sqrt(v_hat) + ADAM_EPS) + ADAM_WD * w), mn, vn


def _adamw_pair(w, g2, m, v, name):
    l, rows, cols = w.shape
    tr = _row_tile(rows, cols)

    def body(w_ref, g0_ref, g1_ref, m_ref, v_ref, g_ref, d_ref, mo_ref, vo_ref):
        gv = g0_ref[...] + g1_ref[...]
        g_ref[...] = gv
        d_ref[...], mo_ref[...], vo_ref[...] = _adamw_math(w_ref[...], gv, m_ref[...], v_ref[...])

    spec = pl.BlockSpec((None, tr, cols), lambda li, i: (li, i, 0))
    half = lambda cc: pl.BlockSpec((None, None, tr, cols), lambda li, i: (li, cc, i, 0))
    return pl.pallas_call(
        body, name=name, grid=(l, rows // tr), in_specs=[spec, half(0), half(1), spec, spec], out_specs=[spec] * 4,
        out_shape=[jax.ShapeDtypeStruct(w.shape, F32)] * 4, compiler_params=_params(("parallel", "parallel")))(w, g2, g2, m, v)


def _adamw(w, g, m, v, name):
    shape = w.shape
    cols = shape[-1]
    rows = w.size // cols
    tr = _row_tile(rows, cols)

    def body(w_ref, g_ref, m_ref, v_ref, d_ref, mo_ref, vo_ref):
        d_ref[...], mo_ref[...], vo_ref[...] = _adamw_math(w_ref[...], g_ref[...], m_ref[...], v_ref[...])

    spec = pl.BlockSpec((tr, cols), lambda i: (i, 0))
    outs = pl.pallas_call(
        body, name=name, grid=(rows // tr,), in_specs=[spec] * 4, out_specs=[spec] * 3,
        out_shape=[jax.ShapeDtypeStruct((rows, cols), F32)] * 3,
        compiler_params=_params(("parallel",)))(*[a.reshape(rows, cols) for a in (w, g, m, v)])
    return [o.reshape(shape) for o in outs]


def _silu_grad(x):
    s = jax.nn.sigmoid(x)
    return s * (1.0 + x * (1.0 - s))


def _mod_fwd(cs16, w_mod, b_sh, name):
    l, d, wc = w_mod.shape
    tn = _pick(wc, (512, 384, 256, 128))

    def body(c_ref, w_ref, b_ref, o_ref):
        a = jax.nn.silu(c_ref[...]).astype(BF16)
        o_ref[...] = jnp.dot(a, w_ref[...].astype(BF16), preferred_element_type=F32) + b_ref[...]

    return pl.pallas_call(
        body, name=name, grid=(l, wc // tn),
        in_specs=[_full_spec((16, d)), pl.BlockSpec((None, d, tn), lambda i, j: (i, 0, j)), pl.BlockSpec((None, 1, tn), lambda i, j: (i, 0, j))],
        out_specs=pl.BlockSpec((None, 16, tn), lambda i, j: (i, 0, j)), out_shape=jax.ShapeDtypeStruct((l, 16, wc), F32),
        compiler_params=_params(("parallel", "parallel")))(cs16, w_mod, b_sh)


def _mod_dw(cs16, dm_sh, name):
    l, _, wc = dm_sh.shape
    d = cs16.shape[1]
    tr = _pick(d, (512, 256, 128))
    tc = _pick(wc, (512, 384, 256, 128))

    def body(c_ref, dm_ref, o_ref):
        a = jax.nn.silu(c_ref[...]).astype(BF16)
        o_ref[...] = lax.dot_general(a, dm_ref[...].astype(BF16), (((0,), (0,)), ((), ())), preferred_element_type=F32)

    return pl.pallas_call(
        body, name=name, grid=(l, d // tr, wc // tc),
        in_specs=[pl.BlockSpec((16, tr), lambda i, r, j: (0, r)), pl.BlockSpec((None, 16, tc), lambda i, r, j: (i, 0, j))],
        out_specs=pl.BlockSpec((None, tr, tc), lambda i, r, j: (i, r, j)), out_shape=jax.ShapeDtypeStruct((l, d, wc), F32),
        compiler_params=_params(("parallel", "parallel", "parallel")))(cs16, dm_sh)


def _mod_dc(dm_sh, w_mod, c_ctx, name):
    l, d, wc = w_mod.shape
    tk = _pick(wc, (512, 384, 256, 128))
    nk = wc // tk

    def body(dm_ref, w_ref, c_ref, o_ref, acc_ref):
        i, j = pl.program_id(0), pl.program_id(1)

        @pl.when(jnp.logical_and(i == 0, j == 0))
        def _():
            acc_ref[...] = jnp.zeros_like(acc_ref)

        acc_ref[...] += lax.dot_general(dm_ref[...].astype(BF16), w_ref[...].astype(BF16), (((1,), (1,)), ((), ())),
                                        preferred_element_type=F32)

        @pl.when(jnp.logical_and(i == l - 1, j == nk - 1))
        def _():
            mine = jnp.where(lax.axis_index("c") == 0, 1.0, 0.0)
            o_ref[...] = acc_ref[8:9, :] * _silu_grad(c_ref[...]) * mine

    return pl.pallas_call(
        body, name=name, grid=(l, nk),
        in_specs=[pl.BlockSpec((None, 16, tk), lambda i, j: (i, 0, j)), pl.BlockSpec((None, d, tk), lambda i, j: (i, 0, j)), _full_spec((1, d))],
        out_specs=_full_spec((1, d)), out_shape=jax.ShapeDtypeStruct((1, d), F32), scratch_shapes=[pltpu.VMEM((16, d), F32)],
        compiler_params=_params(("arbitrary", "arbitrary")))(dm_sh, w_mod, c_ctx)


def _dmod_assemble(gath, name):
    _, l, _, w = gath.shape
    gath = gath.transpose(1, 2, 0, 3)
    tc = _pick(w, (2048, 1024, 512, 256, 128))

    def body(lat_ref, ctx_ref, o_ref, b_ref):
        ctx = ctx_ref[0:1, :]
        for dev in range(1, 8):
            ctx = ctx + ctx_ref[dev:dev + 1, :]
        lat = lat_ref[...]
        o_ref[0:8, :] = lat
        o_ref[8:9, :] = ctx
        o_ref[9:16, :] = jnp.zeros((7, tc), F32)
        b_ref[...] = jnp.sum(lat, axis=0, keepdims=True) + ctx

    return pl.pallas_call(
        body, name=name, grid=(l, w // tc),
        in_specs=[pl.BlockSpec((None, None, 8, tc), lambda i, j: (i, 0, 0, j)), pl.BlockSpec((None, None, 8, tc), lambda i, j: (i, 1, 0, j))],
        out_specs=[pl.BlockSpec((None, 16, tc), lambda i, j: (i, 0, j)), pl.BlockSpec((None, 1, tc), lambda i, j: (i, 0, j))],
        out_shape=[jax.ShapeDtypeStruct((l, 16, w), F32), jax.ShapeDtypeStruct((l, 1, w), F32)],
        compiler_params=_params(("parallel", "parallel")))(gath, gath)


SMALL = ("c_ctx", "g_mix", "g_ffn", "mla_q_a_norm", "mla_kv_a_norm", "mla_q_nope_norm", "mla_q_rope_norm", "mla_k_nope_norm",
         "mla_k_rope_norm", "pool_w", "pool_scale", "swa_q_norm", "swa_k_norm", "swa_sink", "na_q_norm", "na_k_norm", "na_rpb",
         "ffn_conv_b")
PACK_W = 512
PACK_Q = 8 * PACK_W


def _pack(arrs):
    flat = []
    for a in arrs:
        f = a.reshape(-1)
        flat.append(jnp.pad(f, (0, (-f.size) % PACK_Q)))
    return jnp.concatenate(flat).reshape(-1, PACK_W)


def _unpack(packed, shapes):
    flat, out, o = packed.reshape(-1), [], 0
    for s in shapes:
        n = 1
        for dim in s:
            n *= dim
        out.append(flat[o:o + n].reshape(s))
        o += n + (-n) % PACK_Q
    return out


def _all_sum(p, name):
    pair = _pair_up([p], name + "_pair")[0]
    chip = _sum_lead(pair, name + "_sum2")
    return _sum_lead(_chip_gather([chip], ["lead"], name + "_gather")[0], name + "_sum4")


WEIGHTS = ("c_ctx", "w_mod", "b_mod", "g_mix", "g_ffn", "w_in", "w_out", "mla_q_a_norm", "mla_w_qb", "mla_kv_a_norm", "mla_w_kvb",
           "mla_q_nope_norm", "mla_q_rope_norm", "mla_k_nope_norm", "mla_k_rope_norm", "pool_w", "pool_scale", "swa_q_norm",
           "swa_k_norm", "swa_sink", "na_q_norm", "na_k_norm", "na_rpb", "ffn_w_up", "ffn_conv_w", "ffn_conv_b", "ffn_w_down")
BIG = ("w_in", "mla_w_qb", "w_out", "ffn_w_down", "ffn_w_up", "mla_w_kvb")
BIG_KINDS = ("cm", "cm", "cm", "cm", "lb", "lb")


def _step(a):
    x, c, ctx = a["x"], a["c"], a["ctx"]
    n_lat, d = x.shape[1], x.shape[2]
    n_ctx = ctx.shape[1]
    l = DEPTH
    px, py, pc = lax.axis_index("x"), lax.axis_index("y"), lax.axis_index("c")
    chip = 2 * px + py
    pvec = [p.reshape(1).astype(I32) for p in (pc, px, py)]
    cvec = pvec[0]

    bf = {k: a[k].astype(BF16) for k in BIG}
    w = {key: [None] * l for key in ("w_in", "w_qb", "w_out", "w_down", "w_up", "w_kvb")}

    finish = {"w_in": lambda g: _w_in_layout(g.transpose(1, 0, 2).reshape(d, P_COLS)),
              "mla_w_qb": lambda g: _w_qb_layout(g.transpose(1, 0, 2).reshape(512, 768)),
              "w_out": lambda g: g.reshape(-1, d), "ffn_w_down": lambda g: g.reshape(-1, d),
              "ffn_w_up": lambda g: g, "mla_w_kvb": lambda g: g}
    slot = dict(zip(BIG, ("w_in", "w_qb", "w_out", "w_down", "w_up", "w_kvb")))
    kind_of = dict(zip(BIG, BIG_GATHER))

    def gather_start(li, keys, tag, after):
        shards = [bf[k][li] for k in keys]
        shapes = [s.shape for s in shards]
        kinds = [kind_of[k] for k in keys]
        lands = [_place_own(s, kd, pvec[1:], "w_place") for s, kd in zip(shards, kinds)]
        plan = _w_gather_plan(shapes, kinds)
        return _remote_start(f"w_start_{li}{tag}", shards + lands, 3 * len(keys), plan, after) + (shapes, kinds, keys, plan)

    def gather_finish(li, started, tag, after):
        ssem, rsem, bufs, _, shapes, kinds, keys, plan = started
        bufs = _remote_wait(f"w_wait_{li}{tag}", ssem, rsem, bufs, 3 * len(keys), plan, after)
        for k, g in zip(keys, _w_fill(bufs[len(keys):], shapes, kinds, "w_fill")):
            w[slot[k]][li] = finish[k](g)

    first, rest = ("w_in", "mla_w_qb", "mla_w_kvb"), ("w_out", "ffn_w_down", "ffn_w_up")
    started = gather_start(0, first, "a", [c])
    c = c + started[3][0:1, 0:1]

    c_all = _chip_gather(_pair_up([c], "c_pair"), ["lead"], "c_gather")[0].reshape(8, d)
    cs16 = jnp.concatenate([c_all, a["c_ctx"][None, :], jnp.zeros((7, d), F32)], axis=0)
    wc = a["w_mod"].shape[-1]
    b_sh = lax.dynamic_slice_in_dim(a["b_mod"], chip * wc, wc, axis=1)[:, None, :]
    mod_sh = _mod_fwd(cs16, a["w_mod"], b_sh, "mod_fwd")
    mod_all, conv_w_full = _chip_gather([mod_sh, a["ffn_conv_w"]], ["lead", "lane"], "mod_gather")
    started_rest = gather_start(0, rest, "b", [mod_all])
    mod_all = mod_all + started_rest[3][0, 0]
    mod_all = mod_all.transpose(1, 2, 0, 3).reshape(l, 16, 4 * wc)
    mods = jnp.stack([lax.dynamic_index_in_dim(mod_all, 2 * chip + pc, axis=1, keepdims=False), mod_all[:, 8]], axis=1)
    mods = mods.reshape(l, 2, 6, d)

    full = {k: a[k] for k in SMALL if k != "c_ctx"}
    full["ffn_conv_w"] = conv_w_full
    w.update(_small_weights(full, mods))
    w["mods"] = [w["mods"][li] for li in range(l)]
    w["g_mix"] = [w["g_mix"][li] for li in range(l)]
    w["g_ffn"] = [w["g_ffn"][li] for li in range(l)]
    tab = _rope_table(n_lat, n_ctx)
    gather_finish(0, started, "a", mods)
    xs = jnp.concatenate([x[0], ctx[0]], axis=0)
    saved = []
    nxt = {}

    def rest_of_layer0(mix):
        gather_finish(0, started_rest, "b", mix)
        nxt[1] = gather_start(1, BIG, "", [w["w_up"][0]])
        w["g_ffn"][0] = w["g_ffn"][0] + nxt[1][3][0:1, 0:1]

    for li in range(l):
        if 1 <= li < l - 1:
            nxt[li + 1] = gather_start(li + 1, BIG, "", [w["w_kvb"][li]])
            w["g_mix"][li] = w["g_mix"][li] + nxt[li + 1][3][0:1, 0:1]
        xs, s = _layer_fwd(xs, w, li, tab, n_lat, before_out=rest_of_layer0 if li == 0 else None)
        saved.append(s)
        if li + 1 < l:
            gather_finish(li + 1, nxt[li + 1], "", xs)
    loss, dx = _loss_kernel(xs, a["loss_target"][0], n_lat // TM, "loss")
    loss = lax.psum(loss[0, 0], ("x", "y", "c"))

    grads = [None] * l
    g_buf = {k: None for k in BIG}
    kind_g = dict(zip(BIG, BIG_KINDS))
    ffn_keys = ("ffn_w_down", "ffn_w_up")
    att_keys = ("w_in", "mla_w_qb", "w_out", "mla_w_kvb")

    def grad_pieces(g, keys):
        ops = {"w_in": lambda: _w_in_unlayout(g["w_in"]).reshape(d, 4, -1).transpose(1, 0, 2),
               "mla_w_qb": lambda: _w_qb_unlayout(g["w_qb"].astype(GRAD_WIRE)).reshape(512, 4, 192).transpose(1, 0, 2),
               "w_out": lambda: g["w_out"].reshape(4, -1, d), "ffn_w_down": lambda: g["w_down"].reshape(4, -1, d),
               "ffn_w_up": lambda: g["w_up"], "mla_w_kvb": lambda: g["w_kvb"].astype(GRAD_WIRE)}
        return [ops[k]() for k in keys]

    def scatter_start(li, g, keys, tag, after):
        ops = grad_pieces(g, keys)
        lands = [jnp.zeros((3, a[k].shape[1], a[k].shape[2]), GRAD_WIRE) for k in keys]
        plan = _g_scatter_plan([kind_g[k] for k in keys], [a[k].shape[2] for k in keys])
        return _remote_start(f"g_start_{li}{tag}", ops + lands, 3 * len(keys), plan, after) + (keys, plan)

    def scatter_finish(li, started, tag, after):
        ssem, rsem, bufs, _, keys, plan = started
        bufs = _remote_wait(f"g_wait_{li}{tag}", ssem, rsem, bufs, 3 * len(keys), plan, after)
        for k, own, landed in zip(keys, bufs[:len(keys)], bufs[len(keys):]):
            g_buf[k] = _sum_into(own, landed, g_buf[k], li, pvec, kind_g[k], l, "g_sum4")

    pending, early = None, []

    def ffn_grads_of_layer0(g):
        early.append(scatter_start(0, g, ffn_keys, "a", [g["w_up"]]))
        return early[0][3]

    for li in reversed(range(l)):
        dx, grads[li] = _layer_bwd(dx, saved[li], w, li, tab, n_lat, after_ffn=ffn_grads_of_layer0 if li == 0 else None)
        if pending is not None:
            scatter_finish(li + 1, pending, "", dx)
        if li > 0:
            pending = scatter_start(li, grads[li], BIG, "", [grads[li]["w_in"]])
            w["mods"][li - 1] = w["mods"][li - 1] + pending[3][0, 0]

    dmods = jnp.stack([grads[li]["mods"] for li in range(l)], axis=0).reshape(l, 2, 6 * d)
    dm_gath = _chip_gather(_pair_up([dmods], "dmod_pair"), ["lead"], "dmod_gather")[0].reshape(8, l, 2, 6 * d)
    dmod_all, g_b_mod = _dmod_assemble(dm_gath, "dmod_assemble")
    dm_sh = lax.dynamic_slice_in_dim(dmod_all, chip * wc, wc, axis=2)
    g_c_ctx = _mod_dc(dm_sh, a["w_mod"], a["c_ctx"][None, :], "mod_dc")

    rg = _reference_grads(grads, big=False)
    rg["c_ctx"] = g_c_ctx[0]
    packed = _all_sum(_pack([rg[k] for k in SMALL] + [rg["ffn_conv_w"]]), "small")
    late = scatter_start(0, grads[0], att_keys, "b", [packed, dmod_all])
    g_w_mod = _mod_dw(cs16, dm_sh + late[3][0, 0], "mod_dw")
    g_out = {"w_mod": g_w_mod, "b_mod": g_b_mod.reshape(l, 6 * d)}
    small_g = _unpack(packed, [a[k].shape for k in SMALL] + [rg["ffn_conv_w"].shape])
    for k, g in zip(SMALL, small_g[:-1]):
        g_out[k] = g
    cw = a["ffn_conv_w"].shape[-1]
    g_out["ffn_conv_w"] = lax.dynamic_slice_in_dim(small_g[-1], chip * cw, cw, axis=2)

    upd = {}
    pk = lambda pre: _pack([a[pre + k] for k in SMALL])
    outs = _adamw(pk(""), _pack([g_out[k] for k in SMALL]), pk("m_"), pk("v_"), "adamw_small")
    for o, kind in zip(outs, ("delta", "m", "v")):
        for k, val in zip(SMALL, _unpack(o, [a[k].shape for k in SMALL])):
            upd[kind, k] = val
    def adamw_each(keys):
        for k in keys:
            outs = _adamw(a[k], g_out[k], a["m_" + k], a["v_" + k], "adamw_" + k)
            for o, kind in zip(outs, ("delta", "m", "v")):
                upd[kind, k] = o

    adamw_each(("w_mod", "b_mod", "ffn_conv_w"))
    scatter_finish(0, early[0], "a", upd["delta", "w_mod"])
    scatter_finish(0, late, "b", upd["delta", "w_mod"])
    g_big = _sib_fill([g_buf[k] for k in BIG], lambda cc: (slice(None), cc), "g_pair")
    for k, g2 in zip(BIG, g_big):
        g_out[k], upd["delta", k], upd["m", k], upd["v", k] = _adamw_pair(a[k], g2, a["m_" + k], a["v_" + k], "adamw_" + k)
    grad_x = dx[0:n_lat].reshape(x.shape)
    return (loss, grad_x, *[g_out[k] for k in WEIGHTS], *[upd["delta", k] for k in WEIGHTS],
            *[upd["m", k] for k in WEIGHTS], *[upd["v", k] for k in WEIGHTS])


def kernel(x, c, ctx, c_ctx, w_mod, b_mod, g_mix, g_ffn, w_in, w_out, mla_q_a_norm, mla_w_qb, mla_kv_a_norm, mla_w_kvb, mla_q_nope_norm, mla_q_rope_norm, mla_k_nope_norm, mla_k_rope_norm, pool_w, pool_scale, swa_q_norm, swa_k_norm, swa_sink, na_q_norm, na_k_norm, na_rpb, ffn_w_up, ffn_conv_w, ffn_conv_b, ffn_w_down, loss_target, m_c_ctx, m_w_mod, m_b_mod, m_g_mix, m_g_ffn, m_w_in, m_w_out, m_mla_q_a_norm, m_mla_w_qb, m_mla_kv_a_norm, m_mla_w_kvb, m_mla_q_nope_norm, m_mla_q_rope_norm, m_mla_k_nope_norm, m_mla_k_rope_norm, m_pool_w, m_pool_scale, m_swa_q_norm, m_swa_k_norm, m_swa_sink, m_na_q_norm, m_na_k_norm, m_na_rpb, m_ffn_w_up, m_ffn_conv_w, m_ffn_conv_b, m_ffn_w_down, v_c_ctx, v_w_mod, v_b_mod, v_g_mix, v_g_ffn, v_w_in, v_w_out, v_mla_q_a_norm, v_mla_w_qb, v_mla_kv_a_norm, v_mla_w_kvb, v_mla_q_nope_norm, v_mla_q_rope_norm, v_mla_k_nope_norm, v_mla_k_rope_norm, v_pool_w, v_pool_scale, v_swa_q_norm, v_swa_k_norm, v_swa_sink, v_na_q_norm, v_na_k_norm, v_na_rpb, v_ffn_w_up, v_ffn_conv_w, v_ffn_conv_b, v_ffn_w_down):
    return _step(dict(locals()))
```

```python
import functools

import jax
import jax.numpy as jnp
from jax import lax
from jax.experimental import pallas as pl
from jax.experimental.pallas import tpu as pltpu

F32 = jnp.float32
BF16 = jnp.bfloat16
I32 = jnp.int32

DEPTH = 4
GRID_W = 64
ROPE_BASE = 10000.0
EPS = 1e-6
NEG = -1e30
MLA_SCALE = 192.0 ** -0.5
HD_SCALE = 64.0 ** -0.5
NA_KROWS = 12
SWA_KEYS = 512
P_COLS = 3648
PW = 3840
TM = 256
HALO = 8
ADAM_LR, ADAM_B1, ADAM_B2, ADAM_EPS, ADAM_WD, ADAM_STEP = 0.001, 0.9, 0.999, 1e-08, 0.01, 10
VMEM_LIMIT = 56 * 1024 * 1024
GRAD_WIRE = BF16
MESH = pl.DeviceIdType.MESH


def _pick(n, cands):
    for c in cands:
        if n % c == 0:
            return c
    return n


def _params(sem=None):
    return pltpu.CompilerParams(dimension_semantics=sem, vmem_limit_bytes=VMEM_LIMIT)


@jax.custom_vjp
def _bdot(a, b):
    return jnp.dot(a.astype(BF16), b.astype(BF16), preferred_element_type=F32)


def _bdot_fwd(a, b):
    return _bdot(a, b), (a.astype(BF16), b.astype(BF16))


def _bdot_bwd(res, g):
    a, b = res
    gb = g.astype(BF16)
    da = lax.dot_general(gb, b, (((1,), (1,)), ((), ())), preferred_element_type=F32)
    db = lax.dot_general(a, gb, (((0,), (0,)), ((), ())), preferred_element_type=F32)
    return da, db


_bdot.defvjp(_bdot_fwd, _bdot_bwd)


@jax.custom_vjp
def _bdot_nt(a, b):
    return lax.dot_general(a.astype(BF16), b.astype(BF16), (((1,), (1,)), ((), ())), preferred_element_type=F32)


def _bdot_nt_fwd(a, b):
    return _bdot_nt(a, b), (a.astype(BF16), b.astype(BF16))


def _bdot_nt_bwd(res, g):
    a, b = res
    gb = g.astype(BF16)
    da = jnp.dot(gb, b, preferred_element_type=F32)
    db = lax.dot_general(gb, a, (((0,), (0,)), ((), ())), preferred_element_type=F32)
    return da, db


_bdot_nt.defvjp(_bdot_nt_fwd, _bdot_nt_bwd)


def _rms(x, g):
    return x * lax.rsqrt(jnp.mean(x * x, axis=-1, keepdims=True) + EPS) * g


def _rope(x, cos, sin):
    xr = jnp.concatenate([-x[:, 16:32], x[:, 0:16], -x[:, 48:64], x[:, 32:48]], axis=-1)
    return x * cos + xr * sin


def _sel(is_ctx, mod, row):
    return jnp.where(is_ctx, mod[1, row:row + 1, :], mod[0, row:row + 1, :])


MM_VMEM_BUDGET = 40 * 1024 * 1024
HBM_BYTES_PER_STEP = 1 << 20


def _mm_tiles(m, n, k, mode, osize, n_unit, k_unit):
    lanes = (3840, 2816, 2048, 1280, 1024, 768, 512, 256)
    subl = (4352, 2176, 1088, 1024, 640, 544, 512, 256)
    tms = [c for c in (lanes if mode == "tn" else subl) if m % c == 0] or [m]
    tns = [c for c in lanes if n_unit % c == 0] or [n_unit]
    tks = [c for c in (subl if mode == "tn" else lanes) if k_unit % c == 0]
    if k_unit == k:
        tks = [k] + tks
    best = None
    for tm in tms:
        for tn in tns:
            for tk in tks:
                nk = k // tk
                vmem = 4 * (tm * tk + tk * tn) + 2 * tm * tn * osize + tm * tn * 4
                if vmem > MM_VMEM_BUDGET:
                    continue
                a_reads = 1 if nk == 1 else n // tn
                b_reads = 1 if (nk == 1 and n == tn) else m // tm
                steps = (m // tm) * (n // tn) * nk
                cost = (2 * m * k * a_reads + 2 * k * n * b_reads + m * n * osize + steps * HBM_BYTES_PER_STEP
                        + (12 * m * n * nk if nk > 1 else 0))
                if best is None or cost < best[0]:
                    best = (cost, tm, tn, tk)
    return best[1:]


def _mm(a, b, mode, out_dtype, name, a_split=False, b_split=False, o_split=False):
    def dims(x, split):
        return (x.shape[1], 2 * x.shape[2]) if split else x.shape

    ar, ac = dims(a, a_split)
    br, bc = dims(b, b_split)
    if mode == "nn":
        m, k, n = ar, ac, bc
        assert br == k
    elif mode == "nt":
        m, k, n = ar, ac, br
        assert bc == k
    else:
        k, m, n = ar, ac, bc
        assert br == k
    n_unit = n // 2 if (o_split or (b_split and mode != "nt")) else n
    k_unit = k // 2 if (mode != "tn" and (a_split or (b_split and mode == "nt"))) else k
    tm, tn, tk = _mm_tiles(m, n, k, mode, jnp.dtype(out_dtype).itemsize, n_unit, k_unit)
    nk = k // tk

    def spec(split, tr, tc, ncols, ridx, cidx):
        if not split:
            return pl.BlockSpec((tr, tc), lambda i, j, kk: (ridx(i, j, kk), cidx(i, j, kk)))
        nh = (ncols // 2) // tc
        return pl.BlockSpec((None, tr, tc), lambda i, j, kk: (cidx(i, j, kk) // nh, ridx(i, j, kk), cidx(i, j, kk) % nh))

    gi = lambda i, j, kk: i
    gj = lambda i, j, kk: j
    gk = lambda i, j, kk: kk
    if mode == "tn":
        a_spec = spec(a_split, tk, tm, ac, gk, gi)
    else:
        a_spec = spec(a_split, tm, tk, ac, gi, gk)
    if mode == "nt":
        b_spec = spec(b_split, tn, tk, bc, gj, gk)
    else:
        b_spec = spec(b_split, tk, tn, bc, gk, gj)
    o_spec = spec(o_split, tm, tn, n, gi, gj)
    dn = {"nn": (((1,), (0,)), ((), ())), "nt": (((1,), (1,)), ((), ())), "tn": (((0,), (0,)), ((), ()))}[mode]

    def body(a_ref, b_ref, o_ref, acc_ref):
        kk = pl.program_id(2)

        @pl.when(kk == 0)
        def _():
            acc_ref[...] = jnp.zeros_like(acc_ref)

        acc_ref[...] += lax.dot_general(a_ref[...], b_ref[...], dn, preferred_element_type=F32)

        @pl.when(kk == nk - 1)
        def _():
            o_ref[...] = acc_ref[...].astype(o_ref.dtype)

    def body_whole_k(a_ref, b_ref, o_ref):
        o_ref[...] = lax.dot_general(a_ref[...], b_ref[...], dn, preferred_element_type=F32).astype(o_ref.dtype)

    oshape = (2, m, n // 2) if o_split else (m, n)
    return pl.pallas_call(
        body if nk > 1 else body_whole_k, name=name, grid=(m // tm, n // tn, nk), in_specs=[a_spec, b_spec], out_specs=o_spec,
        out_shape=jax.ShapeDtypeStruct(oshape, out_dtype), scratch_shapes=[pltpu.VMEM((tm, tn), F32)] if nk > 1 else [],
        compiler_params=_params(("parallel", "parallel", "arbitrary")))(a, b)


def _mm_exact(a, b, name):
    def body(a_ref, b_ref, o_ref):
        o_ref[...] = jnp.dot(a_ref[...], b_ref[...], preferred_element_type=F32, precision=lax.Precision.HIGHEST)

    return pl.pallas_call(body, name=name, out_shape=jax.ShapeDtypeStruct((a.shape[0], b.shape[1]), F32),
                          compiler_params=_params())(a, b)


def _row_spec(width, col=0):
    return pl.BlockSpec((TM, width), lambda i: (i, col))


def _full_spec(shape):
    nd = len(shape)
    return pl.BlockSpec(shape, lambda *_: (0,) * nd)


def _normmod_fn(x, g, mod, is_ctx, row):
    return _rms(x, g) * (1.0 + _sel(is_ctx, mod, row + 1)) + _sel(is_ctx, mod, row)


def _normmod_fwd(x, g, mod, row, nbl, name):
    r, d = x.shape

    def body(x_ref, g_ref, mod_ref, h_ref):
        is_ctx = pl.program_id(0) >= nbl
        h_ref[...] = _normmod_fn(x_ref[...], g_ref[...], mod_ref[...], is_ctx, row).astype(BF16)

    return pl.pallas_call(
        body, name=name, grid=(r // TM,), in_specs=[_row_spec(d), _full_spec((1, d)), _full_spec(mod.shape)],
        out_specs=_row_spec(d), out_shape=jax.ShapeDtypeStruct((r, d), BF16), compiler_params=_params(("parallel",)))(x, g, mod)


def _normmod_bwd(x, g, mod, dh, dx_in, row, nbl, name):
    r, d = x.shape

    def body(x_ref, g_ref, mod_ref, dh_ref, dxin_ref, dx_ref, dg_ref, dmod_ref):
        i = pl.program_id(0)
        is_ctx = i >= nbl

        @pl.when(i == 0)
        def _():
            dg_ref[...] = jnp.zeros_like(dg_ref)
            dmod_ref[...] = jnp.zeros_like(dmod_ref)

        _, vjp = jax.vjp(lambda xx, gg, mm: _normmod_fn(xx, gg, mm, is_ctx, row), x_ref[...], g_ref[...], mod_ref[...])
        dx, dg, dmod = vjp(dh_ref[...])
        dx_ref[...] = dxin_ref[...] + dx
        dg_ref[...] += dg
        dmod_ref[...] += dmod

    return pl.pallas_call(
        body, name=name, grid=(r // TM,),
        in_specs=[_row_spec(d), _full_spec((1, d)), _full_spec(mod.shape), _row_spec(d), _row_spec(d)],
        out_specs=[_row_spec(d), _full_spec((1, d)), _full_spec(mod.shape)],
        out_shape=[jax.ShapeDtypeStruct((r, d), F32), jax.ShapeDtypeStruct((1, d), F32), jax.ShapeDtypeStruct(mod.shape, F32)],
        compiler_params=_params(("arbitrary",)))(x, g, mod, dh, dx_in)


def _resid_fwd(x, y, mod, row, nbl, name):
    r, d = x.shape

    def body(x_ref, y_ref, mod_ref, o_ref):
        is_ctx = pl.program_id(0) >= nbl
        o_ref[...] = x_ref[...] + _sel(is_ctx, mod_ref[...], row) * y_ref[...]

    return pl.pallas_call(
        body, name=name, grid=(r // TM,), in_specs=[_row_spec(d), _row_spec(d), _full_spec(mod.shape)],
        out_specs=_row_spec(d), out_shape=jax.ShapeDtypeStruct((r, d), F32), compiler_params=_params(("parallel",)))(x, y, mod)


def _resid_bwd(dx, y, mod, row, nbl, name):
    r, d = dx.shape

    def body(dx_ref, y_ref, mod_ref, dy_ref, dmod_ref):
        i = pl.program_id(0)
        is_ctx = i >= nbl

        @pl.when(i == 0)
        def _():
            dmod_ref[...] = jnp.zeros_like(dmod_ref)

        dxv = dx_ref[...]
        dy_ref[...] = (_sel(is_ctx, mod_ref[...], row) * dxv).astype(BF16)
        dgate = jnp.sum(dxv * y_ref[...], axis=0, keepdims=True)

        @pl.when(is_ctx)
        def _():
            dmod_ref[1, row:row + 1, :] += dgate

        @pl.when(jnp.logical_not(is_ctx))
        def _():
            dmod_ref[0, row:row + 1, :] += dgate

    return pl.pallas_call(
        body, name=name, grid=(r // TM,), in_specs=[_row_spec(d), _row_spec(d), _full_spec(mod.shape)],
        out_specs=[_row_spec(d), _full_spec(mod.shape)],
        out_shape=[jax.ShapeDtypeStruct((r, d), BF16), jax.ShapeDtypeStruct(mod.shape, F32)],
        compiler_params=_params(("arbitrary",)))(dx, y, mod)


def _loss_kernel(x, target, nbl, name):
    r, d = x.shape

    def body(x_ref, t_ref, loss_ref, dx_ref):
        i = pl.program_id(0)

        @pl.when(i == 0)
        def _():
            loss_ref[...] = jnp.zeros_like(loss_ref)

        @pl.when(i < nbl)
        def _():
            e = x_ref[...] - t_ref[...]
            dx_ref[...] = e / d
            loss_ref[...] += 0.5 * jnp.sum(jnp.mean(e * e, axis=-1, keepdims=True), axis=0, keepdims=True)

        @pl.when(i >= nbl)
        def _():
            dx_ref[...] = jnp.zeros_like(dx_ref)

    return pl.pallas_call(
        body, name=name, grid=(r // TM,),
        in_specs=[_row_spec(d), pl.BlockSpec((TM, d), lambda i: (jnp.minimum(i, nbl - 1), 0))],
        out_specs=[_full_spec((1, 1)), _row_spec(d)],
        out_shape=[jax.ShapeDtypeStruct((1, 1), F32), jax.ShapeDtypeStruct((r, d), F32)],
        compiler_params=_params(("arbitrary",)))(x, target)


SP_QA, SP_KVA, SP_QN, SP_QR, SP_KN, SP_KR, SP_SQ, SP_SK, SP_NQ, SP_NK, SP_SINK = range(11)
C_CQ, C_CKV, C_KR, C_POOL, C_SQ, C_SK, C_SV, C_NQ, C_NK, C_NV = 0, 512, 768, 896, 1408, 1920, 2048, 2176, 2688, 3200


def _prep_fn(p, tab, sp, wqb, wkvb):
    cos, sin = tab[:, 0:64], tab[:, 64:128]
    q = _bdot(_rms(p[:, C_CQ:C_CQ + 512], sp[SP_QA:SP_QA + 1, 0:512]), wqb)
    kv = _bdot(_rms(p[:, C_CKV:C_CKV + 256], sp[SP_KVA:SP_KVA + 1, 0:256]), wkvb)
    krr = _rope(_rms(p[:, C_KR:C_KR + 64], sp[SP_KR:SP_KR + 1, 0:64]), cos, sin)
    zero = jnp.zeros_like(krr)
    aq, ak, av = [], [], []
    for h in range(4):
        qn = _rms(q[:, 128 * h:128 * h + 128], sp[SP_QN:SP_QN + 1, 0:128])
        qr = _rope(_rms(q[:, 512 + 64 * h:576 + 64 * h], sp[SP_QR:SP_QR + 1, 0:64]), cos, sin)
        kn = _rms(kv[:, 256 * h:256 * h + 128], sp[SP_KN:SP_KN + 1, 0:128])
        aq += [qn, qr, zero]
        ak += [kn, krr, zero]
        av.append(kv[:, 256 * h + 128:256 * h + 256])
    cq = [_rope(_rms(p[:, C_SQ + 64 * h:C_SQ + 64 * h + 64], sp[SP_SQ:SP_SQ + 1, 0:64]), cos, sin) for h in range(8)]
    ck = [_rope(_rms(p[:, C_SK + 64 * h:C_SK + 64 * h + 64], sp[SP_SK:SP_SK + 1, 0:64]), cos, sin) for h in range(2)]
    dq = [_rms(p[:, C_NQ + 64 * h:C_NQ + 64 * h + 64], sp[SP_NQ:SP_NQ + 1, 0:64]) for h in range(8)]
    dk = [_rms(p[:, C_NK + 64 * h:C_NK + 64 * h + 64], sp[SP_NK:SP_NK + 1, 0:64]) for h in range(8)]
    cat = lambda xs: jnp.concatenate(xs, axis=-1)
    return (cat(aq), cat(ak), cat(av), cat(cq), cat(ck), p[:, C_SV:C_SV + 128], cat(dq), cat(dk), p[:, C_NV:C_NV + 512])


PREP_WIDTHS = (1024, 1024, 512, 512, 128, 128, 512, 512, 512)


def _prep_fwd(p, tab, sp, wqb, wkvb, name):
    r = p.shape[0]

    def body(p_ref, tab_ref, sp_ref, wqb_ref, wkvb_ref, *outs):
        res = _prep_fn(p_ref[...], tab_ref[...], sp_ref[...], wqb_ref[...].astype(F32), wkvb_ref[...].astype(F32))
        for o_ref, v in zip(outs, res):
            o_ref[...] = v.astype(BF16)

    return pl.pallas_call(
        body, name=name, grid=(r // TM,),
        in_specs=[_row_spec(PW), _row_spec(128), _full_spec(sp.shape), _full_spec(wqb.shape), _full_spec(wkvb.shape)],
        out_specs=[_row_spec(w) for w in PREP_WIDTHS],
        out_shape=[jax.ShapeDtypeStruct((r, w), BF16) for w in PREP_WIDTHS],
        compiler_params=_params(("parallel",)))(p, tab, sp, wqb, wkvb)


def _prep_bwd(p, tab, sp, wqb, wkvb, cots, dpool, name):
    r = p.shape[0]

    def body(p_ref, tab_ref, sp_ref, wqb_ref, wkvb_ref, *rest):
        cot_refs, dpool_ref = rest[:9], rest[9]
        dp_ref, dsp_ref, dwqb_ref, dwkvb_ref = rest[10:]
        i = pl.program_id(0)

        @pl.when(i == 0)
        def _():
            dsp_ref[...] = jnp.zeros_like(dsp_ref)
            dwqb_ref[...] = jnp.zeros_like(dwqb_ref)
            dwkvb_ref[...] = jnp.zeros_like(dwkvb_ref)

        tab = tab_ref[...]
        _, vjp = jax.vjp(lambda pp, ss, wq, wk: _prep_fn(pp, tab, ss, wq, wk),
                         p_ref[...], sp_ref[...], wqb_ref[...].astype(F32), wkvb_ref[...].astype(F32))
        dp, dsp, dwq, dwk = vjp(tuple(c[...] for c in cot_refs))
        dp_ref[...] = dp.astype(BF16)
        dp_ref[:, C_POOL:C_POOL + 512] = dpool_ref[...].astype(BF16)
        dsp_ref[...] += dsp
        dwqb_ref[...] += dwq
        dwkvb_ref[...] += dwk

    return pl.pallas_call(
        body, name=name, grid=(r // TM,),
        in_specs=[_row_spec(PW), _row_spec(128), _full_spec(sp.shape), _full_spec(wqb.shape), _full_spec(wkvb.shape)]
        + [_row_spec(w) for w in PREP_WIDTHS] + [_row_spec(512)],
        out_specs=[_row_spec(PW), _full_spec(sp.shape), _full_spec(wqb.shape), _full_spec(wkvb.shape)],
        out_shape=[jax.ShapeDtypeStruct((r, PW), BF16), jax.ShapeDtypeStruct(sp.shape, F32),
                   jax.ShapeDtypeStruct(wqb.shape, F32), jax.ShapeDtypeStruct(wkvb.shape, F32)],
        compiler_params=_params(("arbitrary",)))(p, tab, sp, wqb, wkvb, *cots, dpool)


def _mla_scores(q, k, is_ctx, n_lat):
    s = lax.dot_general(q, k, (((1,), (1,)), ((), ())), preferred_element_type=F32) * MLA_SCALE
    kid = lax.broadcasted_iota(I32, (1, s.shape[1]), 1)
    return s + jnp.where(jnp.logical_and(is_ctx, kid < n_lat), NEG, 0.0)


def _mla_fwd(aq, ak, av, n_lat, name):
    r = aq.shape[0]
    nbl = n_lat // TM

    def body(q_ref, k_ref, v_ref, o_ref, lse_ref):
        s = _mla_scores(q_ref[...], k_ref[...], pl.program_id(1) >= nbl, n_lat)
        m = jnp.max(s, axis=-1, keepdims=True)
        e = jnp.exp(s - m)
        l = jnp.sum(e, axis=-1, keepdims=True)
        lse_ref[...] = m + jnp.log(l)
        o_ref[...] = jnp.dot((e * (1.0 / l)).astype(BF16), v_ref[...], preferred_element_type=F32).astype(BF16)

    return pl.pallas_call(
        body, name=name, grid=(4, r // TM),
        in_specs=[pl.BlockSpec((TM, 256), lambda h, i: (i, h)), pl.BlockSpec((r, 256), lambda h, i: (0, h)),
                  pl.BlockSpec((r, 128), lambda h, i: (0, h))],
        out_specs=[pl.BlockSpec((TM, 128), lambda h, i: (i, h)), pl.BlockSpec((None, TM, 1), lambda h, i: (h, i, 0))],
        out_shape=[jax.ShapeDtypeStruct((r, 512), BF16), jax.ShapeDtypeStruct((4, r, 1), F32)],
        compiler_params=_params(("parallel", "parallel")))(aq, ak, av)


MLA_KEYS = 256


def _mla_bwd(aq, ak, av, lse, mix, dmix, n_lat, name):
    r = aq.shape[0]
    nbl = n_lat // TM

    def body(q_ref, k_ref, v_ref, lse_ref, o_ref, do_ref, dq_ref, dk_ref, dv_ref):
        i = pl.program_id(1)

        @pl.when(i == 0)
        def _():
            dk_ref[...] = jnp.zeros_like(dk_ref)
            dv_ref[...] = jnp.zeros_like(dv_ref)

        q = q_ref[...]
        do = do_ref[...]
        dob = do.astype(BF16)
        lse = lse_ref[...]
        delta = jnp.sum(do * o_ref[...].astype(F32), axis=-1, keepdims=True)
        is_ctx = i >= nbl

        def chunk(c, dq):
            k0 = pl.multiple_of(c * MLA_KEYS, MLA_KEYS)
            ks = pl.ds(k0, MLA_KEYS)
            k, v = k_ref[ks, :], v_ref[ks, :]
            s = lax.dot_general(q, k, (((1,), (1,)), ((), ())), preferred_element_type=F32) * MLA_SCALE
            kid = k0 + lax.broadcasted_iota(I32, (1, MLA_KEYS), 1)
            p = jnp.exp(s + jnp.where(jnp.logical_and(is_ctx, kid < n_lat), NEG, 0.0) - lse)
            dv_ref[ks, :] += lax.dot_general(p.astype(BF16), dob, (((0,), (0,)), ((), ())), preferred_element_type=F32)
            dp = lax.dot_general(dob, v, (((1,), (1,)), ((), ())), preferred_element_type=F32)
            ds = (p * (dp - delta) * MLA_SCALE).astype(BF16)
            dk_ref[ks, :] += lax.dot_general(ds, q, (((0,), (0,)), ((), ())), preferred_element_type=F32)
            return dq + jnp.dot(ds, k, preferred_element_type=F32)

        dq_ref[...] = lax.fori_loop(0, r // MLA_KEYS, chunk, jnp.zeros((TM, 256), F32))

    return pl.pallas_call(
        body, name=name, grid=(4, r // TM),
        in_specs=[pl.BlockSpec((TM, 256), lambda h, i: (i, h)), pl.BlockSpec((r, 256), lambda h, i: (0, h)),
                  pl.BlockSpec((r, 128), lambda h, i: (0, h)), pl.BlockSpec((None, TM, 1), lambda h, i: (h, i, 0)),
                  pl.BlockSpec((TM, 128), lambda h, i: (i, h)), pl.BlockSpec((TM, 128), lambda h, i: (i, h))],
        out_specs=[pl.BlockSpec((TM, 256), lambda h, i: (i, h)), pl.BlockSpec((r, 256), lambda h, i: (0, h)),
                   pl.BlockSpec((r, 128), lambda h, i: (0, h))],
        out_shape=[jax.ShapeDtypeStruct((r, 1024), F32), jax.ShapeDtypeStruct((r, 1024), F32), jax.ShapeDtypeStruct((r, 512), F32)],
        compiler_params=_params(("parallel", "arbitrary")))(aq, ak, av, lse, mix, dmix)


def _pool_fn(ext, w, sc, gid0, grp, is_ctx, n_lat, r_all):
    gid = gid0 + lax.broadcasted_iota(I32, (TM + 2 * HALO, 1), 0)
    lo = jnp.where(is_ctx, n_lat, 0)
    hi = jnp.where(is_ctx, r_all, n_lat)
    z = jnp.where(jnp.logical_and(gid >= lo, gid < hi), ext, 0.0)
    w2 = jnp.roll(z, 1, axis=0) + z
    w4 = jnp.roll(w2, 1, axis=0) + jnp.roll(w2, -1, axis=0)
    w8 = jnp.roll(w4, 2, axis=0) + jnp.roll(w4, -2, axis=0)
    w16 = jnp.roll(w8, 4, axis=0) + jnp.roll(w8, -4, axis=0)
    win = jnp.where(grp == 0, w2, jnp.where(grp == 1, w4, jnp.where(grp == 2, w8, w16)))
    half = jnp.left_shift(1, grp)
    cnt = jnp.maximum(jnp.minimum(gid + half, hi) - jnp.maximum(gid - half, lo), 1).astype(F32)
    d = (win / cnt - ext)[HALO:HALO + TM]
    return _bdot(d, w) * sc


def _pool_ext(u_ref, i, r_all):
    s0 = pl.multiple_of(jnp.maximum(i * TM - HALO, 0), HALO)
    s2 = pl.multiple_of(jnp.minimum(i * TM + TM, r_all - HALO), HALO)
    ext = jnp.concatenate([u_ref[pl.ds(s0, HALO), :], u_ref[pl.ds(pl.multiple_of(i * TM, TM), TM), :], u_ref[pl.ds(s2, HALO), :]], axis=0)
    return ext, s0, s2


def _pool_specs(r):
    return [pl.BlockSpec((r, 128), lambda g, i: (0, C_POOL // 128 + g)), pl.BlockSpec((None, 128, 128), lambda g, i: (g, 0, 0)),
            pl.BlockSpec((None, 1, 128), lambda g, i: (g, 0, 0))]


def _pool_fwd(p, pool_w, pool_sc, n_lat, name):
    r = p.shape[0]
    nbl = n_lat // TM

    def body(u_ref, w_ref, sc_ref, o_ref):
        g, i = pl.program_id(0), pl.program_id(1)
        ext, _, _ = _pool_ext(u_ref, i, r)
        o_ref[...] = _pool_fn(ext, w_ref[...], sc_ref[...], i * TM - HALO, g, i >= nbl, n_lat, r).astype(BF16)

    return pl.pallas_call(
        body, name=name, grid=(4, r // TM), in_specs=_pool_specs(r), out_specs=pl.BlockSpec((TM, 128), lambda g, i: (i, g)),
        out_shape=jax.ShapeDtypeStruct((r, 512), BF16), compiler_params=_params(("parallel", "parallel")))(p, pool_w, pool_sc)


def _pool_bwd(p, pool_w, pool_sc, dmix, n_lat, name):
    r = p.shape[0]
    nbl = n_lat // TM

    def body(u_ref, w_ref, sc_ref, do_ref, du_ref, dw_ref, dsc_ref):
        g, i = pl.program_id(0), pl.program_id(1)

        @pl.when(i == 0)
        def _():
            du_ref[...] = jnp.zeros_like(du_ref)
            dw_ref[...] = jnp.zeros_like(dw_ref)
            dsc_ref[...] = jnp.zeros_like(dsc_ref)

        ext, s0, s2 = _pool_ext(u_ref, i, r)
        _, vjp = jax.vjp(lambda e, w, s: _pool_fn(e, w, s, i * TM - HALO, g, i >= nbl, n_lat, r), ext, w_ref[...], sc_ref[...])
        dext, dw, dsc = vjp(do_ref[...])
        du_ref[pl.ds(s0, HALO), :] += dext[0:HALO]
        du_ref[pl.ds(pl.multiple_of(i * TM, TM), TM), :] += dext[HALO:HALO + TM]
        du_ref[pl.ds(s2, HALO), :] += dext[HALO + TM:]
        dw_ref[...] += dw
        dsc_ref[...] += dsc

    return pl.pallas_call(
        body, name=name, grid=(4, r // TM), in_specs=_pool_specs(r) + [pl.BlockSpec((TM, 128), lambda g, i: (i, 4 + g))],
        out_specs=[pl.BlockSpec((r, 128), lambda g, i: (0, g)), pl.BlockSpec((None, 128, 128), lambda g, i: (g, 0, 0)),
                   pl.BlockSpec((None, 1, 128), lambda g, i: (g, 0, 0))],
        out_shape=[jax.ShapeDtypeStruct((r, 512), F32), jax.ShapeDtypeStruct((4, 128, 128), F32), jax.ShapeDtypeStruct((4, 1, 128), F32)],
        compiler_params=_params(("parallel", "arbitrary")))(p, pool_w, pool_sc, dmix)


def _softmax_parts(parts, extra=None):
    m = functools.reduce(jnp.maximum, [jnp.max(s, axis=-1, keepdims=True) for s in parts])
    if extra is not None:
        m = jnp.maximum(m, extra)
    m = lax.stop_gradient(m)
    es = [jnp.exp(s - m) for s in parts]
    den = functools.reduce(jnp.add, [jnp.sum(e, axis=-1, keepdims=True) for e in es])
    if extra is not None:
        den = den + jnp.exp(extra - m)
    inv = 1.0 / den
    return [e * inv for e in es]


def _swa_band(qpos0, kpos0, is_ctx):
    qpos = qpos0 + lax.broadcasted_iota(I32, (TM, SWA_KEYS), 0)
    kpos = kpos0 + lax.broadcasted_iota(I32, (TM, SWA_KEYS), 1)
    valid = jnp.logical_and(jnp.abs(kpos - qpos) <= 128, jnp.logical_not(is_ctx))
    band = jnp.where(valid, 0.0, NEG)
    return jnp.concatenate([band] * 4, axis=0)


def _swa_fn(q4, kw, vw, kc, vc, sink4, band):
    qs = jnp.concatenate([q4[:, 64 * g:64 * g + 64] for g in range(4)], axis=0)
    s_loc = _bdot_nt(qs, kw) * HD_SCALE + band
    s_ctx = _bdot_nt(qs, kc) * HD_SCALE
    sink = jnp.concatenate([jnp.broadcast_to(sink4[:, g:g + 1], (TM, 1)) for g in range(4)], axis=0)
    p_loc, p_ctx = _softmax_parts([s_loc, s_ctx], sink)
    o = _bdot(p_loc, vw) + _bdot(p_ctx, vc)
    return jnp.concatenate([o[TM * g:TM * (g + 1)] for g in range(4)], axis=1)


def _swa_window(i, n_lat):
    return pl.multiple_of(jnp.clip(i * TM - 128, 0, n_lat - SWA_KEYS), 128)


def _swa_fwd(cq, ck, cv, sp, n_lat, name):
    r = cq.shape[0]
    nbl = n_lat // TM

    def body(q_ref, k_ref, v_ref, sp_ref, o_ref):
        i = pl.program_id(0)
        k0 = _swa_window(i, n_lat)
        kw, vw = k_ref[pl.ds(k0, SWA_KEYS), :].astype(F32), v_ref[pl.ds(k0, SWA_KEYS), :].astype(F32)
        kc, vc = k_ref[pl.ds(n_lat, r - n_lat), :].astype(F32), v_ref[pl.ds(n_lat, r - n_lat), :].astype(F32)
        band = _swa_band(i * TM, k0, i >= nbl)
        for j in range(2):
            c = slice(64 * j, 64 * j + 64)
            o = _swa_fn(q_ref[:, 256 * j:256 * j + 256].astype(F32), kw[:, c], vw[:, c], kc[:, c], vc[:, c],
                        sp_ref[SP_SINK:SP_SINK + 1, 4 * j:4 * j + 4], band)
            o_ref[:, 256 * j:256 * j + 256] = o.astype(BF16)

    return pl.pallas_call(
        body, name=name, grid=(r // TM,),
        in_specs=[_row_spec(512), _full_spec((r, 128)), _full_spec((r, 128)), _full_spec(sp.shape)],
        out_specs=_row_spec(512), out_shape=jax.ShapeDtypeStruct((r, 512), BF16), compiler_params=_params(("parallel",)))(cq, ck, cv, sp)


def _swa_bwd(cq, ck, cv, sp, dmix, n_lat, name):
    r = cq.shape[0]
    nbl = n_lat // TM
    nc = r - n_lat

    def body(q_ref, k_ref, v_ref, sp_ref, do_ref, dq_ref, dk_ref, dv_ref, dsp_ref):
        i = pl.program_id(0)

        @pl.when(i == 0)
        def _():
            dk_ref[...] = jnp.zeros_like(dk_ref)
            dv_ref[...] = jnp.zeros_like(dv_ref)
            dsp_ref[...] = jnp.zeros_like(dsp_ref)

        k0 = _swa_window(i, n_lat)
        kw, vw = k_ref[pl.ds(k0, SWA_KEYS), :].astype(F32), v_ref[pl.ds(k0, SWA_KEYS), :].astype(F32)
        kc, vc = k_ref[pl.ds(n_lat, nc), :].astype(F32), v_ref[pl.ds(n_lat, nc), :].astype(F32)
        dkw, dvw, dkc, dvc, dsk = [], [], [], [], []
        band = _swa_band(i * TM, k0, i >= nbl)
        for j in range(2):
            c = slice(64 * j, 64 * j + 64)
            _, vjp = jax.vjp(lambda q4, a, b, cc, d, s: _swa_fn(q4, a, b, cc, d, s, band),
                             q_ref[:, 256 * j:256 * j + 256].astype(F32), kw[:, c], vw[:, c], kc[:, c], vc[:, c],
                             sp_ref[SP_SINK:SP_SINK + 1, 4 * j:4 * j + 4])
            dq4, a, b, cc, d, s = vjp(do_ref[:, 256 * j:256 * j + 256])
            dq_ref[:, 256 * j:256 * j + 256] = dq4
            dkw.append(a), dvw.append(b), dkc.append(cc), dvc.append(d), dsk.append(s)
        cat = lambda xs: jnp.concatenate(xs, axis=1)
        dk_ref[pl.ds(k0, SWA_KEYS), :] += cat(dkw)
        dv_ref[pl.ds(k0, SWA_KEYS), :] += cat(dvw)
        dk_ref[pl.ds(n_lat, nc), :] += cat(dkc)
        dv_ref[pl.ds(n_lat, nc), :] += cat(dvc)
        dsp_ref[SP_SINK:SP_SINK + 1, 0:8] += cat(dsk)

    return pl.pallas_call(
        body, name=name, grid=(r // TM,),
        in_specs=[_row_spec(512), _full_spec((r, 128)), _full_spec((r, 128)), _full_spec(sp.shape), _row_spec(512, 2)],
        out_specs=[_row_spec(512), _full_spec((r, 128)), _full_spec((r, 128)), _full_spec(sp.shape)],
        out_shape=[jax.ShapeDtypeStruct((r, 512), F32), jax.ShapeDtypeStruct((r, 128), F32), jax.ShapeDtypeStruct((r, 128), F32),
                   jax.ShapeDtypeStruct(sp.shape, F32)],
        compiler_params=_params(("arbitrary",)))(cq, ck, cv, sp, dmix)


def _na_fn(q, kw, vw, kc, vc, bias):
    s_loc = _bdot_nt(q, kw) * HD_SCALE + bias
    s_ctx = _bdot_nt(q, kc) * HD_SCALE
    p_loc, p_ctx = _softmax_parts([s_loc, s_ctx])
    return _bdot(p_loc, vw) + _bdot(p_ctx, vc)


NA_MASKED = 15


def _na_geometry(i, n_lat, is_ctx):
    rows = n_lat // GRID_W
    qrow0 = i * (TM // GRID_W)
    krow0 = jnp.clip(qrow0 - 4, 0, rows - NA_KROWS)
    dr = []
    for qi in range(TM // GRID_W):
        r_lo = jnp.clip(qrow0 + qi - 4, 0, rows - 8)
        row = []
        for kj in range(NA_KROWS):
            kr = krow0 + kj
            ok = jnp.logical_and(jnp.logical_and(kr >= r_lo, kr < r_lo + 8), jnp.logical_not(is_ctx))
            row.append(jnp.where(ok, kr - (qrow0 + qi) + 7, NA_MASKED))
        dr.append(row)
    return pl.multiple_of(krow0 * GRID_W, GRID_W), dr


def _na_bias(t1_ref, hh, dr):
    return jnp.concatenate([jnp.concatenate([t1_ref[hh, dr[qi][kj]] for kj in range(NA_KROWS)], axis=1)
                            for qi in range(TM // GRID_W)], axis=0)


def _na_specs(r):
    return [pl.BlockSpec((TM, 128), lambda pr, i: (i, pr)), pl.BlockSpec((r, 128), lambda pr, i: (0, pr)),
            pl.BlockSpec((r, 128), lambda pr, i: (0, pr)), pl.BlockSpec((2, 16, GRID_W, GRID_W), lambda pr, i: (pr, 0, 0, 0))]


def _na_fwd(dq, dk, dv, t1, n_lat, name):
    r = dq.shape[0]
    nbl = n_lat // TM
    nc = r - n_lat
    nk = NA_KROWS * GRID_W

    def body(q_ref, k_ref, v_ref, t1_ref, o_ref):
        i = pl.program_id(1)
        k0, dr = _na_geometry(jnp.minimum(i, nbl - 1), n_lat, i >= nbl)
        kw, vw = k_ref[pl.ds(k0, nk), :].astype(F32), v_ref[pl.ds(k0, nk), :].astype(F32)
        kc, vc = k_ref[pl.ds(n_lat, nc), :].astype(F32), v_ref[pl.ds(n_lat, nc), :].astype(F32)
        for hh in range(2):
            c = slice(64 * hh, 64 * hh + 64)
            o = _na_fn(q_ref[:, c].astype(F32), kw[:, c], vw[:, c], kc[:, c], vc[:, c], _na_bias(t1_ref, hh, dr))
            o_ref[:, c] = o.astype(BF16)

    return pl.pallas_call(
        body, name=name, grid=(4, r // TM), in_specs=_na_specs(r), out_specs=pl.BlockSpec((TM, 128), lambda pr, i: (i, pr)),
        out_shape=jax.ShapeDtypeStruct((r, 512), BF16), compiler_params=_params(("parallel", "parallel")))(dq, dk, dv, t1)


def _na_bwd(dq, dk, dv, t1, dmix, n_lat, name):
    r = dq.shape[0]
    nbl = n_lat // TM
    nc = r - n_lat
    nk = NA_KROWS * GRID_W

    def body(q_ref, k_ref, v_ref, t1_ref, do_ref, dq_ref, dk_ref, dv_ref, dt1_ref):
        i = pl.program_id(1)

        @pl.when(i == 0)
        def _():
            dk_ref[...] = jnp.zeros_like(dk_ref)
            dv_ref[...] = jnp.zeros_like(dv_ref)
            dt1_ref[...] = jnp.zeros_like(dt1_ref)

        k0, dr = _na_geometry(jnp.minimum(i, nbl - 1), n_lat, i >= nbl)
        kw, vw = k_ref[pl.ds(k0, nk), :].astype(F32), v_ref[pl.ds(k0, nk), :].astype(F32)
        kc, vc = k_ref[pl.ds(n_lat, nc), :].astype(F32), v_ref[pl.ds(n_lat, nc), :].astype(F32)
        dkw, dvw, dkc, dvc = [], [], [], []
        for hh in range(2):
            c = slice(64 * hh, 64 * hh + 64)
            _, vjp = jax.vjp(_na_fn,
                             q_ref[:, c].astype(F32), kw[:, c], vw[:, c], kc[:, c], vc[:, c], _na_bias(t1_ref, hh, dr))
            dqh, a, b, cc, d, dbias = vjp(do_ref[:, c])
            dq_ref[:, c] = dqh
            dkw.append(a), dvw.append(b), dkc.append(cc), dvc.append(d)
            for qi in range(TM // GRID_W):
                for kj in range(NA_KROWS):
                    dt1_ref[hh, dr[qi][kj]] += dbias[GRID_W * qi:GRID_W * (qi + 1), GRID_W * kj:GRID_W * (kj + 1)]
        cat = lambda xs: jnp.concatenate(xs, axis=1)
        dk_ref[pl.ds(k0, nk), :] += cat(dkw)
        dv_ref[pl.ds(k0, nk), :] += cat(dvw)
        dk_ref[pl.ds(n_lat, nc), :] += cat(dkc)
        dv_ref[pl.ds(n_lat, nc), :] += cat(dvc)

    return pl.pallas_call(
        body, name=name, grid=(4, r // TM), in_specs=_na_specs(r) + [pl.BlockSpec((TM, 128), lambda pr, i: (i, 12 + pr))],
        out_specs=[pl.BlockSpec((TM, 128), lambda pr, i: (i, pr)), pl.BlockSpec((r, 128), lambda pr, i: (0, pr)),
                   pl.BlockSpec((r, 128), lambda pr, i: (0, pr)), pl.BlockSpec((2, 16, GRID_W, GRID_W), lambda pr, i: (pr, 0, 0, 0))],
        out_shape=[jax.ShapeDtypeStruct((r, 512), F32)] * 3 + [jax.ShapeDtypeStruct((8, 16, GRID_W, GRID_W), F32)],
        compiler_params=_params(("parallel", "arbitrary")))(dq, dk, dv, t1, dmix)


def _conv_ext(main_ref, prev_ref, next_ref, edges):
    prev_ok, next_ok = edges
    return jnp.concatenate([jnp.where(prev_ok, prev_ref[...], 0.0), main_ref[...], jnp.where(next_ok, next_ref[...], 0.0)], axis=0)


def _conv_edges(i, nbl, nb):
    return jnp.logical_and(i != 0, i != nbl), jnp.logical_and(i != nbl - 1, i != nb - 1)


def _conv_apply(ext, w, b):
    up = jnp.roll(ext, 1, axis=0)
    dn = jnp.roll(ext, -1, axis=0)
    return up * w[0:1] + ext * w[1:2] + dn * w[2:3] + b, up, dn


def _conv_in_specs(tc, r):
    nb8 = TM // HALO
    last8 = r // HALO - 1

    def trio(half):
        return [pl.BlockSpec((None, TM, tc), lambda j, i: (half, i, j)),
                pl.BlockSpec((None, HALO, tc), lambda j, i: (half, jnp.maximum(i * nb8 - 1, 0), j)),
                pl.BlockSpec((None, HALO, tc), lambda j, i: (half, jnp.minimum((i + 1) * nb8, last8), j))]

    wb = [pl.BlockSpec((None, 3, tc), lambda j, i: (0, 0, j)), pl.BlockSpec((None, 3, tc), lambda j, i: (1, 0, j)),
          pl.BlockSpec((None, 1, tc), lambda j, i: (0, 0, j)), pl.BlockSpec((None, 1, tc), lambda j, i: (1, 0, j))]
    return trio(0) + trio(1) + wb


def _convgate_fwd(a3, cw, cb, n_lat, name):
    _, r, ff = a3.shape
    nbl = n_lat // TM
    tc = _pick(ff, (512, 256, 128))

    def body(g_ref, gp_ref, gn_ref, v_ref, vp_ref, vn_ref, wg_ref, wv_ref, bg_ref, bv_ref, u_ref):
        edges = _conv_edges(pl.program_id(1), nbl, r // TM)
        gg, _, _ = _conv_apply(_conv_ext(g_ref, gp_ref, gn_ref, edges), wg_ref[...], bg_ref[...])
        gv, _, _ = _conv_apply(_conv_ext(v_ref, vp_ref, vn_ref, edges), wv_ref[...], bv_ref[...])
        u_ref[...] = (jax.nn.silu(gg[HALO:HALO + TM]) * gv[HALO:HALO + TM]).astype(BF16)

    return pl.pallas_call(
        body, name=name, grid=(ff // tc, r // TM), in_specs=_conv_in_specs(tc, r),
        out_specs=pl.BlockSpec((TM, tc), lambda j, i: (i, j)), out_shape=jax.ShapeDtypeStruct((r, ff), BF16),
        compiler_params=_params(("parallel", "parallel")))(a3, a3, a3, a3, a3, a3, cw, cw, cb, cb)


def _convgate_bwd(a3, cw, cb, du, n_lat, name):
    _, r, ff = a3.shape
    nbl = n_lat // TM
    tc = _pick(ff, (512, 256, 128))
    nb8 = TM // HALO
    last8 = r // HALO - 1

    def body(g_ref, gp_ref, gn_ref, v_ref, vp_ref, vn_ref, wg_ref, wv_ref, bg_ref, bv_ref, du_ref, dup_ref, dun_ref,
             da_ref, dcw_ref, dcb_ref):
        i = pl.program_id(1)

        @pl.when(i == 0)
        def _():
            dcw_ref[...] = jnp.zeros_like(dcw_ref)
            dcb_ref[...] = jnp.zeros_like(dcb_ref)

        edges = _conv_edges(i, nbl, r // TM)
        wg, wv = wg_ref[...], wv_ref[...]
        eg, ev = _conv_ext(g_ref, gp_ref, gn_ref, edges), _conv_ext(v_ref, vp_ref, vn_ref, edges)
        gg, ug, dg_ = _conv_apply(eg, wg, bg_ref[...])
        gv, uv, dv_ = _conv_apply(ev, wv, bv_ref[...])
        due = _conv_ext(du_ref, dup_ref, dun_ref, edges)
        sg = jax.nn.sigmoid(gg)
        dgg = due * gv * (sg * (1.0 + gg * (1.0 - sg)))
        dgv = due * (gg * sg)
        main = slice(HALO, HALO + TM)
        for h, (dgx, w, ex, upx, dnx) in enumerate(((dgg, wg, eg, ug, dg_), (dgv, wv, ev, uv, dv_))):
            da = dgx * w[1:2] + jnp.roll(dgx, -1, axis=0) * w[0:1] + jnp.roll(dgx, 1, axis=0) * w[2:3]
            da_ref[h] = da[main].astype(BF16)
            dm = dgx[main]
            dcw_ref[h, 0:1, :] += jnp.sum(dm * upx[main], axis=0, keepdims=True)
            dcw_ref[h, 1:2, :] += jnp.sum(dm * ex[main], axis=0, keepdims=True)
            dcw_ref[h, 2:3, :] += jnp.sum(dm * dnx[main], axis=0, keepdims=True)
            dcb_ref[h] += jnp.sum(dm, axis=0, keepdims=True)

    du_specs = [pl.BlockSpec((TM, tc), lambda j, i: (i, j)),
                pl.BlockSpec((HALO, tc), lambda j, i: (jnp.maximum(i * nb8 - 1, 0), j)),
                pl.BlockSpec((HALO, tc), lambda j, i: (jnp.minimum((i + 1) * nb8, last8), j))]
    return pl.pallas_call(
        body, name=name, grid=(ff // tc, r // TM), in_specs=_conv_in_specs(tc, r) + du_specs,
        out_specs=[pl.BlockSpec((2, TM, tc), lambda j, i: (0, i, j)), pl.BlockSpec((2, 3, tc), lambda j, i: (0, 0, j)),
                   pl.BlockSpec((2, 1, tc), lambda j, i: (0, 0, j))],
        out_shape=[jax.ShapeDtypeStruct((2, r, ff), BF16), jax.ShapeDtypeStruct((2, 3, ff), F32), jax.ShapeDtypeStruct((2, 1, ff), F32)],
        compiler_params=_params(("parallel", "arbitrary")))(a3, a3, a3, a3, a3, a3, cw, cw, cb, cb, du, du, du)


def _layer_fwd(x, w, l, tab, n_lat, before_out=None):
    nbl = n_lat // TM
    mod = w["mods"][l]
    h1 = _normmod_fwd(x, w["g_mix"][l], mod, 0, nbl, "normmod_fwd")
    p = _mm(h1, w["w_in"][l], "nn", F32, "mm_in")
    qkv = _prep_fwd(p, tab, w["sp"][l], w["w_qb"][l], w["w_kvb"][l], "prep_fwd")
    oa, lse = _mla_fwd(qkv[0], qkv[1], qkv[2], n_lat, "mla_fwd")
    ob = _pool_fwd(p, w["pool_w"][l], w["pool_sc"][l], n_lat, "pool_fwd")
    oc = _swa_fwd(qkv[3], qkv[4], qkv[5], w["sp"][l], n_lat, "swa_fwd")
    od = _na_fwd(qkv[6], qkv[7], qkv[8], w["t1"][l], n_lat, "na_fwd")
    mix = jnp.concatenate([oa, ob, oc, od], axis=1)
    if before_out is not None:
        before_out(mix)
    y = _mm(mix, w["w_out"][l], "nn", F32, "mm_out")
    x1 = _resid_fwd(x, y, mod, 2, nbl, "resid_fwd")
    h2 = _normmod_fwd(x1, w["g_ffn"][l], mod, 3, nbl, "normmod_fwd")
    a3 = _mm(h2, w["w_up"][l], "nn", F32, "mm_up", o_split=True)
    u = _convgate_fwd(a3, w["conv_w"][l], w["conv_b"][l], n_lat, "convgate_fwd")
    y2 = _mm(u, w["w_down"][l], "nn", F32, "mm_down")
    x2 = _resid_fwd(x1, y2, mod, 5, nbl, "resid_fwd")
    return x2, dict(x=x, h1=h1, p=p, qkv=qkv, lse=lse, mix=mix, y=y, x1=x1, h2=h2, a3=a3, u=u, y2=y2)


def _layer_bwd(dx, s, w, l, tab, n_lat, after_ffn=None):
    nbl = n_lat // TM
    mod = w["mods"][l]
    g = {}
    dy2, dmod_a = _resid_bwd(dx, s["y2"], mod, 5, nbl, "resid_bwd")
    g["w_down"] = _mm(s["u"], dy2, "tn", GRAD_WIRE, "mm_dwdown")
    du = _mm(dy2, w["w_down"][l], "nt", F32, "mm_du")
    da3, g["conv_w"], g["conv_b"] = _convgate_bwd(s["a3"], w["conv_w"][l], w["conv_b"][l], du, n_lat, "convgate_bwd")
    g["w_up"] = _mm(s["h2"], da3, "tn", GRAD_WIRE, "mm_dwup", b_split=True)
    dh2 = _mm(da3, w["w_up"][l], "nt", F32, "mm_dh2", a_split=True)
    g_ffn = w["g_ffn"][l] if after_ffn is None else w["g_ffn"][l] + after_ffn(g)[0:1, 0:1]
    dx1, g["g_ffn"], dmod_b = _normmod_bwd(s["x1"], g_ffn, mod, dh2, dx, 3, nbl, "normmod_bwd")
    dy, dmod_c = _resid_bwd(dx1, s["y"], mod, 2, nbl, "resid_bwd")
    g["w_out"] = _mm(s["mix"], dy, "tn", GRAD_WIRE, "mm_dwout")
    dmix = _mm(dy, w["w_out"][l], "nt", F32, "mm_dmix")
    qkv = s["qkv"]
    daq, dak, dav = _mla_bwd(qkv[0], qkv[1], qkv[2], s["lse"], s["mix"], dmix, n_lat, "mla_bwd")
    dpool, g["pool_w"], g["pool_sc"] = _pool_bwd(s["p"], w["pool_w"][l], w["pool_sc"][l], dmix, n_lat, "pool_bwd")
    dcq, dck, dcv, dsp_c = _swa_bwd(qkv[3], qkv[4], qkv[5], w["sp"][l], dmix, n_lat, "swa_bwd")
    ddq, ddk, ddv, g["t1"] = _na_bwd(qkv[6], qkv[7], qkv[8], w["t1"][l], dmix, n_lat, "na_bwd")
    dp, dsp_p, g["w_qb"], g["w_kvb"] = _prep_bwd(s["p"], tab, w["sp"][l], w["w_qb"][l], w["w_kvb"][l],
                                                (daq, dak, dav, dcq, dck, dcv, ddq, ddk, ddv), dpool, "prep_bwd")
    g["sp"] = dsp_c + dsp_p
    g["w_in"] = _mm(s["h1"], dp, "tn", GRAD_WIRE, "mm_dwin")
    dh1 = _mm(dp, w["w_in"][l], "nt", F32, "mm_dh1")
    dx0, g["g_mix"], dmod_d = _normmod_bwd(s["x"], w["g_mix"][l], mod, dh1, dx1, 0, nbl, "normmod_bwd")
    g["mods"] = dmod_a + dmod_b + dmod_c + dmod_d
    return dx0, g


def _local_step(x_all, target, w, tab, n_lat):
    saved = []
    x = x_all
    for l in range(DEPTH):
        x, s = _layer_fwd(x, w, l, tab, n_lat)
        saved.append(s)
    loss, dx = _loss_kernel(x, target, n_lat // TM, "loss")
    grads = [None] * DEPTH
    for l in reversed(range(DEPTH)):
        dx, grads[l] = _layer_bwd(dx, saved[l], w, l, tab, n_lat)
    return loss[0, 0], dx, grads


def _pad_cols(a, widths):
    parts, o = [], 0
    for take, pad in widths:
        parts.append(a[..., o:o + take])
        if pad:
            parts.append(jnp.zeros(a.shape[:-1] + (pad,), a.dtype))
        o += take
    return jnp.concatenate(parts, axis=-1)


def _w_in_layout(w_in):
    return _pad_cols(w_in, [(832, 64), (P_COLS - 832, PW - P_COLS - 64)])


def _w_in_unlayout(g):
    return jnp.concatenate([g[..., 0:832], g[..., 896:896 + P_COLS - 832]], axis=-1)


def _w_qb_layout(w):
    s = w.reshape(w.shape[:-1] + (4, 192))
    return jnp.concatenate([s[..., 0:128].reshape(w.shape[:-1] + (512,)), s[..., 128:192].reshape(w.shape[:-1] + (256,))], axis=-1)


def _w_qb_unlayout(g):
    n = g[..., 0:512].reshape(g.shape[:-1] + (4, 128))
    r = g[..., 512:768].reshape(g.shape[:-1] + (4, 64))
    return jnp.concatenate([n, r], axis=-1).reshape(g.shape[:-1] + (768,))


SP_SLOTS = (("mla_q_a_norm", 512), ("mla_kv_a_norm", 256), ("mla_q_nope_norm", 128), ("mla_q_rope_norm", 64),
            ("mla_k_nope_norm", 128), ("mla_k_rope_norm", 64), ("swa_q_norm", 64), ("swa_k_norm", 64),
            ("na_q_norm", 64), ("na_k_norm", 64), ("swa_sink", 8))


def _sp_pack(small):
    rows = [jnp.pad(small[k], ((0, 0), (0, 512 - n))) for k, n in SP_SLOTS]
    rows += [jnp.zeros_like(rows[0])] * (16 - len(rows))
    return jnp.stack(rows, axis=1)


def _sp_unpack(sp):
    return {k: sp[:, i, 0:n] for i, (k, n) in enumerate(SP_SLOTS)}


def _rpb_onehot():
    qc = lax.broadcasted_iota(I32, (GRID_W, GRID_W), 0)
    kc = lax.broadcasted_iota(I32, (GRID_W, GRID_W), 1)
    dc = (jnp.clip(kc - qc, -15, 15) + 15).reshape(1, GRID_W * GRID_W)
    return (lax.broadcasted_iota(I32, (32, GRID_W * GRID_W), 0) == dc).astype(F32)


def _rpb_expand(rpb):
    l = rpb.shape[0]
    flat = jnp.pad(rpb, ((0, 0), (0, 0), (0, 1), (0, 1))).reshape(l * 128, 32)
    t1 = _mm_exact(flat, _rpb_onehot(), "rpb_expand").reshape(l, 8, 16, GRID_W, GRID_W)
    qc = lax.broadcasted_iota(I32, (GRID_W, GRID_W), 0)
    kc = lax.broadcasted_iota(I32, (GRID_W, GRID_W), 1)
    c_lo = jnp.clip(qc - 8, 0, GRID_W - 16)
    col_ok = jnp.logical_and(kc >= c_lo, kc < c_lo + 16)
    row_ok = lax.broadcasted_iota(I32, (16, 1, 1), 0) != NA_MASKED
    return jnp.where(jnp.logical_and(col_ok[None], row_ok), t1, NEG)


def _rpb_fold(dt1):
    l = dt1.shape[0]
    g = _mm_exact(dt1.reshape(l * 128, GRID_W * GRID_W), _rpb_onehot().T, "rpb_fold")
    return g.reshape(l, 8, 16, 32)[:, :, 0:15, 0:31]


def _rope_table(n_lat, n_ctx):
    t = jnp.arange(n_lat)
    inv = ROPE_BASE ** (-jnp.arange(0, 32, 2, dtype=F32) / 32)
    ar = (t // GRID_W).astype(F32)[:, None] * inv
    ac = (t % GRID_W).astype(F32)[:, None] * inv
    cos = jnp.concatenate([jnp.cos(ar), jnp.cos(ar), jnp.cos(ac), jnp.cos(ac)], axis=-1)
    sin = jnp.concatenate([jnp.sin(ar), jnp.sin(ar), jnp.sin(ac), jnp.sin(ac)], axis=-1)
    tab = jnp.concatenate([cos, sin], axis=-1)
    ident = jnp.concatenate([jnp.ones((n_ctx, 64), F32), jnp.zeros((n_ctx, 64), F32)], axis=-1)
    return jnp.concatenate([tab, ident], axis=0)


def _small_weights(full, mods):
    l = full["g_mix"].shape[0]
    ff = full["ffn_conv_b"].shape[1] // 2
    return dict(
        mods=mods, g_mix=full["g_mix"][:, None, :], g_ffn=full["g_ffn"][:, None, :],
        sp=_sp_pack(full), pool_w=full["pool_w"], pool_sc=full["pool_scale"].reshape(l, 4, 1, 128),
        t1=_rpb_expand(full["na_rpb"]),
        conv_w=full["ffn_conv_w"].reshape(l, 3, 2, ff).transpose(0, 2, 1, 3),
        conv_b=full["ffn_conv_b"].reshape(l, 2, 1, ff))


def _kernel_weights(full, mods):
    w = _small_weights(full, mods)
    w.update(w_in=_w_in_layout(full["w_in"]).astype(BF16), w_out=full["w_out"].astype(BF16),
             w_up=full["ffn_w_up"].astype(BF16), w_down=full["ffn_w_down"].astype(BF16),
             w_qb=_w_qb_layout(full["mla_w_qb"]).astype(BF16), w_kvb=full["mla_w_kvb"].astype(BF16))
    return w


def _reference_grads(grads, big=True):
    st = lambda k: jnp.stack([g[k] for g in grads], axis=0)
    l = len(grads)
    out = dict(
        g_mix=st("g_mix")[:, 0], g_ffn=st("g_ffn")[:, 0],
        pool_w=st("pool_w"), pool_scale=st("pool_sc").reshape(l, 512), na_rpb=_rpb_fold(st("t1")),
        ffn_conv_w=st("conv_w").transpose(0, 2, 1, 3).reshape(l, 3, -1), ffn_conv_b=st("conv_b").reshape(l, -1),
        mods=st("mods"))
    if big:
        out.update(w_in=_w_in_unlayout(st("w_in")), w_out=st("w_out"), ffn_w_up=st("w_up"), ffn_w_down=st("w_down"),
                   mla_w_qb=_w_qb_unlayout(st("w_qb")), mla_w_kvb=st("w_kvb"))
    out.update(_sp_unpack(st("sp")))
    return out


ANY = pl.BlockSpec(memory_space=pl.ANY)


def _flip(x, y, j):
    return (1 - x if j >> 1 else x), (1 - y if j & 1 else y)


def _comm_call(name, ins, out_shapes, n_copies, plan, aliases=None):
    n_in, n_out = len(ins), len(out_shapes)

    def body(*refs):
        in_refs, out_refs = refs[:n_in], refs[n_in:n_in + n_out]
        ssem, rsem = refs[n_in + n_out:]
        pos = (lax.axis_index("x"), lax.axis_index("y"), lax.axis_index("c"))
        copies = plan(in_refs, out_refs, pos)
        assert len(copies) == n_copies
        descs = []
        for i, (src, dst, peer) in enumerate(copies):
            if peer is None:
                d = pltpu.make_async_copy(src, dst, ssem.at[i])
            else:
                d = pltpu.make_async_remote_copy(src_ref=src, dst_ref=dst, send_sem=ssem.at[i], recv_sem=rsem.at[i],
                                                 device_id=peer, device_id_type=MESH)
            d.start()
            descs.append(d)
        for d in descs:
            d.wait()

    return pl.pallas_call(
        body, name=name, in_specs=[ANY] * n_in, out_specs=[ANY] * n_out, out_shape=list(out_shapes),
        input_output_aliases=aliases or {},
        scratch_shapes=[pltpu.SemaphoreType.DMA((n_copies,)), pltpu.SemaphoreType.DMA((n_copies,))])(*ins)


def _sib_fill(bufs, part, name):
    def plan(ins, outs, pos):
        x, y, c = pos
        return [(o_ref.at[part(c)], o_ref.at[part(c)], (x, y, 1 - c)) for o_ref in outs]

    shapes = [jax.ShapeDtypeStruct(b.shape, b.dtype) for b in bufs]
    return _comm_call(name, bufs, shapes, len(bufs), plan, aliases={i: i for i in range(len(bufs))})


HBM = pl.BlockSpec(memory_space=pltpu.HBM)
SEM = pl.BlockSpec(memory_space=pltpu.SEMAPHORE)
DATAFLOW = pltpu.SideEffectType.DATAFLOW_SIDE_EFFECTING


def _remote_start(name, bufs, n_copies, plan, after):
    nb, na = len(bufs), len(after)

    def body(*refs):
        ssem, rsem, token = refs[nb + na], refs[nb + na + 1], refs[-1]
        pos = (lax.axis_index("x"), lax.axis_index("y"), lax.axis_index("c"))
        copies = plan(refs[:nb], pos)
        assert len(copies) == n_copies
        for i, (src, dst, peer) in enumerate(copies):
            pltpu.make_async_remote_copy(src_ref=src, dst_ref=dst, send_sem=ssem.at[i], recv_sem=rsem.at[i],
                                         device_id=peer, device_id_type=MESH).start()
        token[...] = jnp.zeros_like(token)

    outs = pl.pallas_call(
        body, name=name,
        out_shape=(pltpu.SemaphoreType.DMA((n_copies,)), pltpu.SemaphoreType.DMA((n_copies,)),
                   *[pltpu.HBM(b.shape, b.dtype) for b in bufs], jax.ShapeDtypeStruct((8, 128), F32)),
        in_specs=[HBM] * nb + [ANY] * na, out_specs=(SEM, SEM, *[HBM] * nb, pl.BlockSpec(memory_space=pltpu.VMEM)),
        input_output_aliases={i: 2 + i for i in range(nb)},
        compiler_params=pltpu.CompilerParams(has_side_effects=DATAFLOW),
    )(*[pltpu.with_memory_space_constraint(b, pltpu.HBM) for b in bufs], *after)
    return outs[0], outs[1], list(outs[2:2 + nb]), outs[-1]


def _remote_wait(name, ssem, rsem, bufs, n_copies, plan, after):
    nb = len(bufs)

    def body(*refs):
        ssem_ref, rsem_ref = refs[nb], refs[nb + 1]
        pos = (lax.axis_index("x"), lax.axis_index("y"), lax.axis_index("c"))
        copies = plan(refs[:nb], pos)
        assert len(copies) == n_copies
        for i, (src, dst, peer) in enumerate(copies):
            cp = pltpu.make_async_remote_copy(src_ref=src, dst_ref=dst, send_sem=ssem_ref.at[i], recv_sem=rsem_ref.at[i],
                                              device_id=peer, device_id_type=MESH)
            cp.wait_send()
            cp.wait_recv()

    outs = pl.pallas_call(
        body, name=name, out_shape=tuple(pltpu.HBM(b.shape, b.dtype) for b in bufs),
        in_specs=[HBM] * nb + [SEM, SEM, ANY], out_specs=tuple([HBM] * nb), input_output_aliases={i: i for i in range(nb)},
        compiler_params=pltpu.CompilerParams(has_side_effects=DATAFLOW),
    )(*bufs, ssem, rsem, after)
    return list(outs)


BIG_GATHER = ("axis1", "axis1", "axis1", "axis1", "lane", "lane")


def _place_own(shard, kind, xyvec, name):
    rows, cols = shard.shape
    tr = _row_tile(rows, cols)
    if kind == "axis1":
        shape = (4, rows, cols)
        o_spec = pl.BlockSpec((None, tr, cols), lambda i, x_ref, y_ref: (2 * x_ref[0] + y_ref[0], i, 0))
    else:
        shape = (rows, 4 * cols)
        o_spec = pl.BlockSpec((tr, cols), lambda i, x_ref, y_ref: (i, 2 * x_ref[0] + y_ref[0]))

    def body(x_ref, y_ref, s_ref, o_ref):
        o_ref[...] = s_ref[...]

    return pl.pallas_call(
        body, name=name,
        grid_spec=pltpu.PrefetchScalarGridSpec(num_scalar_prefetch=2, grid=(rows // tr,),
                                               in_specs=[pl.BlockSpec((tr, cols), lambda i, x_ref, y_ref: (i, 0))], out_specs=o_spec),
        out_shape=jax.ShapeDtypeStruct(shape, shard.dtype), compiler_params=_params(("parallel",)))(*xyvec, shard)


def _w_gather_plan(shapes, kinds):
    n = len(kinds)

    def plan(refs, pos):
        x, y, c = pos
        k = 2 * x + y
        cps = []
        for s_ref, l_ref, kind, shp in zip(refs[:n], refs[n:], kinds, shapes):
            h, w = shp[0] // 2, shp[1]
            rows = pl.ds(pl.multiple_of(c * h, 16), h)
            dst = l_ref.at[k, rows, :] if kind == "axis1" else l_ref.at[rows, pl.ds(pl.multiple_of(k * w, 128), w)]
            for j in (1, 2, 3):
                tx, ty = _flip(x, y, j)
                cps.append((s_ref.at[rows, :], dst, (tx, ty, c)))
        return cps

    return plan


def _w_fill(lands, shapes, kinds, name):
    def plan(ins, outs, pos):
        x, y, c = pos
        cps = []
        for o_ref, kind, shp in zip(outs, kinds, shapes):
            h = shp[0] // 2
            rows = pl.ds(pl.multiple_of(c * h, 16), h)
            part = o_ref.at[:, rows, :] if kind == "axis1" else o_ref.at[rows, :]
            cps.append((part, part, (x, y, 1 - c)))
        return cps

    return _comm_call(name, lands, [jax.ShapeDtypeStruct(b.shape, b.dtype) for b in lands], len(lands), plan,
                      aliases={i: i for i in range(len(lands))})


def _g_scatter_plan(kinds, widths):
    n = len(kinds)

    def plan(refs, pos):
        x, y, c = pos
        cps = []
        for s_ref, l_ref, kind, w in zip(refs[:n], refs[n:], kinds, widths):
            for j in (1, 2, 3):
                tx, ty = _flip(x, y, j)
                kj = 2 * tx + ty
                src = s_ref.at[kj] if kind == "cm" else s_ref.at[:, pl.ds(pl.multiple_of(kj * w, 128), w)]
                cps.append((src, l_ref.at[j - 1], (tx, ty, c)))
        return cps

    return plan


def _pair_up(xs, name):
    def plan(ins, outs, pos):
        x, y, c = pos
        cps = []
        for i_ref, o_ref in zip(ins, outs):
            cps.append((i_ref, o_ref.at[c], None))
            cps.append((i_ref, o_ref.at[c], (x, y, 1 - c)))
        return cps

    return _comm_call(name, xs, [jax.ShapeDtypeStruct((2,) + a.shape, a.dtype) for a in xs], 2 * len(xs), plan)


def _chip_gather(xs, kinds, name):
    def dst(o_ref, kind, k, x_shape):
        if kind == "lead":
            return o_ref.at[k]
        w = x_shape[-1]
        return o_ref.at[(slice(None),) * (len(x_shape) - 1) + (pl.ds(pl.multiple_of(k * w, 128), w),)]

    def plan(ins, outs, pos):
        x, y, c = pos
        k = 2 * x + y
        cps = []
        for i_ref, o_ref, kind, a in zip(ins, outs, kinds, xs):
            cps.append((i_ref, dst(o_ref, kind, k, a.shape), None))
            for j in (1, 2, 3):
                tx, ty = _flip(x, y, j)
                cps.append((i_ref, dst(o_ref, kind, k, a.shape), (tx, ty, c)))
        return cps

    def oshape(a, kind):
        return (4,) + a.shape if kind == "lead" else a.shape[:-1] + (4 * a.shape[-1],)

    return _comm_call(name, xs, [jax.ShapeDtypeStruct(oshape(a, kd), a.dtype) for a, kd in zip(xs, kinds)], 4 * len(xs), plan)


def _row_tile(rows, cols, budget=1 << 20):
    for t in (2048, 1024, 512, 256, 128, 64, 32, 16, 8):
        if rows % t == 0 and t * cols * 4 <= budget:
            return t
    return rows


def _sum_lead(x, name):
    n, rows, w = x.shape
    tr = _row_tile(rows, w, (1 << 21) // n)

    def body(x_ref, o_ref):
        acc = x_ref[0].astype(F32)
        for j in range(1, n):
            acc = acc + x_ref[j].astype(F32)
        o_ref[...] = acc

    return pl.pallas_call(
        body, name=name, grid=(rows // tr,), in_specs=[pl.BlockSpec((n, tr, w), lambda i: (0, i, 0))],
        out_specs=pl.BlockSpec((tr, w), lambda i: (i, 0)), out_shape=jax.ShapeDtypeStruct((rows, w), F32),
        compiler_params=_params(("parallel",)))(x)


def _sum_into(g, landed, buf, layer, pvec, kind, layers, name):
    n, rows, w = landed.shape
    tr = _row_tile(rows, w)

    def body(c_ref, x_ref, y_ref, own_ref, l_ref, *refs):
        o_ref = refs[-1]
        acc = own_ref[...].astype(F32)
        for j in range(n):
            acc = acc + l_ref[j].astype(F32)
        o_ref[...] = acc

    if kind == "cm":
        own_spec = pl.BlockSpec((None, tr, w), lambda i, c_ref, x_ref, y_ref: (2 * x_ref[0] + y_ref[0], i, 0))
    else:
        own_spec = pl.BlockSpec((tr, w), lambda i, c_ref, x_ref, y_ref: (i, 2 * x_ref[0] + y_ref[0]))
    in_specs = [own_spec, pl.BlockSpec((n, tr, w), lambda i, c_ref, x_ref, y_ref: (0, i, 0))]
    args = [*pvec, g, landed]
    if buf is not None:
        in_specs.append(ANY)
        args.append(buf)
    return pl.pallas_call(
        body, name=name,
        grid_spec=pltpu.PrefetchScalarGridSpec(
            num_scalar_prefetch=3, grid=(rows // tr,), in_specs=in_specs,
            out_specs=pl.BlockSpec((None, None, tr, w), lambda i, c_ref, x_ref, y_ref: (layer, c_ref[0], i, 0))),
        out_shape=jax.ShapeDtypeStruct((layers, 2, rows, w), F32), input_output_aliases={} if buf is None else {5: 0},
        compiler_params=_params(("arbitrary",)))(*args)


def _adamw_math(w, g, m, v):
    mn = ADAM_B1 * m + (1.0 - ADAM_B1) * g
    vn = ADAM_B2 * v + (1.0 - ADAM_B2) * jnp.square(g)
    m_hat = mn / (1.0 - ADAM_B1 ** ADAM_STEP)
    v_hat = vn / (1.0 - ADAM_B2 ** ADAM_STEP)
    return -ADAM_LR * (m_hat / (jnp.sqrt(v_hat) + ADAM_EPS) + ADAM_WD * w), mn, vn


def _adamw_pair(w, g2, m, v, name):
    l, rows, cols = w.shape
    tr = _row_tile(rows, cols)

    def body(w_ref, g0_ref, g1_ref, m_ref, v_ref, g_ref, d_ref, mo_ref, vo_ref):
        gv = g0_ref[...] + g1_ref[...]
        g_ref[...] = gv
        d_ref[...], mo_ref[...], vo_ref[...] = _adamw_math(w_ref[...], gv, m_ref[...], v_ref[...])

    spec = pl.BlockSpec((None, tr, cols), lambda li, i: (li, i, 0))
    half = lambda cc: pl.BlockSpec((None, None, tr, cols), lambda li, i: (li, cc, i, 0))
    return pl.pallas_call(
        body, name=name, grid=(l, rows // tr), in_specs=[spec, half(0), half(1), spec, spec], out_specs=[spec] * 4,
        out_shape=[jax.ShapeDtypeStruct(w.shape, F32)] * 4, compiler_params=_params(("parallel", "parallel")))(w, g2, g2, m, v)


def _adamw(w, g, m, v, name):
    shape = w.shape
    cols = shape[-1]
    rows = w.size // cols
    tr = _row_tile(rows, cols)

    def body(w_ref, g_ref, m_ref, v_ref, d_ref, mo_ref, vo_ref):
        d_ref[...], mo_ref[...], vo_ref[...] = _adamw_math(w_ref[...], g_ref[...], m_ref[...], v_ref[...])

    spec = pl.BlockSpec((tr, cols), lambda i: (i, 0))
    outs = pl.pallas_call(
        body, name=name, grid=(rows // tr,), in_specs=[spec] * 4, out_specs=[spec] * 3,
        out_shape=[jax.ShapeDtypeStruct((rows, cols), F32)] * 3,
        compiler_params=_params(("parallel",)))(*[a.reshape(rows, cols) for a in (w, g, m, v)])
    return [o.reshape(shape) for o in outs]


def _silu_grad(x):
    s = jax.nn.sigmoid(x)
    return s * (1.0 + x * (1.0 - s))


def _mod_fwd(cs16, w_mod, b_sh, name):
    l, d, wc = w_mod.shape
    tn = _pick(wc, (512, 384, 256, 128))

    def body(c_ref, w_ref, b_ref, o_ref):
        a = jax.nn.silu(c_ref[...]).astype(BF16)
        o_ref[...] = jnp.dot(a, w_ref[...].astype(BF16), preferred_element_type=F32) + b_ref[...]

    return pl.pallas_call(
        body, name=name, grid=(l, wc // tn),
        in_specs=[_full_spec((16, d)), pl.BlockSpec((None, d, tn), lambda i, j: (i, 0, j)), pl.BlockSpec((None, 1, tn), lambda i, j: (i, 0, j))],
        out_specs=pl.BlockSpec((None, 16, tn), lambda i, j: (i, 0, j)), out_shape=jax.ShapeDtypeStruct((l, 16, wc), F32),
        compiler_params=_params(("parallel", "parallel")))(cs16, w_mod, b_sh)


def _mod_dw(cs16, dm_sh, name):
    l, _, wc = dm_sh.shape
    d = cs16.shape[1]
    tr = _pick(d, (512, 256, 128))
    tc = _pick(wc, (512, 384, 256, 128))

    def body(c_ref, dm_ref, o_ref):
        a = jax.nn.silu(c_ref[...]).astype(BF16)
        o_ref[...] = lax.dot_general(a, dm_ref[...].astype(BF16), (((0,), (0,)), ((), ())), preferred_element_type=F32)

    return pl.pallas_call(
        body, name=name, grid=(l, d // tr, wc // tc),
        in_specs=[pl.BlockSpec((16, tr), lambda i, r, j: (0, r)), pl.BlockSpec((None, 16, tc), lambda i, r, j: (i, 0, j))],
        out_specs=pl.BlockSpec((None, tr, tc), lambda i, r, j: (i, r, j)), out_shape=jax.ShapeDtypeStruct((l, d, wc), F32),
        compiler_params=_params(("parallel", "parallel", "parallel")))(cs16, dm_sh)


def _mod_dc(dm_sh, w_mod, c_ctx, name):
    l, d, wc = w_mod.shape
    tk = _pick(wc, (512, 384, 256, 128))
    nk = wc // tk

    def body(dm_ref, w_ref, c_ref, o_ref, acc_ref):
        i, j = pl.program_id(0), pl.program_id(1)

        @pl.when(jnp.logical_and(i == 0, j == 0))
        def _():
            acc_ref[...] = jnp.zeros_like(acc_ref)

        acc_ref[...] += lax.dot_general(dm_ref[...].astype(BF16), w_ref[...].astype(BF16), (((1,), (1,)), ((), ())),
                                        preferred_element_type=F32)

        @pl.when(jnp.logical_and(i == l - 1, j == nk - 1))
        def _():
            mine = jnp.where(lax.axis_index("c") == 0, 1.0, 0.0)
            o_ref[...] = acc_ref[8:9, :] * _silu_grad(c_ref[...]) * mine

    return pl.pallas_call(
        body, name=name, grid=(l, nk),
        in_specs=[pl.BlockSpec((None, 16, tk), lambda i, j: (i, 0, j)), pl.BlockSpec((None, d, tk), lambda i, j: (i, 0, j)), _full_spec((1, d))],
        out_specs=_full_spec((1, d)), out_shape=jax.ShapeDtypeStruct((1, d), F32), scratch_shapes=[pltpu.VMEM((16, d), F32)],
        compiler_params=_params(("arbitrary", "arbitrary")))(dm_sh, w_mod, c_ctx)


def _dmod_assemble(gath, name):
    _, l, _, w = gath.shape
    gath = gath.transpose(1, 2, 0, 3)
    tc = _pick(w, (2048, 1024, 512, 256, 128))

    def body(lat_ref, ctx_ref, o_ref, b_ref):
        ctx = ctx_ref[0:1, :]
        for dev in range(1, 8):
            ctx = ctx + ctx_ref[dev:dev + 1, :]
        lat = lat_ref[...]
        o_ref[0:8, :] = lat
        o_ref[8:9, :] = ctx
        o_ref[9:16, :] = jnp.zeros((7, tc), F32)
        b_ref[...] = jnp.sum(lat, axis=0, keepdims=True) + ctx

    return pl.pallas_call(
        body, name=name, grid=(l, w // tc),
        in_specs=[pl.BlockSpec((None, None, 8, tc), lambda i, j: (i, 0, 0, j)), pl.BlockSpec((None, None, 8, tc), lambda i, j: (i, 1, 0, j))],
        out_specs=[pl.BlockSpec((None, 16, tc), lambda i, j: (i, 0, j)), pl.BlockSpec((None, 1, tc), lambda i, j: (i, 0, j))],
        out_shape=[jax.ShapeDtypeStruct((l, 16, w), F32), jax.ShapeDtypeStruct((l, 1, w), F32)],
        compiler_params=_params(("parallel", "parallel")))(gath, gath)


SMALL = ("c_ctx", "g_mix", "g_ffn", "mla_q_a_norm", "mla_kv_a_norm", "mla_q_nope_norm", "mla_q_rope_norm", "mla_k_nope_norm",
         "mla_k_rope_norm", "pool_w", "pool_scale", "swa_q_norm", "swa_k_norm", "swa_sink", "na_q_norm", "na_k_norm", "na_rpb",
         "ffn_conv_b")
PACK_W = 512
PACK_Q = 8 * PACK_W


def _pack(arrs):
    flat = []
    for a in arrs:
        f = a.reshape(-1)
        flat.append(jnp.pad(f, (0, (-f.size) % PACK_Q)))
    return jnp.concatenate(flat).reshape(-1, PACK_W)


def _unpack(packed, shapes):
    flat, out, o = packed.reshape(-1), [], 0
    for s in shapes:
        n = 1
        for dim in s:
            n *= dim
        out.append(flat[o:o + n].reshape(s))
        o += n + (-n) % PACK_Q
    return out


def _all_sum(p, name):
    pair = _pair_up([p], name + "_pair")[0]
    chip = _sum_lead(pair, name + "_sum2")
    return _sum_lead(_chip_gather([chip], ["lead"], name + "_gather")[0], name + "_sum4")


WEIGHTS = ("c_ctx", "w_mod", "b_mod", "g_mix", "g_ffn", "w_in", "w_out", "mla_q_a_norm", "mla_w_qb", "mla_kv_a_norm", "mla_w_kvb",
           "mla_q_nope_norm", "mla_q_rope_norm", "mla_k_nope_norm", "mla_k_rope_norm", "pool_w", "pool_scale", "swa_q_norm",
           "swa_k_norm", "swa_sink", "na_q_norm", "na_k_norm", "na_rpb", "ffn_w_up", "ffn_conv_w", "ffn_conv_b", "ffn_w_down")
BIG = ("w_in", "mla_w_qb", "w_out", "ffn_w_down", "ffn_w_up", "mla_w_kvb")
BIG_KINDS = ("cm", "cm", "cm", "cm", "lb", "lb")


def _step(a):
    x, c, ctx = a["x"], a["c"], a["ctx"]
    n_lat, d = x.shape[1], x.shape[2]
    n_ctx = ctx.shape[1]
    l = DEPTH
    px, py, pc = lax.axis_index("x"), lax.axis_index("y"), lax.axis_index("c")
    chip = 2 * px + py
    pvec = [p.reshape(1).astype(I32) for p in (pc, px, py)]
    cvec = pvec[0]

    bf = {k: a[k].astype(BF16) for k in BIG}
    w = {key: [None] * l for key in ("w_in", "w_qb", "w_out", "w_down", "w_up", "w_kvb")}

    finish = {"w_in": lambda g: _w_in_layout(g.transpose(1, 0, 2).reshape(d, P_COLS)),
              "mla_w_qb": lambda g: _w_qb_layout(g.transpose(1, 0, 2).reshape(512, 768)),
              "w_out": lambda g: g.reshape(-1, d), "ffn_w_down": lambda g: g.reshape(-1, d),
              "ffn_w_up": lambda g: g, "mla_w_kvb": lambda g: g}
    slot = dict(zip(BIG, ("w_in", "w_qb", "w_out", "w_down", "w_up", "w_kvb")))
    kind_of = dict(zip(BIG, BIG_GATHER))

    def gather_start(li, keys, tag, after):
        shards = [bf[k][li] for k in keys]
        shapes = [s.shape for s in shards]
        kinds = [kind_of[k] for k in keys]
        lands = [_place_own(s, kd, pvec[1:], "w_place") for s, kd in zip(shards, kinds)]
        plan = _w_gather_plan(shapes, kinds)
        return _remote_start(f"w_start_{li}{tag}", shards + lands, 3 * len(keys), plan, after) + (shapes, kinds, keys, plan)

    def gather_finish(li, started, tag, after):
        ssem, rsem, bufs, _, shapes, kinds, keys, plan = started
        bufs = _remote_wait(f"w_wait_{li}{tag}", ssem, rsem, bufs, 3 * len(keys), plan, after)
        for k, g in zip(keys, _w_fill(bufs[len(keys):], shapes, kinds, "w_fill")):
            w[slot[k]][li] = finish[k](g)

    first, rest = ("w_in", "mla_w_qb", "mla_w_kvb"), ("w_out", "ffn_w_down", "ffn_w_up")
    started = gather_start(0, first, "a", [c])
    c = c + started[3][0:1, 0:1]

    c_all = _chip_gather(_pair_up([c], "c_pair"), ["lead"], "c_gather")[0].reshape(8, d)
    cs16 = jnp.concatenate([c_all, a["c_ctx"][None, :], jnp.zeros((7, d), F32)], axis=0)
    wc = a["w_mod"].shape[-1]
    b_sh = lax.dynamic_slice_in_dim(a["b_mod"], chip * wc, wc, axis=1)[:, None, :]
    mod_sh = _mod_fwd(cs16, a["w_mod"], b_sh, "mod_fwd")
    mod_all, conv_w_full = _chip_gather([mod_sh, a["ffn_conv_w"]], ["lead", "lane"], "mod_gather")
    started_rest = gather_start(0, rest, "b", [mod_all])
    mod_all = mod_all + started_rest[3][0, 0]
    mod_all = mod_all.transpose(1, 2, 0, 3).reshape(l, 16, 4 * wc)
    mods = jnp.stack([lax.dynamic_index_in_dim(mod_all, 2 * chip + pc, axis=1, keepdims=False), mod_all[:, 8]], axis=1)
    mods = mods.reshape(l, 2, 6, d)

    full = {k: a[k] for k in SMALL if k != "c_ctx"}
    full["ffn_conv_w"] = conv_w_full
    w.update(_small_weights(full, mods))
    w["mods"] = [w["mods"][li] for li in range(l)]
    w["g_mix"] = [w["g_mix"][li] for li in range(l)]
    w["g_ffn"] = [w["g_ffn"][li] for li in range(l)]
    tab = _rope_table(n_lat, n_ctx)
    gather_finish(0, started, "a", mods)
    xs = jnp.concatenate([x[0], ctx[0]], axis=0)
    saved = []
    nxt = {}

    def rest_of_layer0(mix):
        gather_finish(0, started_rest, "b", mix)
        nxt[1] = gather_start(1, BIG, "", [w["w_up"][0]])
        w["g_ffn"][0] = w["g_ffn"][0] + nxt[1][3][0:1, 0:1]

    for li in range(l):
        if 1 <= li < l - 1:
            nxt[li + 1] = gather_start(li + 1, BIG, "", [w["w_kvb"][li]])
            w["g_mix"][li] = w["g_mix"][li] + nxt[li + 1][3][0:1, 0:1]
        xs, s = _layer_fwd(xs, w, li, tab, n_lat, before_out=rest_of_layer0 if li == 0 else None)
        saved.append(s)
        if li + 1 < l:
            gather_finish(li + 1, nxt[li + 1], "", xs)
    loss, dx = _loss_kernel(xs, a["loss_target"][0], n_lat // TM, "loss")
    loss = lax.psum(loss[0, 0], ("x", "y", "c"))

    grads = [None] * l
    g_buf = {k: None for k in BIG}
    kind_g = dict(zip(BIG, BIG_KINDS))
    ffn_keys = ("ffn_w_down", "ffn_w_up")
    att_keys = ("w_in", "mla_w_qb", "w_out", "mla_w_kvb")

    def grad_pieces(g, keys):
        ops = {"w_in": lambda: _w_in_unlayout(g["w_in"]).reshape(d, 4, -1).transpose(1, 0, 2),
               "mla_w_qb": lambda: _w_qb_unlayout(g["w_qb"].astype(GRAD_WIRE)).reshape(512, 4, 192).transpose(1, 0, 2),
               "w_out": lambda: g["w_out"].reshape(4, -1, d), "ffn_w_down": lambda: g["w_down"].reshape(4, -1, d),
               "ffn_w_up": lambda: g["w_up"], "mla_w_kvb": lambda: g["w_kvb"].astype(GRAD_WIRE)}
        return [ops[k]() for k in keys]

    def scatter_start(li, g, keys, tag, after):
        ops = grad_pieces(g, keys)
        lands = [jnp.zeros((3, a[k].shape[1], a[k].shape[2]), GRAD_WIRE) for k in keys]
        plan = _g_scatter_plan([kind_g[k] for k in keys], [a[k].shape[2] for k in keys])
        return _remote_start(f"g_start_{li}{tag}", ops + lands, 3 * len(keys), plan, after) + (keys, plan)

    def scatter_finish(li, started, tag, after):
        ssem, rsem, bufs, _, keys, plan = started
        bufs = _remote_wait(f"g_wait_{li}{tag}", ssem, rsem, bufs, 3 * len(keys), plan, after)
        for k, own, landed in zip(keys, bufs[:len(keys)], bufs[len(keys):]):
            g_buf[k] = _sum_into(own, landed, g_buf[k], li, pvec, kind_g[k], l, "g_sum4")

    pending, early = None, []

    def ffn_grads_of_layer0(g):
        early.append(scatter_start(0, g, ffn_keys, "a", [g["w_up"]]))
        return early[0][3]

    for li in reversed(range(l)):
        dx, grads[li] = _layer_bwd(dx, saved[li], w, li, tab, n_lat, after_ffn=ffn_grads_of_layer0 if li == 0 else None)
        if pending is not None:
            scatter_finish(li + 1, pending, "", dx)
        if li > 0:
            pending = scatter_start(li, grads[li], BIG, "", [grads[li]["w_in"]])
            w["mods"][li - 1] = w["mods"][li - 1] + pending[3][0, 0]

    dmods = jnp.stack([grads[li]["mods"] for li in range(l)], axis=0).reshape(l, 2, 6 * d)
    dm_gath = _chip_gather(_pair_up([dmods], "dmod_pair"), ["lead"], "dmod_gather")[0].reshape(8, l, 2, 6 * d)
    dmod_all, g_b_mod = _dmod_assemble(dm_gath, "dmod_assemble")
    dm_sh = lax.dynamic_slice_in_dim(dmod_all, chip * wc, wc, axis=2)
    g_c_ctx = _mod_dc(dm_sh, a["w_mod"], a["c_ctx"][None, :], "mod_dc")

    rg = _reference_grads(grads, big=False)
    rg["c_ctx"] = g_c_ctx[0]
    packed = _all_sum(_pack([rg[k] for k in SMALL] + [rg["ffn_conv_w"]]), "small")
    late = scatter_start(0, grads[0], att_keys, "b", [packed, dmod_all])
    g_w_mod = _mod_dw(cs16, dm_sh + late[3][0, 0], "mod_dw")
    g_out = {"w_mod": g_w_mod, "b_mod": g_b_mod.reshape(l, 6 * d)}
    small_g = _unpack(packed, [a[k].shape for k in SMALL] + [rg["ffn_conv_w"].shape])
    for k, g in zip(SMALL, small_g[:-1]):
        g_out[k] = g
    cw = a["ffn_conv_w"].shape[-1]
    g_out["ffn_conv_w"] = lax.dynamic_slice_in_dim(small_g[-1], chip * cw, cw, axis=2)

    upd = {}
    pk = lambda pre: _pack([a[pre + k] for k in SMALL])
    outs = _adamw(pk(""), _pack([g_out[k] for k in SMALL]), pk("m_"), pk("v_"), "adamw_small")
    for o, kind in zip(outs, ("delta", "m", "v")):
        for k, val in zip(SMALL, _unpack(o, [a[k].shape for k in SMALL])):
            upd[kind, k] = val
    def adamw_each(keys):
        for k in keys:
            outs = _adamw(a[k], g_out[k], a["m_" + k], a["v_" + k], "adamw_" + k)
            for o, kind in zip(outs, ("delta", "m", "v")):
                upd[kind, k] = o

    adamw_each(("w_mod", "b_mod", "ffn_conv_w"))
    scatter_finish(0, early[0], "a", upd["delta", "w_mod"])
    scatter_finish(0, late, "b", upd["delta", "w_mod"])
    g_big = _sib_fill([g_buf[k] for k in BIG], lambda cc: (slice(None), cc), "g_pair")
    for k, g2 in zip(BIG, g_big):
        g_out[k], upd["delta", k], upd["m", k], upd["v", k] = _adamw_pair(a[k], g2, a["m_" + k], a["v_" + k], "adamw_" + k)
    grad_x = dx[0:n_lat].reshape(x.shape)
    return (loss, grad_x, *[g_out[k] for k in WEIGHTS], *[upd["delta", k] for k in WEIGHTS],
            *[upd["m", k] for k in WEIGHTS], *[upd["v", k] for k in WEIGHTS])


def kernel(x, c, ctx, c_ctx, w_mod, b_mod, g_mix, g_ffn, w_in, w_out, mla_q_a_norm, mla_w_qb, mla_kv_a_norm, mla_w_kvb, mla_q_nope_norm, mla_q_rope_norm, mla_k_nope_norm, mla_k_rope_norm, pool_w, pool_scale, swa_q_norm, swa_k_norm, swa_sink, na_q_norm, na_k_norm, na_rpb, ffn_w_up, ffn_conv_w, ffn_conv_b, ffn_w_down, loss_target, m_c_ctx, m_w_mod, m_b_mod, m_g_mix, m_g_ffn, m_w_in, m_w_out, m_mla_q_a_norm, m_mla_w_qb, m_mla_kv_a_norm, m_mla_w_kvb, m_mla_q_nope_norm, m_mla_q_rope_norm, m_mla_k_nope_norm, m_mla_k_rope_norm, m_pool_w, m_pool_scale, m_swa_q_norm, m_swa_k_norm, m_swa_sink, m_na_q_norm, m_na_k_norm, m_na_rpb, m_ffn_w_up, m_ffn_conv_w, m_ffn_conv_b, m_ffn_w_down, v_c_ctx, v_w_mod, v_b_mod, v_g_mix, v_g_ffn, v_w_in, v_w_out, v_mla_q_a_norm, v_mla_w_qb, v_mla_kv_a_norm, v_mla_w_kvb, v_mla_q_nope_norm, v_mla_q_rope_norm, v_mla_k_nope_norm, v_mla_k_rope_norm, v_pool_w, v_pool_scale, v_swa_q_norm, v_swa_k_norm, v_swa_sink, v_na_q_norm, v_na_k_norm, v_na_rpb, v_ffn_w_up, v_ffn_conv_w, v_ffn_conv_b, v_ffn_w_down):
    return _step(dict(locals()))
```
